```python
import math
import jax, jax.numpy as jnp
from jax import lax
import numpy as np


D_MODEL = 1024
BATCH = 8
SEQ = 2048
DEPTH = 2
DEC_BATCH = 128
DEC_SEQ = 1
PAST_LEN = 16384
PAGE_SIZE = 128

MIX_WIDTH = D_MODEL
W_A = MIX_WIDTH // 4
W_B = MIX_WIDTH // 4
W_C = MIX_WIDTH // 4
W_D = MIX_WIDTH - W_A - W_B - W_C
A_HEADS = 4
A_HEAD_DIM = W_A // A_HEADS
CHUNK = 128
POOL_WINDOWS = (2, 4, 8, 16)
POOL_GROUP = W_B // len(POOL_WINDOWS)
POOL_BUF = max(POOL_WINDOWS) - 1
CONV_W = 3
D_HEAD_DIM = 64
D_HEADS = W_D // D_HEAD_DIM
R_DECAY = 32
R_AAA = 32
R_GATE = 64
D_PROJ = 3 * W_D + R_DECAY + R_AAA + R_GATE
PROJ = 2 * W_A + W_B + 3 * W_C + D_PROJ
MEM_LEN = 256
X_HEADS = 4
X_HEAD_DIM = D_MODEL // X_HEADS
D_FF = int(math.ceil(8 * D_MODEL / 3 / 256)) * 256
ALPHA = (2 * DEPTH) ** 0.25
BETA = (8 * DEPTH) ** -0.25
LN_EPS = 1e-5
GN_EPS = 64e-5

kernel_name = 'hymba_style_chunkmlp_pool_conv_rwkv7_decoder_step'


def layer_norm(x, g, b, eps=LN_EPS):
    xf = x.astype(jnp.float32)
    mu = jnp.mean(xf, axis=-1, keepdims=True)
    var = jnp.mean(jnp.square(xf - mu), axis=-1, keepdims=True)
    return ((xf - mu) * lax.rsqrt(var + eps)).astype(x.dtype) * g + b


def chunk_spatial_gate(v, ws, bs):
    bn, t, _ = v.shape
    n_chunks = -(-t // CHUNK)
    vp = jnp.pad(v, ((0, 0), (0, n_chunks * CHUNK - t), (0, 0)))
    vp = vp.reshape(bn, n_chunks, CHUNK, A_HEADS, A_HEAD_DIM)
    ws_causal = ws * jnp.tril(jnp.ones((CHUNK, CHUNK), ws.dtype))
    z = jnp.einsum('hts,bcshd->bcthd', ws_causal, vp) + jnp.swapaxes(bs, 0, 1)[None, None, :, :, None]
    return z.reshape(bn, n_chunks * CHUNK, W_A)[:, :t]


def multiscale_pool(xb, buf, pos0, w_pool, scale):
    t = xb.shape[1]
    x_ext = jnp.concatenate([buf, xb], axis=1)
    cs = jnp.cumsum(x_ext.astype(jnp.float32), axis=1)
    cs = jnp.pad(cs, ((0, 0), (1, 0), (0, 0)))
    end = cs[:, POOL_BUF + 1:]
    pos = pos0 + jnp.arange(t)
    outs = []
    for gi, win in enumerate(POOL_WINDOWS):
        sl = slice(gi * POOL_GROUP, (gi + 1) * POOL_GROUP)
        start = cs[:, POOL_BUF + 1 - win:POOL_BUF + 1 - win + t, sl]
        cnt = jnp.minimum(win, pos + 1).astype(jnp.float32)[None, :, None]
        d = ((end[..., sl] - start) / cnt).astype(xb.dtype) - xb[..., sl]
        outs.append(d @ w_pool[gi])
    y = jnp.concatenate(outs, axis=-1) * scale
    return y, x_ext[:, -POOL_BUF:]


def short_conv(xc, buf, conv_w):
    bg, cg, xin = jnp.split(xc, 3, axis=-1)
    z = cg * xin
    t = z.shape[1]
    z_ext = jnp.concatenate([buf, z], axis=1)
    y = conv_w[0] * z_ext[:, 0:t]
    for j in range(1, CONV_W):
        y = y + conv_w[j] * z_ext[:, j:j + t]
    return bg * y, z_ext[:, -(CONV_W - 1):]


def wkv_scan(s0, r, w, k, v, kk, a):
    def step(s, inp):
        r_t, w_t, k_t, v_t, kk_t, a_t = inp
        sa = jnp.einsum('bhvk,bhk->bhv', s, -kk_t)
        s = s * w_t[:, :, None, :] + sa[..., None] * (kk_t * a_t)[:, :, None, :] + v_t[..., None] * k_t[:, :, None, :]
        return s, jnp.einsum('bhvk,bhk->bhv', s, r_t)
    tm = lambda z: jnp.swapaxes(z, 0, 1)
    s, o = lax.scan(step, s0, (tm(r), tm(w), tm(k), tm(v), tm(kk), tm(a)))
    return tm(o), s


def rwkv7_mix(pd, shift_buf, s0, p):
    bn, t, _ = pd.shape
    prev = jnp.concatenate([shift_buf, pd[:, :-1]], axis=1)
    xs = pd + (prev - pd) * p['mu_d']
    r, k, v, dw, da, dg = jnp.split(xs, (W_D, 2 * W_D, 3 * W_D, 3 * W_D + R_DECAY, 3 * W_D + R_DECAY + R_AAA), axis=-1)
    w_log = -jax.nn.softplus(-(p['rwkv_w0'] + jnp.tanh(dw) @ p['rwkv_w2']).astype(jnp.float32)) - 0.5
    decay = jnp.exp(-jnp.exp(w_log))
    a = jax.nn.sigmoid(p['rwkv_a0'] + da @ p['rwkv_a2'])
    g = jax.nn.sigmoid(dg) @ p['rwkv_g2']
    heads = lambda z: z.reshape(bn, t, D_HEADS, D_HEAD_DIM).astype(jnp.float32)
    kk = heads(k * p['rwkv_k_k'])
    kk = kk * lax.rsqrt(jnp.maximum(jnp.sum(kk * kk, axis=-1, keepdims=True), 1e-12))
    k = k * (1 + (a - 1) * p['rwkv_k_a'])
    rh, kh, vh, ah, wh = heads(r), heads(k), heads(v), heads(a), heads(decay)
    o, s = wkv_scan(s0.astype(jnp.float32), rh, wh, kh, vh, kk, ah)
    mu = jnp.mean(o, axis=-1, keepdims=True)
    var = jnp.mean(jnp.square(o - mu), axis=-1, keepdims=True)
    o = ((o - mu) * lax.rsqrt(var + GN_EPS)).reshape(bn, t, W_D) * p['rwkv_lnx_g'] + p['rwkv_lnx_b']
    bonus = jnp.sum(rh * kh * p['rwkv_r_k'], axis=-1, keepdims=True) * vh
    o = (o + bonus.reshape(bn, t, W_D)) * g
    return o.astype(pd.dtype), pd[:, -1:], s


def hybrid_mixer(h, pos0, pool_buf, conv_buf, shift_buf, wkv_state, p):
    proj = h @ p['w_in']
    pa, pb, pc, pd = jnp.split(proj, (2 * W_A, 2 * W_A + W_B, 2 * W_A + W_B + 3 * W_C), axis=-1)
    u, v = jnp.split(jax.nn.gelu(pa), 2, axis=-1)
    v = layer_norm(v, p['ln_v_g'], p['ln_v_b'])
    ya = u * chunk_spatial_gate(v, p['ws_chunk'], p['b_chunk'])
    yb, pool_buf = multiscale_pool(pb, pool_buf, pos0, p['w_pool'], p['pool_scale'])
    yc, conv_buf = short_conv(pc, conv_buf, p['conv_w'])
    yd, shift_buf, wkv_state = rwkv7_mix(pd, shift_buf, wkv_state, p)
    y = jnp.concatenate([ya, yb, yc, yd], axis=-1) @ p['w_out']
    return y, v, pool_buf, conv_buf, shift_buf, wkv_state


def cross_attention(h, mk, mv, wq, wo):
    bn, t, _ = h.shape
    q = (h @ wq).reshape(bn, t, X_HEADS, X_HEAD_DIM)
    s = jnp.einsum('bthd,bmhd->bhtm', q, mk).astype(jnp.float32) * (X_HEAD_DIM ** -0.5)
    pr = jax.nn.softmax(s, axis=-1).astype(h.dtype)
    o = jnp.einsum('bhtm,bmhd->bthd', pr, mv).reshape(bn, t, X_HEADS * X_HEAD_DIM)
    return o @ wo


def swiglu(h, w1, w3, w2):
    return (jax.nn.silu(h @ w1) * (h @ w3)) @ w2


def decoder_layer(h, pos0, mem_k, mem_v, pool_buf, conv_buf, shift_buf, wkv_state, p):
    y, v_rows, pool_buf, conv_buf, shift_buf, wkv_state = hybrid_mixer(h, pos0, pool_buf, conv_buf, shift_buf, wkv_state, p)
    h = layer_norm(ALPHA * h + y, p['ln1_g'], p['ln1_b'])
    h = layer_norm(ALPHA * h + cross_attention(h, mem_k, mem_v, p['w_xq'], p['w_xo']), p['ln2_g'], p['ln2_b'])
    h = layer_norm(ALPHA * h + swiglu(h, p['ffn_w1'], p['ffn_w3'], p['ffn_w2']), p['ln3_g'], p['ln3_b'])
    return h, v_rows, pool_buf, conv_buf, shift_buf, wkv_state


def setup_inputs(seed: int = 0) -> dict:
    key = jax.random.key(seed)
    ks = iter(jax.random.split(key, 64))
    f32 = jnp.float32
    L = DEPTH
    nrm = lambda shape, scale: jax.random.normal(next(ks), shape, f32) * scale
    gain = lambda shape: 1.0 + 0.05 * jax.random.normal(next(ks), shape, f32)
    return {
        'x_prompt': nrm((BATCH, SEQ, D_MODEL), 1.0),
        'x_sample': nrm((DEC_BATCH, DEC_SEQ, D_MODEL), 1.0),
        'mem_prompt': nrm((BATCH, MEM_LEN, D_MODEL), 1.0),
        'cache_mem_k': nrm((L, DEC_BATCH, MEM_LEN, X_HEADS, X_HEAD_DIM), 1.0),
        'cache_mem_v': nrm((L, DEC_BATCH, MEM_LEN, X_HEADS, X_HEAD_DIM), 1.0),
        'state_pool': nrm((L, DEC_BATCH, POOL_BUF, W_B), 1.0),
        'state_conv': nrm((L, DEC_BATCH, CONV_W - 1, W_C), 1.0),
        'state_shift': nrm((L, DEC_BATCH, 1, D_PROJ), 1.0),
        'state_wkv': nrm((L, DEC_BATCH, D_HEADS, D_HEAD_DIM, D_HEAD_DIM), 0.3),
        'w_in': nrm((L, D_MODEL, PROJ), D_MODEL ** -0.5),
        'mu_d': jax.random.uniform(next(ks), (L, D_PROJ), f32),
        'ln_v_g': gain((L, W_A)),
        'ln_v_b': nrm((L, W_A), 0.02),
        'ws_chunk': nrm((L, A_HEADS, CHUNK, CHUNK), CHUNK ** -0.5),
        'b_chunk': gain((L, A_HEADS, CHUNK)),
        'w_pool': nrm((L, len(POOL_WINDOWS), POOL_GROUP, POOL_GROUP), POOL_GROUP ** -0.5),
        'pool_scale': gain((L, W_B)),
        'conv_w': nrm((L, CONV_W, W_C), 0.5),
        'rwkv_w0': jax.random.uniform(next(ks), (L, W_D), f32, -6.0, 0.5),
        'rwkv_w2': nrm((L, R_DECAY, W_D), 0.1),
        'rwkv_a0': nrm((L, W_D), 0.1),
        'rwkv_a2': nrm((L, R_AAA, W_D), 0.1),
        'rwkv_g2': nrm((L, R_GATE, W_D), R_GATE ** -0.5),
        'rwkv_k_k': 0.85 + 0.05 * jax.random.normal(next(ks), (L, W_D), f32),
        'rwkv_k_a': gain((L, W_D)),
        'rwkv_r_k': nrm((L, D_HEADS, D_HEAD_DIM), 0.1),
        'rwkv_lnx_g': gain((L, W_D)),
        'rwkv_lnx_b': nrm((L, W_D), 0.02),
        'w_out': nrm((L, MIX_WIDTH, D_MODEL), MIX_WIDTH ** -0.5 * BETA),
        'ln1_g': gain((L, D_MODEL)),
        'ln1_b': nrm((L, D_MODEL), 0.02),
        'w_xq': nrm((L, D_MODEL, X_HEADS * X_HEAD_DIM), D_MODEL ** -0.5),
        'w_xk': nrm((L, D_MODEL, X_HEADS * X_HEAD_DIM), D_MODEL ** -0.5),
        'w_xv': nrm((L, D_MODEL, X_HEADS * X_HEAD_DIM), D_MODEL ** -0.5),
        'w_xo': nrm((L, X_HEADS * X_HEAD_DIM, D_MODEL), (X_HEADS * X_HEAD_DIM) ** -0.5 * BETA),
        'ln2_g': gain((L, D_MODEL)),
        'ln2_b': nrm((L, D_MODEL), 0.02),
        'ffn_w1': nrm((L, D_MODEL, D_FF), D_MODEL ** -0.5),
        'ffn_w3': nrm((L, D_MODEL, D_FF), D_MODEL ** -0.5),
        'ffn_w2': nrm((L, D_FF, D_MODEL), D_FF ** -0.5 * BETA),
        'ln3_g': gain((L, D_MODEL)),
        'ln3_b': nrm((L, D_MODEL), 0.02),
    }


def reference(x_prompt, x_sample, mem_prompt, cache_mem_k, cache_mem_v, state_pool, state_conv, state_shift, state_wkv,
              w_in, mu_d, ln_v_g, ln_v_b, ws_chunk, b_chunk, w_pool, pool_scale, conv_w,
              rwkv_w0, rwkv_w2, rwkv_a0, rwkv_a2, rwkv_g2, rwkv_k_k, rwkv_k_a, rwkv_r_k, rwkv_lnx_g, rwkv_lnx_b,
              w_out, ln1_g, ln1_b, w_xq, w_xk, w_xv, w_xo, ln2_g, ln2_b, ffn_w1, ffn_w3, ffn_w2, ln3_g, ln3_b):
    bp, t_p, _ = x_prompt.shape
    last_chunk_start = ((t_p - 1) // CHUNK) * CHUNK
    dt = x_prompt.dtype
    hp, hs = x_prompt, x_sample
    pv, ppool, pconv, pshift, pwkv, pmk, pmv = [], [], [], [], [], [], []
    sv, spool, sconv, sshift, swkv = [], [], [], [], []
    for l in range(DEPTH):
        p = dict(w_in=w_in[l], mu_d=mu_d[l], ln_v_g=ln_v_g[l], ln_v_b=ln_v_b[l], ws_chunk=ws_chunk[l],
                 b_chunk=b_chunk[l], w_pool=w_pool[l], pool_scale=pool_scale[l], conv_w=conv_w[l],
                 rwkv_w0=rwkv_w0[l], rwkv_w2=rwkv_w2[l], rwkv_a0=rwkv_a0[l], rwkv_a2=rwkv_a2[l],
                 rwkv_g2=rwkv_g2[l], rwkv_k_k=rwkv_k_k[l], rwkv_k_a=rwkv_k_a[l], rwkv_r_k=rwkv_r_k[l],
                 rwkv_lnx_g=rwkv_lnx_g[l], rwkv_lnx_b=rwkv_lnx_b[l], w_out=w_out[l],
                 ln1_g=ln1_g[l], ln1_b=ln1_b[l], w_xq=w_xq[l], w_xo=w_xo[l], ln2_g=ln2_g[l], ln2_b=ln2_b[l],
                 ffn_w1=ffn_w1[l], ffn_w3=ffn_w3[l], ffn_w2=ffn_w2[l], ln3_g=ln3_g[l], ln3_b=ln3_b[l])
        mk_p = (mem_prompt @ w_xk[l]).reshape(bp, MEM_LEN, X_HEADS, X_HEAD_DIM)
        mv_p = (mem_prompt @ w_xv[l]).reshape(bp, MEM_LEN, X_HEADS, X_HEAD_DIM)
        hp, v_rows, pb, cb, sb, wk = decoder_layer(
            hp, 0, mk_p, mv_p,
            jnp.zeros((bp, POOL_BUF, W_B), dt), jnp.zeros((bp, CONV_W - 1, W_C), dt),
            jnp.zeros((bp, 1, D_PROJ), dt), jnp.zeros((bp, D_HEADS, D_HEAD_DIM, D_HEAD_DIM), jnp.float32), p)
        pv.append(v_rows[:, last_chunk_start:])
        ppool.append(pb)
        pconv.append(cb)
        pshift.append(sb)
        pwkv.append(wk)
        pmk.append(mk_p)
        pmv.append(mv_p)
        hs, v_rows_s, pb_s, cb_s, sb_s, wk_s = decoder_layer(
            hs, PAST_LEN, cache_mem_k[l], cache_mem_v[l],
            state_pool[l], state_conv[l], state_shift[l], state_wkv[l], p)
        sv.append(v_rows_s)
        spool.append(pb_s)
        sconv.append(cb_s)
        sshift.append(sb_s)
        swkv.append(wk_s)
    return (hp, hs,
            jnp.stack(pv), jnp.stack(ppool), jnp.stack(pconv), jnp.stack(pshift), jnp.stack(pwkv),
            jnp.stack(pmk), jnp.stack(pmv),
            jnp.stack(sv), jnp.stack(spool), jnp.stack(sconv), jnp.stack(sshift), jnp.stack(swkv))
```

```python
import functools
import math

import jax
import jax.numpy as jnp
from jax import lax
from jax.experimental import pallas as pl
from jax.experimental.pallas import tpu as pltpu

F32 = jnp.float32
BF16 = jnp.bfloat16

D_MODEL = 1024
W_A = 256
W_B = 256
W_C = 256
W_D = 256
A_HEADS = 4
A_HEAD_DIM = W_A // A_HEADS
CHUNK = 128
POOL_WINDOWS = (2, 4, 8, 16)
POOL_GROUP = W_B // len(POOL_WINDOWS)
POOL_BUF = max(POOL_WINDOWS) - 1
CONV_W = 3
D_HEAD_DIM = 64
D_HEADS = W_D // D_HEAD_DIM
R_DECAY = 32
R_AAA = 32
R_GATE = 64
D_PROJ = 3 * W_D + R_DECAY + R_AAA + R_GATE
N_ABC = 2 * W_A + W_B + 3 * W_C
PROJ = N_ABC + D_PROJ
MEM_LEN = 256
X_HEADS = 4
X_HEAD_DIM = D_MODEL // X_HEADS
D_FF = int(math.ceil(8 * D_MODEL / 3 / 256)) * 256
PAST_LEN = 16384
LN_EPS = 1e-5
GN_EPS = 64e-5

WKV_CHUNK = 64
POOL_CARRY = 24
ROW_CARRY = 8
V7X_VMEM_BYTES = 64 * 1024 * 1024
VMEM_CAP = V7X_VMEM_BYTES - 8 * 1024 * 1024

NN = (((1,), (0,)), ((), ()))
NT = (((1,), (1,)), ((), ()))
TN = (((0,), (0,)), ((), ()))


def _vmem_limit(nbytes):
    return int(min(VMEM_CAP, max(32 * 1024 * 1024, 2 * nbytes)))


def _params(sem, nbytes):
    return pltpu.CompilerParams(dimension_semantics=sem, vmem_limit_bytes=_vmem_limit(nbytes))


def _dot(a, b, dims=NN):
    return lax.dot_general(a, b, dims, preferred_element_type=F32)


def _split2(a):
    hi = a.astype(BF16)
    lo = (a - hi.astype(F32)).astype(BF16)
    return hi, lo


def _dot_hi(a, b, dims=NN):
    ah, al = _split2(a)
    bh, bl = _split2(b)
    return _dot(ah, bh, dims) + _dot(ah, bl, dims) + _dot(al, bh, dims)


def _dot_ones(x, ones_bf16, dims=NN):
    hi = x.astype(BF16)
    r1 = x - hi.astype(F32)
    mid = r1.astype(BF16)
    lo = (r1 - mid.astype(F32)).astype(BF16)
    return _dot(hi, ones_bf16, dims) + _dot(mid, ones_bf16, dims) + _dot(lo, ones_bf16, dims)


def _ones_dot(ones_bf16, x):
    hi = x.astype(BF16)
    r1 = x - hi.astype(F32)
    mid = r1.astype(BF16)
    lo = (r1 - mid.astype(F32)).astype(BF16)
    return _dot(ones_bf16, hi) + _dot(ones_bf16, mid) + _dot(ones_bf16, lo)


def _ln(x, g, b, eps=LN_EPS):
    mu = jnp.mean(x, axis=-1, keepdims=True)
    xc = x - mu
    var = jnp.mean(xc * xc, axis=-1, keepdims=True)
    return xc * lax.rsqrt(var + eps) * g + b


def _gelu(x):
    c = math.sqrt(2.0 / math.pi)
    return x * (0.5 * (1.0 + jnp.tanh(c * (x + 0.044715 * (x * x * x)))))


def _sigmoid(x):
    return 1.0 / (1.0 + jnp.exp(-x))


def _softplus(x):
    return jnp.maximum(x, 0.0) + jnp.log(1.0 + jnp.exp(-jnp.abs(x)))


def _head_ones(n, group):
    r = lax.broadcasted_iota(jnp.int32, (n, n), 0) // group
    c = lax.broadcasted_iota(jnp.int32, (n, n), 1) // group
    return jnp.where(r == c, 1.0, 0.0).astype(BF16)


def _proj_kernel(x_ref, w_ref, oabc_ref, od_ref):
    y = _dot(x_ref[...].astype(BF16), w_ref[...])
    oabc_ref[...] = y[:, :N_ABC]
    od_ref[...] = y[:, N_ABC:]


def _proj(x, w, tm):
    m, k = x.shape
    nbytes = 2 * (tm * k * 4 + k * PROJ * 2 + tm * PROJ * 4) + tm * PROJ * 4
    return pl.pallas_call(
        _proj_kernel,
        out_shape=(jax.ShapeDtypeStruct((m, N_ABC), F32), jax.ShapeDtypeStruct((m, D_PROJ), F32)),
        grid=(m // tm,),
        in_specs=[pl.BlockSpec((tm, k), lambda i: (i, 0)), pl.BlockSpec((k, PROJ), lambda i: (0, 0))],
        out_specs=(pl.BlockSpec((tm, N_ABC), lambda i: (i, 0)), pl.BlockSpec((tm, D_PROJ), lambda i: (i, 0))),
        compiler_params=_params(("parallel",), nbytes),
        name="proj",
    )(x, w)


def _mm_kernel(x_ref, w_ref, o_ref):
    o_ref[...] = _dot(x_ref[...].astype(BF16), w_ref[...])


def _mm(x, w, tm, name):
    m, k = x.shape
    n = w.shape[1]
    nbytes = 2 * (tm * k * 4 + k * n * 2 + tm * n * 4) + tm * n * 4
    return pl.pallas_call(
        _mm_kernel,
        out_shape=jax.ShapeDtypeStruct((m, n), F32),
        grid=(m // tm,),
        in_specs=[pl.BlockSpec((tm, k), lambda i: (i, 0)), pl.BlockSpec((k, n), lambda i: (0, 0))],
        out_specs=pl.BlockSpec((tm, n), lambda i: (i, 0)),
        compiler_params=_params(("parallel",), nbytes),
        name=name,
    )(x, w)


def _mm_res_ln_kernel(*refs, n_in, alpha):
    xs = refs[:n_in]
    ws = refs[n_in:2 * n_in]
    h_ref, g_ref, b_ref, o_ref = refs[2 * n_in:]
    y = _dot(xs[0][...].astype(BF16), ws[0][...])
    for x_ref, w_ref in zip(xs[1:], ws[1:]):
        y = y + _dot(x_ref[...].astype(BF16), w_ref[...])
    o_ref[...] = _ln(alpha * h_ref[...] + y, g_ref[...], b_ref[...])


def _mm_res_ln(xs, ws, h, g, b, tm, alpha, name):
    m, n = h.shape
    nbytes = 2 * sum(tm * x.shape[1] * 4 + w.shape[0] * n * 2 for x, w in zip(xs, ws)) + 5 * tm * n * 4
    in_specs = [pl.BlockSpec((tm, x.shape[1]), lambda i: (i, 0)) for x in xs]
    in_specs += [pl.BlockSpec(w.shape, lambda i: (0, 0)) for w in ws]
    in_specs += [pl.BlockSpec((tm, n), lambda i: (i, 0)),
                 pl.BlockSpec((1, n), lambda i: (0, 0)), pl.BlockSpec((1, n), lambda i: (0, 0))]
    return pl.pallas_call(
        functools.partial(_mm_res_ln_kernel, n_in=len(xs), alpha=alpha),
        out_shape=jax.ShapeDtypeStruct((m, n), F32),
        grid=(m // tm,),
        in_specs=in_specs,
        out_specs=pl.BlockSpec((tm, n), lambda i: (i, 0)),
        compiler_params=_params(("parallel",), nbytes),
        name=name,
    )(*xs, *ws, h, g, b)


def _ffn_kernel(x_ref, w1_ref, w3_ref, w2_ref, g_ref, b_ref, o_ref, xb_ref, acc_ref, *, alpha):
    j = pl.program_id(1)

    @pl.when(j == 0)
    def _():
        xb_ref[...] = x_ref[...].astype(BF16)
        acc_ref[...] = jnp.zeros_like(acc_ref)

    xb = xb_ref[...]
    h1 = _dot(xb, w1_ref[...])
    h3 = _dot(xb, w3_ref[...])
    a = (h1 * _sigmoid(h1) * h3).astype(BF16)
    acc_ref[...] += _dot(a, w2_ref[...])

    @pl.when(j == pl.num_programs(1) - 1)
    def _():
        o_ref[...] = _ln(alpha * x_ref[...] + acc_ref[...], g_ref[...], b_ref[...])


def _ffn(x, w1, w3, w2, g, b, tm, tf, alpha, name):
    m, d = x.shape
    nbytes = 4 * tm * d * 4 + tm * d * 2 + tm * d * 4 + 2 * 3 * d * tf * 2 + 3 * tm * tf * 4
    return pl.pallas_call(
        functools.partial(_ffn_kernel, alpha=alpha),
        out_shape=jax.ShapeDtypeStruct((m, d), F32),
        grid=(m // tm, D_FF // tf),
        in_specs=[pl.BlockSpec((tm, d), lambda i, j: (i, 0)),
                  pl.BlockSpec((d, tf), lambda i, j: (0, j)),
                  pl.BlockSpec((d, tf), lambda i, j: (0, j)),
                  pl.BlockSpec((tf, d), lambda i, j: (j, 0)),
                  pl.BlockSpec((1, d), lambda i, j: (0, 0)),
                  pl.BlockSpec((1, d), lambda i, j: (0, 0))],
        out_specs=pl.BlockSpec((tm, d), lambda i, j: (i, 0)),
        scratch_shapes=[pltpu.VMEM((tm, d), BF16), pltpu.VMEM((tm, d), F32)],
        compiler_params=_params(("parallel", "arbitrary"), nbytes),
        name=name,
    )(x, w1, w3, w2, g, b)


def _softmax_rows(s):
    m = jnp.max(s, axis=-1, keepdims=True)
    e = jnp.exp(s - m)
    return e / jnp.sum(e, axis=-1, keepdims=True)


def _attn_prompt_kernel(h_ref, mk_ref, mv_ref, wq_ref, wo_ref, g_ref, b_ref, o_ref, ob_ref, *, alpha):
    h = h_ref[0]
    q = _dot(h.astype(BF16), wq_ref[...])
    scale = X_HEAD_DIM ** -0.5
    for hd in range(X_HEADS):
        sl = slice(hd * X_HEAD_DIM, (hd + 1) * X_HEAD_DIM)
        kh = mk_ref[0, :, sl].astype(BF16)
        vh = mv_ref[0, :, sl].astype(BF16)
        s = _dot(q[:, sl].astype(BF16), kh, NT) * scale
        p = _softmax_rows(s)
        ob_ref[:, sl] = _dot(p.astype(BF16), vh).astype(BF16)
    y = _dot(ob_ref[...], wo_ref[...])
    o_ref[0] = _ln(alpha * h + y, g_ref[...], b_ref[...])


def _attn_prompt(h, mk, mv, wq, wo, g, b, tq, alpha):
    bn, t, d = h.shape
    nbytes = (4 * tq * d * 4 + 4 * MEM_LEN * d * 4 + 4 * d * d * 2 + tq * d * 2
              + 2 * tq * d * 4 + 3 * tq * MEM_LEN * 4)
    return pl.pallas_call(
        functools.partial(_attn_prompt_kernel, alpha=alpha),
        out_shape=jax.ShapeDtypeStruct((bn, t, d), F32),
        grid=(bn, t // tq),
        in_specs=[pl.BlockSpec((1, tq, d), lambda i, j: (i, j, 0)),
                  pl.BlockSpec((1, MEM_LEN, d), lambda i, j: (i, 0, 0)),
                  pl.BlockSpec((1, MEM_LEN, d), lambda i, j: (i, 0, 0)),
                  pl.BlockSpec((d, d), lambda i, j: (0, 0)),
                  pl.BlockSpec((d, d), lambda i, j: (0, 0)),
                  pl.BlockSpec((1, d), lambda i, j: (0, 0)),
                  pl.BlockSpec((1, d), lambda i, j: (0, 0))],
        out_specs=pl.BlockSpec((1, tq, d), lambda i, j: (i, j, 0)),
        scratch_shapes=[pltpu.VMEM((tq, d), BF16)],
        compiler_params=_params(("parallel", "parallel"), nbytes),
        name="attn_prompt",
    )(h, mk, mv, wq, wo, g, b)


def _attn_sample_kernel(q_ref, k_ref, v_ref, o_ref, *, bb):
    scale = X_HEAD_DIM ** -0.5
    for i in range(bb):
        q = jnp.broadcast_to(q_ref[i], (8, D_MODEL)).astype(BF16)
        for hd in range(X_HEADS):
            sl = slice(hd * X_HEAD_DIM, (hd + 1) * X_HEAD_DIM)
            kh = k_ref[i, :, sl].astype(BF16)
            vh = v_ref[i, :, sl].astype(BF16)
            s = _dot(q[:, sl], kh, NT) * scale
            p = _softmax_rows(s)
            o = _dot(p.astype(BF16), vh)
            o_ref[i, :, sl] = o[0:1]


def _attn_sample(q, mk, mv, bb):
    n = q.shape[0]
    nbytes = 4 * bb * MEM_LEN * D_MODEL * 4 + 4 * bb * D_MODEL * 4 + 2 * MEM_LEN * X_HEAD_DIM * 4
    return pl.pallas_call(
        functools.partial(_attn_sample_kernel, bb=bb),
        out_shape=jax.ShapeDtypeStruct((n, 1, D_MODEL), F32),
        grid=(n // bb,),
        in_specs=[pl.BlockSpec((bb, 1, D_MODEL), lambda i: (i, 0, 0)),
                  pl.BlockSpec((bb, MEM_LEN, D_MODEL), lambda i: (i, 0, 0)),
                  pl.BlockSpec((bb, MEM_LEN, D_MODEL), lambda i: (i, 0, 0))],
        out_specs=pl.BlockSpec((bb, 1, D_MODEL), lambda i: (i, 0, 0)),
        compiler_params=_params(("parallel",), nbytes),
        name="attn_sample",
    )(q, mk, mv)


def _pool_window_sums(ext, tt):
    s2 = ext + pltpu.roll(ext, 1, 0)
    s4 = s2 + pltpu.roll(s2, 2, 0)
    s8 = s4 + pltpu.roll(s4, 4, 0)
    s16 = s8 + pltpu.roll(s8, 8, 0)
    grp = lax.broadcasted_iota(jnp.int32, (tt, W_B), 1) // POOL_GROUP
    lo = ext.shape[0] - tt
    return jnp.where(grp == 0, s2[lo:], jnp.where(grp == 1, s4[lo:], jnp.where(grp == 2, s8[lo:], s16[lo:])))


def _abc_prompt_kernel(x_ref, lng_ref, lnb_ref, ws_ref, bias_ref, wpool_ref, pscale_ref, cw_ref,
                       mix_ref, vlast_ref, pool_ref, conv_ref, pcar_ref, ccar_ref, *, tt):
    t = pl.program_id(1)
    nt = pl.num_programs(1)

    @pl.when(t == 0)
    def _():
        pcar_ref[...] = jnp.zeros_like(pcar_ref)
        ccar_ref[...] = jnp.zeros_like(ccar_ref)

    x = x_ref[0]

    ga = _gelu(x[:, :2 * W_A])
    u = ga[:, :W_A]
    v = _ln(ga[:, W_A:], lng_ref[...], lnb_ref[...])
    rows = lax.broadcasted_iota(jnp.int32, (A_HEADS * CHUNK, CHUNK), 0) % CHUNK
    cols = lax.broadcasted_iota(jnp.int32, (A_HEADS * CHUNK, CHUNK), 1)
    wsm = jnp.where(rows >= cols, ws_ref[...], 0.0).astype(BF16)
    hid = lax.broadcasted_iota(jnp.int32, (CHUNK, W_A), 1) // A_HEAD_DIM
    for c in range(tt // CHUNK):
        rs = slice(c * CHUNK, (c + 1) * CHUNK)
        zz = _dot(wsm, v[rs].astype(BF16))
        z = zz[(A_HEADS - 1) * CHUNK:]
        for hd in range(A_HEADS - 2, -1, -1):
            z = jnp.where(hid == hd, zz[hd * CHUNK:(hd + 1) * CHUNK], z)
        mix_ref[0, rs, 0:W_A] = u[rs] * (z + bias_ref[...])

    @pl.when(t == nt - 1)
    def _():
        vlast_ref[0] = v[tt - CHUNK:]

    xb = x[:, 2 * W_A:2 * W_A + W_B]
    ext = jnp.concatenate([pcar_ref[...], xb], axis=0)
    sums = _pool_window_sums(ext, tt)
    pos = t * tt + lax.broadcasted_iota(jnp.int32, (tt, W_B), 0)
    win = jnp.left_shift(2, lax.broadcasted_iota(jnp.int32, (tt, W_B), 1) // POOL_GROUP)
    cnt = jnp.minimum(win, pos + 1).astype(F32)
    d = sums / cnt - xb
    mix_ref[0, :, W_A:W_A + W_B] = _dot(d.astype(BF16), wpool_ref[...]) * pscale_ref[...]
    pcar_ref[...] = ext[tt:]

    o = 2 * W_A + W_B
    bg = x[:, o:o + W_C]
    zc = x[:, o + W_C:o + 2 * W_C] * x[:, o + 2 * W_C:o + 3 * W_C]
    extz = jnp.concatenate([ccar_ref[...], zc], axis=0)
    y = (cw_ref[0:1, :] * pltpu.roll(extz, 2, 0) + cw_ref[1:2, :] * pltpu.roll(extz, 1, 0)
         + cw_ref[2:3, :] * extz)
    mix_ref[0, :, W_A + W_B:] = bg * y[ROW_CARRY:]
    ccar_ref[...] = extz[tt:]

    @pl.when(t == nt - 1)
    def _():
        pool_ref[0] = pcar_ref[POOL_CARRY - POOL_BUF:, :]
        conv_ref[0] = ccar_ref[ROW_CARRY - (CONV_W - 1):, :]


def _abc_prompt(pabc, lng, lnb, ws, bias, wpool, pscale, cw, tt):
    bn, t, _ = pabc.shape
    nw = W_A + W_B + W_C
    nbytes = 2 * tt * N_ABC * 4 + 2 * tt * nw * 4 + 12 * tt * W_A * 4 + 4 * A_HEADS * CHUNK * CHUNK * 4
    const = lambda i, j: (0, 0)
    return pl.pallas_call(
        functools.partial(_abc_prompt_kernel, tt=tt),
        out_shape=(jax.ShapeDtypeStruct((bn, t, nw), F32),
                   jax.ShapeDtypeStruct((bn, CHUNK, W_A), F32),
                   jax.ShapeDtypeStruct((bn, POOL_BUF, W_B), F32),
                   jax.ShapeDtypeStruct((bn, CONV_W - 1, W_C), F32)),
        grid=(bn, t // tt),
        in_specs=[pl.BlockSpec((1, tt, N_ABC), lambda i, j: (i, j, 0)),
                  pl.BlockSpec((1, W_A), const), pl.BlockSpec((1, W_A), const),
                  pl.BlockSpec((A_HEADS * CHUNK, CHUNK), const),
                  pl.BlockSpec((CHUNK, W_A), const),
                  pl.BlockSpec((W_B, W_B), const), pl.BlockSpec((1, W_B), const),
                  pl.BlockSpec((CONV_W, W_C), const)],
        out_specs=(pl.BlockSpec((1, tt, nw), lambda i, j: (i, j, 0)),
                   pl.BlockSpec((1, CHUNK, W_A), lambda i, j: (i, 0, 0)),
                   pl.BlockSpec((1, POOL_BUF, W_B), lambda i, j: (i, 0, 0)),
                   pl.BlockSpec((1, CONV_W - 1, W_C), lambda i, j: (i, 0, 0))),
        scratch_shapes=[pltpu.VMEM((POOL_CARRY, W_B), F32), pltpu.VMEM((ROW_CARRY, W_C), F32)],
        compiler_params=_params(("parallel", "arbitrary"), nbytes),
        name="mixer_abc_prompt",
    )(pabc, lng, lnb, ws, bias, wpool, pscale, cw)


def _abc_sample_kernel(x_ref, pool_ref, conv_ref, lng_ref, lnb_ref, ws0_ref, bias0_ref, wpool_ref,
                       pscale_ref, cw_ref, mix_ref, v_ref, poolo_ref, convo_ref, *, pos0):
    x = x_ref[...]
    n = x.shape[0]
    ga = _gelu(x[:, :2 * W_A])
    u = ga[:, :W_A]
    v = _ln(ga[:, W_A:], lng_ref[...], lnb_ref[...])
    v_ref[...] = v
    mix_ref[:, 0:W_A] = u * (v * ws0_ref[...] + bias0_ref[...])

    xb = x[:, 2 * W_A:2 * W_A + W_B]
    grp = lax.broadcasted_iota(jnp.int32, (n, W_B), 1) // POOL_GROUP
    run = xb
    sums = jnp.zeros_like(xb)
    for back in range(1, max(POOL_WINDOWS)):
        r = POOL_BUF - back
        run = run + pool_ref[:, r * W_B:(r + 1) * W_B]
        for gi, win in enumerate(POOL_WINDOWS):
            if back == win - 1:
                sums = jnp.where(grp == gi, run, sums)
    win = jnp.left_shift(2, grp)
    cnt = jnp.minimum(win, pos0 + 1).astype(F32)
    d = sums / cnt - xb
    mix_ref[:, W_A:W_A + W_B] = _dot(d.astype(BF16), wpool_ref[...]) * pscale_ref[...]
    poolo_ref[:, :(POOL_BUF - 1) * W_B] = pool_ref[:, W_B:]
    poolo_ref[:, (POOL_BUF - 1) * W_B:] = xb

    o = 2 * W_A + W_B
    bg = x[:, o:o + W_C]
    zc = x[:, o + W_C:o + 2 * W_C] * x[:, o + 2 * W_C:o + 3 * W_C]
    z0 = conv_ref[:, :W_C]
    z1 = conv_ref[:, W_C:]
    y = cw_ref[0:1, :] * z0 + cw_ref[1:2, :] * z1 + cw_ref[2:3, :] * zc
    mix_ref[:, W_A + W_B:] = bg * y
    convo_ref[:, :W_C] = z1
    convo_ref[:, W_C:] = zc


def _abc_sample(pabc, pool, conv, lng, lnb, ws0, bias0, wpool, pscale, cw, pos0):
    n = pabc.shape[0]
    nw = W_A + W_B + W_C
    return pl.pallas_call(
        functools.partial(_abc_sample_kernel, pos0=pos0),
        out_shape=(jax.ShapeDtypeStruct((n, nw), F32),
                   jax.ShapeDtypeStruct((n, W_A), F32),
                   jax.ShapeDtypeStruct(pool.shape, F32),
                   jax.ShapeDtypeStruct(conv.shape, F32)),
        name="mixer_abc_sample",
    )(pabc, pool, conv, lng, lnb, ws0, bias0, wpool, pscale, cw)


def _rwkv_inputs(xs, w0, w2, a0, a2, g2, kk_w, ka_w, seg):
    r = xs[:, 0:W_D]
    k = xs[:, W_D:2 * W_D]
    v = xs[:, 2 * W_D:3 * W_D]
    o = 3 * W_D
    dw = xs[:, o:o + R_DECAY]
    da = xs[:, o + R_DECAY:o + R_DECAY + R_AAA]
    dg = xs[:, o + R_DECAY + R_AAA:]
    w_log = -_softplus(-(w0 + _dot_hi(jnp.tanh(dw), w2))) - 0.5
    logdecay = -jnp.exp(w_log)
    a = _sigmoid(a0 + _dot_hi(da, a2))
    g = _dot_hi(_sigmoid(dg), g2)
    kk = k * kk_w
    kk = kk * lax.rsqrt(jnp.maximum(_dot_ones(kk * kk, seg), 1e-12))
    k = k * (1.0 + (a - 1.0) * ka_w)
    return r, k, v, kk, a, logdecay, g


def _rwkv_finish(o, r, k, v, g, rk_w, lnx_g, lnx_b, seg):
    inv = 1.0 / D_HEAD_DIM
    mu = _dot_ones(o, seg) * inv
    oc = o - mu
    var = _dot_ones(oc * oc, seg) * inv
    on = oc * lax.rsqrt(var + GN_EPS) * lnx_g + lnx_b
    bonus = _dot_ones(r * k * rk_w, seg) * v
    return (on + bonus) * g


def _wkv_chunk(s0, qt, rt, bt, kt, bbar, kbar, vm, gam):
    c = WKV_CHUNK
    aa = _dot_hi(jnp.concatenate([qt, rt], axis=0), jnp.concatenate([bt, kt], axis=0), NT)
    ri = lax.broadcasted_iota(jnp.int32, (c, c), 0)
    ci = lax.broadcasted_iota(jnp.int32, (c, c), 1)
    strict = ri > ci
    lb = jnp.where(strict, aa[:c, :c], 0.0)
    lk = jnp.where(strict, aa[:c, c:], 0.0)
    ri2 = lax.broadcasted_iota(jnp.int32, (c, 2 * c), 0)
    ci2 = lax.broadcasted_iota(jnp.int32, (c, 2 * c), 1) % c
    abk = jnp.where(ri2 >= ci2, aa[c:, :], 0.0)
    npow = -lb
    tinv = jnp.where(ri == ci, 1.0, 0.0) + npow
    for _ in range(int(math.log2(c)) - 1):
        npow = _dot_hi(npow, npow)
        tinv = tinv + _dot_hi(tinv, npow)
    qh = _dot_hi(tinv, qt)
    z = _dot_hi(tinv, _dot_hi(lk, vm))
    u = -(_dot_hi(qh, s0, NT) + z)
    s1 = s0 * gam + _dot_hi(jnp.concatenate([vm, u], axis=0), jnp.concatenate([kbar, bbar], axis=0), TN)
    o = _dot_hi(rt, s0, NT) + _dot_hi(abk, jnp.concatenate([u, vm], axis=0))
    return o, s1


def _rwkv_prompt_kernel(pd_ref, mu_ref, w0_ref, w2_ref, a0_ref, a2_ref, g2_ref, kk_ref, ka_ref, rk_ref,
                        lg_ref, lb_ref, yd_ref, shift_ref, wkv_ref, car_ref, s_ref, o_ref, *, tt):
    t = pl.program_id(1)
    nt = pl.num_programs(1)

    @pl.when(t == 0)
    def _():
        car_ref[...] = jnp.zeros_like(car_ref)
        s_ref[...] = jnp.zeros_like(s_ref)

    pd = pd_ref[0]
    ext = jnp.concatenate([car_ref[...], pd], axis=0)
    prev = pltpu.roll(ext, 1, 0)[ROW_CARRY:]
    car_ref[...] = ext[tt:]
    xs = pd + (prev - pd) * mu_ref[...]
    seg = _head_ones(W_D, D_HEAD_DIM)
    r, k, v, kk, a, ld, g = _rwkv_inputs(xs, w0_ref[...], w2_ref[...], a0_ref[...], a2_ref[...],
                                         g2_ref[...], kk_ref[...], ka_ref[...], seg)
    c = WKV_CHUNK
    tri = jnp.where(lax.broadcasted_iota(jnp.int32, (c, c), 0) >= lax.broadcasted_iota(jnp.int32, (c, c), 1),
                    1.0, 0.0).astype(BF16)
    for ch in range(tt // c):
        rs = slice(ch * c, (ch + 1) * c)
        ldc = ld[rs]
        cs = _ones_dot(tri, ldc)
        cs_last = cs[c - 1:c]
        e_pos = jnp.exp(cs)
        e_neg = jnp.exp(-cs)
        e_prev = jnp.exp(cs - ldc)
        e_tail = jnp.exp(cs_last - cs)
        b = kk[rs] * a[rs]
        qt = kk[rs] * e_prev
        rt = r[rs] * e_pos
        bt = b * e_neg
        kt = k[rs] * e_neg
        bbar = b * e_tail
        kbar = k[rs] * e_tail
        gam = jnp.exp(cs_last)
        vc = v[rs]
        for hd in range(D_HEADS):
            hs = slice(hd * D_HEAD_DIM, (hd + 1) * D_HEAD_DIM)
            o, s1 = _wkv_chunk(s_ref[hd], qt[:, hs], rt[:, hs], bt[:, hs], kt[:, hs], bbar[:, hs],
                               kbar[:, hs], vc[:, hs], gam[:, hs])
            s_ref[hd] = s1
            o_ref[rs, hs] = o
    yd_ref[0] = _rwkv_finish(o_ref[...], r, k, v, g, rk_ref[...], lg_ref[...], lb_ref[...], seg)

    @pl.when(t == nt - 1)
    def _():
        shift_ref[0] = pd[tt - 1:tt]
        wkv_ref[0] = s_ref[...]


def _rwkv_prompt(pd, mu, w0, w2, a0, a2, g2, kk_w, ka_w, rk_w, lnx_g, lnx_b, tt):
    bn, t, _ = pd.shape
    nbytes = 2 * tt * D_PROJ * 4 + 40 * tt * W_D * 4
    const = lambda i, j: (0, 0)
    row = lambda n: pl.BlockSpec((1, n), const)
    return pl.pallas_call(
        functools.partial(_rwkv_prompt_kernel, tt=tt),
        out_shape=(jax.ShapeDtypeStruct((bn, t, W_D), F32),
                   jax.ShapeDtypeStruct((bn, 1, D_PROJ), F32),
                   jax.ShapeDtypeStruct((bn, D_HEADS, D_HEAD_DIM, D_HEAD_DIM), F32)),
        grid=(bn, t // tt),
        in_specs=[pl.BlockSpec((1, tt, D_PROJ), lambda i, j: (i, j, 0)),
                  row(D_PROJ), row(W_D), pl.BlockSpec((R_DECAY, W_D), const),
                  row(W_D), pl.BlockSpec((R_AAA, W_D), const), pl.BlockSpec((R_GATE, W_D), const),
                  row(W_D), row(W_D), row(W_D), row(W_D), row(W_D)],
        out_specs=(pl.BlockSpec((1, tt, W_D), lambda i, j: (i, j, 0)),
                   pl.BlockSpec((1, 1, D_PROJ), lambda i, j: (i, 0, 0)),
                   pl.BlockSpec((1, D_HEADS, D_HEAD_DIM, D_HEAD_DIM), lambda i, j: (i, 0, 0, 0))),
        scratch_shapes=[pltpu.VMEM((ROW_CARRY, D_PROJ), F32),
                        pltpu.VMEM((D_HEADS, D_HEAD_DIM, D_HEAD_DIM), F32),
                        pltpu.VMEM((tt, W_D), F32)],
        compiler_params=_params(("parallel", "arbitrary"), nbytes),
        name="rwkv_prompt",
    )(pd, mu, w0, w2, a0, a2, g2, kk_w, ka_w, rk_w, lnx_g, lnx_b)


def _rwkv_sample_kernel(pd_ref, sh_ref, s_ref, mu_ref, w0_ref, w2_ref, a0_ref, a2_ref, g2_ref, kk_ref,
                        ka_ref, rk_ref, lg_ref, lb_ref, yd_ref, so_ref, o_ref, *, bb):
    pd = pd_ref[...]
    xs = pd + (sh_ref[...] - pd) * mu_ref[...]
    seg = _head_ones(W_D, D_HEAD_DIM)
    r, k, v, kk, a, ld, g = _rwkv_inputs(xs, w0_ref[...], w2_ref[...], a0_ref[...], a2_ref[...],
                                         g2_ref[...], kk_ref[...], ka_ref[...], seg)
    w = jnp.exp(ld)
    b = kk * a
    n = D_HEAD_DIM
    eye = jnp.where(lax.broadcasted_iota(jnp.int32, (n, n), 0) == lax.broadcasted_iota(jnp.int32, (n, n), 1),
                    1.0, 0.0)
    for i in range(bb):
        for hd in range(D_HEADS):
            hs = slice(hd * n, (hd + 1) * n)
            row = lambda z: z[i:i + 1, hs]
            s = s_ref[i, hd]
            sa = -jnp.sum(s * row(kk), axis=1, keepdims=True)
            vcol = jnp.sum(eye * row(v), axis=1, keepdims=True)
            s = s * row(w) + sa * row(b) + vcol * row(k)
            so_ref[i, hd] = s
            ocol = jnp.sum(s * row(r), axis=1, keepdims=True)
            o_ref[i:i + 1, hs] = jnp.sum(eye * ocol, axis=0, keepdims=True)
    yd_ref[...] = _rwkv_finish(o_ref[...], r, k, v, g, rk_ref[...], lg_ref[...], lb_ref[...], seg)


def _rwkv_sample(pd, shift, state, mu, w0, w2, a0, a2, g2, kk_w, ka_w, rk_w, lnx_g, lnx_b, bb):
    n = pd.shape[0]
    const = lambda i: (0, 0)
    row = lambda m: pl.BlockSpec((1, m), const)
    sblock = pl.BlockSpec((bb, D_HEADS, D_HEAD_DIM, D_HEAD_DIM), lambda i: (i, 0, 0, 0))
    return pl.pallas_call(
        functools.partial(_rwkv_sample_kernel, bb=bb),
        out_shape=(jax.ShapeDtypeStruct((n, W_D), F32), jax.ShapeDtypeStruct(state.shape, F32)),
        grid=(n // bb,),
        in_specs=[pl.BlockSpec((bb, D_PROJ), lambda i: (i, 0)), pl.BlockSpec((bb, D_PROJ), lambda i: (i, 0)),
                  sblock, row(D_PROJ), row(W_D), pl.BlockSpec((R_DECAY, W_D), const),
                  row(W_D), pl.BlockSpec((R_AAA, W_D), const), pl.BlockSpec((R_GATE, W_D), const),
                  row(W_D), row(W_D), row(W_D), row(W_D), row(W_D)],
        out_specs=(pl.BlockSpec((bb, W_D), lambda i: (i, 0)), sblock),
        scratch_shapes=[pltpu.VMEM((bb, W_D), F32)],
        compiler_params=pltpu.CompilerParams(dimension_semantics=("parallel",)),
        name="rwkv_sample",
    )(pd, shift, state, mu, w0, w2, a0, a2, g2, kk_w, ka_w, rk_w, lnx_g, lnx_b)


def _block_diag(w):
    gn, n, _ = w.shape
    eye = jnp.eye(gn, dtype=w.dtype)
    return (eye[:, None, :, None] * w[:, :, None, :]).reshape(gn * n, gn * n)


def kernel(x_prompt, x_sample, mem_prompt, cache_mem_k, cache_mem_v, state_pool, state_conv, state_shift, state_wkv,
           w_in, mu_d, ln_v_g, ln_v_b, ws_chunk, b_chunk, w_pool, pool_scale, conv_w,
           rwkv_w0, rwkv_w2, rwkv_a0, rwkv_a2, rwkv_g2, rwkv_k_k, rwkv_k_a, rwkv_r_k, rwkv_lnx_g, rwkv_lnx_b,
           w_out, ln1_g, ln1_b, w_xq, w_xk, w_xv, w_xo, ln2_g, ln2_b, ffn_w1, ffn_w3, ffn_w2, ln3_g, ln3_b):
    bp, t_p, d = x_prompt.shape
    ns, t_s, _ = x_sample.shape
    depth = w_in.shape[0]
    assert d == D_MODEL and t_s == 1 and t_p % CHUNK == 0 and w_in.shape[2] == PROJ
    alpha = (2 * depth) ** 0.25
    mp = bp * t_p
    row = lambda z: z.reshape(1, -1)

    hp = x_prompt.reshape(mp, d)
    hs = x_sample.reshape(ns, d)
    mem = mem_prompt.reshape(bp * MEM_LEN, d)
    outs = [[] for _ in range(12)]
    for l in range(depth):
        w_in_b = w_in[l].astype(BF16)
        w_out_b = w_out[l].astype(BF16)
        w_xq_b, w_xk_b = w_xq[l].astype(BF16), w_xk[l].astype(BF16)
        w_xv_b, w_xo_b = w_xv[l].astype(BF16), w_xo[l].astype(BF16)
        w1_b, w3_b, w2_b = ffn_w1[l].astype(BF16), ffn_w3[l].astype(BF16), ffn_w2[l].astype(BF16)
        ws_flat = ws_chunk[l].reshape(A_HEADS * CHUNK, CHUNK)
        bias_full = jnp.repeat(b_chunk[l].T, A_HEAD_DIM, axis=1)
        ws0 = jnp.repeat(ws_chunk[l][:, 0, 0], A_HEAD_DIM).reshape(1, W_A)
        wpool_bd = _block_diag(w_pool[l]).astype(BF16)
        abc_w = (row(ln_v_g[l]), row(ln_v_b[l]))
        abc_w2 = (wpool_bd, row(pool_scale[l]), conv_w[l])
        rw = (row(mu_d[l]), row(rwkv_w0[l]), rwkv_w2[l], row(rwkv_a0[l]), rwkv_a2[l], rwkv_g2[l],
              row(rwkv_k_k[l]), row(rwkv_k_a[l]), row(rwkv_r_k[l]), row(rwkv_lnx_g[l]), row(rwkv_lnx_b[l]))
        ln1 = (row(ln1_g[l]), row(ln1_b[l]))
        ln2 = (row(ln2_g[l]), row(ln2_b[l]))
        ln3 = (row(ln3_g[l]), row(ln3_b[l]))

        mk_p = _mm(mem, w_xk_b, 512, "mem_k")
        mv_p = _mm(mem, w_xv_b, 512, "mem_v")
        pabc, pd = _proj(hp, w_in_b, 512)
        mix, v_last, pool_p, conv_p = _abc_prompt(pabc.reshape(bp, t_p, N_ABC), *abc_w, ws_flat, bias_full,
                                                  *abc_w2, tt=256)
        yd, shift_p, wkv_p = _rwkv_prompt(pd.reshape(bp, t_p, D_PROJ), *rw, tt=128)
        nw = W_A + W_B + W_C
        hp = _mm_res_ln([mix.reshape(mp, nw), yd.reshape(mp, W_D)], [w_out_b[:nw], w_out_b[nw:]], hp, *ln1,
                        tm=512, alpha=alpha, name="out_proj")
        hp = _attn_prompt(hp.reshape(bp, t_p, d), mk_p.reshape(bp, MEM_LEN, d), mv_p.reshape(bp, MEM_LEN, d),
                          w_xq_b, w_xo_b, *ln2, tq=512, alpha=alpha).reshape(mp, d)
        hp = _ffn(hp, w1_b, w3_b, w2_b, *ln3, tm=1024, tf=256, alpha=alpha, name="ffn")
        for lst, val in zip(outs[:7], (v_last, pool_p, conv_p, shift_p, wkv_p,
                                       mk_p.reshape(bp, MEM_LEN, X_HEADS, X_HEAD_DIM),
                                       mv_p.reshape(bp, MEM_LEN, X_HEADS, X_HEAD_DIM))):
            lst.append(val)

        pabc_s, pd_s = _proj(hs, w_in_b, ns)
        mix_s, v_s, pool_s, conv_s = _abc_sample(
            pabc_s, state_pool[l].reshape(ns, POOL_BUF * W_B), state_conv[l].reshape(ns, (CONV_W - 1) * W_C),
            *abc_w, ws0, bias_full[0:1], *abc_w2, pos0=PAST_LEN)
        yd_s, wkv_s = _rwkv_sample(pd_s, state_shift[l].reshape(ns, D_PROJ), state_wkv[l], *rw, bb=8)
        hs = _mm_res_ln([mix_s, yd_s], [w_out_b[:nw], w_out_b[nw:]], hs, *ln1, tm=ns, alpha=alpha,
                        name="out_proj_s")
        q_s = _mm(hs, w_xq_b, ns, "q_s")
        o_s = _attn_sample(q_s.reshape(ns, 1, d), cache_mem_k[l].reshape(ns, MEM_LEN, d),
                           cache_mem_v[l].reshape(ns, MEM_LEN, d), bb=4)
        hs = _mm_res_ln([o_s.reshape(ns, d)], [w_xo_b], hs, *ln2, tm=ns, alpha=alpha, name="xo_s")
        hs = _ffn(hs, w1_b, w3_b, w2_b, *ln3, tm=ns, tf=256, alpha=alpha, name="ffn_s")
        for lst, val in zip(outs[7:], (v_s.reshape(ns, 1, W_A), pool_s.reshape(ns, POOL_BUF, W_B),
                                       conv_s.reshape(ns, CONV_W - 1, W_C), pd_s.reshape(ns, 1, D_PROJ), wkv_s)):
            lst.append(val)

    return (hp.reshape(bp, t_p, d), hs.reshape(ns, 1, d)) + tuple(jnp.stack(o) for o in outs)
```

```python
import functools
import math

import jax
import jax.numpy as jnp
from jax import lax
from jax.experimental import pallas as pl
from jax.experimental.pallas import tpu as pltpu

F32 = jnp.float32
BF16 = jnp.bfloat16

D_MODEL = 1024
W_A = 256
W_B = 256
W_C = 256
W_D = 256
A_HEADS = 4
A_HEAD_DIM = W_A // A_HEADS
CHUNK = 128
POOL_WINDOWS = (2, 4, 8, 16)
POOL_GROUP = W_B // len(POOL_WINDOWS)
POOL_BUF = max(POOL_WINDOWS) - 1
CONV_W = 3
D_HEAD_DIM = 64
D_HEADS = W_D // D_HEAD_DIM
R_DECAY = 32
R_AAA = 32
R_GATE = 64
D_PROJ = 3 * W_D + R_DECAY + R_AAA + R_GATE
N_ABC = 2 * W_A + W_B + 3 * W_C
PROJ = N_ABC + D_PROJ
MEM_LEN = 256
X_HEADS = 4
X_HEAD_DIM = D_MODEL // X_HEADS
D_FF = int(math.ceil(8 * D_MODEL / 3 / 256)) * 256
PAST_LEN = 16384
LN_EPS = 1e-5
GN_EPS = 64e-5

WKV_CHUNK = 64
POOL_CARRY = 24
ROW_CARRY = 8
V7X_VMEM_BYTES = 64 * 1024 * 1024
VMEM_CAP = V7X_VMEM_BYTES - 8 * 1024 * 1024

NN = (((1,), (0,)), ((), ()))
NT = (((1,), (1,)), ((), ()))
TN = (((0,), (0,)), ((), ()))


def _vmem_limit(nbytes):
    return int(min(VMEM_CAP, max(32 * 1024 * 1024, 2 * nbytes)))


def _params(sem, nbytes):
    return pltpu.CompilerParams(dimension_semantics=sem, vmem_limit_bytes=_vmem_limit(nbytes))


def _dot(a, b, dims=NN):
    return lax.dot_general(a, b, dims, preferred_element_type=F32)


def _split2(a):
    hi = a.astype(BF16)
    lo = (a - hi.astype(F32)).astype(BF16)
    return hi, lo


def _dot_hi(a, b, dims=NN):
    ah, al = _split2(a)
    bh, bl = _split2(b)
    return _dot(ah, bh, dims) + _dot(ah, bl, dims) + _dot(al, bh, dims)


def _dot_ones(x, ones_bf16, dims=NN):
    hi = x.astype(BF16)
    r1 = x - hi.astype(F32)
    mid = r1.astype(BF16)
    lo = (r1 - mid.astype(F32)).astype(BF16)
    return _dot(hi, ones_bf16, dims) + _dot(mid, ones_bf16, dims) + _dot(lo, ones_bf16, dims)


def _ones_dot(ones_bf16, x):
    hi = x.astype(BF16)
    r1 = x - hi.astype(F32)
    mid = r1.astype(BF16)
    lo = (r1 - mid.astype(F32)).astype(BF16)
    return _dot(ones_bf16, hi) + _dot(ones_bf16, mid) + _dot(ones_bf16, lo)


def _ln(x, g, b, eps=LN_EPS):
    mu = jnp.mean(x, axis=-1, keepdims=True)
    xc = x - mu
    var = jnp.mean(xc * xc, axis=-1, keepdims=True)
    return xc * lax.rsqrt(var + eps) * g + b


def _gelu(x):
    c = math.sqrt(2.0 / math.pi)
    return x * (0.5 * (1.0 + jnp.tanh(c * (x + 0.044715 * (x * x * x)))))


def _sigmoid(x):
    return 1.0 / (1.0 + jnp.exp(-x))


def _softplus(x):
    return jnp.maximum(x, 0.0) + jnp.log(1.0 + jnp.exp(-jnp.abs(x)))


def _head_ones(n, group):
    r = lax.broadcasted_iota(jnp.int32, (n, n), 0) // group
    c = lax.broadcasted_iota(jnp.int32, (n, n), 1) // group
    return jnp.where(r == c, 1.0, 0.0).astype(BF16)


def _proj_kernel(x_ref, w_ref, oabc_ref, od_ref):
    y = _dot(x_ref[...].astype(BF16), w_ref[...])
    oabc_ref[...] = y[:, :N_ABC]
    od_ref[...] = y[:, N_ABC:]


def _proj(x, w, tm):
    m, k = x.shape
    nbytes = 2 * (tm * k * 4 + k * PROJ * 2 + tm * PROJ * 4) + tm * PROJ * 4
    return pl.pallas_call(
        _proj_kernel,
        out_shape=(jax.ShapeDtypeStruct((m, N_ABC), F32), jax.ShapeDtypeStruct((m, D_PROJ), F32)),
        grid=(m // tm,),
        in_specs=[pl.BlockSpec((tm, k), lambda i: (i, 0)), pl.BlockSpec((k, PROJ), lambda i: (0, 0))],
        out_specs=(pl.BlockSpec((tm, N_ABC), lambda i: (i, 0)), pl.BlockSpec((tm, D_PROJ), lambda i: (i, 0))),
        compiler_params=_params(("parallel",), nbytes),
        name="proj",
    )(x, w)


def _mm_kernel(x_ref, w_ref, o_ref):
    o_ref[...] = _dot(x_ref[...].astype(BF16), w_ref[...])


def _mm(x, w, tm, name):
    m, k = x.shape
    n = w.shape[1]
    nbytes = 2 * (tm * k * 4 + k * n * 2 + tm * n * 4) + tm * n * 4
    return pl.pallas_call(
        _mm_kernel,
        out_shape=jax.ShapeDtypeStruct((m, n), F32),
        grid=(m // tm,),
        in_specs=[pl.BlockSpec((tm, k), lambda i: (i, 0)), pl.BlockSpec((k, n), lambda i: (0, 0))],
        out_specs=pl.BlockSpec((tm, n), lambda i: (i, 0)),
        compiler_params=_params(("parallel",), nbytes),
        name=name,
    )(x, w)


def _mm_res_ln_kernel(*refs, n_in, alpha):
    xs = refs[:n_in]
    ws = refs[n_in:2 * n_in]
    h_ref, g_ref, b_ref, o_ref = refs[2 * n_in:]
    y = _dot(xs[0][...].astype(BF16), ws[0][...])
    for x_ref, w_ref in zip(xs[1:], ws[1:]):
        y = y + _dot(x_ref[...].astype(BF16), w_ref[...])
    o_ref[...] = _ln(alpha * h_ref[...] + y, g_ref[...], b_ref[...])


def _mm_res_ln(xs, ws, h, g, b, tm, alpha, name):
    m, n = h.shape
    nbytes = 2 * sum(tm * x.shape[1] * 4 + w.shape[0] * n * 2 for x, w in zip(xs, ws)) + 5 * tm * n * 4
    in_specs = [pl.BlockSpec((tm, x.shape[1]), lambda i: (i, 0)) for x in xs]
    in_specs += [pl.BlockSpec(w.shape, lambda i: (0, 0)) for w in ws]
    in_specs += [pl.BlockSpec((tm, n), lambda i: (i, 0)),
                 pl.BlockSpec((1, n), lambda i: (0, 0)), pl.BlockSpec((1, n), lambda i: (0, 0))]
    return pl.pallas_call(
        functools.partial(_mm_res_ln_kernel, n_in=len(xs), alpha=alpha),
        out_shape=jax.ShapeDtypeStruct((m, n), F32),
        grid=(m // tm,),
        in_specs=in_specs,
        out_specs=pl.BlockSpec((tm, n), lambda i: (i, 0)),
        compiler_params=_params(("parallel",), nbytes),
        name=name,
    )(*xs, *ws, h, g, b)


def _ffn_kernel(x_ref, w1_ref, w3_ref, w2_ref, g_ref, b_ref, o_ref, xb_ref, acc_ref, *, alpha):
    j = pl.program_id(1)

    @pl.when(j == 0)
    def _():
        xb_ref[...] = x_ref[...].astype(BF16)
        acc_ref[...] = jnp.zeros_like(acc_ref)

    xb = xb_ref[...]
    h1 = _dot(xb, w1_ref[...])
    h3 = _dot(xb, w3_ref[...])
    a = (h1 * _sigmoid(h1) * h3).astype(BF16)
    acc_ref[...] += _dot(a, w2_ref[...])

    @pl.when(j == pl.num_programs(1) - 1)
    def _():
        o_ref[...] = _ln(alpha * x_ref[...] + acc_ref[...], g_ref[...], b_ref[...])


def _ffn(x, w1, w3, w2, g, b, tm, tf, alpha, name):
    m, d = x.shape
    nbytes = 4 * tm * d * 4 + tm * d * 2 + tm * d * 4 + 2 * 3 * d * tf * 2 + 3 * tm * tf * 4
    return pl.pallas_call(
        functools.partial(_ffn_kernel, alpha=alpha),
        out_shape=jax.ShapeDtypeStruct((m, d), F32),
        grid=(m // tm, D_FF // tf),
        in_specs=[pl.BlockSpec((tm, d), lambda i, j: (i, 0)),
                  pl.BlockSpec((d, tf), lambda i, j: (0, j)),
                  pl.BlockSpec((d, tf), lambda i, j: (0, j)),
                  pl.BlockSpec((tf, d), lambda i, j: (j, 0)),
                  pl.BlockSpec((1, d), lambda i, j: (0, 0)),
                  pl.BlockSpec((1, d), lambda i, j: (0, 0))],
        out_specs=pl.BlockSpec((tm, d), lambda i, j: (i, 0)),
        scratch_shapes=[pltpu.VMEM((tm, d), BF16), pltpu.VMEM((tm, d), F32)],
        compiler_params=_params(("parallel", "arbitrary"), nbytes),
        name=name,
    )(x, w1, w3, w2, g, b)


def _softmax_rows(s):
    m = jnp.max(s, axis=-1, keepdims=True)
    e = jnp.exp(s - m)
    return e / jnp.sum(e, axis=-1, keepdims=True)


def _attn_prompt_kernel(h_ref, mk_ref, mv_ref, wq_ref, wo_ref, g_ref, b_ref, o_ref, ob_ref, *, alpha):
    h = h_ref[0]
    q = _dot(h.astype(BF16), wq_ref[...])
    scale = X_HEAD_DIM ** -0.5
    for hd in range(X_HEADS):
        sl = slice(hd * X_HEAD_DIM, (hd + 1) * X_HEAD_DIM)
        kh = mk_ref[0, :, sl].astype(BF16)
        vh = mv_ref[0, :, sl].astype(BF16)
        s = _dot(q[:, sl].astype(BF16), kh, NT) * scale
        p = _softmax_rows(s)
        ob_ref[:, sl] = _dot(p.astype(BF16), vh).astype(BF16)
    y = _dot(ob_ref[...], wo_ref[...])
    o_ref[0] = _ln(alpha * h + y, g_ref[...], b_ref[...])


def _attn_prompt(h, mk, mv, wq, wo, g, b, tq, alpha):
    bn, t, d = h.shape
    nbytes = (4 * tq * d * 4 + 4 * MEM_LEN * d * 4 + 4 * d * d * 2 + tq * d * 2
              + 2 * tq * d * 4 + 3 * tq * MEM_LEN * 4)
    return pl.pallas_call(
        functools.partial(_attn_prompt_kernel, alpha=alpha),
        out_shape=jax.ShapeDtypeStruct((bn, t, d), F32),
        grid=(bn, t // tq),
        in_specs=[pl.BlockSpec((1, tq, d), lambda i, j: (i, j, 0)),
                  pl.BlockSpec((1, MEM_LEN, d), lambda i, j: (i, 0, 0)),
                  pl.BlockSpec((1, MEM_LEN, d), lambda i, j: (i, 0, 0)),
                  pl.BlockSpec((d, d), lambda i, j: (0, 0)),
                  pl.BlockSpec((d, d), lambda i, j: (0, 0)),
                  pl.BlockSpec((1, d), lambda i, j: (0, 0)),
                  pl.BlockSpec((1, d), lambda i, j: (0, 0))],
        out_specs=pl.BlockSpec((1, tq, d), lambda i, j: (i, j, 0)),
        scratch_shapes=[pltpu.VMEM((tq, d), BF16)],
        compiler_params=_params(("parallel", "parallel"), nbytes),
        name="attn_prompt",
    )(h, mk, mv, wq, wo, g, b)


def _attn_sample_kernel(q_ref, k_ref, v_ref, o_ref, *, bb):
    scale = X_HEAD_DIM ** -0.5
    for i in range(bb):
        q = jnp.broadcast_to(q_ref[i], (8, D_MODEL)).astype(BF16)
        for hd in range(X_HEADS):
            sl = slice(hd * X_HEAD_DIM, (hd + 1) * X_HEAD_DIM)
            kh = k_ref[i, :, hd, :].astype(BF16)
            vh = v_ref[i, :, hd, :].astype(BF16)
            s = _dot(q[:, sl], kh, NT) * scale
            p = _softmax_rows(s)
            o = _dot(p.astype(BF16), vh)
            o_ref[i, :, sl] = o[0:1]


def _attn_sample(q, cache_k, cache_v, layer, bb):
    n = q.shape[0]
    nbytes = 4 * bb * MEM_LEN * D_MODEL * 4 + 4 * bb * D_MODEL * 4 + 2 * MEM_LEN * X_HEAD_DIM * 4
    cache_spec = pl.BlockSpec((None, bb, MEM_LEN, X_HEADS, X_HEAD_DIM), lambda i: (layer, i, 0, 0, 0))
    return pl.pallas_call(
        functools.partial(_attn_sample_kernel, bb=bb),
        out_shape=jax.ShapeDtypeStruct((n, 1, D_MODEL), F32),
        grid=(n // bb,),
        in_specs=[pl.BlockSpec((bb, 1, D_MODEL), lambda i: (i, 0, 0)), cache_spec, cache_spec],
        out_specs=pl.BlockSpec((bb, 1, D_MODEL), lambda i: (i, 0, 0)),
        compiler_params=_params(("parallel",), nbytes),
        name="attn_sample",
    )(q, cache_k, cache_v)


def _pool_window_sums(ext, tt):
    s2 = ext + pltpu.roll(ext, 1, 0)
    s4 = s2 + pltpu.roll(s2, 2, 0)
    s8 = s4 + pltpu.roll(s4, 4, 0)
    s16 = s8 + pltpu.roll(s8, 8, 0)
    grp = lax.broadcasted_iota(jnp.int32, (tt, W_B), 1) // POOL_GROUP
    lo = ext.shape[0] - tt
    return jnp.where(grp == 0, s2[lo:], jnp.where(grp == 1, s4[lo:], jnp.where(grp == 2, s8[lo:], s16[lo:])))


def _abc_prompt_kernel(x_ref, lng_ref, lnb_ref, ws_ref, bias_ref, wpool_ref, pscale_ref, cw_ref,
                       mix_ref, vlast_ref, pool_ref, conv_ref, pcar_ref, ccar_ref, *, tt):
    t = pl.program_id(1)
    nt = pl.num_programs(1)

    @pl.when(t == 0)
    def _():
        pcar_ref[...] = jnp.zeros_like(pcar_ref)
        ccar_ref[...] = jnp.zeros_like(ccar_ref)

    x = x_ref[0]

    ga = _gelu(x[:, :2 * W_A])
    u = ga[:, :W_A]
    v = _ln(ga[:, W_A:], lng_ref[...], lnb_ref[...])
    rows = lax.broadcasted_iota(jnp.int32, (A_HEADS * CHUNK, CHUNK), 0) % CHUNK
    cols = lax.broadcasted_iota(jnp.int32, (A_HEADS * CHUNK, CHUNK), 1)
    wsm = jnp.where(rows >= cols, ws_ref[...], 0.0).astype(BF16)
    hid = lax.broadcasted_iota(jnp.int32, (CHUNK, W_A), 1) // A_HEAD_DIM
    for c in range(tt // CHUNK):
        rs = slice(c * CHUNK, (c + 1) * CHUNK)
        zz = _dot(wsm, v[rs].astype(BF16))
        z = zz[(A_HEADS - 1) * CHUNK:]
        for hd in range(A_HEADS - 2, -1, -1):
            z = jnp.where(hid == hd, zz[hd * CHUNK:(hd + 1) * CHUNK], z)
        mix_ref[0, rs, 0:W_A] = u[rs] * (z + bias_ref[...])

    @pl.when(t == nt - 1)
    def _():
        vlast_ref[0] = v[tt - CHUNK:]

    xb = x[:, 2 * W_A:2 * W_A + W_B]
    ext = jnp.concatenate([pcar_ref[...], xb], axis=0)
    sums = _pool_window_sums(ext, tt)
    pos = t * tt + lax.broadcasted_iota(jnp.int32, (tt, W_B), 0)
    win = jnp.left_shift(2, lax.broadcasted_iota(jnp.int32, (tt, W_B), 1) // POOL_GROUP)
    cnt = jnp.minimum(win, pos + 1).astype(F32)
    d = sums / cnt - xb
    mix_ref[0, :, W_A:W_A + W_B] = _dot(d.astype(BF16), wpool_ref[...]) * pscale_ref[...]
    pcar_ref[...] = ext[tt:]

    o = 2 * W_A + W_B
    bg = x[:, o:o + W_C]
    zc = x[:, o + W_C:o + 2 * W_C] * x[:, o + 2 * W_C:o + 3 * W_C]
    extz = jnp.concatenate([ccar_ref[...], zc], axis=0)
    y = (cw_ref[0:1, :] * pltpu.roll(extz, 2, 0) + cw_ref[1:2, :] * pltpu.roll(extz, 1, 0)
         + cw_ref[2:3, :] * extz)
    mix_ref[0, :, W_A + W_B:] = bg * y[ROW_CARRY:]
    ccar_ref[...] = extz[tt:]

    @pl.when(t == nt - 1)
    def _():
        pool_ref[0] = pcar_ref[POOL_CARRY - POOL_BUF:, :]
        conv_ref[0] = ccar_ref[ROW_CARRY - (CONV_W - 1):, :]


def _abc_prompt(pabc, lng, lnb, ws, bias, wpool, pscale, cw, tt):
    bn, t, _ = pabc.shape
    nw = W_A + W_B + W_C
    nbytes = 2 * tt * N_ABC * 4 + 2 * tt * nw * 4 + 12 * tt * W_A * 4 + 4 * A_HEADS * CHUNK * CHUNK * 4
    const = lambda i, j: (0, 0)
    return pl.pallas_call(
        functools.partial(_abc_prompt_kernel, tt=tt),
        out_shape=(jax.ShapeDtypeStruct((bn, t, nw), F32),
                   jax.ShapeDtypeStruct((bn, CHUNK, W_A), F32),
                   jax.ShapeDtypeStruct((bn, POOL_BUF, W_B), F32),
                   jax.ShapeDtypeStruct((bn, CONV_W - 1, W_C), F32)),
        grid=(bn, t // tt),
        in_specs=[pl.BlockSpec((1, tt, N_ABC), lambda i, j: (i, j, 0)),
                  pl.BlockSpec((1, W_A), const), pl.BlockSpec((1, W_A), const),
                  pl.BlockSpec((A_HEADS * CHUNK, CHUNK), const),
                  pl.BlockSpec((CHUNK, W_A), const),
                  pl.BlockSpec((W_B, W_B), const), pl.BlockSpec((1, W_B), const),
                  pl.BlockSpec((CONV_W, W_C), const)],
        out_specs=(pl.BlockSpec((1, tt, nw), lambda i, j: (i, j, 0)),
                   pl.BlockSpec((1, CHUNK, W_A), lambda i, j: (i, 0, 0)),
                   pl.BlockSpec((1, POOL_BUF, W_B), lambda i, j: (i, 0, 0)),
                   pl.BlockSpec((1, CONV_W - 1, W_C), lambda i, j: (i, 0, 0))),
        scratch_shapes=[pltpu.VMEM((POOL_CARRY, W_B), F32), pltpu.VMEM((ROW_CARRY, W_C), F32)],
        compiler_params=_params(("parallel", "arbitrary"), nbytes),
        name="mixer_abc_prompt",
    )(pabc, lng, lnb, ws, bias, wpool, pscale, cw)


def _abc_sample_kernel(x_ref, pool_ref, conv_ref, lng_ref, lnb_ref, ws0_ref, bias0_ref, wpool_ref,
                       pscale_ref, cw_ref, mix_ref, v_ref, poolo_ref, convo_ref, *, pos0):
    x = x_ref[...]
    n = x.shape[0]
    ga = _gelu(x[:, :2 * W_A])
    u = ga[:, :W_A]
    v = _ln(ga[:, W_A:], lng_ref[...], lnb_ref[...])
    v_ref[...] = v
    mix_ref[:, 0:W_A] = u * (v * ws0_ref[...] + bias0_ref[...])

    xb = x[:, 2 * W_A:2 * W_A + W_B]
    grp = lax.broadcasted_iota(jnp.int32, (n, W_B), 1) // POOL_GROUP
    run = xb
    sums = jnp.zeros_like(xb)
    for back in range(1, max(POOL_WINDOWS)):
        r = POOL_BUF - back
        run = run + pool_ref[:, r * W_B:(r + 1) * W_B]
        for gi, win in enumerate(POOL_WINDOWS):
            if back == win - 1:
                sums = jnp.where(grp == gi, run, sums)
    win = jnp.left_shift(2, grp)
    cnt = jnp.minimum(win, pos0 + 1).astype(F32)
    d = sums / cnt - xb
    mix_ref[:, W_A:W_A + W_B] = _dot(d.astype(BF16), wpool_ref[...]) * pscale_ref[...]
    poolo_ref[:, :(POOL_BUF - 1) * W_B] = pool_ref[:, W_B:]
    poolo_ref[:, (POOL_BUF - 1) * W_B:] = xb

    o = 2 * W_A + W_B
    bg = x[:, o:o + W_C]
    zc = x[:, o + W_C:o + 2 * W_C] * x[:, o + 2 * W_C:o + 3 * W_C]
    z0 = conv_ref[:, :W_C]
    z1 = conv_ref[:, W_C:]
    y = cw_ref[0:1, :] * z0 + cw_ref[1:2, :] * z1 + cw_ref[2:3, :] * zc
    mix_ref[:, W_A + W_B:] = bg * y
    convo_ref[:, :W_C] = z1
    convo_ref[:, W_C:] = zc


def _abc_sample(pabc, pool, conv, lng, lnb, ws0, bias0, wpool, pscale, cw, pos0):
    n = pabc.shape[0]
    nw = W_A + W_B + W_C
    return pl.pallas_call(
        functools.partial(_abc_sample_kernel, pos0=pos0),
        out_shape=(jax.ShapeDtypeStruct((n, nw), F32),
                   jax.ShapeDtypeStruct((n, W_A), F32),
                   jax.ShapeDtypeStruct(pool.shape, F32),
                   jax.ShapeDtypeStruct(conv.shape, F32)),
        name="mixer_abc_sample",
    )(pabc, pool, conv, lng, lnb, ws0, bias0, wpool, pscale, cw)


def _rwkv_inputs(xs, w0, w2, a0, a2, g2, kk_w, ka_w, seg):
    r = xs[:, 0:W_D]
    k = xs[:, W_D:2 * W_D]
    v = xs[:, 2 * W_D:3 * W_D]
    o = 3 * W_D
    dw = xs[:, o:o + R_DECAY]
    da = xs[:, o + R_DECAY:o + R_DECAY + R_AAA]
    dg = xs[:, o + R_DECAY + R_AAA:]
    w_log = -_softplus(-(w0 + _dot_hi(jnp.tanh(dw), w2))) - 0.5
    logdecay = -jnp.exp(w_log)
    a = _sigmoid(a0 + _dot_hi(da, a2))
    g = _dot_hi(_sigmoid(dg), g2)
    kk = k * kk_w
    kk = kk * lax.rsqrt(jnp.maximum(_dot_ones(kk * kk, seg), 1e-12))
    k = k * (1.0 + (a - 1.0) * ka_w)
    return r, k, v, kk, a, logdecay, g


def _rwkv_finish(o, r, k, v, g, rk_w, lnx_g, lnx_b, seg):
    inv = 1.0 / D_HEAD_DIM
    mu = _dot_ones(o, seg) * inv
    oc = o - mu
    var = _dot_ones(oc * oc, seg) * inv
    on = oc * lax.rsqrt(var + GN_EPS) * lnx_g + lnx_b
    bonus = _dot_ones(r * k * rk_w, seg) * v
    return (on + bonus) * g


def _bdot(a, b, dims=NN):
    return _dot(a.astype(BF16), b.astype(BF16), dims)


def _head_cols(x, hd):
    return x[:, hd * D_HEAD_DIM:(hd + 1) * D_HEAD_DIM]


def _wkv_tile(s_ref, o_ref, qt, rt, bt, kt, bbar, kbar, vm, gam, nc):
    c = WKV_CHUNK
    probs = [(ch, hd) for ch in range(nc) for hd in range(D_HEADS)]
    cut = lambda x, p: _head_cols(x[p[0] * c:(p[0] + 1) * c], p[1])
    qt, rt, bt, kt, bbar, kbar, vm = (x.astype(BF16) for x in (qt, rt, bt, kt, bbar, kbar, vm))
    ri = lax.broadcasted_iota(jnp.int32, (c, c), 0)
    ci = lax.broadcasted_iota(jnp.int32, (c, c), 1)
    strict = ri > ci
    eye = jnp.where(ri == ci, 1.0, 0.0)
    ri2 = lax.broadcasted_iota(jnp.int32, (c, 2 * c), 0)
    ci2 = lax.broadcasted_iota(jnp.int32, (c, 2 * c), 1) % c
    incl = ri2 >= ci2

    qs = {p: cut(qt, p) for p in probs}
    vs = {p: cut(vm, p) for p in probs}
    aa = {p: _dot(jnp.concatenate([qs[p], cut(rt, p)], axis=0),
                  jnp.concatenate([cut(bt, p), cut(kt, p)], axis=0), NT) for p in probs}
    lk = {p: jnp.where(strict, aa[p][:c, c:], 0.0).astype(BF16) for p in probs}
    abk = {p: jnp.where(incl, aa[p][c:, :], 0.0).astype(BF16) for p in probs}
    npow = {p: jnp.where(strict, -aa[p][:c, :c], 0.0) for p in probs}
    tinv = {p: eye + npow[p] for p in probs}
    zp = {p: _dot(lk[p], vs[p]) for p in probs}
    for _ in range(int(math.log2(c)) - 1):
        npow = {p: _bdot(npow[p], npow[p]) for p in probs}
        tinv = {p: tinv[p] + _bdot(tinv[p], npow[p]) for p in probs}
    tb = {p: tinv[p].astype(BF16) for p in probs}
    qh = {p: _dot(tb[p], qs[p]).astype(BF16) for p in probs}
    zn = {p: (-_bdot(tb[p], zp[p])).astype(BF16) for p in probs}
    rh = {p: cut(rt, p).astype(F32) - _dot(abk[p][:, :c], qh[p]) for p in probs}
    gt = {p: -_dot(qh[p], cut(bbar, p), TN) for p in probs}
    cst = {p: _dot(jnp.concatenate([vs[p], zn[p]], axis=0),
                   jnp.concatenate([cut(kbar, p), cut(bbar, p)], axis=0), TN) for p in probs}
    pv = {p: _dot(abk[p], jnp.concatenate([zn[p], vs[p]], axis=0)) for p in probs}
    for ch in range(nc):
        for hd in range(D_HEADS):
            p = (ch, hd)
            s0 = s_ref[hd]
            s0b = s0.astype(BF16)
            o_ref[ch * c:(ch + 1) * c, hd * D_HEAD_DIM:(hd + 1) * D_HEAD_DIM] = _bdot(rh[p], s0b, NT) + pv[p]
            s_ref[hd] = s0 * _head_cols(gam[ch], hd) + _bdot(s0b, gt[p]) + cst[p]


def _rwkv_prompt_kernel(pd_ref, mu_ref, w0_ref, w2_ref, a0_ref, a2_ref, g2_ref, kk_ref, ka_ref, rk_ref,
                        lg_ref, lb_ref, yd_ref, shift_ref, wkv_ref, car_ref, s_ref, o_ref, *, tt):
    t = pl.program_id(1)
    nt = pl.num_programs(1)

    @pl.when(t == 0)
    def _():
        car_ref[...] = jnp.zeros_like(car_ref)
        s_ref[...] = jnp.zeros_like(s_ref)

    pd = pd_ref[0]
    ext = jnp.concatenate([car_ref[...], pd], axis=0)
    prev = pltpu.roll(ext, 1, 0)[ROW_CARRY:]
    car_ref[...] = ext[tt:]
    xs = pd + (prev - pd) * mu_ref[...]
    seg = _head_ones(W_D, D_HEAD_DIM)
    r, k, v, kk, a, ld, g = _rwkv_inputs(xs, w0_ref[...], w2_ref[...], a0_ref[...], a2_ref[...],
                                         g2_ref[...], kk_ref[...], ka_ref[...], seg)
    c = WKV_CHUNK
    nc = tt // c
    tri = jnp.where(lax.broadcasted_iota(jnp.int32, (c, c), 0) >= lax.broadcasted_iota(jnp.int32, (c, c), 1),
                    1.0, 0.0).astype(BF16)
    cs_chunks = [_ones_dot(tri, ld[ch * c:(ch + 1) * c]) for ch in range(nc)]
    cs = jnp.concatenate(cs_chunks, axis=0)
    cs_end = [x[c - 1:c] for x in cs_chunks]
    cs_last = jnp.concatenate([jnp.broadcast_to(x, (c, W_D)) for x in cs_end], axis=0)
    e_neg = jnp.exp(-cs)
    e_tail = jnp.exp(cs_last - cs)
    b = kk * a
    _wkv_tile(s_ref, o_ref, kk * jnp.exp(cs - ld), r * jnp.exp(cs), b * e_neg, k * e_neg, b * e_tail, k * e_tail,
              v, [jnp.exp(x) for x in cs_end], nc)
    yd_ref[0] = _rwkv_finish(o_ref[...], r, k, v, g, rk_ref[...], lg_ref[...], lb_ref[...], seg)

    @pl.when(t == nt - 1)
    def _():
        shift_ref[0] = pd[tt - 1:tt]
        wkv_ref[0] = s_ref[...]


def _rwkv_prompt(pd, mu, w0, w2, a0, a2, g2, kk_w, ka_w, rk_w, lnx_g, lnx_b, tt):
    bn, t, _ = pd.shape
    nbytes = 2 * tt * D_PROJ * 4 + 40 * tt * W_D * 4
    const = lambda i, j: (0, 0)
    row = lambda n: pl.BlockSpec((1, n), const)
    return pl.pallas_call(
        functools.partial(_rwkv_prompt_kernel, tt=tt),
        out_shape=(jax.ShapeDtypeStruct((bn, t, W_D), F32),
                   jax.ShapeDtypeStruct((bn, 1, D_PROJ), F32),
                   jax.ShapeDtypeStruct((bn, D_HEADS, D_HEAD_DIM, D_HEAD_DIM), F32)),
        grid=(bn, t // tt),
        in_specs=[pl.BlockSpec((1, tt, D_PROJ), lambda i, j: (i, j, 0)),
                  row(D_PROJ), row(W_D), pl.BlockSpec((R_DECAY, W_D), const),
                  row(W_D), pl.BlockSpec((R_AAA, W_D), const), pl.BlockSpec((R_GATE, W_D), const),
                  row(W_D), row(W_D), row(W_D), row(W_D), row(W_D)],
        out_specs=(pl.BlockSpec((1, tt, W_D), lambda i, j: (i, j, 0)),
                   pl.BlockSpec((1, 1, D_PROJ), lambda i, j: (i, 0, 0)),
                   pl.BlockSpec((1, D_HEADS, D_HEAD_DIM, D_HEAD_DIM), lambda i, j: (i, 0, 0, 0))),
        scratch_shapes=[pltpu.VMEM((ROW_CARRY, D_PROJ), F32),
                        pltpu.VMEM((D_HEADS, D_HEAD_DIM, D_HEAD_DIM), F32),
                        pltpu.VMEM((tt, W_D), F32)],
        compiler_params=_params(("parallel", "arbitrary"), nbytes),
        name="rwkv_prompt",
    )(pd, mu, w0, w2, a0, a2, g2, kk_w, ka_w, rk_w, lnx_g, lnx_b)


def _rwkv_sample_kernel(pd_ref, sh_ref, s_ref, mu_ref, w0_ref, w2_ref, a0_ref, a2_ref, g2_ref, kk_ref,
                        ka_ref, rk_ref, lg_ref, lb_ref, yd_ref, so_ref, o_ref, *, bb):
    pd = pd_ref[...]
    xs = pd + (sh_ref[...] - pd) * mu_ref[...]
    seg = _head_ones(W_D, D_HEAD_DIM)
    r, k, v, kk, a, ld, g = _rwkv_inputs(xs, w0_ref[...], w2_ref[...], a0_ref[...], a2_ref[...],
                                         g2_ref[...], kk_ref[...], ka_ref[...], seg)
    w = jnp.exp(ld)
    b = kk * a
    n = D_HEAD_DIM
    eye = jnp.where(lax.broadcasted_iota(jnp.int32, (n, n), 0) == lax.broadcasted_iota(jnp.int32, (n, n), 1),
                    1.0, 0.0)
    for i in range(bb):
        for hd in range(D_HEADS):
            hs = slice(hd * n, (hd + 1) * n)
            row = lambda z: z[i:i + 1, hs]
            s = s_ref[i, hd]
            sa = -jnp.sum(s * row(kk), axis=1, keepdims=True)
            vcol = jnp.sum(eye * row(v), axis=1, keepdims=True)
            s = s * row(w) + sa * row(b) + vcol * row(k)
            so_ref[i, hd] = s
            ocol = jnp.sum(s * row(r), axis=1, keepdims=True)
            o_ref[i:i + 1, hs] = jnp.sum(eye * ocol, axis=0, keepdims=True)
    yd_ref[...] = _rwkv_finish(o_ref[...], r, k, v, g, rk_ref[...], lg_ref[...], lb_ref[...], seg)


def _rwkv_sample(pd, shift, state, mu, w0, w2, a0, a2, g2, kk_w, ka_w, rk_w, lnx_g, lnx_b, bb):
    n = pd.shape[0]
    const = lambda i: (0, 0)
    row = lambda m: pl.BlockSpec((1, m), const)
    sblock = pl.BlockSpec((bb, D_HEADS, D_HEAD_DIM, D_HEAD_DIM), lambda i: (i, 0, 0, 0))
    return pl.pallas_call(
        functools.partial(_rwkv_sample_kernel, bb=bb),
        out_shape=(jax.ShapeDtypeStruct((n, W_D), F32), jax.ShapeDtypeStruct(state.shape, F32)),
        grid=(n // bb,),
        in_specs=[pl.BlockSpec((bb, D_PROJ), lambda i: (i, 0)), pl.BlockSpec((bb, D_PROJ), lambda i: (i, 0)),
                  sblock, row(D_PROJ), row(W_D), pl.BlockSpec((R_DECAY, W_D), const),
                  row(W_D), pl.BlockSpec((R_AAA, W_D), const), pl.BlockSpec((R_GATE, W_D), const),
                  row(W_D), row(W_D), row(W_D), row(W_D), row(W_D)],
        out_specs=(pl.BlockSpec((bb, W_D), lambda i: (i, 0)), sblock),
        scratch_shapes=[pltpu.VMEM((bb, W_D), F32)],
        compiler_params=pltpu.CompilerParams(dimension_semantics=("parallel",)),
        name="rwkv_sample",
    )(pd, shift, state, mu, w0, w2, a0, a2, g2, kk_w, ka_w, rk_w, lnx_g, lnx_b)


def _block_diag(w):
    gn, n, _ = w.shape
    eye = jnp.eye(gn, dtype=w.dtype)
    return (eye[:, None, :, None] * w[:, :, None, :]).reshape(gn * n, gn * n)


def kernel(x_prompt, x_sample, mem_prompt, cache_mem_k, cache_mem_v, state_pool, state_conv, state_shift, state_wkv,
           w_in, mu_d, ln_v_g, ln_v_b, ws_chunk, b_chunk, w_pool, pool_scale, conv_w,
           rwkv_w0, rwkv_w2, rwkv_a0, rwkv_a2, rwkv_g2, rwkv_k_k, rwkv_k_a, rwkv_r_k, rwkv_lnx_g, rwkv_lnx_b,
           w_out, ln1_g, ln1_b, w_xq, w_xk, w_xv, w_xo, ln2_g, ln2_b, ffn_w1, ffn_w3, ffn_w2, ln3_g, ln3_b):
    bp, t_p, d = x_prompt.shape
    ns, t_s, _ = x_sample.shape
    depth = w_in.shape[0]
    assert d == D_MODEL and t_s == 1 and t_p % CHUNK == 0 and w_in.shape[2] == PROJ
    alpha = (2 * depth) ** 0.25
    mp = bp * t_p
    row = lambda z: z.reshape(1, -1)

    hp = x_prompt.reshape(mp, d)
    hs = x_sample.reshape(ns, d)
    mem = mem_prompt.reshape(bp * MEM_LEN, d)
    outs = [[] for _ in range(12)]
    for l in range(depth):
        w_in_b = w_in[l].astype(BF16)
        w_out_b = w_out[l].astype(BF16)
        w_xq_b, w_xk_b = w_xq[l].astype(BF16), w_xk[l].astype(BF16)
        w_xv_b, w_xo_b = w_xv[l].astype(BF16), w_xo[l].astype(BF16)
        w1_b, w3_b, w2_b = ffn_w1[l].astype(BF16), ffn_w3[l].astype(BF16), ffn_w2[l].astype(BF16)
        ws_flat = ws_chunk[l].reshape(A_HEADS * CHUNK, CHUNK)
        bias_full = jnp.repeat(b_chunk[l].T, A_HEAD_DIM, axis=1)
        ws0 = jnp.repeat(ws_chunk[l][:, 0, 0], A_HEAD_DIM).reshape(1, W_A)
        wpool_bd = _block_diag(w_pool[l]).astype(BF16)
        abc_w = (row(ln_v_g[l]), row(ln_v_b[l]))
        abc_w2 = (wpool_bd, row(pool_scale[l]), conv_w[l])
        rw = (row(mu_d[l]), row(rwkv_w0[l]), rwkv_w2[l], row(rwkv_a0[l]), rwkv_a2[l], rwkv_g2[l],
              row(rwkv_k_k[l]), row(rwkv_k_a[l]), row(rwkv_r_k[l]), row(rwkv_lnx_g[l]), row(rwkv_lnx_b[l]))
        ln1 = (row(ln1_g[l]), row(ln1_b[l]))
        ln2 = (row(ln2_g[l]), row(ln2_b[l]))
        ln3 = (row(ln3_g[l]), row(ln3_b[l]))

        mk_p = _mm(mem, w_xk_b, 512, "mem_k")
        mv_p = _mm(mem, w_xv_b, 512, "mem_v")
        pabc, pd = _proj(hp, w_in_b, 512)
        mix, v_last, pool_p, conv_p = _abc_prompt(pabc.reshape(bp, t_p, N_ABC), *abc_w, ws_flat, bias_full,
                                                  *abc_w2, tt=256)
        yd, shift_p, wkv_p = _rwkv_prompt(pd.reshape(bp, t_p, D_PROJ), *rw, tt=256)
        nw = W_A + W_B + W_C
        hp = _mm_res_ln([mix.reshape(mp, nw), yd.reshape(mp, W_D)], [w_out_b[:nw], w_out_b[nw:]], hp, *ln1,
                        tm=512, alpha=alpha, name="out_proj")
        hp = _attn_prompt(hp.reshape(bp, t_p, d), mk_p.reshape(bp, MEM_LEN, d), mv_p.reshape(bp, MEM_LEN, d),
                          w_xq_b, w_xo_b, *ln2, tq=512, alpha=alpha).reshape(mp, d)
        hp = _ffn(hp, w1_b, w3_b, w2_b, *ln3, tm=1024, tf=256, alpha=alpha, name="ffn")
        for lst, val in zip(outs[:7], (v_last, pool_p, conv_p, shift_p, wkv_p,
                                       mk_p.reshape(bp, MEM_LEN, X_HEADS, X_HEAD_DIM),
                                       mv_p.reshape(bp, MEM_LEN, X_HEADS, X_HEAD_DIM))):
            lst.append(val)

        pabc_s, pd_s = _proj(hs, w_in_b, ns)
        mix_s, v_s, pool_s, conv_s = _abc_sample(
            pabc_s, state_pool[l].reshape(ns, POOL_BUF * W_B), state_conv[l].reshape(ns, (CONV_W - 1) * W_C),
            *abc_w, ws0, bias_full[0:1], *abc_w2, pos0=PAST_LEN)
        yd_s, wkv_s = _rwkv_sample(pd_s, state_shift[l].reshape(ns, D_PROJ), state_wkv[l], *rw, bb=8)
        hs = _mm_res_ln([mix_s, yd_s], [w_out_b[:nw], w_out_b[nw:]], hs, *ln1, tm=ns, alpha=alpha,
                        name="out_proj_s")
        q_s = _mm(hs, w_xq_b, ns, "q_s")
        o_s = _attn_sample(q_s.reshape(ns, 1, d), cache_mem_k, cache_mem_v, l, bb=4)
        hs = _mm_res_ln([o_s.reshape(ns, d)], [w_xo_b], hs, *ln2, tm=ns, alpha=alpha, name="xo_s")
        hs = _ffn(hs, w1_b, w3_b, w2_b, *ln3, tm=ns, tf=256, alpha=alpha, name="ffn_s")
        for lst, val in zip(outs[7:], (v_s.reshape(ns, 1, W_A), pool_s.reshape(ns, POOL_BUF, W_B),
                                       conv_s.reshape(ns, CONV_W - 1, W_C), pd_s.reshape(ns, 1, D_PROJ), wkv_s)):
            lst.append(val)

    return (hp.reshape(bp, t_p, d), hs.reshape(ns, 1, d)) + tuple(jnp.stack(o) for o in outs)
```

```python
import functools
import math

import jax
import jax.numpy as jnp
from jax import lax
from jax.experimental import pallas as pl
from jax.experimental.pallas import tpu as pltpu

F32 = jnp.float32
BF16 = jnp.bfloat16

D_MODEL = 1024
W_A = 256
W_B = 256
W_C = 256
W_D = 256
A_HEADS = 4
A_HEAD_DIM = W_A // A_HEADS
CHUNK = 128
POOL_WINDOWS = (2, 4, 8, 16)
POOL_GROUP = W_B // len(POOL_WINDOWS)
POOL_BUF = max(POOL_WINDOWS) - 1
CONV_W = 3
D_HEAD_DIM = 64
D_HEADS = W_D // D_HEAD_DIM
R_DECAY = 32
R_AAA = 32
R_GATE = 64
D_PROJ = 3 * W_D + R_DECAY + R_AAA + R_GATE
N_ABC = 2 * W_A + W_B + 3 * W_C
PROJ = N_ABC + D_PROJ
MEM_LEN = 256
X_HEADS = 4
X_HEAD_DIM = D_MODEL // X_HEADS
D_FF = int(math.ceil(8 * D_MODEL / 3 / 256)) * 256
PAST_LEN = 16384
LN_EPS = 1e-5
GN_EPS = 64e-5

WKV_CHUNK = 64
POOL_CARRY = 24
ROW_CARRY = 8
V7X_VMEM_BYTES = 64 * 1024 * 1024
VMEM_CAP = V7X_VMEM_BYTES - 8 * 1024 * 1024

NN = (((1,), (0,)), ((), ()))
NT = (((1,), (1,)), ((), ()))
TN = (((0,), (0,)), ((), ()))


def _vmem_limit(nbytes):
    return int(min(VMEM_CAP, max(32 * 1024 * 1024, 2 * nbytes)))


def _params(sem, nbytes):
    return pltpu.CompilerParams(dimension_semantics=sem, vmem_limit_bytes=_vmem_limit(nbytes))


def _dot(a, b, dims=NN):
    return lax.dot_general(a, b, dims, preferred_element_type=F32)


def _split2(a):
    hi = a.astype(BF16)
    lo = (a - hi.astype(F32)).astype(BF16)
    return hi, lo


def _dot_hi(a, b, dims=NN):
    ah, al = _split2(a)
    bh, bl = _split2(b)
    return _dot(ah, bh, dims) + _dot(ah, bl, dims) + _dot(al, bh, dims)


def _dot_ones(x, ones_bf16, dims=NN):
    hi = x.astype(BF16)
    r1 = x - hi.astype(F32)
    mid = r1.astype(BF16)
    lo = (r1 - mid.astype(F32)).astype(BF16)
    return _dot(hi, ones_bf16, dims) + _dot(mid, ones_bf16, dims) + _dot(lo, ones_bf16, dims)


def _ones_dot(ones_bf16, x):
    hi = x.astype(BF16)
    r1 = x - hi.astype(F32)
    mid = r1.astype(BF16)
    lo = (r1 - mid.astype(F32)).astype(BF16)
    return _dot(ones_bf16, hi) + _dot(ones_bf16, mid) + _dot(ones_bf16, lo)


def _ln(x, g, b, eps=LN_EPS):
    mu = jnp.mean(x, axis=-1, keepdims=True)
    xc = x - mu
    var = jnp.mean(xc * xc, axis=-1, keepdims=True)
    return xc * lax.rsqrt(var + eps) * g + b


def _gelu(x):
    c = math.sqrt(2.0 / math.pi)
    return x * (0.5 * (1.0 + jnp.tanh(c * (x + 0.044715 * (x * x * x)))))


def _sigmoid(x):
    return 1.0 / (1.0 + jnp.exp(-x))


def _softplus(x):
    return jnp.maximum(x, 0.0) + jnp.log(1.0 + jnp.exp(-jnp.abs(x)))


def _head_ones(n, group):
    r = lax.broadcasted_iota(jnp.int32, (n, n), 0) // group
    c = lax.broadcasted_iota(jnp.int32, (n, n), 1) // group
    return jnp.where(r == c, 1.0, 0.0).astype(BF16)


def _proj_kernel(x_ref, w_ref, oabc_ref, od_ref):
    y = _dot(x_ref[...].astype(BF16), w_ref[...])
    oabc_ref[...] = y[:, :N_ABC]
    od_ref[...] = y[:, N_ABC:]


def _proj(x, w, tm):
    m, k = x.shape
    nbytes = 2 * (tm * k * 4 + k * PROJ * 2 + tm * PROJ * 4) + tm * PROJ * 4
    return pl.pallas_call(
        _proj_kernel,
        out_shape=(jax.ShapeDtypeStruct((m, N_ABC), F32), jax.ShapeDtypeStruct((m, D_PROJ), F32)),
        grid=(m // tm,),
        in_specs=[pl.BlockSpec((tm, k), lambda i: (i, 0)), pl.BlockSpec((k, PROJ), lambda i: (0, 0))],
        out_specs=(pl.BlockSpec((tm, N_ABC), lambda i: (i, 0)), pl.BlockSpec((tm, D_PROJ), lambda i: (i, 0))),
        compiler_params=_params(("parallel",), nbytes),
        name="proj",
    )(x, w)


def _mm_kernel(x_ref, w_ref, o_ref):
    o_ref[...] = _dot(x_ref[...].astype(BF16), w_ref[...])


def _mm(x, w, tm, name):
    m, k = x.shape
    n = w.shape[1]
    nbytes = 2 * (tm * k * 4 + k * n * 2 + tm * n * 4) + tm * n * 4
    return pl.pallas_call(
        _mm_kernel,
        out_shape=jax.ShapeDtypeStruct((m, n), F32),
        grid=(m // tm,),
        in_specs=[pl.BlockSpec((tm, k), lambda i: (i, 0)), pl.BlockSpec((k, n), lambda i: (0, 0))],
        out_specs=pl.BlockSpec((tm, n), lambda i: (i, 0)),
        compiler_params=_params(("parallel",), nbytes),
        name=name,
    )(x, w)


def _mm_res_ln_kernel(*refs, n_in, alpha):
    xs = refs[:n_in]
    ws = refs[n_in:2 * n_in]
    h_ref, g_ref, b_ref, o_ref = refs[2 * n_in:]
    y = _dot(xs[0][...].astype(BF16), ws[0][...])
    for x_ref, w_ref in zip(xs[1:], ws[1:]):
        y = y + _dot(x_ref[...].astype(BF16), w_ref[...])
    o_ref[...] = _ln(alpha * h_ref[...] + y, g_ref[...], b_ref[...])


def _mm_res_ln(xs, ws, h, g, b, tm, alpha, name):
    m, n = h.shape
    nbytes = 2 * sum(tm * x.shape[1] * 4 + w.shape[0] * n * 2 for x, w in zip(xs, ws)) + 5 * tm * n * 4
    in_specs = [pl.BlockSpec((tm, x.shape[1]), lambda i: (i, 0)) for x in xs]
    in_specs += [pl.BlockSpec(w.shape, lambda i: (0, 0)) for w in ws]
    in_specs += [pl.BlockSpec((tm, n), lambda i: (i, 0)),
                 pl.BlockSpec((1, n), lambda i: (0, 0)), pl.BlockSpec((1, n), lambda i: (0, 0))]
    return pl.pallas_call(
        functools.partial(_mm_res_ln_kernel, n_in=len(xs), alpha=alpha),
        out_shape=jax.ShapeDtypeStruct((m, n), F32),
        grid=(m // tm,),
        in_specs=in_specs,
        out_specs=pl.BlockSpec((tm, n), lambda i: (i, 0)),
        compiler_params=_params(("parallel",), nbytes),
        name=name,
    )(*xs, *ws, h, g, b)


def _ffn_kernel(x_ref, w1_ref, w3_ref, w2_ref, g_ref, b_ref, o_ref, xb_ref, acc_ref, *, alpha):
    j = pl.program_id(1)

    @pl.when(j == 0)
    def _():
        xb_ref[...] = x_ref[...].astype(BF16)
        acc_ref[...] = jnp.zeros_like(acc_ref)

    xb = xb_ref[...]
    h1 = _dot(xb, w1_ref[...])
    h3 = _dot(xb, w3_ref[...])
    a = (h1 * _sigmoid(h1) * h3).astype(BF16)
    acc_ref[...] += _dot(a, w2_ref[...])

    @pl.when(j == pl.num_programs(1) - 1)
    def _():
        o_ref[...] = _ln(alpha * x_ref[...] + acc_ref[...], g_ref[...], b_ref[...])


def _ffn(x, w1, w3, w2, g, b, tm, tf, alpha, name):
    m, d = x.shape
    nbytes = 4 * tm * d * 4 + tm * d * 2 + tm * d * 4 + 2 * 3 * d * tf * 2 + 3 * tm * tf * 4
    return pl.pallas_call(
        functools.partial(_ffn_kernel, alpha=alpha),
        out_shape=jax.ShapeDtypeStruct((m, d), F32),
        grid=(m // tm, D_FF // tf),
        in_specs=[pl.BlockSpec((tm, d), lambda i, j: (i, 0)),
                  pl.BlockSpec((d, tf), lambda i, j: (0, j)),
                  pl.BlockSpec((d, tf), lambda i, j: (0, j)),
                  pl.BlockSpec((tf, d), lambda i, j: (j, 0)),
                  pl.BlockSpec((1, d), lambda i, j: (0, 0)),
                  pl.BlockSpec((1, d), lambda i, j: (0, 0))],
        out_specs=pl.BlockSpec((tm, d), lambda i, j: (i, 0)),
        scratch_shapes=[pltpu.VMEM((tm, d), BF16), pltpu.VMEM((tm, d), F32)],
        compiler_params=_params(("parallel", "arbitrary"), nbytes),
        name=name,
    )(x, w1, w3, w2, g, b)


def _softmax_rows(s):
    m = jnp.max(s, axis=-1, keepdims=True)
    e = jnp.exp(s - m)
    return e / jnp.sum(e, axis=-1, keepdims=True)


def _attn_prompt_kernel(h_ref, mk_ref, mv_ref, wq_ref, wo_ref, g_ref, b_ref, o_ref, ob_ref, *, alpha):
    h = h_ref[0]
    q = _dot(h.astype(BF16), wq_ref[...])
    scale = X_HEAD_DIM ** -0.5
    sls = [slice(hd * X_HEAD_DIM, (hd + 1) * X_HEAD_DIM) for hd in range(X_HEADS)]
    qb = q.astype(BF16)
    scores = [_dot(qb[:, sl], mk_ref[0, :, sl].astype(BF16), NT) * scale for sl in sls]
    probs = [_softmax_rows(s).astype(BF16) for s in scores]
    for sl, p in zip(sls, probs):
        ob_ref[:, sl] = _dot(p, mv_ref[0, :, sl].astype(BF16)).astype(BF16)
    y = _dot(ob_ref[...], wo_ref[...])
    o_ref[0] = _ln(alpha * h + y, g_ref[...], b_ref[...])


def _attn_prompt(h, mk, mv, wq, wo, g, b, tq, alpha):
    bn, t, d = h.shape
    nbytes = (4 * tq * d * 4 + 4 * MEM_LEN * d * 4 + 4 * d * d * 2 + tq * d * 2
              + 2 * tq * d * 4 + 3 * tq * MEM_LEN * 4)
    return pl.pallas_call(
        functools.partial(_attn_prompt_kernel, alpha=alpha),
        out_shape=jax.ShapeDtypeStruct((bn, t, d), F32),
        grid=(bn, t // tq),
        in_specs=[pl.BlockSpec((1, tq, d), lambda i, j: (i, j, 0)),
                  pl.BlockSpec((1, MEM_LEN, d), lambda i, j: (i, 0, 0)),
                  pl.BlockSpec((1, MEM_LEN, d), lambda i, j: (i, 0, 0)),
                  pl.BlockSpec((d, d), lambda i, j: (0, 0)),
                  pl.BlockSpec((d, d), lambda i, j: (0, 0)),
                  pl.BlockSpec((1, d), lambda i, j: (0, 0)),
                  pl.BlockSpec((1, d), lambda i, j: (0, 0))],
        out_specs=pl.BlockSpec((1, tq, d), lambda i, j: (i, j, 0)),
        scratch_shapes=[pltpu.VMEM((tq, d), BF16)],
        compiler_params=_params(("parallel", "parallel"), nbytes),
        name="attn_prompt",
    )(h, mk, mv, wq, wo, g, b)


LANES = 128
SUBLANES = 8
LANE_TILES = X_HEAD_DIM // LANES
MEM_ROWS = MEM_LEN * LANE_TILES * X_HEADS


def _cache_rows_view(cache):
    nl, n = cache.shape[:2]
    x = cache.reshape(nl, n, MEM_LEN, X_HEADS, LANE_TILES, LANES)
    return x.transpose(0, 1, 2, 4, 3, 5).reshape(nl, n, MEM_ROWS, LANES)


def _attn_sample_kernel(q_ref, k_ref, v_ref, o_ref, *, bb):
    scale = X_HEAD_DIM ** -0.5
    shape = (SUBLANES, MEM_ROWS)
    rowi = lax.broadcasted_iota(jnp.int32, shape, 0)
    coli = lax.broadcasted_iota(jnp.int32, shape, 1)
    valid = (coli % SUBLANES) == rowi
    raw = [_dot(q_ref[i].astype(BF16), k_ref[i].astype(BF16), NT) for i in range(bb)]
    probs = []
    for r in raw:
        r = jnp.where(valid, r, 0.0)
        other = pltpu.roll(r, X_HEADS, 0)
        other = jnp.where(rowi < X_HEADS, pltpu.roll(other, MEM_ROWS - X_HEADS, 1), pltpu.roll(other, X_HEADS, 1))
        s = jnp.where(valid, (r + other) * scale, -jnp.inf)
        m = jnp.max(s, axis=-1, keepdims=True)
        e = jnp.exp(s - m)
        probs.append((e / jnp.sum(e, axis=-1, keepdims=True)).astype(BF16))
    for i in range(bb):
        o_ref[i] = _dot(probs[i], v_ref[i].astype(BF16))


def _attn_sample(q, cache_k, cache_v, layer, bb):
    n = q.shape[0]
    q8 = q.reshape(n, X_HEADS, LANE_TILES, LANES).transpose(0, 2, 1, 3).reshape(n, SUBLANES, LANES)
    nbytes = 4 * bb * MEM_ROWS * LANES * 4 + 2 * bb * MEM_ROWS * LANES * 2 + 8 * SUBLANES * MEM_ROWS * 4
    cache_spec = pl.BlockSpec((None, bb, MEM_ROWS, LANES), lambda i: (layer, i, 0, 0))
    o8 = pl.pallas_call(
        functools.partial(_attn_sample_kernel, bb=bb),
        out_shape=jax.ShapeDtypeStruct((n, SUBLANES, LANES), F32),
        grid=(n // bb,),
        in_specs=[pl.BlockSpec((bb, SUBLANES, LANES), lambda i: (i, 0, 0)), cache_spec, cache_spec],
        out_specs=pl.BlockSpec((bb, SUBLANES, LANES), lambda i: (i, 0, 0)),
        compiler_params=_params(("parallel",), nbytes),
        name="attn_sample",
    )(q8, _cache_rows_view(cache_k), _cache_rows_view(cache_v))
    return o8.reshape(n, LANE_TILES, X_HEADS, LANES).transpose(0, 2, 1, 3).reshape(n, D_MODEL)


def _pool_window_sums(ext, tt):
    s2 = ext + pltpu.roll(ext, 1, 0)
    s4 = s2 + pltpu.roll(s2, 2, 0)
    s8 = s4 + pltpu.roll(s4, 4, 0)
    s16 = s8 + pltpu.roll(s8, 8, 0)
    grp = lax.broadcasted_iota(jnp.int32, (tt, W_B), 1) // POOL_GROUP
    lo = ext.shape[0] - tt
    return jnp.where(grp == 0, s2[lo:], jnp.where(grp == 1, s4[lo:], jnp.where(grp == 2, s8[lo:], s16[lo:])))


def _abc_prompt_kernel(x_ref, lng_ref, lnb_ref, ws_ref, bias_ref, wpool_ref, pscale_ref, cw_ref,
                       mix_ref, vlast_ref, pool_ref, conv_ref, pcar_ref, ccar_ref, *, tt):
    t = pl.program_id(1)
    nt = pl.num_programs(1)

    @pl.when(t == 0)
    def _():
        pcar_ref[...] = jnp.zeros_like(pcar_ref)
        ccar_ref[...] = jnp.zeros_like(ccar_ref)

    x = x_ref[0]

    ga = _gelu(x[:, :2 * W_A])
    u = ga[:, :W_A]
    v = _ln(ga[:, W_A:], lng_ref[...], lnb_ref[...])
    rows = lax.broadcasted_iota(jnp.int32, (A_HEADS * CHUNK, CHUNK), 0) % CHUNK
    cols = lax.broadcasted_iota(jnp.int32, (A_HEADS * CHUNK, CHUNK), 1)
    wsm = jnp.where(rows >= cols, ws_ref[...], 0.0).astype(BF16)
    hid = lax.broadcasted_iota(jnp.int32, (CHUNK, W_A), 1) // A_HEAD_DIM
    for c in range(tt // CHUNK):
        rs = slice(c * CHUNK, (c + 1) * CHUNK)
        zz = _dot(wsm, v[rs].astype(BF16))
        z = zz[(A_HEADS - 1) * CHUNK:]
        for hd in range(A_HEADS - 2, -1, -1):
            z = jnp.where(hid == hd, zz[hd * CHUNK:(hd + 1) * CHUNK], z)
        mix_ref[0, rs, 0:W_A] = u[rs] * (z + bias_ref[...])

    @pl.when(t == nt - 1)
    def _():
        vlast_ref[0] = v[tt - CHUNK:]

    xb = x[:, 2 * W_A:2 * W_A + W_B]
    ext = jnp.concatenate([pcar_ref[...], xb], axis=0)
    sums = _pool_window_sums(ext, tt)
    pos = t * tt + lax.broadcasted_iota(jnp.int32, (tt, W_B), 0)
    win = jnp.left_shift(2, lax.broadcasted_iota(jnp.int32, (tt, W_B), 1) // POOL_GROUP)
    cnt = jnp.minimum(win, pos + 1).astype(F32)
    d = sums / cnt - xb
    mix_ref[0, :, W_A:W_A + W_B] = _dot(d.astype(BF16), wpool_ref[...]) * pscale_ref[...]
    pcar_ref[...] = ext[tt:]

    o = 2 * W_A + W_B
    bg = x[:, o:o + W_C]
    zc = x[:, o + W_C:o + 2 * W_C] * x[:, o + 2 * W_C:o + 3 * W_C]
    extz = jnp.concatenate([ccar_ref[...], zc], axis=0)
    y = (cw_ref[0:1, :] * pltpu.roll(extz, 2, 0) + cw_ref[1:2, :] * pltpu.roll(extz, 1, 0)
         + cw_ref[2:3, :] * extz)
    mix_ref[0, :, W_A + W_B:] = bg * y[ROW_CARRY:]
    ccar_ref[...] = extz[tt:]

    @pl.when(t == nt - 1)
    def _():
        pool_ref[0] = pcar_ref[POOL_CARRY - POOL_BUF:, :]
        conv_ref[0] = ccar_ref[ROW_CARRY - (CONV_W - 1):, :]


def _abc_prompt(pabc, lng, lnb, ws, bias, wpool, pscale, cw, tt):
    bn, t, _ = pabc.shape
    nw = W_A + W_B + W_C
    nbytes = 2 * tt * N_ABC * 4 + 2 * tt * nw * 4 + 12 * tt * W_A * 4 + 4 * A_HEADS * CHUNK * CHUNK * 4
    const = lambda i, j: (0, 0)
    return pl.pallas_call(
        functools.partial(_abc_prompt_kernel, tt=tt),
        out_shape=(jax.ShapeDtypeStruct((bn, t, nw), F32),
                   jax.ShapeDtypeStruct((bn, CHUNK, W_A), F32),
                   jax.ShapeDtypeStruct((bn, POOL_BUF, W_B), F32),
                   jax.ShapeDtypeStruct((bn, CONV_W - 1, W_C), F32)),
        grid=(bn, t // tt),
        in_specs=[pl.BlockSpec((1, tt, N_ABC), lambda i, j: (i, j, 0)),
                  pl.BlockSpec((1, W_A), const), pl.BlockSpec((1, W_A), const),
                  pl.BlockSpec((A_HEADS * CHUNK, CHUNK), const),
                  pl.BlockSpec((CHUNK, W_A), const),
                  pl.BlockSpec((W_B, W_B), const), pl.BlockSpec((1, W_B), const),
                  pl.BlockSpec((CONV_W, W_C), const)],
        out_specs=(pl.BlockSpec((1, tt, nw), lambda i, j: (i, j, 0)),
                   pl.BlockSpec((1, CHUNK, W_A), lambda i, j: (i, 0, 0)),
                   pl.BlockSpec((1, POOL_BUF, W_B), lambda i, j: (i, 0, 0)),
                   pl.BlockSpec((1, CONV_W - 1, W_C), lambda i, j: (i, 0, 0))),
        scratch_shapes=[pltpu.VMEM((POOL_CARRY, W_B), F32), pltpu.VMEM((ROW_CARRY, W_C), F32)],
        compiler_params=_params(("parallel", "arbitrary"), nbytes),
        name="mixer_abc_prompt",
    )(pabc, lng, lnb, ws, bias, wpool, pscale, cw)


def _abc_sample_kernel(x_ref, pool_ref, conv_ref, lng_ref, lnb_ref, ws0_ref, bias0_ref, wpool_ref,
                       pscale_ref, cw_ref, mix_ref, v_ref, poolo_ref, convo_ref, *, pos0):
    x = x_ref[...]
    n = x.shape[0]
    ga = _gelu(x[:, :2 * W_A])
    u = ga[:, :W_A]
    v = _ln(ga[:, W_A:], lng_ref[...], lnb_ref[...])
    v_ref[...] = v
    mix_ref[:, 0:W_A] = u * (v * ws0_ref[...] + bias0_ref[...])

    xb = x[:, 2 * W_A:2 * W_A + W_B]
    grp = lax.broadcasted_iota(jnp.int32, (n, W_B), 1) // POOL_GROUP
    run = xb
    sums = jnp.zeros_like(xb)
    for back in range(1, max(POOL_WINDOWS)):
        r = POOL_BUF - back
        run = run + pool_ref[:, r * W_B:(r + 1) * W_B]
        for gi, win in enumerate(POOL_WINDOWS):
            if back == win - 1:
                sums = jnp.where(grp == gi, run, sums)
    win = jnp.left_shift(2, grp)
    cnt = jnp.minimum(win, pos0 + 1).astype(F32)
    d = sums / cnt - xb
    mix_ref[:, W_A:W_A + W_B] = _dot(d.astype(BF16), wpool_ref[...]) * pscale_ref[...]
    poolo_ref[:, :(POOL_BUF - 1) * W_B] = pool_ref[:, W_B:]
    poolo_ref[:, (POOL_BUF - 1) * W_B:] = xb

    o = 2 * W_A + W_B
    bg = x[:, o:o + W_C]
    zc = x[:, o + W_C:o + 2 * W_C] * x[:, o + 2 * W_C:o + 3 * W_C]
    z0 = conv_ref[:, :W_C]
    z1 = conv_ref[:, W_C:]
    y = cw_ref[0:1, :] * z0 + cw_ref[1:2, :] * z1 + cw_ref[2:3, :] * zc
    mix_ref[:, W_A + W_B:] = bg * y
    convo_ref[:, :W_C] = z1
    convo_ref[:, W_C:] = zc


def _abc_sample(pabc, pool, conv, lng, lnb, ws0, bias0, wpool, pscale, cw, pos0):
    n = pabc.shape[0]
    nw = W_A + W_B + W_C
    return pl.pallas_call(
        functools.partial(_abc_sample_kernel, pos0=pos0),
        out_shape=(jax.ShapeDtypeStruct((n, nw), F32),
                   jax.ShapeDtypeStruct((n, W_A), F32),
                   jax.ShapeDtypeStruct(pool.shape, F32),
                   jax.ShapeDtypeStruct(conv.shape, F32)),
        name="mixer_abc_sample",
    )(pabc, pool, conv, lng, lnb, ws0, bias0, wpool, pscale, cw)


def _rwkv_inputs(xs, w0, w2, a0, a2, g2, kk_w, ka_w, seg):
    r = xs[:, 0:W_D]
    k = xs[:, W_D:2 * W_D]
    v = xs[:, 2 * W_D:3 * W_D]
    o = 3 * W_D
    dw = xs[:, o:o + R_DECAY]
    da = xs[:, o + R_DECAY:o + R_DECAY + R_AAA]
    dg = xs[:, o + R_DECAY + R_AAA:]
    w_log = -_softplus(-(w0 + _dot_hi(jnp.tanh(dw), w2))) - 0.5
    logdecay = -jnp.exp(w_log)
    a = _sigmoid(a0 + _dot_hi(da, a2))
    g = _dot_hi(_sigmoid(dg), g2)
    kk = k * kk_w
    kk = kk * lax.rsqrt(jnp.maximum(_dot_ones(kk * kk, seg), 1e-12))
    k = k * (1.0 + (a - 1.0) * ka_w)
    return r, k, v, kk, a, logdecay, g


def _rwkv_finish(o, r, k, v, g, rk_w, lnx_g, lnx_b, seg):
    inv = 1.0 / D_HEAD_DIM
    mu = _dot_ones(o, seg) * inv
    oc = o - mu
    var = _dot_ones(oc * oc, seg) * inv
    on = oc * lax.rsqrt(var + GN_EPS) * lnx_g + lnx_b
    bonus = _dot_ones(r * k * rk_w, seg) * v
    return (on + bonus) * g


def _bdot(a, b, dims=NN):
    return _dot(a.astype(BF16), b.astype(BF16), dims)


def _head_cols(x, hd):
    return x[:, hd * D_HEAD_DIM:(hd + 1) * D_HEAD_DIM]


HEAD_PAIRS = W_D // LANES


def _pair_diag(y, low):
    zero = jnp.zeros_like(y)
    return jnp.concatenate([jnp.where(low, y, zero), jnp.where(low, zero, y)], axis=0)


def _wkv_tile(s_ref, o_ref, qt, rt, bt, kt, bbar, kbar, vm, gam, nc):
    c = WKV_CHUNK
    n = D_HEAD_DIM
    probs = [(ch, pr) for ch in range(nc) for pr in range(HEAD_PAIRS)]
    cut = lambda x, p: x[p[0] * c:(p[0] + 1) * c, p[1] * LANES:(p[1] + 1) * LANES]
    qt, rt, bt, kt, bbar, kbar, vm = (x.astype(BF16) for x in (qt, rt, bt, kt, bbar, kbar, vm))
    ri = lax.broadcasted_iota(jnp.int32, (c, LANES), 0)
    li = lax.broadcasted_iota(jnp.int32, (c, LANES), 1)
    low = li < n
    strict = ri > li % n
    incl = ri >= li % n
    eye = jnp.where(ri == li % n, 1.0, 0.0)
    diag = lambda y: _pair_diag(y, low)
    halves = lambda x: jnp.where(low, x[:n], x[n:])

    qs = {p: cut(qt, p) for p in probs}
    vd = {p: diag(cut(vm, p)) for p in probs}
    aa = {p: _dot(jnp.concatenate([qs[p], cut(rt, p)], axis=0),
                  jnp.concatenate([diag(cut(bt, p)), diag(cut(kt, p))], axis=0), NT) for p in probs}
    lk = {p: jnp.where(strict, aa[p][:c, LANES:], 0.0).astype(BF16) for p in probs}
    ab = {p: jnp.where(incl, aa[p][c:, :LANES], 0.0).astype(BF16) for p in probs}
    ak = {p: jnp.where(incl, aa[p][c:, LANES:], 0.0).astype(BF16) for p in probs}
    npow = {p: jnp.where(strict, -aa[p][:c, :LANES], 0.0).astype(BF16) for p in probs}
    tinv = {p: eye + npow[p].astype(F32) for p in probs}
    lv = {p: _dot(jnp.concatenate([lk[p], ak[p]], axis=0), vd[p]) for p in probs}
    zp = {p: lv[p][:c].astype(BF16) for p in probs}
    npow = {p: _dot(npow[p], diag(npow[p])).astype(BF16) for p in probs}
    for _ in range(int(math.log2(c)) - 2):
        both = {p: _dot(jnp.concatenate([tinv[p].astype(BF16), npow[p]], axis=0), diag(npow[p])) for p in probs}
        tinv = {p: tinv[p] + both[p][:c] for p in probs}
        npow = {p: both[p][c:].astype(BF16) for p in probs}
    tinv = {p: (tinv[p] + _dot(tinv[p].astype(BF16), diag(npow[p]))).astype(BF16) for p in probs}
    tq = {p: _dot(tinv[p], jnp.concatenate([diag(qs[p]), diag(zp[p])], axis=1)) for p in probs}
    qh = {p: tq[p][:, :LANES].astype(BF16) for p in probs}
    zn = {p: (-tq[p][:, LANES:]).astype(BF16) for p in probs}
    abq = {p: _dot(ab[p], jnp.concatenate([diag(qh[p]), diag(zn[p])], axis=1)) for p in probs}
    rh = {p: (cut(rt, p).astype(F32) - abq[p][:, :LANES]).astype(BF16) for p in probs}
    pv = {p: abq[p][:, LANES:] + lv[p][c:] for p in probs}
    gt = {p: halves(-_dot(qh[p], cut(bbar, p), TN)).astype(BF16) for p in probs}
    cst = {p: halves(_dot(jnp.concatenate([cut(vm, p), zn[p]], axis=0),
                          jnp.concatenate([cut(kbar, p), cut(bbar, p)], axis=0), TN)) for p in probs}
    for ch in range(nc):
        for pr in range(HEAD_PAIRS):
            p = (ch, pr)
            s0 = s_ref[pr]
            s0b = s0.astype(BF16)
            o_ref[ch * c:(ch + 1) * c, pr * LANES:(pr + 1) * LANES] = _dot(rh[p], diag(s0b), NT) + pv[p]
            s_ref[pr] = s0 * gam[ch][:, pr * LANES:(pr + 1) * LANES] + _dot(s0b, diag(gt[p])) + cst[p]


def _rwkv_prompt_kernel(pd_ref, mu_ref, w0_ref, w2_ref, a0_ref, a2_ref, g2_ref, kk_ref, ka_ref, rk_ref,
                        lg_ref, lb_ref, yd_ref, shift_ref, wkv_ref, car_ref, s_ref, o_ref, *, tt):
    t = pl.program_id(1)
    nt = pl.num_programs(1)

    @pl.when(t == 0)
    def _():
        car_ref[...] = jnp.zeros_like(car_ref)
        s_ref[...] = jnp.zeros_like(s_ref)

    pd = pd_ref[0]
    ext = jnp.concatenate([car_ref[...], pd], axis=0)
    prev = pltpu.roll(ext, 1, 0)[ROW_CARRY:]
    car_ref[...] = ext[tt:]
    xs = pd + (prev - pd) * mu_ref[...]
    seg = _head_ones(W_D, D_HEAD_DIM)
    r, k, v, kk, a, ld, g = _rwkv_inputs(xs, w0_ref[...], w2_ref[...], a0_ref[...], a2_ref[...],
                                         g2_ref[...], kk_ref[...], ka_ref[...], seg)
    c = WKV_CHUNK
    nc = tt // c
    tri = jnp.where(lax.broadcasted_iota(jnp.int32, (c, c), 0) >= lax.broadcasted_iota(jnp.int32, (c, c), 1),
                    1.0, 0.0).astype(BF16)
    cs_chunks = [_ones_dot(tri, ld[ch * c:(ch + 1) * c]) for ch in range(nc)]
    cs = jnp.concatenate(cs_chunks, axis=0)
    cs_end = [x[c - 1:c] for x in cs_chunks]
    cs_last = jnp.concatenate([jnp.broadcast_to(x, (c, W_D)) for x in cs_end], axis=0)
    e_neg = jnp.exp(-cs)
    e_tail = jnp.exp(cs_last - cs)
    b = kk * a
    _wkv_tile(s_ref, o_ref, kk * jnp.exp(cs - ld), r * jnp.exp(cs), b * e_neg, k * e_neg, b * e_tail, k * e_tail,
              v, [jnp.exp(x) for x in cs_end], nc)
    yd_ref[0] = _rwkv_finish(o_ref[...], r, k, v, g, rk_ref[...], lg_ref[...], lb_ref[...], seg)

    @pl.when(t == nt - 1)
    def _():
        shift_ref[0] = pd[tt - 1:tt]
        for hd in range(D_HEADS):
            wkv_ref[0, hd] = _head_cols(s_ref[hd // 2], hd % 2)


def _rwkv_prompt(pd, mu, w0, w2, a0, a2, g2, kk_w, ka_w, rk_w, lnx_g, lnx_b, tt):
    bn, t, _ = pd.shape
    nbytes = 2 * tt * D_PROJ * 4 + 40 * tt * W_D * 4
    const = lambda i, j: (0, 0)
    row = lambda n: pl.BlockSpec((1, n), const)
    return pl.pallas_call(
        functools.partial(_rwkv_prompt_kernel, tt=tt),
        out_shape=(jax.ShapeDtypeStruct((bn, t, W_D), F32),
                   jax.ShapeDtypeStruct((bn, 1, D_PROJ), F32),
                   jax.ShapeDtypeStruct((bn, D_HEADS, D_HEAD_DIM, D_HEAD_DIM), F32)),
        grid=(bn, t // tt),
        in_specs=[pl.BlockSpec((1, tt, D_PROJ), lambda i, j: (i, j, 0)),
                  row(D_PROJ), row(W_D), pl.BlockSpec((R_DECAY, W_D), const),
                  row(W_D), pl.BlockSpec((R_AAA, W_D), const), pl.BlockSpec((R_GATE, W_D), const),
                  row(W_D), row(W_D), row(W_D), row(W_D), row(W_D)],
        out_specs=(pl.BlockSpec((1, tt, W_D), lambda i, j: (i, j, 0)),
                   pl.BlockSpec((1, 1, D_PROJ), lambda i, j: (i, 0, 0)),
                   pl.BlockSpec((1, D_HEADS, D_HEAD_DIM, D_HEAD_DIM), lambda i, j: (i, 0, 0, 0))),
        scratch_shapes=[pltpu.VMEM((ROW_CARRY, D_PROJ), F32),
                        pltpu.VMEM((HEAD_PAIRS, D_HEAD_DIM, LANES), F32),
                        pltpu.VMEM((tt, W_D), F32)],
        compiler_params=_params(("parallel", "arbitrary"), nbytes),
        name="rwkv_prompt",
    )(pd, mu, w0, w2, a0, a2, g2, kk_w, ka_w, rk_w, lnx_g, lnx_b)


def _rwkv_sample_kernel(pd_ref, sh_ref, s_ref, mu_ref, w0_ref, w2_ref, a0_ref, a2_ref, g2_ref, kk_ref,
                        ka_ref, rk_ref, lg_ref, lb_ref, yd_ref, so_ref, u_ref, swr_ref, *, bb):
    pd = pd_ref[...]
    xs = pd + (sh_ref[...] - pd) * mu_ref[...]
    seg = _head_ones(W_D, D_HEAD_DIM)
    r, k, v, kk, a, ld, g = _rwkv_inputs(xs, w0_ref[...], w2_ref[...], a0_ref[...], a2_ref[...],
                                         g2_ref[...], kk_ref[...], ka_ref[...], seg)
    w = jnp.exp(ld)
    b = kk * a
    n = D_HEAD_DIM
    heads = range(D_HEADS)
    probs = [(i, hd) for i in range(bb) for hd in heads]
    lhs = [jnp.concatenate([_head_cols(kk, hd), _head_cols(w * r, hd)], axis=0) for hd in heads]
    sk = {p: _dot_hi(lhs[p[1]], s_ref[p[0], p[1]], NT) for p in probs}
    for i, hd in probs:
        u_ref[i:i + 1, hd * n:(hd + 1) * n] = -sk[i, hd][i:i + 1]
        swr_ref[i:i + 1, hd * n:(hd + 1) * n] = sk[i, hd][bb + i:bb + i + 1]
    u = u_ref[...]
    o = swr_ref[...] + u * _dot_ones(b * r, seg) + v * _dot_ones(k * r, seg)
    uv = [jnp.concatenate([_head_cols(u, hd), _head_cols(v, hd)], axis=0) for hd in heads]
    bk = [jnp.concatenate([_head_cols(b, hd), _head_cols(k, hd)], axis=0) for hd in heads]
    rowid = lax.broadcasted_iota(jnp.int32, (2 * bb, n), 0) % bb
    upd = {(i, hd): _dot_hi(jnp.where(rowid == i, uv[hd], 0.0), bk[hd], TN) for i, hd in probs}
    for i, hd in probs:
        so_ref[i, hd] = s_ref[i, hd] * w[i:i + 1, hd * n:(hd + 1) * n] + upd[i, hd]
    yd_ref[...] = _rwkv_finish(o, r, k, v, g, rk_ref[...], lg_ref[...], lb_ref[...], seg)


def _rwkv_sample(pd, shift, state, layer, mu, w0, w2, a0, a2, g2, kk_w, ka_w, rk_w, lnx_g, lnx_b, bb):
    n = pd.shape[0]
    const = lambda i: (0, 0)
    row = lambda m: pl.BlockSpec((1, m), const)
    sshape = (bb, D_HEADS, D_HEAD_DIM, D_HEAD_DIM)
    return pl.pallas_call(
        functools.partial(_rwkv_sample_kernel, bb=bb),
        out_shape=(jax.ShapeDtypeStruct((n, W_D), F32), jax.ShapeDtypeStruct(state.shape[1:], F32)),
        grid=(n // bb,),
        in_specs=[pl.BlockSpec((bb, D_PROJ), lambda i: (i, 0)), pl.BlockSpec((bb, D_PROJ), lambda i: (i, 0)),
                  pl.BlockSpec((None,) + sshape, lambda i: (layer, i, 0, 0, 0)),
                  row(D_PROJ), row(W_D), pl.BlockSpec((R_DECAY, W_D), const),
                  row(W_D), pl.BlockSpec((R_AAA, W_D), const), pl.BlockSpec((R_GATE, W_D), const),
                  row(W_D), row(W_D), row(W_D), row(W_D), row(W_D)],
        out_specs=(pl.BlockSpec((bb, W_D), lambda i: (i, 0)), pl.BlockSpec(sshape, lambda i: (i, 0, 0, 0))),
        scratch_shapes=[pltpu.VMEM((bb, W_D), F32), pltpu.VMEM((bb, W_D), F32)],
        compiler_params=pltpu.CompilerParams(dimension_semantics=("parallel",)),
        name="rwkv_sample",
    )(pd, shift, state, mu, w0, w2, a0, a2, g2, kk_w, ka_w, rk_w, lnx_g, lnx_b)


def _block_diag(w):
    gn, n, _ = w.shape
    eye = jnp.eye(gn, dtype=w.dtype)
    return (eye[:, None, :, None] * w[:, :, None, :]).reshape(gn * n, gn * n)


def kernel(x_prompt, x_sample, mem_prompt, cache_mem_k, cache_mem_v, state_pool, state_conv, state_shift, state_wkv,
           w_in, mu_d, ln_v_g, ln_v_b, ws_chunk, b_chunk, w_pool, pool_scale, conv_w,
           rwkv_w0, rwkv_w2, rwkv_a0, rwkv_a2, rwkv_g2, rwkv_k_k, rwkv_k_a, rwkv_r_k, rwkv_lnx_g, rwkv_lnx_b,
           w_out, ln1_g, ln1_b, w_xq, w_xk, w_xv, w_xo, ln2_g, ln2_b, ffn_w1, ffn_w3, ffn_w2, ln3_g, ln3_b):
    bp, t_p, d = x_prompt.shape
    ns, t_s, _ = x_sample.shape
    depth = w_in.shape[0]
    assert d == D_MODEL and t_s == 1 and t_p % CHUNK == 0 and w_in.shape[2] == PROJ
    alpha = (2 * depth) ** 0.25
    mp = bp * t_p
    row = lambda z: z.reshape(1, -1)

    hp = x_prompt.reshape(mp, d)
    hs = x_sample.reshape(ns, d)
    mem = mem_prompt.reshape(bp * MEM_LEN, d)
    outs = [[] for _ in range(12)]
    for l in range(depth):
        w_in_b = w_in[l].astype(BF16)
        w_out_b = w_out[l].astype(BF16)
        w_xq_b, w_xk_b = w_xq[l].astype(BF16), w_xk[l].astype(BF16)
        w_xv_b, w_xo_b = w_xv[l].astype(BF16), w_xo[l].astype(BF16)
        w1_b, w3_b, w2_b = ffn_w1[l].astype(BF16), ffn_w3[l].astype(BF16), ffn_w2[l].astype(BF16)
        ws_flat = ws_chunk[l].reshape(A_HEADS * CHUNK, CHUNK)
        bias_full = jnp.repeat(b_chunk[l].T, A_HEAD_DIM, axis=1)
        ws0 = jnp.repeat(ws_chunk[l][:, 0, 0], A_HEAD_DIM).reshape(1, W_A)
        wpool_bd = _block_diag(w_pool[l]).astype(BF16)
        abc_w = (row(ln_v_g[l]), row(ln_v_b[l]))
        abc_w2 = (wpool_bd, row(pool_scale[l]), conv_w[l])
        rw = (row(mu_d[l]), row(rwkv_w0[l]), rwkv_w2[l], row(rwkv_a0[l]), rwkv_a2[l], rwkv_g2[l],
              row(rwkv_k_k[l]), row(rwkv_k_a[l]), row(rwkv_r_k[l]), row(rwkv_lnx_g[l]), row(rwkv_lnx_b[l]))
        ln1 = (row(ln1_g[l]), row(ln1_b[l]))
        ln2 = (row(ln2_g[l]), row(ln2_b[l]))
        ln3 = (row(ln3_g[l]), row(ln3_b[l]))

        mk_p = _mm(mem, w_xk_b, 512, "mem_k")
        mv_p = _mm(mem, w_xv_b, 512, "mem_v")
        pabc, pd = _proj(hp, w_in_b, 512)
        mix, v_last, pool_p, conv_p = _abc_prompt(pabc.reshape(bp, t_p, N_ABC), *abc_w, ws_flat, bias_full,
                                                  *abc_w2, tt=256)
        yd, shift_p, wkv_p = _rwkv_prompt(pd.reshape(bp, t_p, D_PROJ), *rw, tt=256)
        nw = W_A + W_B + W_C
        hp = _mm_res_ln([mix.reshape(mp, nw), yd.reshape(mp, W_D)], [w_out_b[:nw], w_out_b[nw:]], hp, *ln1,
                        tm=512, alpha=alpha, name="out_proj")
        hp = _attn_prompt(hp.reshape(bp, t_p, d), mk_p.reshape(bp, MEM_LEN, d), mv_p.reshape(bp, MEM_LEN, d),
                          w_xq_b, w_xo_b, *ln2, tq=512, alpha=alpha).reshape(mp, d)
        hp = _ffn(hp, w1_b, w3_b, w2_b, *ln3, tm=1024, tf=256, alpha=alpha, name="ffn")
        for lst, val in zip(outs[:7], (v_last, pool_p, conv_p, shift_p, wkv_p,
                                       mk_p.reshape(bp, MEM_LEN, X_HEADS, X_HEAD_DIM),
                                       mv_p.reshape(bp, MEM_LEN, X_HEADS, X_HEAD_DIM))):
            lst.append(val)

        pabc_s, pd_s = _proj(hs, w_in_b, ns)
        mix_s, v_s, pool_s, conv_s = _abc_sample(
            pabc_s, state_pool[l].reshape(ns, POOL_BUF * W_B), state_conv[l].reshape(ns, (CONV_W - 1) * W_C),
            *abc_w, ws0, bias_full[0:1], *abc_w2, pos0=PAST_LEN)
        yd_s, wkv_s = _rwkv_sample(pd_s, state_shift[l].reshape(ns, D_PROJ), state_wkv, l, *rw, bb=8)
        hs = _mm_res_ln([mix_s, yd_s], [w_out_b[:nw], w_out_b[nw:]], hs, *ln1, tm=ns, alpha=alpha,
                        name="out_proj_s")
        q_s = _mm(hs, w_xq_b, ns, "q_s")
        o_s = _attn_sample(q_s, cache_mem_k, cache_mem_v, l, bb=4)
        hs = _mm_res_ln([o_s], [w_xo_b], hs, *ln2, tm=ns, alpha=alpha, name="xo_s")
        hs = _ffn(hs, w1_b, w3_b, w2_b, *ln3, tm=ns, tf=256, alpha=alpha, name="ffn_s")
        for lst, val in zip(outs[7:], (v_s.reshape(ns, 1, W_A), pool_s.reshape(ns, POOL_BUF, W_B),
                                       conv_s.reshape(ns, CONV_W - 1, W_C), pd_s.reshape(ns, 1, D_PROJ), wkv_s)):
            lst.append(val)

    return (hp.reshape(bp, t_p, d), hs.reshape(ns, 1, d)) + tuple(jnp.stack(o) for o in outs)
```

```python
import functools
import math

import jax
import jax.numpy as jnp
from jax import lax
from jax.experimental import pallas as pl
from jax.experimental.pallas import tpu as pltpu

F32 = jnp.float32
BF16 = jnp.bfloat16

D_MODEL = 1024
W_A = 256
W_B = 256
W_C = 256
W_D = 256
A_HEADS = 4
A_HEAD_DIM = W_A // A_HEADS
CHUNK = 128
POOL_WINDOWS = (2, 4, 8, 16)
POOL_GROUP = W_B // len(POOL_WINDOWS)
POOL_BUF = max(POOL_WINDOWS) - 1
CONV_W = 3
D_HEAD_DIM = 64
D_HEADS = W_D // D_HEAD_DIM
R_DECAY = 32
R_AAA = 32
R_GATE = 64
D_PROJ = 3 * W_D + R_DECAY + R_AAA + R_GATE
N_ABC = 2 * W_A + W_B + 3 * W_C
PROJ = N_ABC + D_PROJ
MEM_LEN = 256
X_HEADS = 4
X_HEAD_DIM = D_MODEL // X_HEADS
D_FF = int(math.ceil(8 * D_MODEL / 3 / 256)) * 256
PAST_LEN = 16384
LN_EPS = 1e-5
GN_EPS = 64e-5

WKV_CHUNK = 64
POOL_CARRY = 24
ROW_CARRY = 8
V7X_VMEM_BYTES = 64 * 1024 * 1024
VMEM_CAP = V7X_VMEM_BYTES - 8 * 1024 * 1024

NN = (((1,), (0,)), ((), ()))
NT = (((1,), (1,)), ((), ()))
TN = (((0,), (0,)), ((), ()))


def _vmem_limit(nbytes):
    return int(min(VMEM_CAP, max(32 * 1024 * 1024, 2 * nbytes)))


def _params(sem, nbytes):
    return pltpu.CompilerParams(dimension_semantics=sem, vmem_limit_bytes=_vmem_limit(nbytes))


def _lspec(arr, layer):
    tail = arr.shape[1:]
    zeros = (0,) * len(tail)
    return pl.BlockSpec((None,) + tail, lambda *_: (layer,) + zeros)


def _dot(a, b, dims=NN):
    return lax.dot_general(a, b, dims, preferred_element_type=F32)


def _split2(a):
    hi = a.astype(BF16)
    lo = (a - hi.astype(F32)).astype(BF16)
    return hi, lo


def _dot_hi(a, b, dims=NN):
    ah, al = _split2(a)
    bh, bl = _split2(b)
    return _dot(ah, bh, dims) + _dot(ah, bl, dims) + _dot(al, bh, dims)


def _dot_ones(x, ones_bf16, dims=NN):
    hi, lo = _split2(x)
    return _dot(hi, ones_bf16, dims) + _dot(lo, ones_bf16, dims)


def _ones_dot(ones_bf16, x):
    hi = x.astype(BF16)
    r1 = x - hi.astype(F32)
    mid = r1.astype(BF16)
    lo = (r1 - mid.astype(F32)).astype(BF16)
    return _dot(ones_bf16, hi) + _dot(ones_bf16, mid) + _dot(ones_bf16, lo)


def _ln(x, g, b, eps=LN_EPS):
    mu = jnp.mean(x, axis=-1, keepdims=True)
    xc = x - mu
    var = jnp.mean(xc * xc, axis=-1, keepdims=True)
    return xc * lax.rsqrt(var + eps) * g + b


def _gelu(x):
    c = math.sqrt(2.0 / math.pi)
    return x * (0.5 * (1.0 + jnp.tanh(c * (x + 0.044715 * (x * x * x)))))


def _sigmoid(x):
    return 1.0 / (1.0 + jnp.exp(-x))


def _softplus(x):
    return jnp.maximum(x, 0.0) + jnp.log(1.0 + jnp.exp(-jnp.abs(x)))


def _head_ones(n, group):
    r = lax.broadcasted_iota(jnp.int32, (n, n), 0) // group
    c = lax.broadcasted_iota(jnp.int32, (n, n), 1) // group
    return jnp.where(r == c, 1.0, 0.0).astype(BF16)


def _proj_kernel(x_ref, w_ref, oabc_ref, od_ref):
    y = _dot(x_ref[...].astype(BF16), w_ref[...])
    oabc_ref[...] = y[:, :N_ABC]
    od_ref[...] = y[:, N_ABC:]


def _proj(x, w, layer, tm):
    m, k = x.shape
    nbytes = 2 * (tm * k * 4 + k * PROJ * 2 + tm * PROJ * 4) + tm * PROJ * 4
    return pl.pallas_call(
        _proj_kernel,
        out_shape=(jax.ShapeDtypeStruct((m, N_ABC), F32), jax.ShapeDtypeStruct((m, D_PROJ), F32)),
        grid=(m // tm,),
        in_specs=[pl.BlockSpec((tm, k), lambda i: (i, 0)), _lspec(w, layer)],
        out_specs=(pl.BlockSpec((tm, N_ABC), lambda i: (i, 0)), pl.BlockSpec((tm, D_PROJ), lambda i: (i, 0))),
        compiler_params=_params(("parallel",), nbytes),
        name="proj",
    )(x, w)


def _mm_kernel(x_ref, w_ref, o_ref):
    o_ref[...] = _dot(x_ref[...].astype(BF16), w_ref[...])


def _mm(x, w, layer, tm, name):
    m, k = x.shape
    n = w.shape[2]
    nbytes = 2 * (tm * k * 4 + k * n * 2 + tm * n * 4) + tm * n * 4
    return pl.pallas_call(
        _mm_kernel,
        out_shape=jax.ShapeDtypeStruct((m, n), F32),
        grid=(m // tm,),
        in_specs=[pl.BlockSpec((tm, k), lambda i: (i, 0)), _lspec(w, layer)],
        out_specs=pl.BlockSpec((tm, n), lambda i: (i, 0)),
        compiler_params=_params(("parallel",), nbytes),
        name=name,
    )(x, w)


def _mm_res_ln_kernel(*refs, n_in, alpha):
    xs = refs[:n_in]
    ws = refs[n_in:2 * n_in]
    h_ref, g_ref, b_ref, o_ref = refs[2 * n_in:]
    y = _dot(xs[0][...].astype(BF16), ws[0][...])
    for x_ref, w_ref in zip(xs[1:], ws[1:]):
        y = y + _dot(x_ref[...].astype(BF16), w_ref[...])
    o_ref[...] = _ln(alpha * h_ref[...] + y, g_ref[...], b_ref[...])


def _mm_res_ln(xs, w, layer, h, g, b, tm, alpha, name):
    m, n = h.shape
    nbytes = 2 * sum(tm * x.shape[1] * 4 + x.shape[1] * n * 2 for x in xs) + 5 * tm * n * 4
    in_specs = [pl.BlockSpec((tm, x.shape[1]), lambda i: (i, 0)) for x in xs]
    start = 0
    for x in xs:
        width = x.shape[1]
        assert start % width == 0
        in_specs.append(pl.BlockSpec((None, width, n), lambda i, blk=start // width: (layer, blk, 0)))
        start += width
    assert start == w.shape[1]
    in_specs += [pl.BlockSpec((tm, n), lambda i: (i, 0)), _lspec(g, layer), _lspec(b, layer)]
    return pl.pallas_call(
        functools.partial(_mm_res_ln_kernel, n_in=len(xs), alpha=alpha),
        out_shape=jax.ShapeDtypeStruct((m, n), F32),
        grid=(m // tm,),
        in_specs=in_specs,
        out_specs=pl.BlockSpec((tm, n), lambda i: (i, 0)),
        compiler_params=_params(("parallel",), nbytes),
        name=name,
    )(*xs, *([w] * len(xs)), h, g, b)


def _ffn_kernel(x_ref, w1_ref, w3_ref, w2_ref, g_ref, b_ref, o_ref, xb_ref, acc_ref, *, alpha):
    j = pl.program_id(1)

    @pl.when(j == 0)
    def _():
        xb_ref[...] = x_ref[...].astype(BF16)
        acc_ref[...] = jnp.zeros_like(acc_ref)

    xb = xb_ref[...]
    h1 = _dot(xb, w1_ref[...])
    h3 = _dot(xb, w3_ref[...])
    a = (h1 * _sigmoid(h1) * h3).astype(BF16)
    acc_ref[...] += _dot(a, w2_ref[...])

    @pl.when(j == pl.num_programs(1) - 1)
    def _():
        o_ref[...] = _ln(alpha * x_ref[...] + acc_ref[...], g_ref[...], b_ref[...])


def _ffn(x, w1, w3, w2, g, b, layer, tm, tf, alpha, name):
    m, d = x.shape
    nbytes = 4 * tm * d * 4 + tm * d * 2 + tm * d * 4 + 2 * 3 * d * tf * 2 + 3 * tm * tf * 4
    return pl.pallas_call(
        functools.partial(_ffn_kernel, alpha=alpha),
        out_shape=jax.ShapeDtypeStruct((m, d), F32),
        grid=(m // tm, D_FF // tf),
        in_specs=[pl.BlockSpec((tm, d), lambda i, j: (i, 0)),
                  pl.BlockSpec((None, d, tf), lambda i, j: (layer, 0, j)),
                  pl.BlockSpec((None, d, tf), lambda i, j: (layer, 0, j)),
                  pl.BlockSpec((None, tf, d), lambda i, j: (layer, j, 0)),
                  _lspec(g, layer), _lspec(b, layer)],
        out_specs=pl.BlockSpec((tm, d), lambda i, j: (i, 0)),
        scratch_shapes=[pltpu.VMEM((tm, d), BF16), pltpu.VMEM((tm, d), F32)],
        compiler_params=_params(("parallel", "arbitrary"), nbytes),
        name=name,
    )(x, w1, w3, w2, g, b)


def _softmax_rows(s):
    m = jnp.max(s, axis=-1, keepdims=True)
    e = jnp.exp(s - m)
    return e / jnp.sum(e, axis=-1, keepdims=True)


def _attn_prompt_kernel(h_ref, mk_ref, mv_ref, wq_ref, wo_ref, g_ref, b_ref, o_ref, ob_ref, *, alpha):
    h = h_ref[0]
    q = _dot(h.astype(BF16), wq_ref[...])
    scale = X_HEAD_DIM ** -0.5
    sls = [slice(hd * X_HEAD_DIM, (hd + 1) * X_HEAD_DIM) for hd in range(X_HEADS)]
    qb = q.astype(BF16)
    scores = [_dot(qb[:, sl], mk_ref[0, :, sl].astype(BF16), NT) * scale for sl in sls]
    probs = [_softmax_rows(s).astype(BF16) for s in scores]
    for sl, p in zip(sls, probs):
        ob_ref[:, sl] = _dot(p, mv_ref[0, :, sl].astype(BF16)).astype(BF16)
    y = _dot(ob_ref[...], wo_ref[...])
    o_ref[0] = _ln(alpha * h + y, g_ref[...], b_ref[...])


def _attn_prompt(h, mk, mv, wq, wo, g, b, layer, tq, alpha):
    bn, t, d = h.shape
    nbytes = (4 * tq * d * 4 + 4 * MEM_LEN * d * 4 + 4 * d * d * 2 + tq * d * 2
              + 2 * tq * d * 4 + 3 * tq * MEM_LEN * 4)
    return pl.pallas_call(
        functools.partial(_attn_prompt_kernel, alpha=alpha),
        out_shape=jax.ShapeDtypeStruct((bn, t, d), F32),
        grid=(bn, t // tq),
        in_specs=[pl.BlockSpec((1, tq, d), lambda i, j: (i, j, 0)),
                  pl.BlockSpec((1, MEM_LEN, d), lambda i, j: (i, 0, 0)),
                  pl.BlockSpec((1, MEM_LEN, d), lambda i, j: (i, 0, 0)),
                  _lspec(wq, layer), _lspec(wo, layer), _lspec(g, layer), _lspec(b, layer)],
        out_specs=pl.BlockSpec((1, tq, d), lambda i, j: (i, j, 0)),
        scratch_shapes=[pltpu.VMEM((tq, d), BF16)],
        compiler_params=_params(("parallel", "parallel"), nbytes),
        name="attn_prompt",
    )(h, mk, mv, wq, wo, g, b)


LANES = 128
SUBLANES = 8
LANE_TILES = X_HEAD_DIM // LANES
MEM_ROWS = MEM_LEN * LANE_TILES * X_HEADS


def _cache_rows_view(cache):
    nl, n = cache.shape[:2]
    x = cache.reshape(nl, n, MEM_LEN, X_HEADS, LANE_TILES, LANES)
    return x.transpose(0, 1, 2, 4, 3, 5).reshape(nl, n, MEM_ROWS, LANES)


def _attn_sample_kernel(q_ref, k_ref, v_ref, o_ref, *, bb):
    scale = X_HEAD_DIM ** -0.5
    shape = (SUBLANES, MEM_ROWS)
    rowi = lax.broadcasted_iota(jnp.int32, shape, 0)
    coli = lax.broadcasted_iota(jnp.int32, shape, 1)
    valid = (coli % SUBLANES) == rowi
    raw = [_dot(q_ref[i].astype(BF16), k_ref[i].astype(BF16), NT) for i in range(bb)]
    probs = []
    for r in raw:
        r = jnp.where(valid, r, 0.0)
        other = pltpu.roll(r, X_HEADS, 0)
        other = jnp.where(rowi < X_HEADS, pltpu.roll(other, MEM_ROWS - X_HEADS, 1), pltpu.roll(other, X_HEADS, 1))
        s = jnp.where(valid, (r + other) * scale, -jnp.inf)
        m = jnp.max(s, axis=-1, keepdims=True)
        e = jnp.exp(s - m)
        probs.append((e / jnp.sum(e, axis=-1, keepdims=True)).astype(BF16))
    for i in range(bb):
        o_ref[i] = _dot(probs[i], v_ref[i].astype(BF16))


def _attn_sample(q, cache_k, cache_v, layer, bb):
    n = q.shape[0]
    q8 = q.reshape(n, X_HEADS, LANE_TILES, LANES).transpose(0, 2, 1, 3).reshape(n, SUBLANES, LANES)
    nbytes = 4 * bb * MEM_ROWS * LANES * 4 + 2 * bb * MEM_ROWS * LANES * 2 + 8 * SUBLANES * MEM_ROWS * 4
    cache_spec = pl.BlockSpec((None, bb, MEM_ROWS, LANES), lambda i: (layer, i, 0, 0))
    o8 = pl.pallas_call(
        functools.partial(_attn_sample_kernel, bb=bb),
        out_shape=jax.ShapeDtypeStruct((n, SUBLANES, LANES), F32),
        grid=(n // bb,),
        in_specs=[pl.BlockSpec((bb, SUBLANES, LANES), lambda i: (i, 0, 0)), cache_spec, cache_spec],
        out_specs=pl.BlockSpec((bb, SUBLANES, LANES), lambda i: (i, 0, 0)),
        compiler_params=_params(("parallel",), nbytes),
        name="attn_sample",
    )(q8, _cache_rows_view(cache_k), _cache_rows_view(cache_v))
    return o8.reshape(n, LANE_TILES, X_HEADS, LANES).transpose(0, 2, 1, 3).reshape(n, D_MODEL)


def _pool_window_sums(ext, tt):
    s2 = ext + pltpu.roll(ext, 1, 0)
    s4 = s2 + pltpu.roll(s2, 2, 0)
    s8 = s4 + pltpu.roll(s4, 4, 0)
    s16 = s8 + pltpu.roll(s8, 8, 0)
    grp = lax.broadcasted_iota(jnp.int32, (tt, W_B), 1) // POOL_GROUP
    lo = ext.shape[0] - tt
    return jnp.where(grp == 0, s2[lo:], jnp.where(grp == 1, s4[lo:], jnp.where(grp == 2, s8[lo:], s16[lo:])))


def _abc_prompt_kernel(x_ref, lng_ref, lnb_ref, ws_ref, bias_ref, wpool_ref, pscale_ref, cw_ref,
                       mix_ref, vlast_ref, pool_ref, conv_ref, pcar_ref, ccar_ref, *, tt):
    t = pl.program_id(1)
    nt = pl.num_programs(1)

    @pl.when(t == 0)
    def _():
        pcar_ref[...] = jnp.zeros_like(pcar_ref)
        ccar_ref[...] = jnp.zeros_like(ccar_ref)

    x = x_ref[0]

    ga = _gelu(x[:, :2 * W_A])
    u = ga[:, :W_A]
    v = _ln(ga[:, W_A:], lng_ref[...], lnb_ref[...])
    rows = lax.broadcasted_iota(jnp.int32, (A_HEADS * CHUNK, CHUNK), 0) % CHUNK
    cols = lax.broadcasted_iota(jnp.int32, (A_HEADS * CHUNK, CHUNK), 1)
    wsm = jnp.where(rows >= cols, ws_ref[...], 0.0).astype(BF16)
    hid = lax.broadcasted_iota(jnp.int32, (CHUNK, W_A), 1) // A_HEAD_DIM
    for c in range(tt // CHUNK):
        rs = slice(c * CHUNK, (c + 1) * CHUNK)
        zz = _dot(wsm, v[rs].astype(BF16))
        z = zz[(A_HEADS - 1) * CHUNK:]
        for hd in range(A_HEADS - 2, -1, -1):
            z = jnp.where(hid == hd, zz[hd * CHUNK:(hd + 1) * CHUNK], z)
        mix_ref[0, rs, 0:W_A] = u[rs] * (z + bias_ref[...])

    @pl.when(t == nt - 1)
    def _():
        vlast_ref[0] = v[tt - CHUNK:]

    xb = x[:, 2 * W_A:2 * W_A + W_B]
    ext = jnp.concatenate([pcar_ref[...], xb], axis=0)
    sums = _pool_window_sums(ext, tt)
    pos = t * tt + lax.broadcasted_iota(jnp.int32, (tt, W_B), 0)
    win = jnp.left_shift(2, lax.broadcasted_iota(jnp.int32, (tt, W_B), 1) // POOL_GROUP)
    cnt = jnp.minimum(win, pos + 1).astype(F32)
    d = sums / cnt - xb
    mix_ref[0, :, W_A:W_A + W_B] = _dot(d.astype(BF16), wpool_ref[...]) * pscale_ref[...]
    pcar_ref[...] = ext[tt:]

    o = 2 * W_A + W_B
    bg = x[:, o:o + W_C]
    zc = x[:, o + W_C:o + 2 * W_C] * x[:, o + 2 * W_C:o + 3 * W_C]
    extz = jnp.concatenate([ccar_ref[...], zc], axis=0)
    y = (cw_ref[0:1, :] * pltpu.roll(extz, 2, 0) + cw_ref[1:2, :] * pltpu.roll(extz, 1, 0)
         + cw_ref[2:3, :] * extz)
    mix_ref[0, :, W_A + W_B:] = bg * y[ROW_CARRY:]
    ccar_ref[...] = extz[tt:]

    @pl.when(t == nt - 1)
    def _():
        pool_ref[0] = pcar_ref[POOL_CARRY - POOL_BUF:, :]
        conv_ref[0] = ccar_ref[ROW_CARRY - (CONV_W - 1):, :]


def _abc_prompt(pabc, lng, lnb, ws, bias, wpool, pscale, cw, layer, tt):
    bn, t, _ = pabc.shape
    nw = W_A + W_B + W_C
    nbytes = 2 * tt * N_ABC * 4 + 2 * tt * nw * 4 + 12 * tt * W_A * 4 + 4 * A_HEADS * CHUNK * CHUNK * 4
    return pl.pallas_call(
        functools.partial(_abc_prompt_kernel, tt=tt),
        out_shape=(jax.ShapeDtypeStruct((bn, t, nw), F32),
                   jax.ShapeDtypeStruct((bn, CHUNK, W_A), F32),
                   jax.ShapeDtypeStruct((bn, POOL_BUF, W_B), F32),
                   jax.ShapeDtypeStruct((bn, CONV_W - 1, W_C), F32)),
        grid=(bn, t // tt),
        in_specs=[pl.BlockSpec((1, tt, N_ABC), lambda i, j: (i, j, 0)),
                  *[_lspec(z, layer) for z in (lng, lnb, ws, bias, wpool, pscale, cw)]],
        out_specs=(pl.BlockSpec((1, tt, nw), lambda i, j: (i, j, 0)),
                   pl.BlockSpec((1, CHUNK, W_A), lambda i, j: (i, 0, 0)),
                   pl.BlockSpec((1, POOL_BUF, W_B), lambda i, j: (i, 0, 0)),
                   pl.BlockSpec((1, CONV_W - 1, W_C), lambda i, j: (i, 0, 0))),
        scratch_shapes=[pltpu.VMEM((POOL_CARRY, W_B), F32), pltpu.VMEM((ROW_CARRY, W_C), F32)],
        compiler_params=_params(("parallel", "arbitrary"), nbytes),
        name="mixer_abc_prompt",
    )(pabc, lng, lnb, ws, bias, wpool, pscale, cw)


def _abc_sample_kernel(x_ref, pool_ref, conv_ref, lng_ref, lnb_ref, ws0_ref, bias0_ref, wpool_ref,
                       pscale_ref, cw_ref, mix_ref, v_ref, poolo_ref, convo_ref, *, pos0):
    x = x_ref[...]
    n = x.shape[0]
    ga = _gelu(x[:, :2 * W_A])
    u = ga[:, :W_A]
    v = _ln(ga[:, W_A:], lng_ref[...], lnb_ref[...])
    v_ref[...] = v
    mix_ref[:, 0:W_A] = u * (v * ws0_ref[...] + bias0_ref[0:1, :])

    xb = x[:, 2 * W_A:2 * W_A + W_B]
    grp = lax.broadcasted_iota(jnp.int32, (n, W_B), 1) // POOL_GROUP
    run = xb
    sums = jnp.zeros_like(xb)
    for back in range(1, max(POOL_WINDOWS)):
        run = run + pool_ref[POOL_BUF - back]
        for gi, win in enumerate(POOL_WINDOWS):
            if back == win - 1:
                sums = jnp.where(grp == gi, run, sums)
    win = jnp.left_shift(2, grp)
    cnt = jnp.minimum(win, pos0 + 1).astype(F32)
    d = sums / cnt - xb
    mix_ref[:, W_A:W_A + W_B] = _dot(d.astype(BF16), wpool_ref[...]) * pscale_ref[...]
    for r in range(POOL_BUF - 1):
        poolo_ref[r] = pool_ref[r + 1]
    poolo_ref[POOL_BUF - 1] = xb

    o = 2 * W_A + W_B
    bg = x[:, o:o + W_C]
    zc = x[:, o + W_C:o + 2 * W_C] * x[:, o + 2 * W_C:o + 3 * W_C]
    z0 = conv_ref[:, :W_C]
    z1 = conv_ref[:, W_C:]
    y = cw_ref[0:1, :] * z0 + cw_ref[1:2, :] * z1 + cw_ref[2:3, :] * zc
    mix_ref[:, W_A + W_B:] = bg * y
    convo_ref[:, :W_C] = z1
    convo_ref[:, W_C:] = zc


def _abc_sample(pabc, pool, conv, lng, lnb, ws0, bias, wpool, pscale, cw, layer, pos0):
    n = pabc.shape[0]
    nw = W_A + W_B + W_C
    whole = lambda shape: pl.BlockSpec(shape, lambda i: (0,) * len(shape))
    return pl.pallas_call(
        functools.partial(_abc_sample_kernel, pos0=pos0),
        out_shape=(jax.ShapeDtypeStruct((n, nw), F32),
                   jax.ShapeDtypeStruct((n, W_A), F32),
                   jax.ShapeDtypeStruct(pool.shape[1:], F32),
                   jax.ShapeDtypeStruct(conv.shape[1:], F32)),
        grid=(1,),
        in_specs=[whole(pabc.shape)] + [_lspec(z, layer) for z in (pool, conv, lng, lnb, ws0, bias, wpool, pscale, cw)],
        out_specs=(whole((n, nw)), whole((n, W_A)), whole(pool.shape[1:]), whole(conv.shape[1:])),
        name="mixer_abc_sample",
    )(pabc, pool, conv, lng, lnb, ws0, bias, wpool, pscale, cw)


def _rwkv_inputs(xs, w0, w2, a0, a2, g2, kk_w, ka_w, seg):
    r = xs[:, 0:W_D]
    k = xs[:, W_D:2 * W_D]
    v = xs[:, 2 * W_D:3 * W_D]
    o = 3 * W_D
    dw = xs[:, o:o + R_DECAY]
    da = xs[:, o + R_DECAY:o + R_DECAY + R_AAA]
    dg = xs[:, o + R_DECAY + R_AAA:]
    w_log = -_softplus(-(w0 + _dot_hi(jnp.tanh(dw), w2))) - 0.5
    logdecay = -jnp.exp(w_log)
    a = _sigmoid(a0 + _dot_hi(da, a2))
    g = _dot_hi(_sigmoid(dg), g2)
    kk = k * kk_w
    kk = kk * lax.rsqrt(jnp.maximum(_dot_ones(kk * kk, seg), 1e-12))
    k = k * (1.0 + (a - 1.0) * ka_w)
    return r, k, v, kk, a, logdecay, g


def _rwkv_finish(o, r, k, v, g, rk_w, lnx_g, lnx_b, seg):
    inv = 1.0 / D_HEAD_DIM
    mu = _dot_ones(o, seg) * inv
    oc = o - mu
    var = _dot_ones(oc * oc, seg) * inv
    on = oc * lax.rsqrt(var + GN_EPS) * lnx_g + lnx_b
    bonus = _dot_ones(r * k * rk_w, seg) * v
    return (on + bonus) * g


def _bdot(a, b, dims=NN):
    return _dot(a.astype(BF16), b.astype(BF16), dims)


def _head_cols(x, hd):
    return x[:, hd * D_HEAD_DIM:(hd + 1) * D_HEAD_DIM]


HEAD_PAIRS = W_D // LANES


def _pair_diag(y, low):
    zero = jnp.zeros_like(y)
    return jnp.concatenate([jnp.where(low, y, zero), jnp.where(low, zero, y)], axis=0)


def _wkv_tile(s_ref, o_ref, qt, rt, bt, kt, bbar, kbar, vm, gam, nc):
    c = WKV_CHUNK
    n = D_HEAD_DIM
    probs = [(ch, pr) for ch in range(nc) for pr in range(HEAD_PAIRS)]
    cut = lambda x, p: x[p[0] * c:(p[0] + 1) * c, p[1] * LANES:(p[1] + 1) * LANES]
    qt, rt, bt, kt, bbar, kbar, vm = (x.astype(BF16) for x in (qt, rt, bt, kt, bbar, kbar, vm))
    ri = lax.broadcasted_iota(jnp.int32, (c, LANES), 0)
    li = lax.broadcasted_iota(jnp.int32, (c, LANES), 1)
    low = li < n
    strict = ri > li % n
    incl = ri >= li % n
    eye = jnp.where(ri == li % n, 1.0, 0.0)
    diag = lambda y: _pair_diag(y, low)
    halves = lambda x: jnp.where(low, x[:n], x[n:])

    qs = {p: cut(qt, p) for p in probs}
    vd = {p: diag(cut(vm, p)) for p in probs}
    aa = {p: _dot(jnp.concatenate([qs[p], cut(rt, p)], axis=0),
                  jnp.concatenate([diag(cut(bt, p)), diag(cut(kt, p))], axis=0), NT) for p in probs}
    lk = {p: jnp.where(strict, aa[p][:c, LANES:], 0.0).astype(BF16) for p in probs}
    ab = {p: jnp.where(incl, aa[p][c:, :LANES], 0.0).astype(BF16) for p in probs}
    ak = {p: jnp.where(incl, aa[p][c:, LANES:], 0.0).astype(BF16) for p in probs}
    npow = {p: jnp.where(strict, -aa[p][:c, :LANES], 0.0).astype(BF16) for p in probs}
    tinv = {p: eye + npow[p].astype(F32) for p in probs}
    lv = {p: _dot(jnp.concatenate([lk[p], ak[p]], axis=0), vd[p]) for p in probs}
    zp = {p: lv[p][:c].astype(BF16) for p in probs}
    npow = {p: _dot(npow[p], diag(npow[p])).astype(BF16) for p in probs}
    for _ in range(int(math.log2(c)) - 2):
        both = {p: _dot(jnp.concatenate([tinv[p].astype(BF16), npow[p]], axis=0), diag(npow[p])) for p in probs}
        tinv = {p: tinv[p] + both[p][:c] for p in probs}
        npow = {p: both[p][c:].astype(BF16) for p in probs}
    tinv = {p: (tinv[p] + _dot(tinv[p].astype(BF16), diag(npow[p]))).astype(BF16) for p in probs}
    tq = {p: _dot(tinv[p], jnp.concatenate([diag(qs[p]), diag(zp[p])], axis=1)) for p in probs}
    qh = {p: tq[p][:, :LANES].astype(BF16) for p in probs}
    zn = {p: (-tq[p][:, LANES:]).astype(BF16) for p in probs}
    abq = {p: _dot(ab[p], jnp.concatenate([diag(qh[p]), diag(zn[p])], axis=1)) for p in probs}
    rh = {p: (cut(rt, p).astype(F32) - abq[p][:, :LANES]).astype(BF16) for p in probs}
    pv = {p: abq[p][:, LANES:] + lv[p][c:] for p in probs}
    gt = {p: halves(-_dot(qh[p], cut(bbar, p), TN)).astype(BF16) for p in probs}
    cst = {p: halves(_dot(jnp.concatenate([cut(vm, p), zn[p]], axis=0),
                          jnp.concatenate([cut(kbar, p), cut(bbar, p)], axis=0), TN)) for p in probs}
    for ch in range(nc):
        for pr in range(HEAD_PAIRS):
            p = (ch, pr)
            s0 = s_ref[pr]
            s0b = s0.astype(BF16)
            o_ref[ch * c:(ch + 1) * c, pr * LANES:(pr + 1) * LANES] = _dot(rh[p], diag(s0b), NT) + pv[p]
            s_ref[pr] = s0 * gam[ch][:, pr * LANES:(pr + 1) * LANES] + _dot(s0b, diag(gt[p])) + cst[p]


def _rwkv_prompt_kernel(pd_ref, mu_ref, w0_ref, w2_ref, a0_ref, a2_ref, g2_ref, kk_ref, ka_ref, rk_ref,
                        lg_ref, lb_ref, yd_ref, shift_ref, wkv_ref, car_ref, s_ref, o_ref, *, tt):
    t = pl.program_id(1)
    nt = pl.num_programs(1)

    @pl.when(t == 0)
    def _():
        car_ref[...] = jnp.zeros_like(car_ref)
        s_ref[...] = jnp.zeros_like(s_ref)

    pd = pd_ref[0]
    ext = jnp.concatenate([car_ref[...], pd], axis=0)
    prev = pltpu.roll(ext, 1, 0)[ROW_CARRY:]
    car_ref[...] = ext[tt:]
    xs = pd + (prev - pd) * mu_ref[...]
    seg = _head_ones(W_D, D_HEAD_DIM)
    r, k, v, kk, a, ld, g = _rwkv_inputs(xs, w0_ref[...], w2_ref[...], a0_ref[...], a2_ref[...],
                                         g2_ref[...], kk_ref[...], ka_ref[...], seg)
    c = WKV_CHUNK
    nc = tt // c
    tri = jnp.where(lax.broadcasted_iota(jnp.int32, (c, c), 0) >= lax.broadcasted_iota(jnp.int32, (c, c), 1),
                    1.0, 0.0).astype(BF16)
    cs_chunks = [_ones_dot(tri, ld[ch * c:(ch + 1) * c]) for ch in range(nc)]
    cs = jnp.concatenate(cs_chunks, axis=0)
    cs_end = [x[c - 1:c] for x in cs_chunks]
    cs_last = jnp.concatenate([jnp.broadcast_to(x, (c, W_D)) for x in cs_end], axis=0)
    e_neg = jnp.exp(-cs)
    e_tail = jnp.exp(cs_last - cs)
    b = kk * a
    _wkv_tile(s_ref, o_ref, kk * jnp.exp(cs - ld), r * jnp.exp(cs), b * e_neg, k * e_neg, b * e_tail, k * e_tail,
              v, [jnp.exp(x) for x in cs_end], nc)
    yd_ref[0] = _rwkv_finish(o_ref[...], r, k, v, g, rk_ref[...], lg_ref[...], lb_ref[...], seg)

    @pl.when(t == nt - 1)
    def _():
        shift_ref[0] = pd[tt - 1:tt]
        for hd in range(D_HEADS):
            wkv_ref[0, hd] = _head_cols(s_ref[hd // 2], hd % 2)


def _rwkv_prompt(pd, params, layer, tt):
    bn, t, _ = pd.shape
    nbytes = 2 * tt * D_PROJ * 4 + 40 * tt * W_D * 4
    return pl.pallas_call(
        functools.partial(_rwkv_prompt_kernel, tt=tt),
        out_shape=(jax.ShapeDtypeStruct((bn, t, W_D), F32),
                   jax.ShapeDtypeStruct((bn, 1, D_PROJ), F32),
                   jax.ShapeDtypeStruct((bn, D_HEADS, D_HEAD_DIM, D_HEAD_DIM), F32)),
        grid=(bn, t // tt),
        in_specs=[pl.BlockSpec((1, tt, D_PROJ), lambda i, j: (i, j, 0))] + [_lspec(z, layer) for z in params],
        out_specs=(pl.BlockSpec((1, tt, W_D), lambda i, j: (i, j, 0)),
                   pl.BlockSpec((1, 1, D_PROJ), lambda i, j: (i, 0, 0)),
                   pl.BlockSpec((1, D_HEADS, D_HEAD_DIM, D_HEAD_DIM), lambda i, j: (i, 0, 0, 0))),
        scratch_shapes=[pltpu.VMEM((ROW_CARRY, D_PROJ), F32),
                        pltpu.VMEM((HEAD_PAIRS, D_HEAD_DIM, LANES), F32),
                        pltpu.VMEM((tt, W_D), F32)],
        compiler_params=_params(("parallel", "arbitrary"), nbytes),
        name="rwkv_prompt",
    )(pd, *params)


def _rwkv_sample_kernel(pd_ref, sh_ref, st_ref, mu_ref, w0_ref, w2_ref, a0_ref, a2_ref, g2_ref, kk_ref,
                        ka_ref, rk_ref, lg_ref, lb_ref, yd_ref, so_ref, rows_ref, cols_ref, ot_ref):
    h = pl.program_id(0)
    n = D_HEAD_DIM

    @pl.when(h == 0)
    def _():
        pd = pd_ref[...]
        xs = pd + (sh_ref[...] - pd) * mu_ref[...]
        seg = _head_ones(W_D, D_HEAD_DIM)
        r, k, v, kk, a, ld, g = _rwkv_inputs(xs, w0_ref[...], w2_ref[...], a0_ref[...], a2_ref[...],
                                             g2_ref[...], kk_ref[...], ka_ref[...], seg)
        for j, x in enumerate((r, k, v, g)):
            rows_ref[j] = x
        for j, x in enumerate((kk, jnp.exp(ld), kk * a, k, r, v)):
            cols_ref[j] = x.T

    base = pl.multiple_of(h * n, n)
    kap, dec, bvec, kvec, rvec = (cols_ref[j, pl.ds(base, n), :] for j in range(5))
    for vi in range(n):
        s = st_ref[0, vi]
        u = -jnp.sum(s * kap, axis=0, keepdims=True)
        s = s * dec + u * bvec + cols_ref[5, pl.ds(base + vi, 1), :] * kvec
        so_ref[0, vi] = s
        ot_ref[pl.ds(base + vi, 1), :] = jnp.sum(s * rvec, axis=0, keepdims=True)

    @pl.when(h == pl.num_programs(0) - 1)
    def _():
        seg = _head_ones(W_D, D_HEAD_DIM)
        yd_ref[...] = _rwkv_finish(ot_ref[...].T, rows_ref[0], rows_ref[1], rows_ref[2], rows_ref[3],
                                   rk_ref[...], lg_ref[...], lb_ref[...], seg)


def _rwkv_sample(pd, shift, state, params, layer):
    n = pd.shape[0]
    sblock = (1, D_HEAD_DIM, D_HEAD_DIM, n)
    return pl.pallas_call(
        _rwkv_sample_kernel,
        out_shape=(jax.ShapeDtypeStruct((n, W_D), F32), jax.ShapeDtypeStruct(state.shape[1:], F32)),
        grid=(D_HEADS,),
        in_specs=[pl.BlockSpec((n, D_PROJ), lambda i: (0, 0)), _lspec(shift, layer),
                  pl.BlockSpec((None,) + sblock, lambda i: (layer, i, 0, 0, 0))]
                 + [_lspec(z, layer) for z in params],
        out_specs=(pl.BlockSpec((n, W_D), lambda i: (0, 0)), pl.BlockSpec(sblock, lambda i: (i, 0, 0, 0))),
        scratch_shapes=[pltpu.VMEM((4, n, W_D), F32), pltpu.VMEM((6, W_D, n), F32), pltpu.VMEM((W_D, n), F32)],
        compiler_params=pltpu.CompilerParams(dimension_semantics=("arbitrary",)),
        name="rwkv_sample",
    )(pd, shift, state, *params)


def _block_diag(w):
    gn, n, _ = w.shape
    eye = jnp.eye(gn, dtype=w.dtype)
    return (eye[:, None, :, None] * w[:, :, None, :]).reshape(gn * n, gn * n)


def kernel(x_prompt, x_sample, mem_prompt, cache_mem_k, cache_mem_v, state_pool, state_conv, state_shift, state_wkv,
           w_in, mu_d, ln_v_g, ln_v_b, ws_chunk, b_chunk, w_pool, pool_scale, conv_w,
           rwkv_w0, rwkv_w2, rwkv_a0, rwkv_a2, rwkv_g2, rwkv_k_k, rwkv_k_a, rwkv_r_k, rwkv_lnx_g, rwkv_lnx_b,
           w_out, ln1_g, ln1_b, w_xq, w_xk, w_xv, w_xo, ln2_g, ln2_b, ffn_w1, ffn_w3, ffn_w2, ln3_g, ln3_b):
    bp, t_p, d = x_prompt.shape
    ns, t_s, _ = x_sample.shape
    depth = w_in.shape[0]
    assert d == D_MODEL and t_s == 1 and t_p % CHUNK == 0 and w_in.shape[2] == PROJ
    alpha = (2 * depth) ** 0.25
    mp = bp * t_p
    nw = W_A + W_B + W_C
    row = lambda z: z.reshape(depth, 1, -1)

    w_in_b, w_out_b = w_in.astype(BF16), w_out.astype(BF16)
    w_xq_b, w_xk_b, w_xv_b, w_xo_b = (w.astype(BF16) for w in (w_xq, w_xk, w_xv, w_xo))
    w1_b, w3_b, w2_b = ffn_w1.astype(BF16), ffn_w3.astype(BF16), ffn_w2.astype(BF16)
    ws_flat = ws_chunk.reshape(depth, A_HEADS * CHUNK, CHUNK)
    bias_full = jnp.repeat(jnp.swapaxes(b_chunk, 1, 2), A_HEAD_DIM, axis=2)
    ws0 = jnp.repeat(ws_chunk[:, :, 0, 0], A_HEAD_DIM, axis=1).reshape(depth, 1, W_A)
    wpool_bd = jnp.stack([_block_diag(w_pool[l]) for l in range(depth)]).astype(BF16)
    abc_w = (row(ln_v_g), row(ln_v_b))
    abc_w2 = (wpool_bd, row(pool_scale), conv_w)
    rw = (row(mu_d), row(rwkv_w0), rwkv_w2, row(rwkv_a0), rwkv_a2, rwkv_g2,
          row(rwkv_k_k), row(rwkv_k_a), row(rwkv_r_k), row(rwkv_lnx_g), row(rwkv_lnx_b))
    ln1, ln2, ln3 = (row(ln1_g), row(ln1_b)), (row(ln2_g), row(ln2_b)), (row(ln3_g), row(ln3_b))
    pool_view = jnp.swapaxes(state_pool, 1, 2)
    wkv_view = state_wkv.transpose(0, 2, 3, 4, 1)
    conv_view = state_conv.reshape(depth, ns, (CONV_W - 1) * W_C)
    shift_view = state_shift.reshape(depth, ns, D_PROJ)

    hp = x_prompt.reshape(mp, d)
    hs = x_sample.reshape(ns, d)
    mem = mem_prompt.reshape(bp * MEM_LEN, d)
    outs = [[] for _ in range(12)]
    for l in range(depth):
        mk_p = _mm(mem, w_xk_b, l, 512, "mem_k")
        mv_p = _mm(mem, w_xv_b, l, 512, "mem_v")
        pabc, pd = _proj(hp, w_in_b, l, 512)
        mix, v_last, pool_p, conv_p = _abc_prompt(pabc.reshape(bp, t_p, N_ABC), *abc_w, ws_flat, bias_full,
                                                  *abc_w2, l, tt=256)
        yd, shift_p, wkv_p = _rwkv_prompt(pd.reshape(bp, t_p, D_PROJ), rw, l, tt=256)
        hp = _mm_res_ln([mix.reshape(mp, nw), yd.reshape(mp, W_D)], w_out_b, l, hp, *ln1,
                        tm=512, alpha=alpha, name="out_proj")
        hp = _attn_prompt(hp.reshape(bp, t_p, d), mk_p.reshape(bp, MEM_LEN, d), mv_p.reshape(bp, MEM_LEN, d),
                          w_xq_b, w_xo_b, *ln2, l, tq=512, alpha=alpha).reshape(mp, d)
        hp = _ffn(hp, w1_b, w3_b, w2_b, *ln3, l, tm=1024, tf=256, alpha=alpha, name="ffn")
        for lst, val in zip(outs[:7], (v_last, pool_p, conv_p, shift_p, wkv_p,
                                       mk_p.reshape(bp, MEM_LEN, X_HEADS, X_HEAD_DIM),
                                       mv_p.reshape(bp, MEM_LEN, X_HEADS, X_HEAD_DIM))):
            lst.append(val)

        pabc_s, pd_s = _proj(hs, w_in_b, l, ns)
        mix_s, v_s, pool_s, conv_s = _abc_sample(pabc_s, pool_view, conv_view, *abc_w, ws0, bias_full, *abc_w2,
                                                 l, pos0=PAST_LEN)
        yd_s, wkv_s = _rwkv_sample(pd_s, shift_view, wkv_view, rw, l)
        hs = _mm_res_ln([mix_s, yd_s], w_out_b, l, hs, *ln1, tm=ns, alpha=alpha, name="out_proj_s")
        q_s = _mm(hs, w_xq_b, l, ns, "q_s")
        o_s = _attn_sample(q_s, cache_mem_k, cache_mem_v, l, bb=4)
        hs = _mm_res_ln([o_s], w_xo_b, l, hs, *ln2, tm=ns, alpha=alpha, name="xo_s")
        hs = _ffn(hs, w1_b, w3_b, w2_b, *ln3, l, tm=ns, tf=256, alpha=alpha, name="ffn_s")
        for lst, val in zip(outs[7:], (v_s.reshape(ns, 1, W_A), pool_s,
                                       conv_s.reshape(ns, CONV_W - 1, W_C), pd_s.reshape(ns, 1, D_PROJ), wkv_s)):
            lst.append(val)

    stacked = [jnp.stack(o) for o in outs]
    stacked[8] = jnp.swapaxes(stacked[8], 1, 2)
    stacked[11] = stacked[11].transpose(0, 4, 1, 2, 3)
    return (hp.reshape(bp, t_p, d), hs.reshape(ns, 1, d)) + tuple(stacked)
```

```python
import functools
import math

import jax
import jax.numpy as jnp
from jax import lax
from jax.experimental import pallas as pl
from jax.experimental.pallas import tpu as pltpu

F32 = jnp.float32
BF16 = jnp.bfloat16

D_MODEL = 1024
W_A = 256
W_B = 256
W_C = 256
W_D = 256
A_HEADS = 4
A_HEAD_DIM = W_A // A_HEADS
CHUNK = 128
POOL_WINDOWS = (2, 4, 8, 16)
POOL_GROUP = W_B // len(POOL_WINDOWS)
POOL_BUF = max(POOL_WINDOWS) - 1
CONV_W = 3
D_HEAD_DIM = 64
D_HEADS = W_D // D_HEAD_DIM
R_DECAY = 32
R_AAA = 32
R_GATE = 64
D_PROJ = 3 * W_D + R_DECAY + R_AAA + R_GATE
N_ABC = 2 * W_A + W_B + 3 * W_C
PROJ = N_ABC + D_PROJ
MEM_LEN = 256
X_HEADS = 4
X_HEAD_DIM = D_MODEL // X_HEADS
D_FF = int(math.ceil(8 * D_MODEL / 3 / 256)) * 256
PAST_LEN = 16384
LN_EPS = 1e-5
GN_EPS = 64e-5

WKV_CHUNK = 64
POOL_CARRY = 24
ROW_CARRY = 8
V7X_VMEM_BYTES = 64 * 1024 * 1024
VMEM_CAP = V7X_VMEM_BYTES - 8 * 1024 * 1024

NN = (((1,), (0,)), ((), ()))
NT = (((1,), (1,)), ((), ()))
TN = (((0,), (0,)), ((), ()))


def _vmem_limit(nbytes):
    return int(min(VMEM_CAP, max(32 * 1024 * 1024, 2 * nbytes)))


def _params(sem, nbytes):
    return pltpu.CompilerParams(dimension_semantics=sem, vmem_limit_bytes=_vmem_limit(nbytes))


def _lspec(arr, layer):
    tail = arr.shape[1:]
    zeros = (0,) * len(tail)
    return pl.BlockSpec((None,) + tail, lambda *_: (layer,) + zeros)


def _dot(a, b, dims=NN):
    return lax.dot_general(a, b, dims, preferred_element_type=F32)


def _split2(a):
    hi = a.astype(BF16)
    lo = (a - hi.astype(F32)).astype(BF16)
    return hi, lo


def _dot_hi(a, b, dims=NN):
    ah, al = _split2(a)
    bh, bl = _split2(b)
    return _dot(ah, bh, dims) + _dot(ah, bl, dims) + _dot(al, bh, dims)


def _dot_ones(x, ones_bf16, dims=NN):
    hi, lo = _split2(x)
    return _dot(hi, ones_bf16, dims) + _dot(lo, ones_bf16, dims)


def _ones_dot(ones_bf16, x):
    hi = x.astype(BF16)
    r1 = x - hi.astype(F32)
    mid = r1.astype(BF16)
    lo = (r1 - mid.astype(F32)).astype(BF16)
    return _dot(ones_bf16, hi) + _dot(ones_bf16, mid) + _dot(ones_bf16, lo)


def _ln(x, g, b, eps=LN_EPS):
    mu = jnp.mean(x, axis=-1, keepdims=True)
    xc = x - mu
    var = jnp.mean(xc * xc, axis=-1, keepdims=True)
    return xc * lax.rsqrt(var + eps) * g + b


def _gelu(x):
    c = math.sqrt(2.0 / math.pi)
    return x * (0.5 * (1.0 + jnp.tanh(c * (x + 0.044715 * (x * x * x)))))


def _sigmoid(x):
    return 1.0 / (1.0 + jnp.exp(-x))


def _softplus(x):
    return jnp.maximum(x, 0.0) + jnp.log(1.0 + jnp.exp(-jnp.abs(x)))


def _head_ones(n, group):
    r = lax.broadcasted_iota(jnp.int32, (n, n), 0) // group
    c = lax.broadcasted_iota(jnp.int32, (n, n), 1) // group
    return jnp.where(r == c, 1.0, 0.0).astype(BF16)


def _proj_kernel(x_ref, w_ref, oabc_ref, od_ref):
    y = _dot(x_ref[...].astype(BF16), w_ref[...])
    oabc_ref[...] = y[:, :N_ABC]
    od_ref[...] = y[:, N_ABC:]


def _proj(x, w, layer, tm):
    m, k = x.shape
    nbytes = 2 * (tm * k * 4 + k * PROJ * 2 + tm * PROJ * 4) + tm * PROJ * 4
    return pl.pallas_call(
        _proj_kernel,
        out_shape=(jax.ShapeDtypeStruct((m, N_ABC), F32), jax.ShapeDtypeStruct((m, D_PROJ), F32)),
        grid=(m // tm,),
        in_specs=[pl.BlockSpec((tm, k), lambda i: (i, 0)), _lspec(w, layer)],
        out_specs=(pl.BlockSpec((tm, N_ABC), lambda i: (i, 0)), pl.BlockSpec((tm, D_PROJ), lambda i: (i, 0))),
        compiler_params=_params(("parallel",), nbytes),
        name="proj",
    )(x, w)


def _mm_kernel(x_ref, w_ref, o_ref):
    o_ref[...] = _dot(x_ref[...].astype(BF16), w_ref[...])


def _mm(x, w, layer, tm, name):
    m, k = x.shape
    n = w.shape[2]
    nbytes = 2 * (tm * k * 4 + k * n * 2 + tm * n * 4) + tm * n * 4
    return pl.pallas_call(
        _mm_kernel,
        out_shape=jax.ShapeDtypeStruct((m, n), F32),
        grid=(m // tm,),
        in_specs=[pl.BlockSpec((tm, k), lambda i: (i, 0)), _lspec(w, layer)],
        out_specs=pl.BlockSpec((tm, n), lambda i: (i, 0)),
        compiler_params=_params(("parallel",), nbytes),
        name=name,
    )(x, w)


def _mm_res_ln_kernel(*refs, n_in, alpha):
    xs = refs[:n_in]
    ws = refs[n_in:2 * n_in]
    h_ref, g_ref, b_ref, o_ref = refs[2 * n_in:]
    y = _dot(xs[0][...].astype(BF16), ws[0][...])
    for x_ref, w_ref in zip(xs[1:], ws[1:]):
        y = y + _dot(x_ref[...].astype(BF16), w_ref[...])
    o_ref[...] = _ln(alpha * h_ref[...] + y, g_ref[...], b_ref[...])


def _mm_res_ln(xs, w, layer, h, g, b, tm, alpha, name):
    m, n = h.shape
    nbytes = 2 * sum(tm * x.shape[1] * 4 + x.shape[1] * n * 2 for x in xs) + 5 * tm * n * 4
    in_specs = [pl.BlockSpec((tm, x.shape[1]), lambda i: (i, 0)) for x in xs]
    start = 0
    for x in xs:
        width = x.shape[1]
        assert start % width == 0
        in_specs.append(pl.BlockSpec((None, width, n), lambda i, blk=start // width: (layer, blk, 0)))
        start += width
    assert start == w.shape[1]
    in_specs += [pl.BlockSpec((tm, n), lambda i: (i, 0)), _lspec(g, layer), _lspec(b, layer)]
    return pl.pallas_call(
        functools.partial(_mm_res_ln_kernel, n_in=len(xs), alpha=alpha),
        out_shape=jax.ShapeDtypeStruct((m, n), F32),
        grid=(m // tm,),
        in_specs=in_specs,
        out_specs=pl.BlockSpec((tm, n), lambda i: (i, 0)),
        compiler_params=_params(("parallel",), nbytes),
        name=name,
    )(*xs, *([w] * len(xs)), h, g, b)


def _ffn_kernel(x_ref, w1_ref, w3_ref, w2_ref, g_ref, b_ref, o_ref, xb_ref, acc_ref, *, alpha):
    j = pl.program_id(1)

    @pl.when(j == 0)
    def _():
        xb_ref[...] = x_ref[...].astype(BF16)
        acc_ref[...] = jnp.zeros_like(acc_ref)

    xb = xb_ref[...]
    h1 = _dot(xb, w1_ref[...])
    h3 = _dot(xb, w3_ref[...])
    a = (h1 * _sigmoid(h1) * h3).astype(BF16)
    acc_ref[...] += _dot(a, w2_ref[...])

    @pl.when(j == pl.num_programs(1) - 1)
    def _():
        o_ref[...] = _ln(alpha * x_ref[...] + acc_ref[...], g_ref[...], b_ref[...])


def _ffn(x, w1, w3, w2, g, b, layer, tm, tf, alpha, name):
    m, d = x.shape
    nbytes = 4 * tm * d * 4 + tm * d * 2 + tm * d * 4 + 2 * 3 * d * tf * 2 + 3 * tm * tf * 4
    return pl.pallas_call(
        functools.partial(_ffn_kernel, alpha=alpha),
        out_shape=jax.ShapeDtypeStruct((m, d), F32),
        grid=(m // tm, D_FF // tf),
        in_specs=[pl.BlockSpec((tm, d), lambda i, j: (i, 0)),
                  pl.BlockSpec((None, d, tf), lambda i, j: (layer, 0, j)),
                  pl.BlockSpec((None, d, tf), lambda i, j: (layer, 0, j)),
                  pl.BlockSpec((None, tf, d), lambda i, j: (layer, j, 0)),
                  _lspec(g, layer), _lspec(b, layer)],
        out_specs=pl.BlockSpec((tm, d), lambda i, j: (i, 0)),
        scratch_shapes=[pltpu.VMEM((tm, d), BF16), pltpu.VMEM((tm, d), F32)],
        compiler_params=_params(("parallel", "arbitrary"), nbytes),
        name=name,
    )(x, w1, w3, w2, g, b)


def _softmax_rows(s):
    m = jnp.max(s, axis=-1, keepdims=True)
    e = jnp.exp(s - m)
    return e / jnp.sum(e, axis=-1, keepdims=True)


def _attn_prompt_kernel(mix_ref, yd_ref, h_ref, mk_ref, mv_ref, wma_ref, wmd_ref, g1_ref, b1_ref,
                        wq_ref, wo_ref, g_ref, b_ref, o_ref, ob_ref, *, alpha):
    y = _dot(mix_ref[0].astype(BF16), wma_ref[...]) + _dot(yd_ref[0].astype(BF16), wmd_ref[...])
    h = _ln(alpha * h_ref[0] + y, g1_ref[...], b1_ref[...])
    q = _dot(h.astype(BF16), wq_ref[...])
    scale = X_HEAD_DIM ** -0.5
    sls = [slice(hd * X_HEAD_DIM, (hd + 1) * X_HEAD_DIM) for hd in range(X_HEADS)]
    qb = q.astype(BF16)
    scores = [_dot(qb[:, sl], mk_ref[0, :, sl].astype(BF16), NT) * scale for sl in sls]
    probs = [_softmax_rows(s).astype(BF16) for s in scores]
    for sl, p in zip(sls, probs):
        ob_ref[:, sl] = _dot(p, mv_ref[0, :, sl].astype(BF16)).astype(BF16)
    y = _dot(ob_ref[...], wo_ref[...])
    o_ref[0] = _ln(alpha * h + y, g_ref[...], b_ref[...])


def _attn_prompt(mix, yd, h, mk, mv, w_mix, g1, b1, wq, wo, g, b, layer, tq, alpha):
    bn, t, d = h.shape
    nw = mix.shape[2]
    assert nw % W_D == 0 and w_mix.shape[1] == nw + W_D
    nbytes = (6 * tq * d * 4 + 4 * MEM_LEN * d * 4 + 6 * d * d * 2 + tq * d * 2
              + 3 * tq * d * 4 + 3 * tq * MEM_LEN * 4)
    tile = lambda n: pl.BlockSpec((1, tq, n), lambda i, j: (i, j, 0))
    return pl.pallas_call(
        functools.partial(_attn_prompt_kernel, alpha=alpha),
        out_shape=jax.ShapeDtypeStruct((bn, t, d), F32),
        grid=(bn, t // tq),
        in_specs=[tile(nw), tile(W_D), tile(d),
                  pl.BlockSpec((1, MEM_LEN, d), lambda i, j: (i, 0, 0)),
                  pl.BlockSpec((1, MEM_LEN, d), lambda i, j: (i, 0, 0)),
                  pl.BlockSpec((None, nw, d), lambda i, j: (layer, 0, 0)),
                  pl.BlockSpec((None, W_D, d), lambda i, j: (layer, nw // W_D, 0)),
                  _lspec(g1, layer), _lspec(b1, layer),
                  _lspec(wq, layer), _lspec(wo, layer), _lspec(g, layer), _lspec(b, layer)],
        out_specs=tile(d),
        scratch_shapes=[pltpu.VMEM((tq, d), BF16)],
        compiler_params=_params(("parallel", "parallel"), nbytes),
        name="attn_prompt",
    )(mix, yd, h, mk, mv, w_mix, w_mix, g1, b1, wq, wo, g, b)


LANES = 128
SUBLANES = 8
LANE_TILES = X_HEAD_DIM // LANES
MEM_ROWS = MEM_LEN * LANE_TILES * X_HEADS


def _cache_rows_view(cache):
    nl, n = cache.shape[:2]
    x = cache.reshape(nl, n, MEM_LEN, X_HEADS, LANE_TILES, LANES)
    return x.transpose(0, 1, 2, 4, 3, 5).reshape(nl, n, MEM_ROWS, LANES)


def _attn_sample_kernel(q_ref, k_ref, v_ref, o_ref, *, bb):
    scale = X_HEAD_DIM ** -0.5
    shape = (SUBLANES, MEM_ROWS)
    rowi = lax.broadcasted_iota(jnp.int32, shape, 0)
    coli = lax.broadcasted_iota(jnp.int32, shape, 1)
    valid = (coli % SUBLANES) == rowi
    raw = [_dot(q_ref[i].astype(BF16), k_ref[i].astype(BF16), NT) for i in range(bb)]
    probs = []
    for r in raw:
        r = jnp.where(valid, r, 0.0)
        other = pltpu.roll(r, X_HEADS, 0)
        other = jnp.where(rowi < X_HEADS, pltpu.roll(other, MEM_ROWS - X_HEADS, 1), pltpu.roll(other, X_HEADS, 1))
        s = jnp.where(valid, (r + other) * scale, -jnp.inf)
        m = jnp.max(s, axis=-1, keepdims=True)
        e = jnp.exp(s - m)
        probs.append((e / jnp.sum(e, axis=-1, keepdims=True)).astype(BF16))
    for i in range(bb):
        o_ref[i] = _dot(probs[i], v_ref[i].astype(BF16))


def _attn_sample(q, cache_k, cache_v, layer, bb):
    n = q.shape[0]
    q8 = q.reshape(n, X_HEADS, LANE_TILES, LANES).transpose(0, 2, 1, 3).reshape(n, SUBLANES, LANES)
    nbytes = 4 * bb * MEM_ROWS * LANES * 4 + 2 * bb * MEM_ROWS * LANES * 2 + 8 * SUBLANES * MEM_ROWS * 4
    cache_spec = pl.BlockSpec((None, bb, MEM_ROWS, LANES), lambda i: (layer, i, 0, 0))
    o8 = pl.pallas_call(
        functools.partial(_attn_sample_kernel, bb=bb),
        out_shape=jax.ShapeDtypeStruct((n, SUBLANES, LANES), F32),
        grid=(n // bb,),
        in_specs=[pl.BlockSpec((bb, SUBLANES, LANES), lambda i: (i, 0, 0)), cache_spec, cache_spec],
        out_specs=pl.BlockSpec((bb, SUBLANES, LANES), lambda i: (i, 0, 0)),
        compiler_params=_params(("parallel",), nbytes),
        name="attn_sample",
    )(q8, _cache_rows_view(cache_k), _cache_rows_view(cache_v))
    return o8.reshape(n, LANE_TILES, X_HEADS, LANES).transpose(0, 2, 1, 3).reshape(n, D_MODEL)


def _pool_window_sums(ext, tt):
    s2 = ext + pltpu.roll(ext, 1, 0)
    s4 = s2 + pltpu.roll(s2, 2, 0)
    s8 = s4 + pltpu.roll(s4, 4, 0)
    s16 = s8 + pltpu.roll(s8, 8, 0)
    grp = lax.broadcasted_iota(jnp.int32, (tt, W_B), 1) // POOL_GROUP
    lo = ext.shape[0] - tt
    return jnp.where(grp == 0, s2[lo:], jnp.where(grp == 1, s4[lo:], jnp.where(grp == 2, s8[lo:], s16[lo:])))


def _proj_abc_kernel(x_ref, w_ref, lng_ref, lnb_ref, ws_ref, bias_ref, wpool_ref, pscale_ref, cw_ref,
                     pd_ref, mix_ref, vlast_ref, pool_ref, conv_ref, pcar_ref, ccar_ref, *, tt):
    t = pl.program_id(1)
    nt = pl.num_programs(1)

    @pl.when(t == 0)
    def _():
        pcar_ref[...] = jnp.zeros_like(pcar_ref)
        ccar_ref[...] = jnp.zeros_like(ccar_ref)

    rows = lax.broadcasted_iota(jnp.int32, (A_HEADS * CHUNK, CHUNK), 0) % CHUNK
    cols = lax.broadcasted_iota(jnp.int32, (A_HEADS * CHUNK, CHUNK), 1)
    wsm = jnp.where(rows >= cols, ws_ref[...], 0.0).astype(BF16)
    hid = lax.broadcasted_iota(jnp.int32, (CHUNK, W_A), 1) // A_HEAD_DIM
    win = jnp.left_shift(2, lax.broadcasted_iota(jnp.int32, (CHUNK, W_B), 1) // POOL_GROUP)
    rowi = lax.broadcasted_iota(jnp.int32, (CHUNK, W_B), 0)
    w = w_ref[...]
    pcar = pcar_ref[...]
    ccar = ccar_ref[...]
    v = None
    for c in range(tt // CHUNK):
        rs = slice(c * CHUNK, (c + 1) * CHUNK)
        y = _dot(x_ref[0, rs, :].astype(BF16), w)
        pd_ref[0, rs, :] = y[:, N_ABC:]

        ga = _gelu(y[:, :2 * W_A])
        u = ga[:, :W_A]
        v = _ln(ga[:, W_A:], lng_ref[...], lnb_ref[...])
        zz = _dot(wsm, v.astype(BF16))
        z = zz[(A_HEADS - 1) * CHUNK:]
        for hd in range(A_HEADS - 2, -1, -1):
            z = jnp.where(hid == hd, zz[hd * CHUNK:(hd + 1) * CHUNK], z)
        mix_ref[0, rs, 0:W_A] = u * (z + bias_ref[...])

        xb = y[:, 2 * W_A:2 * W_A + W_B]
        ext = jnp.concatenate([pcar, xb], axis=0)
        sums = _pool_window_sums(ext, CHUNK)
        pos = t * tt + c * CHUNK + rowi
        cnt = jnp.minimum(win, pos + 1).astype(F32)
        d = sums / cnt - xb
        mix_ref[0, rs, W_A:W_A + W_B] = _dot(d.astype(BF16), wpool_ref[...]) * pscale_ref[...]
        pcar = ext[CHUNK:]

        o = 2 * W_A + W_B
        bg = y[:, o:o + W_C]
        zc = y[:, o + W_C:o + 2 * W_C] * y[:, o + 2 * W_C:o + 3 * W_C]
        extz = jnp.concatenate([ccar, zc], axis=0)
        conv = (cw_ref[0:1, :] * pltpu.roll(extz, 2, 0) + cw_ref[1:2, :] * pltpu.roll(extz, 1, 0)
                + cw_ref[2:3, :] * extz)
        mix_ref[0, rs, W_A + W_B:] = bg * conv[ROW_CARRY:]
        ccar = extz[CHUNK:]
    pcar_ref[...] = pcar
    ccar_ref[...] = ccar

    @pl.when(t == nt - 1)
    def _():
        vlast_ref[0] = v
        pool_ref[0] = pcar_ref[POOL_CARRY - POOL_BUF:, :]
        conv_ref[0] = ccar_ref[ROW_CARRY - (CONV_W - 1):, :]


def _proj_abc_prompt(h, w, lng, lnb, ws, bias, wpool, pscale, cw, layer, tt):
    bn, t, d = h.shape
    nw = W_A + W_B + W_C
    nbytes = (2 * tt * d * 4 + 2 * d * PROJ * 2 + 2 * tt * (D_PROJ + nw) * 4 + 3 * CHUNK * PROJ * 4
              + 4 * A_HEADS * CHUNK * CHUNK * 4)
    tile = lambda n: pl.BlockSpec((1, tt, n), lambda i, j: (i, j, 0))
    last = lambda r, n: pl.BlockSpec((1, r, n), lambda i, j: (i, 0, 0))
    return pl.pallas_call(
        functools.partial(_proj_abc_kernel, tt=tt),
        out_shape=(jax.ShapeDtypeStruct((bn, t, D_PROJ), F32),
                   jax.ShapeDtypeStruct((bn, t, nw), F32),
                   jax.ShapeDtypeStruct((bn, CHUNK, W_A), F32),
                   jax.ShapeDtypeStruct((bn, POOL_BUF, W_B), F32),
                   jax.ShapeDtypeStruct((bn, CONV_W - 1, W_C), F32)),
        grid=(bn, t // tt),
        in_specs=[tile(d), *[_lspec(z, layer) for z in (w, lng, lnb, ws, bias, wpool, pscale, cw)]],
        out_specs=(tile(D_PROJ), tile(nw), last(CHUNK, W_A), last(POOL_BUF, W_B), last(CONV_W - 1, W_C)),
        scratch_shapes=[pltpu.VMEM((POOL_CARRY, W_B), F32), pltpu.VMEM((ROW_CARRY, W_C), F32)],
        compiler_params=_params(("parallel", "arbitrary"), nbytes),
        name="proj_abc_prompt",
    )(h, w, lng, lnb, ws, bias, wpool, pscale, cw)


def _abc_sample_kernel(x_ref, pool_ref, conv_ref, lng_ref, lnb_ref, ws0_ref, bias0_ref, wpool_ref,
                       pscale_ref, cw_ref, mix_ref, v_ref, poolo_ref, convo_ref, *, pos0):
    x = x_ref[...]
    n = x.shape[0]
    ga = _gelu(x[:, :2 * W_A])
    u = ga[:, :W_A]
    v = _ln(ga[:, W_A:], lng_ref[...], lnb_ref[...])
    v_ref[...] = v
    mix_ref[:, 0:W_A] = u * (v * ws0_ref[...] + bias0_ref[0:1, :])

    xb = x[:, 2 * W_A:2 * W_A + W_B]
    grp = lax.broadcasted_iota(jnp.int32, (n, W_B), 1) // POOL_GROUP
    run = xb
    sums = jnp.zeros_like(xb)
    for back in range(1, max(POOL_WINDOWS)):
        run = run + pool_ref[POOL_BUF - back]
        for gi, win in enumerate(POOL_WINDOWS):
            if back == win - 1:
                sums = jnp.where(grp == gi, run, sums)
    win = jnp.left_shift(2, grp)
    cnt = jnp.minimum(win, pos0 + 1).astype(F32)
    d = sums / cnt - xb
    mix_ref[:, W_A:W_A + W_B] = _dot(d.astype(BF16), wpool_ref[...]) * pscale_ref[...]
    for r in range(POOL_BUF - 1):
        poolo_ref[r] = pool_ref[r + 1]
    poolo_ref[POOL_BUF - 1] = xb

    o = 2 * W_A + W_B
    bg = x[:, o:o + W_C]
    zc = x[:, o + W_C:o + 2 * W_C] * x[:, o + 2 * W_C:o + 3 * W_C]
    z0 = conv_ref[:, :W_C]
    z1 = conv_ref[:, W_C:]
    y = cw_ref[0:1, :] * z0 + cw_ref[1:2, :] * z1 + cw_ref[2:3, :] * zc
    mix_ref[:, W_A + W_B:] = bg * y
    convo_ref[:, :W_C] = z1
    convo_ref[:, W_C:] = zc


def _abc_sample(pabc, pool, conv, lng, lnb, ws0, bias, wpool, pscale, cw, layer, pos0):
    n = pabc.shape[0]
    nw = W_A + W_B + W_C
    whole = lambda shape: pl.BlockSpec(shape, lambda i: (0,) * len(shape))
    return pl.pallas_call(
        functools.partial(_abc_sample_kernel, pos0=pos0),
        out_shape=(jax.ShapeDtypeStruct((n, nw), F32),
                   jax.ShapeDtypeStruct((n, W_A), F32),
                   jax.ShapeDtypeStruct(pool.shape[1:], F32),
                   jax.ShapeDtypeStruct(conv.shape[1:], F32)),
        grid=(1,),
        in_specs=[whole(pabc.shape)] + [_lspec(z, layer) for z in (pool, conv, lng, lnb, ws0, bias, wpool, pscale, cw)],
        out_specs=(whole((n, nw)), whole((n, W_A)), whole(pool.shape[1:]), whole(conv.shape[1:])),
        name="mixer_abc_sample",
    )(pabc, pool, conv, lng, lnb, ws0, bias, wpool, pscale, cw)


def _rwkv_inputs(xs, w0, w2, a0, a2, g2, kk_w, ka_w, seg):
    r = xs[:, 0:W_D]
    k = xs[:, W_D:2 * W_D]
    v = xs[:, 2 * W_D:3 * W_D]
    o = 3 * W_D
    dw = xs[:, o:o + R_DECAY]
    da = xs[:, o + R_DECAY:o + R_DECAY + R_AAA]
    dg = xs[:, o + R_DECAY + R_AAA:]
    w_log = -_softplus(-(w0 + _dot_hi(jnp.tanh(dw), w2))) - 0.5
    logdecay = -jnp.exp(w_log)
    a = _sigmoid(a0 + _dot_hi(da, a2))
    g = _dot_hi(_sigmoid(dg), g2)
    kk = k * kk_w
    kk = kk * lax.rsqrt(jnp.maximum(_dot_ones(kk * kk, seg), 1e-12))
    k = k * (1.0 + (a - 1.0) * ka_w)
    return r, k, v, kk, a, logdecay, g


def _rwkv_finish(o, r, k, v, g, rk_w, lnx_g, lnx_b, seg):
    inv = 1.0 / D_HEAD_DIM
    mu = _dot_ones(o, seg) * inv
    oc = o - mu
    var = _dot_ones(oc * oc, seg) * inv
    on = oc * lax.rsqrt(var + GN_EPS) * lnx_g + lnx_b
    bonus = _dot_ones(r * k * rk_w, seg) * v
    return (on + bonus) * g


def _bdot(a, b, dims=NN):
    return _dot(a.astype(BF16), b.astype(BF16), dims)


def _head_cols(x, hd):
    return x[:, hd * D_HEAD_DIM:(hd + 1) * D_HEAD_DIM]


HEAD_PAIRS = W_D // LANES


def _pair_diag(y, low):
    zero = jnp.zeros_like(y)
    return jnp.concatenate([jnp.where(low, y, zero), jnp.where(low, zero, y)], axis=0)


def _wkv_tile(s_ref, o_ref, qt, rt, bt, kt, bbar, kbar, vm, gam, nc):
    c = WKV_CHUNK
    n = D_HEAD_DIM
    probs = [(ch, pr) for ch in range(nc) for pr in range(HEAD_PAIRS)]
    cut = lambda x, p: x[p[0] * c:(p[0] + 1) * c, p[1] * LANES:(p[1] + 1) * LANES]
    qt, rt, bt, kt, bbar, kbar, vm = (x.astype(BF16) for x in (qt, rt, bt, kt, bbar, kbar, vm))
    ri = lax.broadcasted_iota(jnp.int32, (c, LANES), 0)
    li = lax.broadcasted_iota(jnp.int32, (c, LANES), 1)
    low = li < n
    strict = ri > li % n
    incl = ri >= li % n
    eye = jnp.where(ri == li % n, 1.0, 0.0)
    diag = lambda y: _pair_diag(y, low)
    halves = lambda x: jnp.where(low, x[:n], x[n:])

    qs = {p: cut(qt, p) for p in probs}
    vd = {p: diag(cut(vm, p)) for p in probs}
    aa = {p: _dot(jnp.concatenate([qs[p], cut(rt, p)], axis=0),
                  jnp.concatenate([diag(cut(bt, p)), diag(cut(kt, p))], axis=0), NT) for p in probs}
    lk = {p: jnp.where(strict, aa[p][:c, LANES:], 0.0).astype(BF16) for p in probs}
    ab = {p: jnp.where(incl, aa[p][c:, :LANES], 0.0).astype(BF16) for p in probs}
    ak = {p: jnp.where(incl, aa[p][c:, LANES:], 0.0).astype(BF16) for p in probs}
    npow = {p: jnp.where(strict, -aa[p][:c, :LANES], 0.0).astype(BF16) for p in probs}
    tinv = {p: eye + npow[p].astype(F32) for p in probs}
    lv = {p: _dot(jnp.concatenate([lk[p], ak[p]], axis=0), vd[p]) for p in probs}
    zp = {p: lv[p][:c].astype(BF16) for p in probs}
    npow = {p: _dot(npow[p], diag(npow[p])).astype(BF16) for p in probs}
    for _ in range(int(math.log2(c)) - 2):
        both = {p: _dot(jnp.concatenate([tinv[p].astype(BF16), npow[p]], axis=0), diag(npow[p])) for p in probs}
        tinv = {p: tinv[p] + both[p][:c] for p in probs}
        npow = {p: both[p][c:].astype(BF16) for p in probs}
    tinv = {p: (tinv[p] + _dot(tinv[p].astype(BF16), diag(npow[p]))).astype(BF16) for p in probs}
    tq = {p: _dot(tinv[p], jnp.concatenate([diag(qs[p]), diag(zp[p])], axis=1)) for p in probs}
    qh = {p: tq[p][:, :LANES].astype(BF16) for p in probs}
    zn = {p: (-tq[p][:, LANES:]).astype(BF16) for p in probs}
    abq = {p: _dot(ab[p], jnp.concatenate([diag(qh[p]), diag(zn[p])], axis=1)) for p in probs}
    rh = {p: (cut(rt, p).astype(F32) - abq[p][:, :LANES]).astype(BF16) for p in probs}
    pv = {p: abq[p][:, LANES:] + lv[p][c:] for p in probs}
    gt = {p: halves(-_dot(qh[p], cut(bbar, p), TN)).astype(BF16) for p in probs}
    cst = {p: halves(_dot(jnp.concatenate([cut(vm, p), zn[p]], axis=0),
                          jnp.concatenate([cut(kbar, p), cut(bbar, p)], axis=0), TN)) for p in probs}
    for ch in range(nc):
        for pr in range(HEAD_PAIRS):
            p = (ch, pr)
            s0 = s_ref[pr]
            s0b = s0.astype(BF16)
            o_ref[ch * c:(ch + 1) * c, pr * LANES:(pr + 1) * LANES] = _dot(rh[p], diag(s0b), NT) + pv[p]
            s_ref[pr] = s0 * gam[ch][:, pr * LANES:(pr + 1) * LANES] + _dot(s0b, diag(gt[p])) + cst[p]


def _rwkv_prompt_kernel(pd_ref, mu_ref, w0_ref, w2_ref, a0_ref, a2_ref, g2_ref, kk_ref, ka_ref, rk_ref,
                        lg_ref, lb_ref, yd_ref, shift_ref, wkv_ref, car_ref, s_ref, o_ref, *, tt):
    t = pl.program_id(1)
    nt = pl.num_programs(1)

    @pl.when(t == 0)
    def _():
        car_ref[...] = jnp.zeros_like(car_ref)
        s_ref[...] = jnp.zeros_like(s_ref)

    pd = pd_ref[0]
    ext = jnp.concatenate([car_ref[...], pd], axis=0)
    prev = pltpu.roll(ext, 1, 0)[ROW_CARRY:]
    car_ref[...] = ext[tt:]
    xs = pd + (prev - pd) * mu_ref[...]
    seg = _head_ones(W_D, D_HEAD_DIM)
    r, k, v, kk, a, ld, g = _rwkv_inputs(xs, w0_ref[...], w2_ref[...], a0_ref[...], a2_ref[...],
                                         g2_ref[...], kk_ref[...], ka_ref[...], seg)
    c = WKV_CHUNK
    nc = tt // c
    tri = jnp.where(lax.broadcasted_iota(jnp.int32, (c, c), 0) >= lax.broadcasted_iota(jnp.int32, (c, c), 1),
                    1.0, 0.0).astype(BF16)
    cs_chunks = [_ones_dot(tri, ld[ch * c:(ch + 1) * c]) for ch in range(nc)]
    cs = jnp.concatenate(cs_chunks, axis=0)
    cs_end = [x[c - 1:c] for x in cs_chunks]
    cs_last = jnp.concatenate([jnp.broadcast_to(x, (c, W_D)) for x in cs_end], axis=0)
    e_neg = jnp.exp(-cs)
    e_tail = jnp.exp(cs_last - cs)
    b = kk * a
    _wkv_tile(s_ref, o_ref, kk * jnp.exp(cs - ld), r * jnp.exp(cs), b * e_neg, k * e_neg, b * e_tail, k * e_tail,
              v, [jnp.exp(x) for x in cs_end], nc)
    yd_ref[0] = _rwkv_finish(o_ref[...], r, k, v, g, rk_ref[...], lg_ref[...], lb_ref[...], seg)

    @pl.when(t == nt - 1)
    def _():
        shift_ref[0] = pd[tt - 1:tt]
        for hd in range(D_HEADS):
            wkv_ref[0, hd] = _head_cols(s_ref[hd // 2], hd % 2)


def _rwkv_prompt(pd, params, layer, tt):
    bn, t, _ = pd.shape
    nbytes = 2 * tt * D_PROJ * 4 + 40 * tt * W_D * 4
    return pl.pallas_call(
        functools.partial(_rwkv_prompt_kernel, tt=tt),
        out_shape=(jax.ShapeDtypeStruct((bn, t, W_D), F32),
                   jax.ShapeDtypeStruct((bn, 1, D_PROJ), F32),
                   jax.ShapeDtypeStruct((bn, D_HEADS, D_HEAD_DIM, D_HEAD_DIM), F32)),
        grid=(bn, t // tt),
        in_specs=[pl.BlockSpec((1, tt, D_PROJ), lambda i, j: (i, j, 0))] + [_lspec(z, layer) for z in params],
        out_specs=(pl.BlockSpec((1, tt, W_D), lambda i, j: (i, j, 0)),
                   pl.BlockSpec((1, 1, D_PROJ), lambda i, j: (i, 0, 0)),
                   pl.BlockSpec((1, D_HEADS, D_HEAD_DIM, D_HEAD_DIM), lambda i, j: (i, 0, 0, 0))),
        scratch_shapes=[pltpu.VMEM((ROW_CARRY, D_PROJ), F32),
                        pltpu.VMEM((HEAD_PAIRS, D_HEAD_DIM, LANES), F32),
                        pltpu.VMEM((tt, W_D), F32)],
        compiler_params=_params(("parallel", "arbitrary"), nbytes),
        name="rwkv_prompt",
    )(pd, *params)


def _rwkv_sample_kernel(pd_ref, sh_ref, st_ref, mu_ref, w0_ref, w2_ref, a0_ref, a2_ref, g2_ref, kk_ref,
                        ka_ref, rk_ref, lg_ref, lb_ref, yd_ref, so_ref, rows_ref, cols_ref, ot_ref):
    h = pl.program_id(0)
    n = D_HEAD_DIM

    @pl.when(h == 0)
    def _():
        pd = pd_ref[...]
        xs = pd + (sh_ref[...] - pd) * mu_ref[...]
        seg = _head_ones(W_D, D_HEAD_DIM)
        r, k, v, kk, a, ld, g = _rwkv_inputs(xs, w0_ref[...], w2_ref[...], a0_ref[...], a2_ref[...],
                                             g2_ref[...], kk_ref[...], ka_ref[...], seg)
        for j, x in enumerate((r, k, v, g)):
            rows_ref[j] = x
        for j, x in enumerate((kk, jnp.exp(ld), kk * a, k, r, v)):
            cols_ref[j] = x.T

    base = pl.multiple_of(h * n, n)
    kap, dec, bvec, kvec, rvec = (cols_ref[j, pl.ds(base, n), :] for j in range(5))
    for vi in range(n):
        s = st_ref[0, vi]
        u = -jnp.sum(s * kap, axis=0, keepdims=True)
        s = s * dec + u * bvec + cols_ref[5, pl.ds(base + vi, 1), :] * kvec
        so_ref[0, vi] = s
        ot_ref[pl.ds(base + vi, 1), :] = jnp.sum(s * rvec, axis=0, keepdims=True)

    @pl.when(h == pl.num_programs(0) - 1)
    def _():
        seg = _head_ones(W_D, D_HEAD_DIM)
        yd_ref[...] = _rwkv_finish(ot_ref[...].T, rows_ref[0], rows_ref[1], rows_ref[2], rows_ref[3],
                                   rk_ref[...], lg_ref[...], lb_ref[...], seg)


def _rwkv_sample(pd, shift, state, params, layer):
    n = pd.shape[0]
    sblock = (1, D_HEAD_DIM, D_HEAD_DIM, n)
    return pl.pallas_call(
        _rwkv_sample_kernel,
        out_shape=(jax.ShapeDtypeStruct((n, W_D), F32), jax.ShapeDtypeStruct(state.shape[1:], F32)),
        grid=(D_HEADS,),
        in_specs=[pl.BlockSpec((n, D_PROJ), lambda i: (0, 0)), _lspec(shift, layer),
                  pl.BlockSpec((None,) + sblock, lambda i: (layer, i, 0, 0, 0))]
                 + [_lspec(z, layer) for z in params],
        out_specs=(pl.BlockSpec((n, W_D), lambda i: (0, 0)), pl.BlockSpec(sblock, lambda i: (i, 0, 0, 0))),
        scratch_shapes=[pltpu.VMEM((4, n, W_D), F32), pltpu.VMEM((6, W_D, n), F32), pltpu.VMEM((W_D, n), F32)],
        compiler_params=pltpu.CompilerParams(dimension_semantics=("arbitrary",)),
        name="rwkv_sample",
    )(pd, shift, state, *params)


def _block_diag(w):
    gn, n, _ = w.shape
    eye = jnp.eye(gn, dtype=w.dtype)
    return (eye[:, None, :, None] * w[:, :, None, :]).reshape(gn * n, gn * n)


def kernel(x_prompt, x_sample, mem_prompt, cache_mem_k, cache_mem_v, state_pool, state_conv, state_shift, state_wkv,
           w_in, mu_d, ln_v_g, ln_v_b, ws_chunk, b_chunk, w_pool, pool_scale, conv_w,
           rwkv_w0, rwkv_w2, rwkv_a0, rwkv_a2, rwkv_g2, rwkv_k_k, rwkv_k_a, rwkv_r_k, rwkv_lnx_g, rwkv_lnx_b,
           w_out, ln1_g, ln1_b, w_xq, w_xk, w_xv, w_xo, ln2_g, ln2_b, ffn_w1, ffn_w3, ffn_w2, ln3_g, ln3_b):
    bp, t_p, d = x_prompt.shape
    ns, t_s, _ = x_sample.shape
    depth = w_in.shape[0]
    assert d == D_MODEL and t_s == 1 and t_p % CHUNK == 0 and w_in.shape[2] == PROJ
    alpha = (2 * depth) ** 0.25
    mp = bp * t_p
    nw = W_A + W_B + W_C
    row = lambda z: z.reshape(depth, 1, -1)

    w_in_b, w_out_b = w_in.astype(BF16), w_out.astype(BF16)
    w_xq_b, w_xk_b, w_xv_b, w_xo_b = (w.astype(BF16) for w in (w_xq, w_xk, w_xv, w_xo))
    w1_b, w3_b, w2_b = ffn_w1.astype(BF16), ffn_w3.astype(BF16), ffn_w2.astype(BF16)
    ws_flat = ws_chunk.reshape(depth, A_HEADS * CHUNK, CHUNK)
    bias_full = jnp.repeat(jnp.swapaxes(b_chunk, 1, 2), A_HEAD_DIM, axis=2)
    ws0 = jnp.repeat(ws_chunk[:, :, 0, 0], A_HEAD_DIM, axis=1).reshape(depth, 1, W_A)
    wpool_bd = jnp.stack([_block_diag(w_pool[l]) for l in range(depth)]).astype(BF16)
    abc_w = (row(ln_v_g), row(ln_v_b))
    abc_w2 = (wpool_bd, row(pool_scale), conv_w)
    rw = (row(mu_d), row(rwkv_w0), rwkv_w2, row(rwkv_a0), rwkv_a2, rwkv_g2,
          row(rwkv_k_k), row(rwkv_k_a), row(rwkv_r_k), row(rwkv_lnx_g), row(rwkv_lnx_b))
    ln1, ln2, ln3 = (row(ln1_g), row(ln1_b)), (row(ln2_g), row(ln2_b)), (row(ln3_g), row(ln3_b))
    pool_view = jnp.swapaxes(state_pool, 1, 2)
    wkv_view = state_wkv.transpose(0, 2, 3, 4, 1)
    conv_view = state_conv.reshape(depth, ns, (CONV_W - 1) * W_C)
    shift_view = state_shift.reshape(depth, ns, D_PROJ)

    hp = x_prompt
    hs = x_sample.reshape(ns, d)
    mem = mem_prompt.reshape(bp * MEM_LEN, d)
    outs = [[] for _ in range(12)]
    for l in range(depth):
        mk_p = _mm(mem, w_xk_b, l, 512, "mem_k")
        mv_p = _mm(mem, w_xv_b, l, 512, "mem_v")
        pd, mix, v_last, pool_p, conv_p = _proj_abc_prompt(hp, w_in_b, *abc_w, ws_flat, bias_full, *abc_w2, l, tt=512)
        yd, shift_p, wkv_p = _rwkv_prompt(pd, rw, l, tt=512)
        hp = _attn_prompt(mix, yd, hp, mk_p.reshape(bp, MEM_LEN, d), mv_p.reshape(bp, MEM_LEN, d),
                          w_out_b, *ln1, w_xq_b, w_xo_b, *ln2, l, tq=512, alpha=alpha)
        hp = _ffn(hp.reshape(mp, d), w1_b, w3_b, w2_b, *ln3, l, tm=1024, tf=256, alpha=alpha,
                  name="ffn").reshape(bp, t_p, d)
        for lst, val in zip(outs[:7], (v_last, pool_p, conv_p, shift_p, wkv_p,
                                       mk_p.reshape(bp, MEM_LEN, X_HEADS, X_HEAD_DIM),
                                       mv_p.reshape(bp, MEM_LEN, X_HEADS, X_HEAD_DIM))):
            lst.append(val)

        pabc_s, pd_s = _proj(hs, w_in_b, l, ns)
        mix_s, v_s, pool_s, conv_s = _abc_sample(pabc_s, pool_view, conv_view, *abc_w, ws0, bias_full, *abc_w2,
                                                 l, pos0=PAST_LEN)
        yd_s, wkv_s = _rwkv_sample(pd_s, shift_view, wkv_view, rw, l)
        hs = _mm_res_ln([mix_s, yd_s], w_out_b, l, hs, *ln1, tm=ns, alpha=alpha, name="out_proj_s")
        q_s = _mm(hs, w_xq_b, l, ns, "q_s")
        o_s = _attn_sample(q_s, cache_mem_k, cache_mem_v, l, bb=4)
        hs = _mm_res_ln([o_s], w_xo_b, l, hs, *ln2, tm=ns, alpha=alpha, name="xo_s")
        hs = _ffn(hs, w1_b, w3_b, w2_b, *ln3, l, tm=ns, tf=256, alpha=alpha, name="ffn_s")
        for lst, val in zip(outs[7:], (v_s.reshape(ns, 1, W_A), pool_s,
                                       conv_s.reshape(ns, CONV_W - 1, W_C), pd_s.reshape(ns, 1, D_PROJ), wkv_s)):
            lst.append(val)

    stacked = [jnp.stack(o) for o in outs]
    stacked[8] = jnp.swapaxes(stacked[8], 1, 2)
    stacked[11] = stacked[11].transpose(0, 4, 1, 2, 3)
    return (hp, hs.reshape(ns, 1, d)) + tuple(stacked)
```

```python
import functools
import math

import jax
import jax.numpy as jnp
from jax import lax
from jax.experimental import pallas as pl
from jax.experimental.pallas import tpu as pltpu

F32 = jnp.float32
BF16 = jnp.bfloat16

D_MODEL = 1024
W_A = 256
W_B = 256
W_C = 256
W_D = 256
A_HEADS = 4
A_HEAD_DIM = W_A // A_HEADS
CHUNK = 128
POOL_WINDOWS = (2, 4, 8, 16)
POOL_GROUP = W_B // len(POOL_WINDOWS)
POOL_BUF = max(POOL_WINDOWS) - 1
CONV_W = 3
D_HEAD_DIM = 64
D_HEADS = W_D // D_HEAD_DIM
R_DECAY = 32
R_AAA = 32
R_GATE = 64
D_PROJ = 3 * W_D + R_DECAY + R_AAA + R_GATE
N_ABC = 2 * W_A + W_B + 3 * W_C
PROJ = N_ABC + D_PROJ
MEM_LEN = 256
X_HEADS = 4
X_HEAD_DIM = D_MODEL // X_HEADS
D_FF = int(math.ceil(8 * D_MODEL / 3 / 256)) * 256
PAST_LEN = 16384
LN_EPS = 1e-5
GN_EPS = 64e-5

WKV_CHUNK = 64
POOL_CARRY = 24
ROW_CARRY = 8
V7X_VMEM_BYTES = 64 * 1024 * 1024
VMEM_CAP = V7X_VMEM_BYTES - 8 * 1024 * 1024

NN = (((1,), (0,)), ((), ()))
NT = (((1,), (1,)), ((), ()))
TN = (((0,), (0,)), ((), ()))


def _vmem_limit(nbytes):
    return int(min(VMEM_CAP, max(32 * 1024 * 1024, 2 * nbytes)))


def _params(sem, nbytes):
    return pltpu.CompilerParams(dimension_semantics=sem, vmem_limit_bytes=_vmem_limit(nbytes))


def _lspec(arr, layer):
    tail = arr.shape[1:]
    zeros = (0,) * len(tail)
    return pl.BlockSpec((None,) + tail, lambda *_: (layer,) + zeros)


def _dot(a, b, dims=NN):
    return lax.dot_general(a, b, dims, preferred_element_type=F32)


def _split2(a):
    hi = a.astype(BF16)
    lo = (a - hi.astype(F32)).astype(BF16)
    return hi, lo


def _dot_hi(a, b, dims=NN):
    ah, al = _split2(a)
    bh, bl = _split2(b)
    return _dot(ah, bh, dims) + _dot(ah, bl, dims) + _dot(al, bh, dims)


def _dot_ones(x, ones_bf16, dims=NN):
    hi, lo = _split2(x)
    return _dot(hi, ones_bf16, dims) + _dot(lo, ones_bf16, dims)


def _ones_dot(ones_bf16, x):
    hi = x.astype(BF16)
    r1 = x - hi.astype(F32)
    mid = r1.astype(BF16)
    lo = (r1 - mid.astype(F32)).astype(BF16)
    return _dot(ones_bf16, hi) + _dot(ones_bf16, mid) + _dot(ones_bf16, lo)


def _ln(x, g, b, eps=LN_EPS):
    mu = jnp.mean(x, axis=-1, keepdims=True)
    xc = x - mu
    var = jnp.mean(xc * xc, axis=-1, keepdims=True)
    return xc * lax.rsqrt(var + eps) * g + b


def _gelu(x):
    c = math.sqrt(2.0 / math.pi)
    return x * (0.5 * (1.0 + jnp.tanh(c * (x + 0.044715 * (x * x * x)))))


def _sigmoid(x):
    return 1.0 / (1.0 + jnp.exp(-x))


def _softplus(x):
    return jnp.maximum(x, 0.0) + jnp.log(1.0 + jnp.exp(-jnp.abs(x)))


def _head_ones(n, group):
    r = lax.broadcasted_iota(jnp.int32, (n, n), 0) // group
    c = lax.broadcasted_iota(jnp.int32, (n, n), 1) // group
    return jnp.where(r == c, 1.0, 0.0).astype(BF16)


def _proj_kernel(x_ref, w_ref, oabc_ref, od_ref):
    y = _dot(x_ref[...].astype(BF16), w_ref[...])
    oabc_ref[...] = y[:, :N_ABC]
    od_ref[...] = y[:, N_ABC:]


def _proj(x, w, layer, tm):
    m, k = x.shape
    nbytes = 2 * (tm * k * 4 + k * PROJ * 2 + tm * PROJ * 4) + tm * PROJ * 4
    return pl.pallas_call(
        _proj_kernel,
        out_shape=(jax.ShapeDtypeStruct((m, N_ABC), F32), jax.ShapeDtypeStruct((m, D_PROJ), F32)),
        grid=(m // tm,),
        in_specs=[pl.BlockSpec((tm, k), lambda i: (i, 0)), _lspec(w, layer)],
        out_specs=(pl.BlockSpec((tm, N_ABC), lambda i: (i, 0)), pl.BlockSpec((tm, D_PROJ), lambda i: (i, 0))),
        compiler_params=_params(("parallel",), nbytes),
        name="proj",
    )(x, w)


def _mm_kernel(x_ref, w_ref, o_ref):
    o_ref[...] = _dot(x_ref[...].astype(BF16), w_ref[...])


def _mm(x, w, layer, tm, name):
    m, k = x.shape
    n = w.shape[2]
    nbytes = 2 * (tm * k * 4 + k * n * 2 + tm * n * 4) + tm * n * 4
    return pl.pallas_call(
        _mm_kernel,
        out_shape=jax.ShapeDtypeStruct((m, n), F32),
        grid=(m // tm,),
        in_specs=[pl.BlockSpec((tm, k), lambda i: (i, 0)), _lspec(w, layer)],
        out_specs=pl.BlockSpec((tm, n), lambda i: (i, 0)),
        compiler_params=_params(("parallel",), nbytes),
        name=name,
    )(x, w)


def _mem_kv_kernel(x_ref, wk_ref, wv_ref, k_ref, v_ref):
    xb = x_ref[...].astype(BF16)
    k_ref[...] = _dot(xb, wk_ref[...])
    v_ref[...] = _dot(xb, wv_ref[...])


def _mem_kv(mem, wk, wv, tm):
    m, k = mem.shape
    nl, _, n = wk.shape
    nbytes = 2 * (tm * k * 4 + 2 * k * n * 2 + 2 * tm * n * 4) + 2 * tm * n * 4
    wspec = pl.BlockSpec((None, k, n), lambda l, i: (l, 0, 0))
    ospec = pl.BlockSpec((None, tm, n), lambda l, i: (l, i, 0))
    shape = jax.ShapeDtypeStruct((nl, m, n), F32)
    return pl.pallas_call(
        _mem_kv_kernel,
        out_shape=(shape, shape),
        grid=(nl, m // tm),
        in_specs=[pl.BlockSpec((tm, k), lambda l, i: (i, 0)), wspec, wspec],
        out_specs=(ospec, ospec),
        compiler_params=_params(("parallel", "parallel"), nbytes),
        name="mem_kv",
    )(mem, wk, wv)


def _mm_res_ln_kernel(*refs, n_in, alpha):
    xs = refs[:n_in]
    ws = refs[n_in:2 * n_in]
    h_ref, g_ref, b_ref, o_ref = refs[2 * n_in:]
    y = _dot(xs[0][...].astype(BF16), ws[0][...])
    for x_ref, w_ref in zip(xs[1:], ws[1:]):
        y = y + _dot(x_ref[...].astype(BF16), w_ref[...])
    o_ref[...] = _ln(alpha * h_ref[...] + y, g_ref[...], b_ref[...])


def _mm_res_ln(xs, w, layer, h, g, b, tm, alpha, name):
    m, n = h.shape
    nbytes = 2 * sum(tm * x.shape[1] * 4 + x.shape[1] * n * 2 for x in xs) + 5 * tm * n * 4
    in_specs = [pl.BlockSpec((tm, x.shape[1]), lambda i: (i, 0)) for x in xs]
    start = 0
    for x in xs:
        width = x.shape[1]
        assert start % width == 0
        in_specs.append(pl.BlockSpec((None, width, n), lambda i, blk=start // width: (layer, blk, 0)))
        start += width
    assert start == w.shape[1]
    in_specs += [pl.BlockSpec((tm, n), lambda i: (i, 0)), _lspec(g, layer), _lspec(b, layer)]
    return pl.pallas_call(
        functools.partial(_mm_res_ln_kernel, n_in=len(xs), alpha=alpha),
        out_shape=jax.ShapeDtypeStruct((m, n), F32),
        grid=(m // tm,),
        in_specs=in_specs,
        out_specs=pl.BlockSpec((tm, n), lambda i: (i, 0)),
        compiler_params=_params(("parallel",), nbytes),
        name=name,
    )(*xs, *([w] * len(xs)), h, g, b)


def _ffn_kernel(x_ref, w1_ref, w3_ref, w2_ref, g_ref, b_ref, o_ref, xb_ref, acc_ref, *, alpha):
    j = pl.program_id(1)

    @pl.when(j == 0)
    def _():
        xb_ref[...] = x_ref[...].astype(BF16)
        acc_ref[...] = jnp.zeros_like(acc_ref)

    xb = xb_ref[...]
    h1 = _dot(xb, w1_ref[...])
    h3 = _dot(xb, w3_ref[...])
    a = (h1 * _sigmoid(h1) * h3).astype(BF16)
    acc_ref[...] += _dot(a, w2_ref[...])

    @pl.when(j == pl.num_programs(1) - 1)
    def _():
        o_ref[...] = _ln(alpha * x_ref[...] + acc_ref[...], g_ref[...], b_ref[...])


def _ffn(x, w1, w3, w2, g, b, layer, tm, tf, alpha, name):
    m, d = x.shape
    nbytes = 4 * tm * d * 4 + tm * d * 2 + tm * d * 4 + 2 * 3 * d * tf * 2 + 3 * tm * tf * 4
    return pl.pallas_call(
        functools.partial(_ffn_kernel, alpha=alpha),
        out_shape=jax.ShapeDtypeStruct((m, d), F32),
        grid=(m // tm, D_FF // tf),
        in_specs=[pl.BlockSpec((tm, d), lambda i, j: (i, 0)),
                  pl.BlockSpec((None, d, tf), lambda i, j: (layer, 0, j)),
                  pl.BlockSpec((None, d, tf), lambda i, j: (layer, 0, j)),
                  pl.BlockSpec((None, tf, d), lambda i, j: (layer, j, 0)),
                  _lspec(g, layer), _lspec(b, layer)],
        out_specs=pl.BlockSpec((tm, d), lambda i, j: (i, 0)),
        scratch_shapes=[pltpu.VMEM((tm, d), BF16), pltpu.VMEM((tm, d), F32)],
        compiler_params=_params(("parallel", "arbitrary"), nbytes),
        name=name,
    )(x, w1, w3, w2, g, b)


def _softmax_rows(s):
    m = jnp.max(s, axis=-1, keepdims=True)
    e = jnp.exp(s - m)
    return e / jnp.sum(e, axis=-1, keepdims=True)


ATTN_PIECES = 2


def _attn_prompt_kernel(mix_ref, yd_ref, h_ref, mk_ref, mv_ref, wma_ref, wmd_ref, g1_ref, b1_ref,
                        wq_ref, wo_ref, g_ref, b_ref, o_ref, ob_ref, *, alpha):
    rows = h_ref.shape[1] // ATTN_PIECES
    scale = X_HEAD_DIM ** -0.5
    sls = [slice(hd * X_HEAD_DIM, (hd + 1) * X_HEAD_DIM) for hd in range(X_HEADS)]
    kb = [mk_ref[0, :, sl].astype(BF16) for sl in sls]
    vb = [mv_ref[0, :, sl].astype(BF16) for sl in sls]
    y, h, q, sc = {}, {}, {}, {}

    def project(p, rs):
        y[p] = _dot(mix_ref[0, rs, :].astype(BF16), wma_ref[...]) + _dot(yd_ref[0, rs, :].astype(BF16), wmd_ref[...])

    def query(p, rs):
        h[p] = _ln(alpha * h_ref[0, rs, :] + y[p], g1_ref[...], b1_ref[...])
        q[p] = _dot(h[p].astype(BF16), wq_ref[...]).astype(BF16)

    def scores(p, rs):
        sc[p] = [_dot(q[p][:, sl], k, NT) * scale for sl, k in zip(sls, kb)]

    def values(p, rs):
        for sl, s, v in zip(sls, sc[p], vb):
            ob_ref[rs, sl] = _dot(_softmax_rows(s).astype(BF16), v).astype(BF16)

    def output(p, rs):
        o_ref[0, rs, :] = _ln(alpha * h[p] + _dot(ob_ref[rs, :], wo_ref[...]), g_ref[...], b_ref[...])

    stages = (project, query, scores, values, output)
    for step in range(len(stages) + ATTN_PIECES - 1):
        for p in range(ATTN_PIECES):
            if 0 <= step - p < len(stages):
                stages[step - p](p, slice(p * rows, (p + 1) * rows))


def _attn_prompt(mix, yd, h, mk, mv, w_mix, g1, b1, wq, wo, g, b, layer, tq, alpha):
    bn, t, d = h.shape
    nw = mix.shape[2]
    assert nw % W_D == 0 and w_mix.shape[1] == nw + W_D
    nbytes = (6 * tq * d * 4 + 4 * MEM_LEN * d * 4 + 6 * d * d * 2 + tq * d * 2
              + 3 * tq * d * 4 + 3 * tq * MEM_LEN * 4)
    tile = lambda n: pl.BlockSpec((1, tq, n), lambda i, j: (i, j, 0))
    return pl.pallas_call(
        functools.partial(_attn_prompt_kernel, alpha=alpha),
        out_shape=jax.ShapeDtypeStruct((bn, t, d), F32),
        grid=(bn, t // tq),
        in_specs=[tile(nw), tile(W_D), tile(d),
                  pl.BlockSpec((None, 1, MEM_LEN, d), lambda i, j: (layer, i, 0, 0)),
                  pl.BlockSpec((None, 1, MEM_LEN, d), lambda i, j: (layer, i, 0, 0)),
                  pl.BlockSpec((None, nw, d), lambda i, j: (layer, 0, 0)),
                  pl.BlockSpec((None, W_D, d), lambda i, j: (layer, nw // W_D, 0)),
                  _lspec(g1, layer), _lspec(b1, layer),
                  _lspec(wq, layer), _lspec(wo, layer), _lspec(g, layer), _lspec(b, layer)],
        out_specs=tile(d),
        scratch_shapes=[pltpu.VMEM((tq, d), BF16)],
        compiler_params=_params(("parallel", "parallel"), nbytes),
        name="attn_prompt",
    )(mix, yd, h, mk, mv, w_mix, w_mix, g1, b1, wq, wo, g, b)


LANES = 128
SUBLANES = 8
LANE_TILES = X_HEAD_DIM // LANES
MEM_ROWS = MEM_LEN * LANE_TILES * X_HEADS


def _cache_rows_view(cache):
    nl, n = cache.shape[:2]
    x = cache.reshape(nl, n, MEM_LEN, X_HEADS, LANE_TILES, LANES)
    return x.transpose(0, 1, 2, 4, 3, 5).reshape(nl, n, MEM_ROWS, LANES)


def _attn_sample_kernel(q_ref, k_ref, v_ref, o_ref, *, bb):
    scale = X_HEAD_DIM ** -0.5
    shape = (SUBLANES, MEM_ROWS)
    rowi = lax.broadcasted_iota(jnp.int32, shape, 0)
    coli = lax.broadcasted_iota(jnp.int32, shape, 1)
    valid = (coli % SUBLANES) == rowi
    raw = [_dot(q_ref[i].astype(BF16), k_ref[i].astype(BF16), NT) for i in range(bb)]
    probs = []
    for r in raw:
        r = jnp.where(valid, r, 0.0)
        other = pltpu.roll(r, X_HEADS, 0)
        other = jnp.where(rowi < X_HEADS, pltpu.roll(other, MEM_ROWS - X_HEADS, 1), pltpu.roll(other, X_HEADS, 1))
        s = jnp.where(valid, (r + other) * scale, -jnp.inf)
        m = jnp.max(s, axis=-1, keepdims=True)
        e = jnp.exp(s - m)
        probs.append((e / jnp.sum(e, axis=-1, keepdims=True)).astype(BF16))
    for i in range(bb):
        o_ref[i] = _dot(probs[i], v_ref[i].astype(BF16))


def _attn_sample(q, cache_k, cache_v, layer, bb):
    n = q.shape[0]
    q8 = q.reshape(n, X_HEADS, LANE_TILES, LANES).transpose(0, 2, 1, 3).reshape(n, SUBLANES, LANES)
    nbytes = 4 * bb * MEM_ROWS * LANES * 4 + 2 * bb * MEM_ROWS * LANES * 2 + 8 * SUBLANES * MEM_ROWS * 4
    cache_spec = pl.BlockSpec((None, bb, MEM_ROWS, LANES), lambda i: (layer, i, 0, 0))
    o8 = pl.pallas_call(
        functools.partial(_attn_sample_kernel, bb=bb),
        out_shape=jax.ShapeDtypeStruct((n, SUBLANES, LANES), F32),
        grid=(n // bb,),
        in_specs=[pl.BlockSpec((bb, SUBLANES, LANES), lambda i: (i, 0, 0)), cache_spec, cache_spec],
        out_specs=pl.BlockSpec((bb, SUBLANES, LANES), lambda i: (i, 0, 0)),
        compiler_params=_params(("parallel",), nbytes),
        name="attn_sample",
    )(q8, _cache_rows_view(cache_k), _cache_rows_view(cache_v))
    return o8.reshape(n, LANE_TILES, X_HEADS, LANES).transpose(0, 2, 1, 3).reshape(n, D_MODEL)


def _pool_window_sums(ext, tt):
    s2 = ext + pltpu.roll(ext, 1, 0)
    s4 = s2 + pltpu.roll(s2, 2, 0)
    s8 = s4 + pltpu.roll(s4, 4, 0)
    s16 = s8 + pltpu.roll(s8, 8, 0)
    grp = lax.broadcasted_iota(jnp.int32, (tt, W_B), 1) // POOL_GROUP
    lo = ext.shape[0] - tt
    return jnp.where(grp == 0, s2[lo:], jnp.where(grp == 1, s4[lo:], jnp.where(grp == 2, s8[lo:], s16[lo:])))


def _proj_abc_kernel(x_ref, w_ref, lng_ref, lnb_ref, ws_ref, bias_ref, wpool_ref, pscale_ref, cw_ref,
                     pd_ref, mix_ref, vlast_ref, pool_ref, conv_ref, pcar_ref, ccar_ref, *, tt):
    t = pl.program_id(1)
    nt = pl.num_programs(1)

    @pl.when(t == 0)
    def _():
        pcar_ref[...] = jnp.zeros_like(pcar_ref)
        ccar_ref[...] = jnp.zeros_like(ccar_ref)

    rows = lax.broadcasted_iota(jnp.int32, (A_HEADS * CHUNK, CHUNK), 0) % CHUNK
    cols = lax.broadcasted_iota(jnp.int32, (A_HEADS * CHUNK, CHUNK), 1)
    wsm = jnp.where(rows >= cols, ws_ref[...], 0.0).astype(BF16)
    hid = lax.broadcasted_iota(jnp.int32, (CHUNK, W_A), 1) // A_HEAD_DIM
    win = jnp.left_shift(2, lax.broadcasted_iota(jnp.int32, (CHUNK, W_B), 1) // POOL_GROUP)
    rowi = lax.broadcasted_iota(jnp.int32, (CHUNK, W_B), 0)
    w = w_ref[...]
    pcar = pcar_ref[...]
    ccar = ccar_ref[...]
    v = None
    for c in range(tt // CHUNK):
        rs = slice(c * CHUNK, (c + 1) * CHUNK)
        y = _dot(x_ref[0, rs, :].astype(BF16), w)
        pd_ref[0, rs, :] = y[:, N_ABC:]

        ga = _gelu(y[:, :2 * W_A])
        u = ga[:, :W_A]
        v = _ln(ga[:, W_A:], lng_ref[...], lnb_ref[...])
        zz = _dot(wsm, v.astype(BF16))
        z = zz[(A_HEADS - 1) * CHUNK:]
        for hd in range(A_HEADS - 2, -1, -1):
            z = jnp.where(hid == hd, zz[hd * CHUNK:(hd + 1) * CHUNK], z)
        mix_ref[0, rs, 0:W_A] = (u * (z + bias_ref[...])).astype(mix_ref.dtype)

        xb = y[:, 2 * W_A:2 * W_A + W_B]
        ext = jnp.concatenate([pcar, xb], axis=0)
        sums = _pool_window_sums(ext, CHUNK)
        pos = t * tt + c * CHUNK + rowi
        cnt = jnp.minimum(win, pos + 1).astype(F32)
        d = sums / cnt - xb
        mix_ref[0, rs, W_A:W_A + W_B] = (_dot(d.astype(BF16), wpool_ref[...]) * pscale_ref[...]).astype(mix_ref.dtype)
        pcar = ext[CHUNK:]

        o = 2 * W_A + W_B
        bg = y[:, o:o + W_C]
        zc = y[:, o + W_C:o + 2 * W_C] * y[:, o + 2 * W_C:o + 3 * W_C]
        extz = jnp.concatenate([ccar, zc], axis=0)
        conv = (cw_ref[0:1, :] * pltpu.roll(extz, 2, 0) + cw_ref[1:2, :] * pltpu.roll(extz, 1, 0)
                + cw_ref[2:3, :] * extz)
        mix_ref[0, rs, W_A + W_B:] = (bg * conv[ROW_CARRY:]).astype(mix_ref.dtype)
        ccar = extz[CHUNK:]
    pcar_ref[...] = pcar
    ccar_ref[...] = ccar

    @pl.when(t == nt - 1)
    def _():
        vlast_ref[0] = v
        pool_ref[0] = pcar_ref[POOL_CARRY - POOL_BUF:, :]
        conv_ref[0] = ccar_ref[ROW_CARRY - (CONV_W - 1):, :]


def _proj_abc_prompt(h, w, lng, lnb, ws, bias, wpool, pscale, cw, layer, tt):
    bn, t, d = h.shape
    nw = W_A + W_B + W_C
    nbytes = (2 * tt * d * 4 + 2 * d * PROJ * 2 + 2 * tt * (D_PROJ + nw) * 4 + 3 * CHUNK * PROJ * 4
              + 4 * A_HEADS * CHUNK * CHUNK * 4)
    tile = lambda n: pl.BlockSpec((1, tt, n), lambda i, j: (i, j, 0))
    last = lambda r, n: pl.BlockSpec((1, r, n), lambda i, j: (i, 0, 0))
    return pl.pallas_call(
        functools.partial(_proj_abc_kernel, tt=tt),
        out_shape=(jax.ShapeDtypeStruct((bn, t, D_PROJ), F32),
                   jax.ShapeDtypeStruct((bn, t, nw), BF16),
                   jax.ShapeDtypeStruct((bn, CHUNK, W_A), F32),
                   jax.ShapeDtypeStruct((bn, POOL_BUF, W_B), F32),
                   jax.ShapeDtypeStruct((bn, CONV_W - 1, W_C), F32)),
        grid=(bn, t // tt),
        in_specs=[tile(d), *[_lspec(z, layer) for z in (w, lng, lnb, ws, bias, wpool, pscale, cw)]],
        out_specs=(tile(D_PROJ), tile(nw), last(CHUNK, W_A), last(POOL_BUF, W_B), last(CONV_W - 1, W_C)),
        scratch_shapes=[pltpu.VMEM((POOL_CARRY, W_B), F32), pltpu.VMEM((ROW_CARRY, W_C), F32)],
        compiler_params=_params(("parallel", "arbitrary"), nbytes),
        name="proj_abc_prompt",
    )(h, w, lng, lnb, ws, bias, wpool, pscale, cw)


def _abc_sample_kernel(x_ref, pool_ref, conv_ref, lng_ref, lnb_ref, ws0_ref, bias0_ref, wpool_ref,
                       pscale_ref, cw_ref, mix_ref, v_ref, poolo_ref, convo_ref, *, pos0):
    x = x_ref[...]
    n = x.shape[0]
    ga = _gelu(x[:, :2 * W_A])
    u = ga[:, :W_A]
    v = _ln(ga[:, W_A:], lng_ref[...], lnb_ref[...])
    v_ref[...] = v
    mix_ref[:, 0:W_A] = u * (v * ws0_ref[...] + bias0_ref[0:1, :])

    xb = x[:, 2 * W_A:2 * W_A + W_B]
    grp = lax.broadcasted_iota(jnp.int32, (n, W_B), 1) // POOL_GROUP
    run = xb
    sums = jnp.zeros_like(xb)
    for back in range(1, max(POOL_WINDOWS)):
        run = run + pool_ref[POOL_BUF - back]
        for gi, win in enumerate(POOL_WINDOWS):
            if back == win - 1:
                sums = jnp.where(grp == gi, run, sums)
    win = jnp.left_shift(2, grp)
    cnt = jnp.minimum(win, pos0 + 1).astype(F32)
    d = sums / cnt - xb
    mix_ref[:, W_A:W_A + W_B] = _dot(d.astype(BF16), wpool_ref[...]) * pscale_ref[...]
    for r in range(POOL_BUF - 1):
        poolo_ref[r] = pool_ref[r + 1]
    poolo_ref[POOL_BUF - 1] = xb

    o = 2 * W_A + W_B
    bg = x[:, o:o + W_C]
    zc = x[:, o + W_C:o + 2 * W_C] * x[:, o + 2 * W_C:o + 3 * W_C]
    z0 = conv_ref[:, :W_C]
    z1 = conv_ref[:, W_C:]
    y = cw_ref[0:1, :] * z0 + cw_ref[1:2, :] * z1 + cw_ref[2:3, :] * zc
    mix_ref[:, W_A + W_B:] = bg * y
    convo_ref[:, :W_C] = z1
    convo_ref[:, W_C:] = zc


def _abc_sample(pabc, pool, conv, lng, lnb, ws0, bias, wpool, pscale, cw, layer, pos0):
    n = pabc.shape[0]
    nw = W_A + W_B + W_C
    whole = lambda shape: pl.BlockSpec(shape, lambda i: (0,) * len(shape))
    return pl.pallas_call(
        functools.partial(_abc_sample_kernel, pos0=pos0),
        out_shape=(jax.ShapeDtypeStruct((n, nw), F32),
                   jax.ShapeDtypeStruct((n, W_A), F32),
                   jax.ShapeDtypeStruct(pool.shape[1:], F32),
                   jax.ShapeDtypeStruct(conv.shape[1:], F32)),
        grid=(1,),
        in_specs=[whole(pabc.shape)] + [_lspec(z, layer) for z in (pool, conv, lng, lnb, ws0, bias, wpool, pscale, cw)],
        out_specs=(whole((n, nw)), whole((n, W_A)), whole(pool.shape[1:]), whole(conv.shape[1:])),
        name="mixer_abc_sample",
    )(pabc, pool, conv, lng, lnb, ws0, bias, wpool, pscale, cw)


def _rwkv_inputs(xs, w0, w2, a0, a2, g2, kk_w, ka_w, seg):
    r = xs[:, 0:W_D]
    k = xs[:, W_D:2 * W_D]
    v = xs[:, 2 * W_D:3 * W_D]
    o = 3 * W_D
    dw = xs[:, o:o + R_DECAY]
    da = xs[:, o + R_DECAY:o + R_DECAY + R_AAA]
    dg = xs[:, o + R_DECAY + R_AAA:]
    w_log = -_softplus(-(w0 + _dot_hi(jnp.tanh(dw), w2))) - 0.5
    logdecay = -jnp.exp(w_log)
    a = _sigmoid(a0 + _dot_hi(da, a2))
    g = _dot_hi(_sigmoid(dg), g2)
    kk = k * kk_w
    kk = kk * lax.rsqrt(jnp.maximum(_dot_ones(kk * kk, seg), 1e-12))
    k = k * (1.0 + (a - 1.0) * ka_w)
    return r, k, v, kk, a, logdecay, g


def _rwkv_finish(o, r, k, v, g, rk_w, lnx_g, lnx_b, seg):
    inv = 1.0 / D_HEAD_DIM
    mu = _dot_ones(o, seg) * inv
    oc = o - mu
    var = _dot_ones(oc * oc, seg) * inv
    on = oc * lax.rsqrt(var + GN_EPS) * lnx_g + lnx_b
    bonus = _dot_ones(r * k * rk_w, seg) * v
    return (on + bonus) * g


def _bdot(a, b, dims=NN):
    return _dot(a.astype(BF16), b.astype(BF16), dims)


def _head_cols(x, hd):
    return x[:, hd * D_HEAD_DIM:(hd + 1) * D_HEAD_DIM]


HEAD_PAIRS = W_D // LANES


def _pair_diag(y, low):
    zero = jnp.zeros_like(y)
    return jnp.concatenate([jnp.where(low, y, zero), jnp.where(low, zero, y)], axis=0)


def _wkv_tile(s_ref, o_ref, qt, rt, bt, kt, bbar, kbar, vm, gam, nc):
    c = WKV_CHUNK
    n = D_HEAD_DIM
    probs = [(ch, pr) for ch in range(nc) for pr in range(HEAD_PAIRS)]
    cut = lambda x, p: x[p[0] * c:(p[0] + 1) * c, p[1] * LANES:(p[1] + 1) * LANES]
    qt, rt, bt, kt, bbar, kbar, vm = (x.astype(BF16) for x in (qt, rt, bt, kt, bbar, kbar, vm))
    ri = lax.broadcasted_iota(jnp.int32, (c, LANES), 0)
    li = lax.broadcasted_iota(jnp.int32, (c, LANES), 1)
    low = li < n
    strict = ri > li % n
    incl = ri >= li % n
    eye = jnp.where(ri == li % n, 1.0, 0.0)
    diag = lambda y: _pair_diag(y, low)
    halves = lambda x: jnp.where(low, x[:n], x[n:])

    qs = {p: cut(qt, p) for p in probs}
    vd = {p: diag(cut(vm, p)) for p in probs}
    aa = {p: _dot(jnp.concatenate([qs[p], cut(rt, p)], axis=0),
                  jnp.concatenate([diag(cut(bt, p)), diag(cut(kt, p))], axis=0), NT) for p in probs}
    lk = {p: jnp.where(strict, aa[p][:c, LANES:], 0.0).astype(BF16) for p in probs}
    ab = {p: jnp.where(incl, aa[p][c:, :LANES], 0.0).astype(BF16) for p in probs}
    ak = {p: jnp.where(incl, aa[p][c:, LANES:], 0.0).astype(BF16) for p in probs}
    npow = {p: jnp.where(strict, -aa[p][:c, :LANES], 0.0).astype(BF16) for p in probs}
    tinv = {p: eye + npow[p].astype(F32) for p in probs}
    lv = {p: _dot(jnp.concatenate([lk[p], ak[p]], axis=0), vd[p]) for p in probs}
    zp = {p: lv[p][:c].astype(BF16) for p in probs}
    npow = {p: _dot(npow[p], diag(npow[p])).astype(BF16) for p in probs}
    for _ in range(int(math.log2(c)) - 2):
        both = {p: _dot(jnp.concatenate([tinv[p].astype(BF16), npow[p]], axis=0), diag(npow[p])) for p in probs}
        tinv = {p: tinv[p] + both[p][:c] for p in probs}
        npow = {p: both[p][c:].astype(BF16) for p in probs}
    tinv = {p: (tinv[p] + _dot(tinv[p].astype(BF16), diag(npow[p]))).astype(BF16) for p in probs}
    tq = {p: _dot(tinv[p], jnp.concatenate([diag(qs[p]), diag(zp[p])], axis=1)) for p in probs}
    qh = {p: tq[p][:, :LANES].astype(BF16) for p in probs}
    zn = {p: (-tq[p][:, LANES:]).astype(BF16) for p in probs}
    abq = {p: _dot(ab[p], jnp.concatenate([diag(qh[p]), diag(zn[p])], axis=1)) for p in probs}
    rh = {p: (cut(rt, p).astype(F32) - abq[p][:, :LANES]).astype(BF16) for p in probs}
    pv = {p: abq[p][:, LANES:] + lv[p][c:] for p in probs}
    gt = {p: halves(-_dot(qh[p], cut(bbar, p), TN)).astype(BF16) for p in probs}
    cst = {p: halves(_dot(jnp.concatenate([cut(vm, p), zn[p]], axis=0),
                          jnp.concatenate([cut(kbar, p), cut(bbar, p)], axis=0), TN)) for p in probs}
    for ch in range(nc):
        for pr in range(HEAD_PAIRS):
            p = (ch, pr)
            s0 = s_ref[pr]
            s0b = s0.astype(BF16)
            o_ref[ch * c:(ch + 1) * c, pr * LANES:(pr + 1) * LANES] = _dot(rh[p], diag(s0b), NT) + pv[p]
            s_ref[pr] = s0 * gam[ch][:, pr * LANES:(pr + 1) * LANES] + _dot(s0b, diag(gt[p])) + cst[p]


def _rwkv_prompt_kernel(pd_ref, mu_ref, w0_ref, w2_ref, a0_ref, a2_ref, g2_ref, kk_ref, ka_ref, rk_ref,
                        lg_ref, lb_ref, yd_ref, shift_ref, wkv_ref, car_ref, s_ref, o_ref, *, tt):
    t = pl.program_id(1)
    nt = pl.num_programs(1)

    @pl.when(t == 0)
    def _():
        car_ref[...] = jnp.zeros_like(car_ref)
        s_ref[...] = jnp.zeros_like(s_ref)

    pd = pd_ref[0]
    ext = jnp.concatenate([car_ref[...], pd], axis=0)
    prev = pltpu.roll(ext, 1, 0)[ROW_CARRY:]
    car_ref[...] = ext[tt:]
    xs = pd + (prev - pd) * mu_ref[...]
    seg = _head_ones(W_D, D_HEAD_DIM)
    r, k, v, kk, a, ld, g = _rwkv_inputs(xs, w0_ref[...], w2_ref[...], a0_ref[...], a2_ref[...],
                                         g2_ref[...], kk_ref[...], ka_ref[...], seg)
    c = WKV_CHUNK
    nc = tt // c
    tri = jnp.where(lax.broadcasted_iota(jnp.int32, (c, c), 0) >= lax.broadcasted_iota(jnp.int32, (c, c), 1),
                    1.0, 0.0).astype(BF16)
    cs_chunks = [_ones_dot(tri, ld[ch * c:(ch + 1) * c]) for ch in range(nc)]
    cs = jnp.concatenate(cs_chunks, axis=0)
    cs_end = [x[c - 1:c] for x in cs_chunks]
    cs_last = jnp.concatenate([jnp.broadcast_to(x, (c, W_D)) for x in cs_end], axis=0)
    e_neg = jnp.exp(-cs)
    e_tail = jnp.exp(cs_last - cs)
    b = kk * a
    _wkv_tile(s_ref, o_ref, kk * jnp.exp(cs - ld), r * jnp.exp(cs), b * e_neg, k * e_neg, b * e_tail, k * e_tail,
              v, [jnp.exp(x) for x in cs_end], nc)
    yd_ref[0] = _rwkv_finish(o_ref[...], r, k, v, g, rk_ref[...], lg_ref[...], lb_ref[...], seg).astype(yd_ref.dtype)

    @pl.when(t == nt - 1)
    def _():
        shift_ref[0] = pd[tt - 1:tt]
        for hd in range(D_HEADS):
            wkv_ref[0, hd] = _head_cols(s_ref[hd // 2], hd % 2)


def _rwkv_prompt(pd, params, layer, tt):
    bn, t, _ = pd.shape
    nbytes = 2 * tt * D_PROJ * 4 + 40 * tt * W_D * 4
    return pl.pallas_call(
        functools.partial(_rwkv_prompt_kernel, tt=tt),
        out_shape=(jax.ShapeDtypeStruct((bn, t, W_D), BF16),
                   jax.ShapeDtypeStruct((bn, 1, D_PROJ), F32),
                   jax.ShapeDtypeStruct((bn, D_HEADS, D_HEAD_DIM, D_HEAD_DIM), F32)),
        grid=(bn, t // tt),
        in_specs=[pl.BlockSpec((1, tt, D_PROJ), lambda i, j: (i, j, 0))] + [_lspec(z, layer) for z in params],
        out_specs=(pl.BlockSpec((1, tt, W_D), lambda i, j: (i, j, 0)),
                   pl.BlockSpec((1, 1, D_PROJ), lambda i, j: (i, 0, 0)),
                   pl.BlockSpec((1, D_HEADS, D_HEAD_DIM, D_HEAD_DIM), lambda i, j: (i, 0, 0, 0))),
        scratch_shapes=[pltpu.VMEM((ROW_CARRY, D_PROJ), F32),
                        pltpu.VMEM((HEAD_PAIRS, D_HEAD_DIM, LANES), F32),
                        pltpu.VMEM((tt, W_D), F32)],
        compiler_params=_params(("parallel", "arbitrary"), nbytes),
        name="rwkv_prompt",
    )(pd, *params)


def _rwkv_sample_kernel(pd_ref, sh_ref, st_ref, mu_ref, w0_ref, w2_ref, a0_ref, a2_ref, g2_ref, kk_ref,
                        ka_ref, rk_ref, lg_ref, lb_ref, yd_ref, so_ref, rows_ref, cols_ref, ot_ref):
    h = pl.program_id(0)
    n = D_HEAD_DIM

    @pl.when(h == 0)
    def _():
        pd = pd_ref[...]
        xs = pd + (sh_ref[...] - pd) * mu_ref[...]
        seg = _head_ones(W_D, D_HEAD_DIM)
        r, k, v, kk, a, ld, g = _rwkv_inputs(xs, w0_ref[...], w2_ref[...], a0_ref[...], a2_ref[...],
                                             g2_ref[...], kk_ref[...], ka_ref[...], seg)
        for j, x in enumerate((r, k, v, g)):
            rows_ref[j] = x
        for j, x in enumerate((kk, jnp.exp(ld), kk * a, k, r, v)):
            cols_ref[j] = x.T

    base = pl.multiple_of(h * n, n)
    kap, dec, bvec, kvec, rvec = (cols_ref[j, pl.ds(base, n), :] for j in range(5))
    for vi in range(n):
        s = st_ref[0, vi]
        u = -jnp.sum(s * kap, axis=0, keepdims=True)
        s = s * dec + u * bvec + cols_ref[5, pl.ds(base + vi, 1), :] * kvec
        so_ref[0, vi] = s
        ot_ref[pl.ds(base + vi, 1), :] = jnp.sum(s * rvec, axis=0, keepdims=True)

    @pl.when(h == pl.num_programs(0) - 1)
    def _():
        seg = _head_ones(W_D, D_HEAD_DIM)
        yd_ref[...] = _rwkv_finish(ot_ref[...].T, rows_ref[0], rows_ref[1], rows_ref[2], rows_ref[3],
                                   rk_ref[...], lg_ref[...], lb_ref[...], seg)


def _rwkv_sample(pd, shift, state, params, layer):
    n = pd.shape[0]
    sblock = (1, D_HEAD_DIM, D_HEAD_DIM, n)
    return pl.pallas_call(
        _rwkv_sample_kernel,
        out_shape=(jax.ShapeDtypeStruct((n, W_D), F32), jax.ShapeDtypeStruct(state.shape[1:], F32)),
        grid=(D_HEADS,),
        in_specs=[pl.BlockSpec((n, D_PROJ), lambda i: (0, 0)), _lspec(shift, layer),
                  pl.BlockSpec((None,) + sblock, lambda i: (layer, i, 0, 0, 0))]
                 + [_lspec(z, layer) for z in params],
        out_specs=(pl.BlockSpec((n, W_D), lambda i: (0, 0)), pl.BlockSpec(sblock, lambda i: (i, 0, 0, 0))),
        scratch_shapes=[pltpu.VMEM((4, n, W_D), F32), pltpu.VMEM((6, W_D, n), F32), pltpu.VMEM((W_D, n), F32)],
        compiler_params=pltpu.CompilerParams(dimension_semantics=("arbitrary",)),
        name="rwkv_sample",
    )(pd, shift, state, *params)


def _block_diag(w):
    gn, n, _ = w.shape
    eye = jnp.eye(gn, dtype=w.dtype)
    return (eye[:, None, :, None] * w[:, :, None, :]).reshape(gn * n, gn * n)


def kernel(x_prompt, x_sample, mem_prompt, cache_mem_k, cache_mem_v, state_pool, state_conv, state_shift, state_wkv,
           w_in, mu_d, ln_v_g, ln_v_b, ws_chunk, b_chunk, w_pool, pool_scale, conv_w,
           rwkv_w0, rwkv_w2, rwkv_a0, rwkv_a2, rwkv_g2, rwkv_k_k, rwkv_k_a, rwkv_r_k, rwkv_lnx_g, rwkv_lnx_b,
           w_out, ln1_g, ln1_b, w_xq, w_xk, w_xv, w_xo, ln2_g, ln2_b, ffn_w1, ffn_w3, ffn_w2, ln3_g, ln3_b):
    bp, t_p, d = x_prompt.shape
    ns, t_s, _ = x_sample.shape
    depth = w_in.shape[0]
    assert d == D_MODEL and t_s == 1 and t_p % CHUNK == 0 and w_in.shape[2] == PROJ
    alpha = (2 * depth) ** 0.25
    mp = bp * t_p
    nw = W_A + W_B + W_C
    row = lambda z: z.reshape(depth, 1, -1)

    w_in_b, w_out_b = w_in.astype(BF16), w_out.astype(BF16)
    w_xq_b, w_xk_b, w_xv_b, w_xo_b = (w.astype(BF16) for w in (w_xq, w_xk, w_xv, w_xo))
    w1_b, w3_b, w2_b = ffn_w1.astype(BF16), ffn_w3.astype(BF16), ffn_w2.astype(BF16)
    ws_flat = ws_chunk.reshape(depth, A_HEADS * CHUNK, CHUNK)
    bias_full = jnp.repeat(jnp.swapaxes(b_chunk, 1, 2), A_HEAD_DIM, axis=2)
    ws0 = jnp.repeat(ws_chunk[:, :, 0, 0], A_HEAD_DIM, axis=1).reshape(depth, 1, W_A)
    wpool_bd = jnp.stack([_block_diag(w_pool[l]) for l in range(depth)]).astype(BF16)
    abc_w = (row(ln_v_g), row(ln_v_b))
    abc_w2 = (wpool_bd, row(pool_scale), conv_w)
    rw = (row(mu_d), row(rwkv_w0), rwkv_w2, row(rwkv_a0), rwkv_a2, rwkv_g2,
          row(rwkv_k_k), row(rwkv_k_a), row(rwkv_r_k), row(rwkv_lnx_g), row(rwkv_lnx_b))
    ln1, ln2, ln3 = (row(ln1_g), row(ln1_b)), (row(ln2_g), row(ln2_b)), (row(ln3_g), row(ln3_b))
    pool_view = jnp.swapaxes(state_pool, 1, 2)
    wkv_view = state_wkv.transpose(0, 2, 3, 4, 1)
    conv_view = state_conv.reshape(depth, ns, (CONV_W - 1) * W_C)
    shift_view = state_shift.reshape(depth, ns, D_PROJ)

    hp = x_prompt
    hs = x_sample.reshape(ns, d)
    mem = mem_prompt.reshape(bp * MEM_LEN, d)
    mk_all, mv_all = _mem_kv(mem, w_xk_b, w_xv_b, 512)
    mk_all = mk_all.reshape(depth, bp, MEM_LEN, d)
    mv_all = mv_all.reshape(depth, bp, MEM_LEN, d)
    outs = [[] for _ in range(10)]
    for l in range(depth):
        pd, mix, v_last, pool_p, conv_p = _proj_abc_prompt(hp, w_in_b, *abc_w, ws_flat, bias_full, *abc_w2, l, tt=512)
        yd, shift_p, wkv_p = _rwkv_prompt(pd, rw, l, tt=512)
        hp = _attn_prompt(mix, yd, hp, mk_all, mv_all, w_out_b, *ln1, w_xq_b, w_xo_b, *ln2, l, tq=512, alpha=alpha)
        hp = _ffn(hp.reshape(mp, d), w1_b, w3_b, w2_b, *ln3, l, tm=1024, tf=256, alpha=alpha,
                  name="ffn").reshape(bp, t_p, d)
        for lst, val in zip(outs[:5], (v_last, pool_p, conv_p, shift_p, wkv_p)):
            lst.append(val)

        pabc_s, pd_s = _proj(hs, w_in_b, l, ns)
        mix_s, v_s, pool_s, conv_s = _abc_sample(pabc_s, pool_view, conv_view, *abc_w, ws0, bias_full, *abc_w2,
                                                 l, pos0=PAST_LEN)
        yd_s, wkv_s = _rwkv_sample(pd_s, shift_view, wkv_view, rw, l)
        hs = _mm_res_ln([mix_s, yd_s], w_out_b, l, hs, *ln1, tm=ns, alpha=alpha, name="out_proj_s")
        q_s = _mm(hs, w_xq_b, l, ns, "q_s")
        o_s = _attn_sample(q_s, cache_mem_k, cache_mem_v, l, bb=4)
        hs = _mm_res_ln([o_s], w_xo_b, l, hs, *ln2, tm=ns, alpha=alpha, name="xo_s")
        hs = _ffn(hs, w1_b, w3_b, w2_b, *ln3, l, tm=ns, tf=256, alpha=alpha, name="ffn_s")
        for lst, val in zip(outs[5:], (v_s.reshape(ns, 1, W_A), pool_s,
                                       conv_s.reshape(ns, CONV_W - 1, W_C), pd_s.reshape(ns, 1, D_PROJ), wkv_s)):
            lst.append(val)

    stacked = [jnp.stack(o) for o in outs]
    stacked[6] = jnp.swapaxes(stacked[6], 1, 2)
    stacked[9] = stacked[9].transpose(0, 4, 1, 2, 3)
    mem_shape = (depth, bp, MEM_LEN, X_HEADS, X_HEAD_DIM)
    return ((hp, hs.reshape(ns, 1, d)) + tuple(stacked[:5]) + (mk_all.reshape(mem_shape), mv_all.reshape(mem_shape))
            + tuple(stacked[5:]))
```

```python
import functools
import math

import jax
import jax.numpy as jnp
from jax import lax
from jax.experimental import pallas as pl
from jax.experimental.pallas import tpu as pltpu

F32 = jnp.float32
BF16 = jnp.bfloat16

D_MODEL = 1024
W_A = 256
W_B = 256
W_C = 256
W_D = 256
A_HEADS = 4
A_HEAD_DIM = W_A // A_HEADS
CHUNK = 128
POOL_WINDOWS = (2, 4, 8, 16)
POOL_GROUP = W_B // len(POOL_WINDOWS)
POOL_BUF = max(POOL_WINDOWS) - 1
CONV_W = 3
D_HEAD_DIM = 64
D_HEADS = W_D // D_HEAD_DIM
R_DECAY = 32
R_AAA = 32
R_GATE = 64
D_PROJ = 3 * W_D + R_DECAY + R_AAA + R_GATE
N_ABC = 2 * W_A + W_B + 3 * W_C
PROJ = N_ABC + D_PROJ
MEM_LEN = 256
X_HEADS = 4
X_HEAD_DIM = D_MODEL // X_HEADS
D_FF = int(math.ceil(8 * D_MODEL / 3 / 256)) * 256
PAST_LEN = 16384
LN_EPS = 1e-5
GN_EPS = 64e-5

WKV_CHUNK = 64
POOL_CARRY = 24
ROW_CARRY = 8
V7X_VMEM_BYTES = 64 * 1024 * 1024
VMEM_CAP = V7X_VMEM_BYTES - 8 * 1024 * 1024

NN = (((1,), (0,)), ((), ()))
NT = (((1,), (1,)), ((), ()))
TN = (((0,), (0,)), ((), ()))


def _vmem_limit(nbytes):
    return int(min(VMEM_CAP, max(32 * 1024 * 1024, 2 * nbytes)))


def _params(sem, nbytes):
    return pltpu.CompilerParams(dimension_semantics=sem, vmem_limit_bytes=_vmem_limit(nbytes))


def _lspec(arr, layer):
    tail = arr.shape[1:]
    zeros = (0,) * len(tail)
    return pl.BlockSpec((None,) + tail, lambda *_: (layer,) + zeros)


def _dot(a, b, dims=NN):
    return lax.dot_general(a, b, dims, preferred_element_type=F32)


def _split2(a):
    hi = a.astype(BF16)
    lo = (a - hi.astype(F32)).astype(BF16)
    return hi, lo


def _dot_hi(a, b, dims=NN):
    ah, al = _split2(a)
    bh, bl = _split2(b)
    return _dot(ah, bh, dims) + _dot(ah, bl, dims) + _dot(al, bh, dims)


def _dot_ones(x, ones_bf16, dims=NN):
    hi, lo = _split2(x)
    return _dot(hi, ones_bf16, dims) + _dot(lo, ones_bf16, dims)


def _ones_dot(ones_bf16, x):
    hi = x.astype(BF16)
    r1 = x - hi.astype(F32)
    mid = r1.astype(BF16)
    lo = (r1 - mid.astype(F32)).astype(BF16)
    return _dot(ones_bf16, hi) + _dot(ones_bf16, mid) + _dot(ones_bf16, lo)


def _ln(x, g, b, eps=LN_EPS):
    mu = jnp.mean(x, axis=-1, keepdims=True)
    xc = x - mu
    var = jnp.mean(xc * xc, axis=-1, keepdims=True)
    return xc * lax.rsqrt(var + eps) * g + b


def _gelu(x):
    c = math.sqrt(2.0 / math.pi)
    return x * (0.5 * (1.0 + jnp.tanh(c * (x + 0.044715 * (x * x * x)))))


def _sigmoid(x):
    return 1.0 / (1.0 + jnp.exp(-x))


def _softplus(x):
    return jnp.maximum(x, 0.0) + jnp.log(1.0 + jnp.exp(-jnp.abs(x)))


def _head_ones(n, group):
    r = lax.broadcasted_iota(jnp.int32, (n, n), 0) // group
    c = lax.broadcasted_iota(jnp.int32, (n, n), 1) // group
    return jnp.where(r == c, 1.0, 0.0).astype(BF16)


def _proj_kernel(x_ref, w_ref, oabc_ref, od_ref):
    y = _dot(x_ref[...].astype(BF16), w_ref[...].astype(BF16))
    oabc_ref[...] = y[:, :N_ABC]
    od_ref[...] = y[:, N_ABC:]


def _proj(x, w, layer, tm):
    m, k = x.shape
    nbytes = 2 * (tm * k * 4 + k * PROJ * 4 + tm * PROJ * 4) + tm * PROJ * 4 + k * PROJ * 2
    return pl.pallas_call(
        _proj_kernel,
        out_shape=(jax.ShapeDtypeStruct((m, N_ABC), F32), jax.ShapeDtypeStruct((m, D_PROJ), F32)),
        grid=(m // tm,),
        in_specs=[pl.BlockSpec((tm, k), lambda i: (i, 0)), _lspec(w, layer)],
        out_specs=(pl.BlockSpec((tm, N_ABC), lambda i: (i, 0)), pl.BlockSpec((tm, D_PROJ), lambda i: (i, 0))),
        compiler_params=_params(("parallel",), nbytes),
        name="proj",
    )(x, w)


def _mm_kernel(x_ref, w_ref, o_ref):
    o_ref[...] = _dot(x_ref[...].astype(BF16), w_ref[...].astype(BF16))


def _mm(x, w, layer, tm, name):
    m, k = x.shape
    n = w.shape[2]
    nbytes = 2 * (tm * k * 4 + k * n * 4 + tm * n * 4) + tm * n * 4 + k * n * 2
    return pl.pallas_call(
        _mm_kernel,
        out_shape=jax.ShapeDtypeStruct((m, n), F32),
        grid=(m // tm,),
        in_specs=[pl.BlockSpec((tm, k), lambda i: (i, 0)), _lspec(w, layer)],
        out_specs=pl.BlockSpec((tm, n), lambda i: (i, 0)),
        compiler_params=_params(("parallel",), nbytes),
        name=name,
    )(x, w)


def _mem_kv_kernel(x_ref, wk_ref, wv_ref, k_ref, v_ref, wkb_ref, wvb_ref):
    @pl.when(pl.program_id(1) == 0)
    def _():
        wkb_ref[...] = wk_ref[...].astype(BF16)
        wvb_ref[...] = wv_ref[...].astype(BF16)

    xb = x_ref[...].astype(BF16)
    k_ref[...] = _dot(xb, wkb_ref[...])
    v_ref[...] = _dot(xb, wvb_ref[...])


def _mem_kv(mem, wk, wv, tm):
    m, k = mem.shape
    nl, _, n = wk.shape
    nbytes = 2 * (tm * k * 4 + 2 * k * n * 4 + 2 * tm * n * 4) + 2 * tm * n * 4 + 2 * k * n * 2
    wspec = pl.BlockSpec((None, k, n), lambda l, i: (l, 0, 0))
    ospec = pl.BlockSpec((None, tm, n), lambda l, i: (l, i, 0))
    shape = jax.ShapeDtypeStruct((nl, m, n), F32)
    return pl.pallas_call(
        _mem_kv_kernel,
        out_shape=(shape, shape),
        grid=(nl, m // tm),
        in_specs=[pl.BlockSpec((tm, k), lambda l, i: (i, 0)), wspec, wspec],
        out_specs=(ospec, ospec),
        scratch_shapes=[pltpu.VMEM((k, n), BF16), pltpu.VMEM((k, n), BF16)],
        compiler_params=_params(("arbitrary", "arbitrary"), nbytes),
        name="mem_kv",
    )(mem, wk, wv)


def _mm_res_ln_kernel(*refs, n_in, alpha):
    xs = refs[:n_in]
    ws = refs[n_in:2 * n_in]
    h_ref, g_ref, b_ref, o_ref = refs[2 * n_in:]
    y = _dot(xs[0][...].astype(BF16), ws[0][...].astype(BF16))
    for x_ref, w_ref in zip(xs[1:], ws[1:]):
        y = y + _dot(x_ref[...].astype(BF16), w_ref[...].astype(BF16))
    o_ref[...] = _ln(alpha * h_ref[...] + y, g_ref[...], b_ref[...])


def _mm_res_ln(xs, w, layer, h, g, b, tm, alpha, name):
    m, n = h.shape
    nbytes = 2 * sum(tm * x.shape[1] * 4 + x.shape[1] * n * 4 for x in xs) + 5 * tm * n * 4 + w.shape[1] * n * 2
    in_specs = [pl.BlockSpec((tm, x.shape[1]), lambda i: (i, 0)) for x in xs]
    start = 0
    for x in xs:
        width = x.shape[1]
        assert start % width == 0
        in_specs.append(pl.BlockSpec((None, width, n), lambda i, blk=start // width: (layer, blk, 0)))
        start += width
    assert start == w.shape[1]
    in_specs += [pl.BlockSpec((tm, n), lambda i: (i, 0)), _lspec(g, layer), _lspec(b, layer)]
    return pl.pallas_call(
        functools.partial(_mm_res_ln_kernel, n_in=len(xs), alpha=alpha),
        out_shape=jax.ShapeDtypeStruct((m, n), F32),
        grid=(m // tm,),
        in_specs=in_specs,
        out_specs=pl.BlockSpec((tm, n), lambda i: (i, 0)),
        compiler_params=_params(("parallel",), nbytes),
        name=name,
    )(*xs, *([w] * len(xs)), h, g, b)


def _ffn_kernel(x_ref, w1_ref, w3_ref, w2_ref, g_ref, b_ref, o_ref, xb_ref, acc_ref, *, alpha):
    j = pl.program_id(1)

    @pl.when(j == 0)
    def _():
        xb_ref[...] = x_ref[...].astype(BF16)
        acc_ref[...] = jnp.zeros_like(acc_ref)

    xb = xb_ref[...]
    h1 = _dot(xb, w1_ref[...].astype(BF16))
    h3 = _dot(xb, w3_ref[...].astype(BF16))
    a = (h1 * _sigmoid(h1) * h3).astype(BF16)
    acc_ref[...] += _dot(a, w2_ref[...].astype(BF16))

    @pl.when(j == pl.num_programs(1) - 1)
    def _():
        o_ref[...] = _ln(alpha * x_ref[...] + acc_ref[...], g_ref[...], b_ref[...])


def _ffn(x, w1, w3, w2, g, b, layer, tm, tf, alpha, name):
    m, d = x.shape
    nbytes = 4 * tm * d * 4 + tm * d * 2 + tm * d * 4 + 3 * 3 * d * tf * 4 + 3 * tm * tf * 4
    return pl.pallas_call(
        functools.partial(_ffn_kernel, alpha=alpha),
        out_shape=jax.ShapeDtypeStruct((m, d), F32),
        grid=(m // tm, D_FF // tf),
        in_specs=[pl.BlockSpec((tm, d), lambda i, j: (i, 0)),
                  pl.BlockSpec((None, d, tf), lambda i, j: (layer, 0, j)),
                  pl.BlockSpec((None, d, tf), lambda i, j: (layer, 0, j)),
                  pl.BlockSpec((None, tf, d), lambda i, j: (layer, j, 0)),
                  _lspec(g, layer), _lspec(b, layer)],
        out_specs=pl.BlockSpec((tm, d), lambda i, j: (i, 0)),
        scratch_shapes=[pltpu.VMEM((tm, d), BF16), pltpu.VMEM((tm, d), F32)],
        compiler_params=_params(("parallel", "arbitrary"), nbytes),
        name=name,
    )(x, w1, w3, w2, g, b)


def _softmax_rows(s):
    m = jnp.max(s, axis=-1, keepdims=True)
    e = jnp.exp(s - m)
    return e / jnp.sum(e, axis=-1, keepdims=True)


ATTN_PIECES = 2


def _attn_prompt_kernel(mix_ref, yd_ref, h_ref, mk_ref, mv_ref, wma_ref, wmd_ref, g1_ref, b1_ref,
                        wq_ref, wo_ref, g_ref, b_ref, o_ref, ob_ref, wmb_ref, wqb_ref, wob_ref, *, alpha):
    nw = wma_ref.shape[0]

    @pl.when((pl.program_id(0) == 0) & (pl.program_id(1) == 0))
    def _():
        wmb_ref[:nw, :] = wma_ref[...].astype(BF16)
        wmb_ref[nw:, :] = wmd_ref[...].astype(BF16)
        wqb_ref[...] = wq_ref[...].astype(BF16)
        wob_ref[...] = wo_ref[...].astype(BF16)

    rows = h_ref.shape[1] // ATTN_PIECES
    scale = X_HEAD_DIM ** -0.5
    sls = [slice(hd * X_HEAD_DIM, (hd + 1) * X_HEAD_DIM) for hd in range(X_HEADS)]
    kb = [mk_ref[0, :, sl].astype(BF16) for sl in sls]
    vb = [mv_ref[0, :, sl].astype(BF16) for sl in sls]
    y, h, q, sc = {}, {}, {}, {}

    def project(p, rs):
        y[p] = (_dot(mix_ref[0, rs, :].astype(BF16), wmb_ref[:nw, :])
                + _dot(yd_ref[0, rs, :].astype(BF16), wmb_ref[nw:, :]))

    def query(p, rs):
        h[p] = _ln(alpha * h_ref[0, rs, :] + y[p], g1_ref[...], b1_ref[...])
        q[p] = _dot(h[p].astype(BF16), wqb_ref[...]).astype(BF16)

    def scores(p, rs):
        sc[p] = [_dot(q[p][:, sl], k, NT) * scale for sl, k in zip(sls, kb)]

    def values(p, rs):
        for sl, s, v in zip(sls, sc[p], vb):
            ob_ref[rs, sl] = _dot(_softmax_rows(s).astype(BF16), v).astype(BF16)

    def output(p, rs):
        o_ref[0, rs, :] = _ln(alpha * h[p] + _dot(ob_ref[rs, :], wob_ref[...]), g_ref[...], b_ref[...])

    stages = (project, query, scores, values, output)
    for step in range(len(stages) + ATTN_PIECES - 1):
        for p in range(ATTN_PIECES):
            if 0 <= step - p < len(stages):
                stages[step - p](p, slice(p * rows, (p + 1) * rows))


def _attn_prompt(mix, yd, h, mk, mv, w_mix, g1, b1, wq, wo, g, b, layer, tq, alpha):
    bn, t, d = h.shape
    nw = mix.shape[2]
    assert nw % W_D == 0 and w_mix.shape[1] == nw + W_D
    nbytes = (6 * tq * d * 4 + 4 * MEM_LEN * d * 4 + 6 * d * d * 4 + 3 * d * d * 2 + tq * d * 2
              + 3 * tq * d * 4 + 3 * tq * MEM_LEN * 4)
    tile = lambda n: pl.BlockSpec((1, tq, n), lambda i, j: (i, j, 0))
    return pl.pallas_call(
        functools.partial(_attn_prompt_kernel, alpha=alpha),
        out_shape=jax.ShapeDtypeStruct((bn, t, d), F32),
        grid=(bn, t // tq),
        in_specs=[tile(nw), tile(W_D), tile(d),
                  pl.BlockSpec((None, 1, MEM_LEN, d), lambda i, j: (layer, i, 0, 0)),
                  pl.BlockSpec((None, 1, MEM_LEN, d), lambda i, j: (layer, i, 0, 0)),
                  pl.BlockSpec((None, nw, d), lambda i, j: (layer, 0, 0)),
                  pl.BlockSpec((None, W_D, d), lambda i, j: (layer, nw // W_D, 0)),
                  _lspec(g1, layer), _lspec(b1, layer),
                  _lspec(wq, layer), _lspec(wo, layer), _lspec(g, layer), _lspec(b, layer)],
        out_specs=tile(d),
        scratch_shapes=[pltpu.VMEM((tq, d), BF16), pltpu.VMEM((nw + W_D, d), BF16),
                        pltpu.VMEM((d, d), BF16), pltpu.VMEM((d, d), BF16)],
        compiler_params=_params(("arbitrary", "arbitrary"), nbytes),
        name="attn_prompt",
    )(mix, yd, h, mk, mv, w_mix, w_mix, g1, b1, wq, wo, g, b)


LANES = 128
SUBLANES = 8
LANE_TILES = X_HEAD_DIM // LANES
MEM_ROWS = MEM_LEN * LANE_TILES * X_HEADS


def _cache_rows_view(cache):
    nl, n = cache.shape[:2]
    x = cache.reshape(nl, n, MEM_LEN, X_HEADS, LANE_TILES, LANES)
    return x.transpose(0, 1, 2, 4, 3, 5).reshape(nl, n, MEM_ROWS, LANES)


def _attn_sample_kernel(q_ref, k_ref, v_ref, o_ref, *, bb):
    scale = X_HEAD_DIM ** -0.5
    shape = (SUBLANES, MEM_ROWS)
    rowi = lax.broadcasted_iota(jnp.int32, shape, 0)
    coli = lax.broadcasted_iota(jnp.int32, shape, 1)
    valid = (coli % SUBLANES) == rowi
    raw = [_dot(q_ref[i].astype(BF16), k_ref[i].astype(BF16), NT) for i in range(bb)]
    probs = []
    for r in raw:
        r = jnp.where(valid, r, 0.0)
        other = pltpu.roll(r, X_HEADS, 0)
        other = jnp.where(rowi < X_HEADS, pltpu.roll(other, MEM_ROWS - X_HEADS, 1), pltpu.roll(other, X_HEADS, 1))
        s = jnp.where(valid, (r + other) * scale, -jnp.inf)
        m = jnp.max(s, axis=-1, keepdims=True)
        e = jnp.exp(s - m)
        probs.append((e / jnp.sum(e, axis=-1, keepdims=True)).astype(BF16))
    for i in range(bb):
        o_ref[i] = _dot(probs[i], v_ref[i].astype(BF16))


def _attn_sample(q, cache_k, cache_v, layer, bb):
    n = q.shape[0]
    q8 = q.reshape(n, X_HEADS, LANE_TILES, LANES).transpose(0, 2, 1, 3).reshape(n, SUBLANES, LANES)
    nbytes = 4 * bb * MEM_ROWS * LANES * 4 + 2 * bb * MEM_ROWS * LANES * 2 + 8 * SUBLANES * MEM_ROWS * 4
    cache_spec = pl.BlockSpec((None, bb, MEM_ROWS, LANES), lambda i: (layer, i, 0, 0))
    o8 = pl.pallas_call(
        functools.partial(_attn_sample_kernel, bb=bb),
        out_shape=jax.ShapeDtypeStruct((n, SUBLANES, LANES), F32),
        grid=(n // bb,),
        in_specs=[pl.BlockSpec((bb, SUBLANES, LANES), lambda i: (i, 0, 0)), cache_spec, cache_spec],
        out_specs=pl.BlockSpec((bb, SUBLANES, LANES), lambda i: (i, 0, 0)),
        compiler_params=_params(("parallel",), nbytes),
        name="attn_sample",
    )(q8, _cache_rows_view(cache_k), _cache_rows_view(cache_v))
    return o8.reshape(n, LANE_TILES, X_HEADS, LANES).transpose(0, 2, 1, 3).reshape(n, D_MODEL)


def _pool_window_sums(ext, tt):
    s2 = ext + pltpu.roll(ext, 1, 0)
    s4 = s2 + pltpu.roll(s2, 2, 0)
    s8 = s4 + pltpu.roll(s4, 4, 0)
    s16 = s8 + pltpu.roll(s8, 8, 0)
    grp = lax.broadcasted_iota(jnp.int32, (tt, W_B), 1) // POOL_GROUP
    lo = ext.shape[0] - tt
    return jnp.where(grp == 0, s2[lo:], jnp.where(grp == 1, s4[lo:], jnp.where(grp == 2, s8[lo:], s16[lo:])))


def _proj_abc_kernel(x_ref, w_ref, lng_ref, lnb_ref, ws_ref, bias_ref, wpool_ref, pscale_ref, cw_ref,
                     pd_ref, mix_ref, vlast_ref, pool_ref, conv_ref, pcar_ref, ccar_ref, wb_ref, *, tt):
    t = pl.program_id(1)
    nt = pl.num_programs(1)

    @pl.when((pl.program_id(0) == 0) & (t == 0))
    def _():
        wb_ref[...] = w_ref[...].astype(BF16)

    @pl.when(t == 0)
    def _():
        pcar_ref[...] = jnp.zeros_like(pcar_ref)
        ccar_ref[...] = jnp.zeros_like(ccar_ref)

    rows = lax.broadcasted_iota(jnp.int32, (A_HEADS * CHUNK, CHUNK), 0) % CHUNK
    cols = lax.broadcasted_iota(jnp.int32, (A_HEADS * CHUNK, CHUNK), 1)
    wsm = jnp.where(rows >= cols, ws_ref[...], 0.0).astype(BF16)
    hid = lax.broadcasted_iota(jnp.int32, (CHUNK, W_A), 1) // A_HEAD_DIM
    win = jnp.left_shift(2, lax.broadcasted_iota(jnp.int32, (CHUNK, W_B), 1) // POOL_GROUP)
    rowi = lax.broadcasted_iota(jnp.int32, (CHUNK, W_B), 0)
    w = wb_ref[...]
    pcar = pcar_ref[...]
    ccar = ccar_ref[...]
    v = None
    for c in range(tt // CHUNK):
        rs = slice(c * CHUNK, (c + 1) * CHUNK)
        y = _dot(x_ref[0, rs, :].astype(BF16), w)
        pd_ref[0, rs, :] = y[:, N_ABC:]

        ga = _gelu(y[:, :2 * W_A])
        u = ga[:, :W_A]
        v = _ln(ga[:, W_A:], lng_ref[...], lnb_ref[...])
        zz = _dot(wsm, v.astype(BF16))
        z = zz[(A_HEADS - 1) * CHUNK:]
        for hd in range(A_HEADS - 2, -1, -1):
            z = jnp.where(hid == hd, zz[hd * CHUNK:(hd + 1) * CHUNK], z)
        mix_ref[0, rs, 0:W_A] = (u * (z + bias_ref[...])).astype(mix_ref.dtype)

        xb = y[:, 2 * W_A:2 * W_A + W_B]
        ext = jnp.concatenate([pcar, xb], axis=0)
        sums = _pool_window_sums(ext, CHUNK)
        pos = t * tt + c * CHUNK + rowi
        cnt = jnp.minimum(win, pos + 1).astype(F32)
        d = sums / cnt - xb
        mix_ref[0, rs, W_A:W_A + W_B] = (_dot(d.astype(BF16), wpool_ref[...]) * pscale_ref[...]).astype(mix_ref.dtype)
        pcar = ext[CHUNK:]

        o = 2 * W_A + W_B
        bg = y[:, o:o + W_C]
        zc = y[:, o + W_C:o + 2 * W_C] * y[:, o + 2 * W_C:o + 3 * W_C]
        extz = jnp.concatenate([ccar, zc], axis=0)
        conv = (cw_ref[0:1, :] * pltpu.roll(extz, 2, 0) + cw_ref[1:2, :] * pltpu.roll(extz, 1, 0)
                + cw_ref[2:3, :] * extz)
        mix_ref[0, rs, W_A + W_B:] = (bg * conv[ROW_CARRY:]).astype(mix_ref.dtype)
        ccar = extz[CHUNK:]
    pcar_ref[...] = pcar
    ccar_ref[...] = ccar

    @pl.when(t == nt - 1)
    def _():
        vlast_ref[0] = v
        pool_ref[0] = pcar_ref[POOL_CARRY - POOL_BUF:, :]
        conv_ref[0] = ccar_ref[ROW_CARRY - (CONV_W - 1):, :]


def _proj_abc_prompt(h, w, lng, lnb, ws, bias, wpool, pscale, cw, layer, tt):
    bn, t, d = h.shape
    nw = W_A + W_B + W_C
    nbytes = (2 * tt * d * 4 + 2 * d * PROJ * 4 + d * PROJ * 2 + 2 * tt * (D_PROJ + nw) * 4 + 3 * CHUNK * PROJ * 4
              + 4 * A_HEADS * CHUNK * CHUNK * 4)
    tile = lambda n: pl.BlockSpec((1, tt, n), lambda i, j: (i, j, 0))
    last = lambda r, n: pl.BlockSpec((1, r, n), lambda i, j: (i, 0, 0))
    return pl.pallas_call(
        functools.partial(_proj_abc_kernel, tt=tt),
        out_shape=(jax.ShapeDtypeStruct((bn, t, D_PROJ), F32),
                   jax.ShapeDtypeStruct((bn, t, nw), BF16),
                   jax.ShapeDtypeStruct((bn, CHUNK, W_A), F32),
                   jax.ShapeDtypeStruct((bn, POOL_BUF, W_B), F32),
                   jax.ShapeDtypeStruct((bn, CONV_W - 1, W_C), F32)),
        grid=(bn, t // tt),
        in_specs=[tile(d), *[_lspec(z, layer) for z in (w, lng, lnb, ws, bias, wpool, pscale, cw)]],
        out_specs=(tile(D_PROJ), tile(nw), last(CHUNK, W_A), last(POOL_BUF, W_B), last(CONV_W - 1, W_C)),
        scratch_shapes=[pltpu.VMEM((POOL_CARRY, W_B), F32), pltpu.VMEM((ROW_CARRY, W_C), F32),
                        pltpu.VMEM((d, PROJ), BF16)],
        compiler_params=_params(("arbitrary", "arbitrary"), nbytes),
        name="proj_abc_prompt",
    )(h, w, lng, lnb, ws, bias, wpool, pscale, cw)


def _abc_sample_kernel(x_ref, pool_ref, conv_ref, lng_ref, lnb_ref, ws0_ref, bias0_ref, wpool_ref,
                       pscale_ref, cw_ref, mix_ref, v_ref, poolo_ref, convo_ref, *, pos0):
    x = x_ref[...]
    n = x.shape[0]
    ga = _gelu(x[:, :2 * W_A])
    u = ga[:, :W_A]
    v = _ln(ga[:, W_A:], lng_ref[...], lnb_ref[...])
    v_ref[...] = v
    mix_ref[:, 0:W_A] = u * (v * ws0_ref[...] + bias0_ref[0:1, :])

    xb = x[:, 2 * W_A:2 * W_A + W_B]
    grp = lax.broadcasted_iota(jnp.int32, (n, W_B), 1) // POOL_GROUP
    run = xb
    sums = jnp.zeros_like(xb)
    for back in range(1, max(POOL_WINDOWS)):
        run = run + pool_ref[POOL_BUF - back]
        for gi, win in enumerate(POOL_WINDOWS):
            if back == win - 1:
                sums = jnp.where(grp == gi, run, sums)
    win = jnp.left_shift(2, grp)
    cnt = jnp.minimum(win, pos0 + 1).astype(F32)
    d = sums / cnt - xb
    mix_ref[:, W_A:W_A + W_B] = _dot(d.astype(BF16), wpool_ref[...]) * pscale_ref[...]
    for r in range(POOL_BUF - 1):
        poolo_ref[r] = pool_ref[r + 1]
    poolo_ref[POOL_BUF - 1] = xb

    o = 2 * W_A + W_B
    bg = x[:, o:o + W_C]
    zc = x[:, o + W_C:o + 2 * W_C] * x[:, o + 2 * W_C:o + 3 * W_C]
    z0 = conv_ref[:, :W_C]
    z1 = conv_ref[:, W_C:]
    y = cw_ref[0:1, :] * z0 + cw_ref[1:2, :] * z1 + cw_ref[2:3, :] * zc
    mix_ref[:, W_A + W_B:] = bg * y
    convo_ref[:, :W_C] = z1
    convo_ref[:, W_C:] = zc


def _abc_sample(pabc, pool, conv, lng, lnb, ws0, bias, wpool, pscale, cw, layer, pos0):
    n = pabc.shape[0]
    nw = W_A + W_B + W_C
    whole = lambda shape: pl.BlockSpec(shape, lambda i: (0,) * len(shape))
    return pl.pallas_call(
        functools.partial(_abc_sample_kernel, pos0=pos0),
        out_shape=(jax.ShapeDtypeStruct((n, nw), F32),
                   jax.ShapeDtypeStruct((n, W_A), F32),
                   jax.ShapeDtypeStruct(pool.shape[1:], F32),
                   jax.ShapeDtypeStruct(conv.shape[1:], F32)),
        grid=(1,),
        in_specs=[whole(pabc.shape)] + [_lspec(z, layer) for z in (pool, conv, lng, lnb, ws0, bias, wpool, pscale, cw)],
        out_specs=(whole((n, nw)), whole((n, W_A)), whole(pool.shape[1:]), whole(conv.shape[1:])),
        name="mixer_abc_sample",
    )(pabc, pool, conv, lng, lnb, ws0, bias, wpool, pscale, cw)


def _rwkv_inputs(xs, w0, w2, a0, a2, g2, kk_w, ka_w, seg):
    r = xs[:, 0:W_D]
    k = xs[:, W_D:2 * W_D]
    v = xs[:, 2 * W_D:3 * W_D]
    o = 3 * W_D
    dw = xs[:, o:o + R_DECAY]
    da = xs[:, o + R_DECAY:o + R_DECAY + R_AAA]
    dg = xs[:, o + R_DECAY + R_AAA:]
    w_log = -_softplus(-(w0 + _dot_hi(jnp.tanh(dw), w2))) - 0.5
    logdecay = -jnp.exp(w_log)
    a = _sigmoid(a0 + _dot_hi(da, a2))
    g = _dot_hi(_sigmoid(dg), g2)
    kk = k * kk_w
    kk = kk * lax.rsqrt(jnp.maximum(_dot_ones(kk * kk, seg), 1e-12))
    k = k * (1.0 + (a - 1.0) * ka_w)
    return r, k, v, kk, a, logdecay, g


def _rwkv_finish(o, r, k, v, g, rk_w, lnx_g, lnx_b, seg):
    inv = 1.0 / D_HEAD_DIM
    mu = _dot_ones(o, seg) * inv
    oc = o - mu
    var = _dot_ones(oc * oc, seg) * inv
    on = oc * lax.rsqrt(var + GN_EPS) * lnx_g + lnx_b
    bonus = _dot_ones(r * k * rk_w, seg) * v
    return (on + bonus) * g


def _bdot(a, b, dims=NN):
    return _dot(a.astype(BF16), b.astype(BF16), dims)


def _head_cols(x, hd):
    return x[:, hd * D_HEAD_DIM:(hd + 1) * D_HEAD_DIM]


HEAD_PAIRS = W_D // LANES


def _pair_diag(y, low):
    zero = jnp.zeros_like(y)
    return jnp.concatenate([jnp.where(low, y, zero), jnp.where(low, zero, y)], axis=0)


def _wkv_tile(s_ref, o_ref, qt, rt, bt, kt, bbar, kbar, vm, gam, nc):
    c = WKV_CHUNK
    n = D_HEAD_DIM
    probs = [(ch, pr) for ch in range(nc) for pr in range(HEAD_PAIRS)]
    cut = lambda x, p: x[p[0] * c:(p[0] + 1) * c, p[1] * LANES:(p[1] + 1) * LANES]
    qt, rt, bt, kt, bbar, kbar, vm = (x.astype(BF16) for x in (qt, rt, bt, kt, bbar, kbar, vm))
    ri = lax.broadcasted_iota(jnp.int32, (c, LANES), 0)
    li = lax.broadcasted_iota(jnp.int32, (c, LANES), 1)
    low = li < n
    strict = ri > li % n
    incl = ri >= li % n
    eye = jnp.where(ri == li % n, 1.0, 0.0)
    diag = lambda y: _pair_diag(y, low)
    halves = lambda x: jnp.where(low, x[:n], x[n:])

    qs = {p: cut(qt, p) for p in probs}
    vd = {p: diag(cut(vm, p)) for p in probs}
    aa = {p: _dot(jnp.concatenate([qs[p], cut(rt, p)], axis=0),
                  jnp.concatenate([diag(cut(bt, p)), diag(cut(kt, p))], axis=0), NT) for p in probs}
    lk = {p: jnp.where(strict, aa[p][:c, LANES:], 0.0).astype(BF16) for p in probs}
    ab = {p: jnp.where(incl, aa[p][c:, :LANES], 0.0).astype(BF16) for p in probs}
    ak = {p: jnp.where(incl, aa[p][c:, LANES:], 0.0).astype(BF16) for p in probs}
    npow = {p: jnp.where(strict, -aa[p][:c, :LANES], 0.0).astype(BF16) for p in probs}
    tinv = {p: eye + npow[p].astype(F32) for p in probs}
    lv = {p: _dot(jnp.concatenate([lk[p], ak[p]], axis=0), vd[p]) for p in probs}
    zp = {p: lv[p][:c].astype(BF16) for p in probs}
    npow = {p: _dot(npow[p], diag(npow[p])).astype(BF16) for p in probs}
    for _ in range(int(math.log2(c)) - 2):
        both = {p: _dot(jnp.concatenate([tinv[p].astype(BF16), npow[p]], axis=0), diag(npow[p])) for p in probs}
        tinv = {p: tinv[p] + both[p][:c] for p in probs}
        npow = {p: both[p][c:].astype(BF16) for p in probs}
    tinv = {p: (tinv[p] + _dot(tinv[p].astype(BF16), diag(npow[p]))).astype(BF16) for p in probs}
    tq = {p: _dot(tinv[p], jnp.concatenate([diag(qs[p]), diag(zp[p])], axis=1)) for p in probs}
    qh = {p: tq[p][:, :LANES].astype(BF16) for p in probs}
    zn = {p: (-tq[p][:, LANES:]).astype(BF16) for p in probs}
    abq = {p: _dot(ab[p], jnp.concatenate([diag(qh[p]), diag(zn[p])], axis=1)) for p in probs}
    rh = {p: (cut(rt, p).astype(F32) - abq[p][:, :LANES]).astype(BF16) for p in probs}
    pv = {p: abq[p][:, LANES:] + lv[p][c:] for p in probs}
    gt = {p: halves(-_dot(qh[p], cut(bbar, p), TN)).astype(BF16) for p in probs}
    cst = {p: halves(_dot(jnp.concatenate([cut(vm, p), zn[p]], axis=0),
                          jnp.concatenate([cut(kbar, p), cut(bbar, p)], axis=0), TN)) for p in probs}
    for ch in range(nc):
        for pr in range(HEAD_PAIRS):
            p = (ch, pr)
            s0 = s_ref[pr]
            s0b = s0.astype(BF16)
            o_ref[ch * c:(ch + 1) * c, pr * LANES:(pr + 1) * LANES] = _dot(rh[p], diag(s0b), NT) + pv[p]
            s_ref[pr] = s0 * gam[ch][:, pr * LANES:(pr + 1) * LANES] + _dot(s0b, diag(gt[p])) + cst[p]


def _rwkv_prompt_kernel(pd_ref, mu_ref, w0_ref, w2_ref, a0_ref, a2_ref, g2_ref, kk_ref, ka_ref, rk_ref,
                        lg_ref, lb_ref, yd_ref, shift_ref, wkv_ref, car_ref, s_ref, o_ref, *, tt):
    t = pl.program_id(1)
    nt = pl.num_programs(1)

    @pl.when(t == 0)
    def _():
        car_ref[...] = jnp.zeros_like(car_ref)
        s_ref[...] = jnp.zeros_like(s_ref)

    pd = pd_ref[0]
    ext = jnp.concatenate([car_ref[...], pd], axis=0)
    prev = pltpu.roll(ext, 1, 0)[ROW_CARRY:]
    car_ref[...] = ext[tt:]
    xs = pd + (prev - pd) * mu_ref[...]
    seg = _head_ones(W_D, D_HEAD_DIM)
    r, k, v, kk, a, ld, g = _rwkv_inputs(xs, w0_ref[...], w2_ref[...], a0_ref[...], a2_ref[...],
                                         g2_ref[...], kk_ref[...], ka_ref[...], seg)
    c = WKV_CHUNK
    nc = tt // c
    tri = jnp.where(lax.broadcasted_iota(jnp.int32, (c, c), 0) >= lax.broadcasted_iota(jnp.int32, (c, c), 1),
                    1.0, 0.0).astype(BF16)
    cs_chunks = [_ones_dot(tri, ld[ch * c:(ch + 1) * c]) for ch in range(nc)]
    cs = jnp.concatenate(cs_chunks, axis=0)
    cs_end = [x[c - 1:c] for x in cs_chunks]
    cs_last = jnp.concatenate([jnp.broadcast_to(x, (c, W_D)) for x in cs_end], axis=0)
    e_neg = jnp.exp(-cs)
    e_tail = jnp.exp(cs_last - cs)
    b = kk * a
    _wkv_tile(s_ref, o_ref, kk * jnp.exp(cs - ld), r * jnp.exp(cs), b * e_neg, k * e_neg, b * e_tail, k * e_tail,
              v, [jnp.exp(x) for x in cs_end], nc)
    yd_ref[0] = _rwkv_finish(o_ref[...], r, k, v, g, rk_ref[...], lg_ref[...], lb_ref[...], seg).astype(yd_ref.dtype)

    @pl.when(t == nt - 1)
    def _():
        shift_ref[0] = pd[tt - 1:tt]
        for hd in range(D_HEADS):
            wkv_ref[0, hd] = _head_cols(s_ref[hd // 2], hd % 2)


def _rwkv_prompt(pd, params, layer, tt):
    bn, t, _ = pd.shape
    nbytes = 2 * tt * D_PROJ * 4 + 40 * tt * W_D * 4
    return pl.pallas_call(
        functools.partial(_rwkv_prompt_kernel, tt=tt),
        out_shape=(jax.ShapeDtypeStruct((bn, t, W_D), BF16),
                   jax.ShapeDtypeStruct((bn, 1, D_PROJ), F32),
                   jax.ShapeDtypeStruct((bn, D_HEADS, D_HEAD_DIM, D_HEAD_DIM), F32)),
        grid=(bn, t // tt),
        in_specs=[pl.BlockSpec((1, tt, D_PROJ), lambda i, j: (i, j, 0))] + [_lspec(z, layer) for z in params],
        out_specs=(pl.BlockSpec((1, tt, W_D), lambda i, j: (i, j, 0)),
                   pl.BlockSpec((1, 1, D_PROJ), lambda i, j: (i, 0, 0)),
                   pl.BlockSpec((1, D_HEADS, D_HEAD_DIM, D_HEAD_DIM), lambda i, j: (i, 0, 0, 0))),
        scratch_shapes=[pltpu.VMEM((ROW_CARRY, D_PROJ), F32),
                        pltpu.VMEM((HEAD_PAIRS, D_HEAD_DIM, LANES), F32),
                        pltpu.VMEM((tt, W_D), F32)],
        compiler_params=_params(("parallel", "arbitrary"), nbytes),
        name="rwkv_prompt",
    )(pd, *params)


def _rwkv_sample_kernel(pd_ref, sh_ref, st_ref, mu_ref, w0_ref, w2_ref, a0_ref, a2_ref, g2_ref, kk_ref,
                        ka_ref, rk_ref, lg_ref, lb_ref, yd_ref, so_ref, rows_ref, cols_ref, ot_ref):
    h = pl.program_id(0)
    n = D_HEAD_DIM

    @pl.when(h == 0)
    def _():
        pd = pd_ref[...]
        xs = pd + (sh_ref[...] - pd) * mu_ref[...]
        seg = _head_ones(W_D, D_HEAD_DIM)
        r, k, v, kk, a, ld, g = _rwkv_inputs(xs, w0_ref[...], w2_ref[...], a0_ref[...], a2_ref[...],
                                             g2_ref[...], kk_ref[...], ka_ref[...], seg)
        for j, x in enumerate((r, k, v, g)):
            rows_ref[j] = x
        for j, x in enumerate((kk, jnp.exp(ld), kk * a, k, r, v)):
            cols_ref[j] = x.T

    base = pl.multiple_of(h * n, n)
    kap, dec, bvec, kvec, rvec = (cols_ref[j, pl.ds(base, n), :] for j in range(5))
    for vi in range(n):
        s = st_ref[0, vi]
        u = -jnp.sum(s * kap, axis=0, keepdims=True)
        s = s * dec + u * bvec + cols_ref[5, pl.ds(base + vi, 1), :] * kvec
        so_ref[0, vi] = s
        ot_ref[pl.ds(base + vi, 1), :] = jnp.sum(s * rvec, axis=0, keepdims=True)

    @pl.when(h == pl.num_programs(0) - 1)
    def _():
        seg = _head_ones(W_D, D_HEAD_DIM)
        yd_ref[...] = _rwkv_finish(ot_ref[...].T, rows_ref[0], rows_ref[1], rows_ref[2], rows_ref[3],
                                   rk_ref[...], lg_ref[...], lb_ref[...], seg)


def _rwkv_sample(pd, shift, state, params, layer):
    n = pd.shape[0]
    sblock = (1, D_HEAD_DIM, D_HEAD_DIM, n)
    return pl.pallas_call(
        _rwkv_sample_kernel,
        out_shape=(jax.ShapeDtypeStruct((n, W_D), F32), jax.ShapeDtypeStruct(state.shape[1:], F32)),
        grid=(D_HEADS,),
        in_specs=[pl.BlockSpec((n, D_PROJ), lambda i: (0, 0)), _lspec(shift, layer),
                  pl.BlockSpec((None,) + sblock, lambda i: (layer, i, 0, 0, 0))]
                 + [_lspec(z, layer) for z in params],
        out_specs=(pl.BlockSpec((n, W_D), lambda i: (0, 0)), pl.BlockSpec(sblock, lambda i: (i, 0, 0, 0))),
        scratch_shapes=[pltpu.VMEM((4, n, W_D), F32), pltpu.VMEM((6, W_D, n), F32), pltpu.VMEM((W_D, n), F32)],
        compiler_params=pltpu.CompilerParams(dimension_semantics=("arbitrary",)),
        name="rwkv_sample",
    )(pd, shift, state, *params)


def _block_diag(w):
    gn, n, _ = w.shape
    eye = jnp.eye(gn, dtype=w.dtype)
    return (eye[:, None, :, None] * w[:, :, None, :]).reshape(gn * n, gn * n)


def kernel(x_prompt, x_sample, mem_prompt, cache_mem_k, cache_mem_v, state_pool, state_conv, state_shift, state_wkv,
           w_in, mu_d, ln_v_g, ln_v_b, ws_chunk, b_chunk, w_pool, pool_scale, conv_w,
           rwkv_w0, rwkv_w2, rwkv_a0, rwkv_a2, rwkv_g2, rwkv_k_k, rwkv_k_a, rwkv_r_k, rwkv_lnx_g, rwkv_lnx_b,
           w_out, ln1_g, ln1_b, w_xq, w_xk, w_xv, w_xo, ln2_g, ln2_b, ffn_w1, ffn_w3, ffn_w2, ln3_g, ln3_b):
    bp, t_p, d = x_prompt.shape
    ns, t_s, _ = x_sample.shape
    depth = w_in.shape[0]
    assert d == D_MODEL and t_s == 1 and t_p % CHUNK == 0 and w_in.shape[2] == PROJ
    alpha = (2 * depth) ** 0.25
    mp = bp * t_p
    nw = W_A + W_B + W_C
    row = lambda z: z.reshape(depth, 1, -1)

    w_in_b, w_out_b = w_in, w_out
    w_xq_b, w_xk_b, w_xv_b, w_xo_b = w_xq, w_xk, w_xv, w_xo
    w1_b, w3_b, w2_b = ffn_w1, ffn_w3, ffn_w2
    ws_flat = ws_chunk.reshape(depth, A_HEADS * CHUNK, CHUNK)
    bias_full = jnp.repeat(jnp.swapaxes(b_chunk, 1, 2), A_HEAD_DIM, axis=2)
    ws0 = jnp.repeat(ws_chunk[:, :, 0, 0], A_HEAD_DIM, axis=1).reshape(depth, 1, W_A)
    wpool_bd = jnp.stack([_block_diag(w_pool[l]) for l in range(depth)]).astype(BF16)
    abc_w = (row(ln_v_g), row(ln_v_b))
    abc_w2 = (wpool_bd, row(pool_scale), conv_w)
    rw = (row(mu_d), row(rwkv_w0), rwkv_w2, row(rwkv_a0), rwkv_a2, rwkv_g2,
          row(rwkv_k_k), row(rwkv_k_a), row(rwkv_r_k), row(rwkv_lnx_g), row(rwkv_lnx_b))
    ln1, ln2, ln3 = (row(ln1_g), row(ln1_b)), (row(ln2_g), row(ln2_b)), (row(ln3_g), row(ln3_b))
    pool_view = jnp.swapaxes(state_pool, 1, 2)
    wkv_view = state_wkv.transpose(0, 2, 3, 4, 1)
    conv_view = state_conv.reshape(depth, ns, (CONV_W - 1) * W_C)
    shift_view = state_shift.reshape(depth, ns, D_PROJ)

    hp = x_prompt
    hs = x_sample.reshape(ns, d)
    mem = mem_prompt.reshape(bp * MEM_LEN, d)
    mk_all, mv_all = _mem_kv(mem, w_xk_b, w_xv_b, 512)
    mk_all = mk_all.reshape(depth, bp, MEM_LEN, d)
    mv_all = mv_all.reshape(depth, bp, MEM_LEN, d)
    outs = [[] for _ in range(10)]
    for l in range(depth):
        pd, mix, v_last, pool_p, conv_p = _proj_abc_prompt(hp, w_in_b, *abc_w, ws_flat, bias_full, *abc_w2, l, tt=512)
        yd, shift_p, wkv_p = _rwkv_prompt(pd, rw, l, tt=512)
        hp = _attn_prompt(mix, yd, hp, mk_all, mv_all, w_out_b, *ln1, w_xq_b, w_xo_b, *ln2, l, tq=512, alpha=alpha)
        hp = _ffn(hp.reshape(mp, d), w1_b, w3_b, w2_b, *ln3, l, tm=1024, tf=256, alpha=alpha,
                  name="ffn").reshape(bp, t_p, d)
        for lst, val in zip(outs[:5], (v_last, pool_p, conv_p, shift_p, wkv_p)):
            lst.append(val)

        pabc_s, pd_s = _proj(hs, w_in_b, l, ns)
        mix_s, v_s, pool_s, conv_s = _abc_sample(pabc_s, pool_view, conv_view, *abc_w, ws0, bias_full, *abc_w2,
                                                 l, pos0=PAST_LEN)
        yd_s, wkv_s = _rwkv_sample(pd_s, shift_view, wkv_view, rw, l)
        hs = _mm_res_ln([mix_s, yd_s], w_out_b, l, hs, *ln1, tm=ns, alpha=alpha, name="out_proj_s")
        q_s = _mm(hs, w_xq_b, l, ns, "q_s")
        o_s = _attn_sample(q_s, cache_mem_k, cache_mem_v, l, bb=4)
        hs = _mm_res_ln([o_s], w_xo_b, l, hs, *ln2, tm=ns, alpha=alpha, name="xo_s")
        hs = _ffn(hs, w1_b, w3_b, w2_b, *ln3, l, tm=ns, tf=256, alpha=alpha, name="ffn_s")
        for lst, val in zip(outs[5:], (v_s.reshape(ns, 1, W_A), pool_s,
                                       conv_s.reshape(ns, CONV_W - 1, W_C), pd_s.reshape(ns, 1, D_PROJ), wkv_s)):
            lst.append(val)

    stacked = [jnp.stack(o) for o in outs]
    stacked[6] = jnp.swapaxes(stacked[6], 1, 2)
    stacked[9] = stacked[9].transpose(0, 4, 1, 2, 3)
    mem_shape = (depth, bp, MEM_LEN, X_HEADS, X_HEAD_DIM)
    return ((hp, hs.reshape(ns, 1, d)) + tuple(stacked[:5]) + (mk_all.reshape(mem_shape), mv_all.reshape(mem_shape))
            + tuple(stacked[5:]))
```

```python
import functools
import math

import jax
import jax.numpy as jnp
from jax import lax
from jax.experimental import pallas as pl
from jax.experimental.pallas import tpu as pltpu

F32 = jnp.float32
BF16 = jnp.bfloat16

D_MODEL = 1024
W_A = 256
W_B = 256
W_C = 256
W_D = 256
A_HEADS = 4
A_HEAD_DIM = W_A // A_HEADS
CHUNK = 128
POOL_WINDOWS = (2, 4, 8, 16)
POOL_GROUP = W_B // len(POOL_WINDOWS)
POOL_BUF = max(POOL_WINDOWS) - 1
CONV_W = 3
D_HEAD_DIM = 64
D_HEADS = W_D // D_HEAD_DIM
R_DECAY = 32
R_AAA = 32
R_GATE = 64
D_PROJ = 3 * W_D + R_DECAY + R_AAA + R_GATE
N_ABC = 2 * W_A + W_B + 3 * W_C
PROJ = N_ABC + D_PROJ
MEM_LEN = 256
X_HEADS = 4
X_HEAD_DIM = D_MODEL // X_HEADS
D_FF = int(math.ceil(8 * D_MODEL / 3 / 256)) * 256
PAST_LEN = 16384
LN_EPS = 1e-5
GN_EPS = 64e-5

WKV_CHUNK = 64
POOL_CARRY = 24
ROW_CARRY = 8
V7X_VMEM_BYTES = 64 * 1024 * 1024
VMEM_CAP = V7X_VMEM_BYTES - 8 * 1024 * 1024

NN = (((1,), (0,)), ((), ()))
NT = (((1,), (1,)), ((), ()))
TN = (((0,), (0,)), ((), ()))


def _vmem_limit(nbytes):
    return int(min(VMEM_CAP, max(32 * 1024 * 1024, 2 * nbytes)))


def _params(sem, nbytes):
    return pltpu.CompilerParams(dimension_semantics=sem, vmem_limit_bytes=_vmem_limit(nbytes))


def _lspec(arr, layer):
    tail = arr.shape[1:]
    zeros = (0,) * len(tail)
    return pl.BlockSpec((None,) + tail, lambda *_: (layer,) + zeros)


def _dot(a, b, dims=NN):
    return lax.dot_general(a, b, dims, preferred_element_type=F32)


def _split2(a):
    hi = a.astype(BF16)
    lo = (a - hi.astype(F32)).astype(BF16)
    return hi, lo


def _dot_hi(a, b, dims=NN):
    ah, al = _split2(a)
    bh, bl = _split2(b)
    return _dot(ah, bh, dims) + _dot(ah, bl, dims) + _dot(al, bh, dims)


def _dot_ones(x, ones_bf16, dims=NN):
    hi, lo = _split2(x)
    return _dot(hi, ones_bf16, dims) + _dot(lo, ones_bf16, dims)


def _ones_dot(ones_bf16, x):
    hi = x.astype(BF16)
    r1 = x - hi.astype(F32)
    mid = r1.astype(BF16)
    lo = (r1 - mid.astype(F32)).astype(BF16)
    return _dot(ones_bf16, hi) + _dot(ones_bf16, mid) + _dot(ones_bf16, lo)


def _ln(x, g, b, eps=LN_EPS):
    mu = jnp.mean(x, axis=-1, keepdims=True)
    xc = x - mu
    var = jnp.mean(xc * xc, axis=-1, keepdims=True)
    return xc * lax.rsqrt(var + eps) * g + b


def _gelu(x):
    c = math.sqrt(2.0 / math.pi)
    return x * (0.5 * (1.0 + jnp.tanh(c * (x + 0.044715 * (x * x * x)))))


def _sigmoid(x):
    return 1.0 / (1.0 + jnp.exp(-x))


def _softplus(x):
    return jnp.maximum(x, 0.0) + jnp.log(1.0 + jnp.exp(-jnp.abs(x)))


def _interleave(*staged):
    live = list(staged)
    while live:
        for steps in list(live):
            if next(steps, StopIteration) is StopIteration:
                live.remove(steps)


def _head_ones(n, group):
    r = lax.broadcasted_iota(jnp.int32, (n, n), 0) // group
    c = lax.broadcasted_iota(jnp.int32, (n, n), 1) // group
    return jnp.where(r == c, 1.0, 0.0).astype(BF16)


def _proj_kernel(x_ref, w_ref, oabc_ref, od_ref):
    y = _dot(x_ref[...].astype(BF16), w_ref[...].astype(BF16))
    oabc_ref[...] = y[:, :N_ABC]
    od_ref[...] = y[:, N_ABC:]


def _proj(x, w, layer, tm):
    m, k = x.shape
    nbytes = 2 * (tm * k * 4 + k * PROJ * 4 + tm * PROJ * 4) + tm * PROJ * 4 + k * PROJ * 2
    return pl.pallas_call(
        _proj_kernel,
        out_shape=(jax.ShapeDtypeStruct((m, N_ABC), F32), jax.ShapeDtypeStruct((m, D_PROJ), F32)),
        grid=(m // tm,),
        in_specs=[pl.BlockSpec((tm, k), lambda i: (i, 0)), _lspec(w, layer)],
        out_specs=(pl.BlockSpec((tm, N_ABC), lambda i: (i, 0)), pl.BlockSpec((tm, D_PROJ), lambda i: (i, 0))),
        compiler_params=_params(("parallel",), nbytes),
        name="proj",
    )(x, w)


def _mm_kernel(x_ref, w_ref, o_ref):
    o_ref[...] = _dot(x_ref[...].astype(BF16), w_ref[...].astype(BF16))


def _mm(x, w, layer, tm, name):
    m, k = x.shape
    n = w.shape[2]
    nbytes = 2 * (tm * k * 4 + k * n * 4 + tm * n * 4) + tm * n * 4 + k * n * 2
    return pl.pallas_call(
        _mm_kernel,
        out_shape=jax.ShapeDtypeStruct((m, n), F32),
        grid=(m // tm,),
        in_specs=[pl.BlockSpec((tm, k), lambda i: (i, 0)), _lspec(w, layer)],
        out_specs=pl.BlockSpec((tm, n), lambda i: (i, 0)),
        compiler_params=_params(("parallel",), nbytes),
        name=name,
    )(x, w)


def _mem_kv_kernel(x_ref, wk_ref, wv_ref, k_ref, v_ref, wkb_ref, wvb_ref):
    @pl.when(pl.program_id(1) == 0)
    def _():
        wkb_ref[...] = wk_ref[...].astype(BF16)
        wvb_ref[...] = wv_ref[...].astype(BF16)

    xb = x_ref[...].astype(BF16)
    k_ref[...] = _dot(xb, wkb_ref[...])
    v_ref[...] = _dot(xb, wvb_ref[...])


def _mem_kv(mem, wk, wv, tm):
    m, k = mem.shape
    nl, _, n = wk.shape
    nbytes = 2 * (tm * k * 4 + 2 * k * n * 4 + 2 * tm * n * 4) + 2 * tm * n * 4 + 2 * k * n * 2
    wspec = pl.BlockSpec((None, k, n), lambda l, i: (l, 0, 0))
    ospec = pl.BlockSpec((None, tm, n), lambda l, i: (l, i, 0))
    shape = jax.ShapeDtypeStruct((nl, m, n), F32)
    return pl.pallas_call(
        _mem_kv_kernel,
        out_shape=(shape, shape),
        grid=(nl, m // tm),
        in_specs=[pl.BlockSpec((tm, k), lambda l, i: (i, 0)), wspec, wspec],
        out_specs=(ospec, ospec),
        scratch_shapes=[pltpu.VMEM((k, n), BF16), pltpu.VMEM((k, n), BF16)],
        compiler_params=_params(("arbitrary", "arbitrary"), nbytes),
        name="mem_kv",
    )(mem, wk, wv)


def _mm_res_ln_kernel(*refs, n_in, alpha):
    xs = refs[:n_in]
    ws = refs[n_in:2 * n_in]
    h_ref, g_ref, b_ref, o_ref = refs[2 * n_in:]
    y = _dot(xs[0][...].astype(BF16), ws[0][...].astype(BF16))
    for x_ref, w_ref in zip(xs[1:], ws[1:]):
        y = y + _dot(x_ref[...].astype(BF16), w_ref[...].astype(BF16))
    o_ref[...] = _ln(alpha * h_ref[...] + y, g_ref[...], b_ref[...])


def _mm_res_ln(xs, w, layer, h, g, b, tm, alpha, name):
    m, n = h.shape
    nbytes = 2 * sum(tm * x.shape[1] * 4 + x.shape[1] * n * 4 for x in xs) + 5 * tm * n * 4 + w.shape[1] * n * 2
    in_specs = [pl.BlockSpec((tm, x.shape[1]), lambda i: (i, 0)) for x in xs]
    start = 0
    for x in xs:
        width = x.shape[1]
        assert start % width == 0
        in_specs.append(pl.BlockSpec((None, width, n), lambda i, blk=start // width: (layer, blk, 0)))
        start += width
    assert start == w.shape[1]
    in_specs += [pl.BlockSpec((tm, n), lambda i: (i, 0)), _lspec(g, layer), _lspec(b, layer)]
    return pl.pallas_call(
        functools.partial(_mm_res_ln_kernel, n_in=len(xs), alpha=alpha),
        out_shape=jax.ShapeDtypeStruct((m, n), F32),
        grid=(m // tm,),
        in_specs=in_specs,
        out_specs=pl.BlockSpec((tm, n), lambda i: (i, 0)),
        compiler_params=_params(("parallel",), nbytes),
        name=name,
    )(*xs, *([w] * len(xs)), h, g, b)


def _ffn_kernel(x_ref, w1_ref, w3_ref, w2_ref, g_ref, b_ref, o_ref, xb_ref, acc_ref, *, alpha):
    j = pl.program_id(1)

    @pl.when(j == 0)
    def _():
        xb_ref[...] = x_ref[...].astype(BF16)
        acc_ref[...] = jnp.zeros_like(acc_ref)

    xb = xb_ref[...]
    h1 = _dot(xb, w1_ref[...].astype(BF16))
    h3 = _dot(xb, w3_ref[...].astype(BF16))
    a = (h1 * _sigmoid(h1) * h3).astype(BF16)
    acc_ref[...] += _dot(a, w2_ref[...].astype(BF16))

    @pl.when(j == pl.num_programs(1) - 1)
    def _():
        o_ref[...] = _ln(alpha * x_ref[...] + acc_ref[...], g_ref[...], b_ref[...])


def _ffn(x, w1, w3, w2, g, b, layer, tm, tf, alpha, name):
    m, d = x.shape
    nbytes = 4 * tm * d * 4 + tm * d * 2 + tm * d * 4 + 3 * 3 * d * tf * 4 + 3 * tm * tf * 4
    return pl.pallas_call(
        functools.partial(_ffn_kernel, alpha=alpha),
        out_shape=jax.ShapeDtypeStruct((m, d), F32),
        grid=(m // tm, D_FF // tf),
        in_specs=[pl.BlockSpec((tm, d), lambda i, j: (i, 0)),
                  pl.BlockSpec((None, d, tf), lambda i, j: (layer, 0, j)),
                  pl.BlockSpec((None, d, tf), lambda i, j: (layer, 0, j)),
                  pl.BlockSpec((None, tf, d), lambda i, j: (layer, j, 0)),
                  _lspec(g, layer), _lspec(b, layer)],
        out_specs=pl.BlockSpec((tm, d), lambda i, j: (i, 0)),
        scratch_shapes=[pltpu.VMEM((tm, d), BF16), pltpu.VMEM((tm, d), F32)],
        compiler_params=_params(("parallel", "arbitrary"), nbytes),
        name=name,
    )(x, w1, w3, w2, g, b)


def _softmax_rows(s):
    m = jnp.max(s, axis=-1, keepdims=True)
    e = jnp.exp(s - m)
    return e / jnp.sum(e, axis=-1, keepdims=True)


ATTN_PIECES = 2


def _attn_prompt_kernel(mix_ref, yd_ref, h_ref, mk_ref, mv_ref, wma_ref, wmd_ref, g1_ref, b1_ref,
                        wq_ref, wo_ref, g_ref, b_ref, o_ref, ob_ref, wmb_ref, wqb_ref, wob_ref, *, alpha):
    nw = wma_ref.shape[0]

    @pl.when((pl.program_id(0) == 0) & (pl.program_id(1) == 0))
    def _():
        wmb_ref[:nw, :] = wma_ref[...].astype(BF16)
        wmb_ref[nw:, :] = wmd_ref[...].astype(BF16)
        wqb_ref[...] = wq_ref[...].astype(BF16)
        wob_ref[...] = wo_ref[...].astype(BF16)

    rows = h_ref.shape[1] // ATTN_PIECES
    scale = X_HEAD_DIM ** -0.5
    sls = [slice(hd * X_HEAD_DIM, (hd + 1) * X_HEAD_DIM) for hd in range(X_HEADS)]
    kb = [mk_ref[0, :, sl].astype(BF16) for sl in sls]
    vb = [mv_ref[0, :, sl].astype(BF16) for sl in sls]
    y, h, q, sc = {}, {}, {}, {}

    def project(p, rs):
        y[p] = (_dot(mix_ref[0, rs, :].astype(BF16), wmb_ref[:nw, :])
                + _dot(yd_ref[0, rs, :].astype(BF16), wmb_ref[nw:, :]))

    def query(p, rs):
        h[p] = _ln(alpha * h_ref[0, rs, :] + y[p], g1_ref[...], b1_ref[...])
        q[p] = _dot(h[p].astype(BF16), wqb_ref[...]).astype(BF16)

    def scores(p, rs):
        sc[p] = [_dot(q[p][:, sl], k, NT) * scale for sl, k in zip(sls, kb)]

    def values(p, rs):
        for sl, s, v in zip(sls, sc[p], vb):
            ob_ref[rs, sl] = _dot(_softmax_rows(s).astype(BF16), v).astype(BF16)

    def output(p, rs):
        o_ref[0, rs, :] = _ln(alpha * h[p] + _dot(ob_ref[rs, :], wob_ref[...]), g_ref[...], b_ref[...])

    stages = (project, query, scores, values, output)
    for step in range(len(stages) + ATTN_PIECES - 1):
        for p in range(ATTN_PIECES):
            if 0 <= step - p < len(stages):
                stages[step - p](p, slice(p * rows, (p + 1) * rows))


def _attn_prompt(mix, yd, h, mk, mv, w_mix, g1, b1, wq, wo, g, b, layer, tq, alpha):
    bn, t, d = h.shape
    nw = mix.shape[2]
    assert nw % W_D == 0 and w_mix.shape[1] == nw + W_D
    nbytes = (6 * tq * d * 4 + 4 * MEM_LEN * d * 4 + 6 * d * d * 4 + 3 * d * d * 2 + tq * d * 2
              + 3 * tq * d * 4 + 3 * tq * MEM_LEN * 4)
    tile = lambda n: pl.BlockSpec((1, tq, n), lambda i, j: (i, j, 0))
    return pl.pallas_call(
        functools.partial(_attn_prompt_kernel, alpha=alpha),
        out_shape=jax.ShapeDtypeStruct((bn, t, d), F32),
        grid=(bn, t // tq),
        in_specs=[tile(nw), tile(W_D), tile(d),
                  pl.BlockSpec((None, 1, MEM_LEN, d), lambda i, j: (layer, i, 0, 0)),
                  pl.BlockSpec((None, 1, MEM_LEN, d), lambda i, j: (layer, i, 0, 0)),
                  pl.BlockSpec((None, nw, d), lambda i, j: (layer, 0, 0)),
                  pl.BlockSpec((None, W_D, d), lambda i, j: (layer, nw // W_D, 0)),
                  _lspec(g1, layer), _lspec(b1, layer),
                  _lspec(wq, layer), _lspec(wo, layer), _lspec(g, layer), _lspec(b, layer)],
        out_specs=tile(d),
        scratch_shapes=[pltpu.VMEM((tq, d), BF16), pltpu.VMEM((nw + W_D, d), BF16),
                        pltpu.VMEM((d, d), BF16), pltpu.VMEM((d, d), BF16)],
        compiler_params=_params(("arbitrary", "arbitrary"), nbytes),
        name="attn_prompt",
    )(mix, yd, h, mk, mv, w_mix, w_mix, g1, b1, wq, wo, g, b)


LANES = 128
SUBLANES = 8
LANE_TILES = X_HEAD_DIM // LANES
MEM_ROWS = MEM_LEN * LANE_TILES * X_HEADS


def _cache_rows_view(cache):
    nl, n = cache.shape[:2]
    x = cache.reshape(nl, n, MEM_LEN, X_HEADS, LANE_TILES, LANES)
    return x.transpose(0, 1, 2, 4, 3, 5).reshape(nl, n, MEM_ROWS, LANES)


def _attn_sample_steps(q_ref, k_ref, v_ref, o_ref, bb):
    scale = X_HEAD_DIM ** -0.5
    shape = (SUBLANES, MEM_ROWS)
    rowi = lax.broadcasted_iota(jnp.int32, shape, 0)
    coli = lax.broadcasted_iota(jnp.int32, shape, 1)
    valid = (coli % SUBLANES) == rowi
    raw = [_dot(q_ref[i].astype(BF16), k_ref[i].astype(BF16), NT) for i in range(bb)]
    yield
    probs = []
    for r in raw:
        r = jnp.where(valid, r, 0.0)
        other = pltpu.roll(r, X_HEADS, 0)
        other = jnp.where(rowi < X_HEADS, pltpu.roll(other, MEM_ROWS - X_HEADS, 1), pltpu.roll(other, X_HEADS, 1))
        sc = jnp.where(valid, (r + other) * scale, -jnp.inf)
        m = jnp.max(sc, axis=-1, keepdims=True)
        e = jnp.exp(sc - m)
        probs.append((e / jnp.sum(e, axis=-1, keepdims=True)).astype(BF16))
        yield
    for i in range(bb):
        o_ref[i] = _dot(probs[i], v_ref[i].astype(BF16))
        yield


def _to_pair_rows(q):
    n = q.shape[0]
    return q.reshape(n, X_HEADS, LANE_TILES, LANES).transpose(0, 2, 1, 3).reshape(n, SUBLANES, LANES)


def _from_pair_rows(o8):
    n = o8.shape[0]
    return o8.reshape(n, LANE_TILES, X_HEADS, LANES).transpose(0, 2, 1, 3).reshape(n, D_MODEL)


def _pool_window_sums(ext, tt):
    s2 = ext + pltpu.roll(ext, 1, 0)
    s4 = s2 + pltpu.roll(s2, 2, 0)
    s8 = s4 + pltpu.roll(s4, 4, 0)
    s16 = s8 + pltpu.roll(s8, 8, 0)
    grp = lax.broadcasted_iota(jnp.int32, (tt, W_B), 1) // POOL_GROUP
    lo = ext.shape[0] - tt
    return jnp.where(grp == 0, s2[lo:], jnp.where(grp == 1, s4[lo:], jnp.where(grp == 2, s8[lo:], s16[lo:])))


def _proj_abc_kernel(x_ref, w_ref, lng_ref, lnb_ref, ws_ref, bias_ref, wpool_ref, pscale_ref, cw_ref,
                     pd_ref, mix_ref, vlast_ref, pool_ref, conv_ref, pcar_ref, ccar_ref, wb_ref, *, tt):
    t = pl.program_id(1)
    nt = pl.num_programs(1)

    @pl.when((pl.program_id(0) == 0) & (t == 0))
    def _():
        wb_ref[...] = w_ref[...].astype(BF16)

    @pl.when(t == 0)
    def _():
        pcar_ref[...] = jnp.zeros_like(pcar_ref)
        ccar_ref[...] = jnp.zeros_like(ccar_ref)

    rows = lax.broadcasted_iota(jnp.int32, (A_HEADS * CHUNK, CHUNK), 0) % CHUNK
    cols = lax.broadcasted_iota(jnp.int32, (A_HEADS * CHUNK, CHUNK), 1)
    wsm = jnp.where(rows >= cols, ws_ref[...], 0.0).astype(BF16)
    hid = lax.broadcasted_iota(jnp.int32, (CHUNK, W_A), 1) // A_HEAD_DIM
    win = jnp.left_shift(2, lax.broadcasted_iota(jnp.int32, (CHUNK, W_B), 1) // POOL_GROUP)
    rowi = lax.broadcasted_iota(jnp.int32, (CHUNK, W_B), 0)
    w = wb_ref[...]
    pcar = pcar_ref[...]
    ccar = ccar_ref[...]
    v = None
    for c in range(tt // CHUNK):
        rs = slice(c * CHUNK, (c + 1) * CHUNK)
        y = _dot(x_ref[0, rs, :].astype(BF16), w)
        pd_ref[0, rs, :] = y[:, N_ABC:]

        ga = _gelu(y[:, :2 * W_A])
        u = ga[:, :W_A]
        v = _ln(ga[:, W_A:], lng_ref[...], lnb_ref[...])
        zz = _dot(wsm, v.astype(BF16))
        z = zz[(A_HEADS - 1) * CHUNK:]
        for hd in range(A_HEADS - 2, -1, -1):
            z = jnp.where(hid == hd, zz[hd * CHUNK:(hd + 1) * CHUNK], z)
        mix_ref[0, rs, 0:W_A] = (u * (z + bias_ref[...])).astype(mix_ref.dtype)

        xb = y[:, 2 * W_A:2 * W_A + W_B]
        ext = jnp.concatenate([pcar, xb], axis=0)
        sums = _pool_window_sums(ext, CHUNK)
        pos = t * tt + c * CHUNK + rowi
        cnt = jnp.minimum(win, pos + 1).astype(F32)
        d = sums / cnt - xb
        mix_ref[0, rs, W_A:W_A + W_B] = (_dot(d.astype(BF16), wpool_ref[...]) * pscale_ref[...]).astype(mix_ref.dtype)
        pcar = ext[CHUNK:]

        o = 2 * W_A + W_B
        bg = y[:, o:o + W_C]
        zc = y[:, o + W_C:o + 2 * W_C] * y[:, o + 2 * W_C:o + 3 * W_C]
        extz = jnp.concatenate([ccar, zc], axis=0)
        conv = (cw_ref[0:1, :] * pltpu.roll(extz, 2, 0) + cw_ref[1:2, :] * pltpu.roll(extz, 1, 0)
                + cw_ref[2:3, :] * extz)
        mix_ref[0, rs, W_A + W_B:] = (bg * conv[ROW_CARRY:]).astype(mix_ref.dtype)
        ccar = extz[CHUNK:]
    pcar_ref[...] = pcar
    ccar_ref[...] = ccar

    @pl.when(t == nt - 1)
    def _():
        vlast_ref[0] = v
        pool_ref[0] = pcar_ref[POOL_CARRY - POOL_BUF:, :]
        conv_ref[0] = ccar_ref[ROW_CARRY - (CONV_W - 1):, :]


def _proj_abc_prompt(h, w, lng, lnb, ws, bias, wpool, pscale, cw, layer, tt):
    bn, t, d = h.shape
    nw = W_A + W_B + W_C
    nbytes = (2 * tt * d * 4 + 2 * d * PROJ * 4 + d * PROJ * 2 + 2 * tt * (D_PROJ + nw) * 4 + 3 * CHUNK * PROJ * 4
              + 4 * A_HEADS * CHUNK * CHUNK * 4)
    tile = lambda n: pl.BlockSpec((1, tt, n), lambda i, j: (i, j, 0))
    last = lambda r, n: pl.BlockSpec((1, r, n), lambda i, j: (i, 0, 0))
    return pl.pallas_call(
        functools.partial(_proj_abc_kernel, tt=tt),
        out_shape=(jax.ShapeDtypeStruct((bn, t, D_PROJ), F32),
                   jax.ShapeDtypeStruct((bn, t, nw), BF16),
                   jax.ShapeDtypeStruct((bn, CHUNK, W_A), F32),
                   jax.ShapeDtypeStruct((bn, POOL_BUF, W_B), F32),
                   jax.ShapeDtypeStruct((bn, CONV_W - 1, W_C), F32)),
        grid=(bn, t // tt),
        in_specs=[tile(d), *[_lspec(z, layer) for z in (w, lng, lnb, ws, bias, wpool, pscale, cw)]],
        out_specs=(tile(D_PROJ), tile(nw), last(CHUNK, W_A), last(POOL_BUF, W_B), last(CONV_W - 1, W_C)),
        scratch_shapes=[pltpu.VMEM((POOL_CARRY, W_B), F32), pltpu.VMEM((ROW_CARRY, W_C), F32),
                        pltpu.VMEM((d, PROJ), BF16)],
        compiler_params=_params(("arbitrary", "arbitrary"), nbytes),
        name="proj_abc_prompt",
    )(h, w, lng, lnb, ws, bias, wpool, pscale, cw)


def _abc_sample_kernel(x_ref, pool_ref, conv_ref, lng_ref, lnb_ref, ws0_ref, bias0_ref, wpool_ref,
                       pscale_ref, cw_ref, mix_ref, v_ref, poolo_ref, convo_ref, *, pos0):
    x = x_ref[...]
    n = x.shape[0]
    ga = _gelu(x[:, :2 * W_A])
    u = ga[:, :W_A]
    v = _ln(ga[:, W_A:], lng_ref[...], lnb_ref[...])
    v_ref[...] = v
    mix_ref[:, 0:W_A] = u * (v * ws0_ref[...] + bias0_ref[0:1, :])

    xb = x[:, 2 * W_A:2 * W_A + W_B]
    grp = lax.broadcasted_iota(jnp.int32, (n, W_B), 1) // POOL_GROUP
    run = xb
    sums = jnp.zeros_like(xb)
    for back in range(1, max(POOL_WINDOWS)):
        run = run + pool_ref[POOL_BUF - back]
        for gi, win in enumerate(POOL_WINDOWS):
            if back == win - 1:
                sums = jnp.where(grp == gi, run, sums)
    win = jnp.left_shift(2, grp)
    cnt = jnp.minimum(win, pos0 + 1).astype(F32)
    d = sums / cnt - xb
    mix_ref[:, W_A:W_A + W_B] = _dot(d.astype(BF16), wpool_ref[...]) * pscale_ref[...]
    for r in range(POOL_BUF - 1):
        poolo_ref[r] = pool_ref[r + 1]
    poolo_ref[POOL_BUF - 1] = xb

    o = 2 * W_A + W_B
    bg = x[:, o:o + W_C]
    zc = x[:, o + W_C:o + 2 * W_C] * x[:, o + 2 * W_C:o + 3 * W_C]
    z0 = conv_ref[:, :W_C]
    z1 = conv_ref[:, W_C:]
    y = cw_ref[0:1, :] * z0 + cw_ref[1:2, :] * z1 + cw_ref[2:3, :] * zc
    mix_ref[:, W_A + W_B:] = bg * y
    convo_ref[:, :W_C] = z1
    convo_ref[:, W_C:] = zc


def _abc_sample(pabc, pool, conv, lng, lnb, ws0, bias, wpool, pscale, cw, layer, pos0):
    n = pabc.shape[0]
    nw = W_A + W_B + W_C
    whole = lambda shape: pl.BlockSpec(shape, lambda i: (0,) * len(shape))
    return pl.pallas_call(
        functools.partial(_abc_sample_kernel, pos0=pos0),
        out_shape=(jax.ShapeDtypeStruct((n, nw), F32),
                   jax.ShapeDtypeStruct((n, W_A), F32),
                   jax.ShapeDtypeStruct(pool.shape[1:], F32),
                   jax.ShapeDtypeStruct(conv.shape[1:], F32)),
        grid=(1,),
        in_specs=[whole(pabc.shape)] + [_lspec(z, layer) for z in (pool, conv, lng, lnb, ws0, bias, wpool, pscale, cw)],
        out_specs=(whole((n, nw)), whole((n, W_A)), whole(pool.shape[1:]), whole(conv.shape[1:])),
        name="mixer_abc_sample",
    )(pabc, pool, conv, lng, lnb, ws0, bias, wpool, pscale, cw)


def _rwkv_inputs(xs, w0, w2, a0, a2, g2, kk_w, ka_w, seg):
    r = xs[:, 0:W_D]
    k = xs[:, W_D:2 * W_D]
    v = xs[:, 2 * W_D:3 * W_D]
    o = 3 * W_D
    dw = xs[:, o:o + R_DECAY]
    da = xs[:, o + R_DECAY:o + R_DECAY + R_AAA]
    dg = xs[:, o + R_DECAY + R_AAA:]
    w_log = -_softplus(-(w0 + _dot_hi(jnp.tanh(dw), w2))) - 0.5
    logdecay = -jnp.exp(w_log)
    a = _sigmoid(a0 + _dot_hi(da, a2))
    g = _dot_hi(_sigmoid(dg), g2)
    kk = k * kk_w
    kk = kk * lax.rsqrt(jnp.maximum(_dot_ones(kk * kk, seg), 1e-12))
    k = k * (1.0 + (a - 1.0) * ka_w)
    return r, k, v, kk, a, logdecay, g


def _rwkv_finish(o, r, k, v, g, rk_w, lnx_g, lnx_b, seg):
    inv = 1.0 / D_HEAD_DIM
    mu = _dot_ones(o, seg) * inv
    oc = o - mu
    var = _dot_ones(oc * oc, seg) * inv
    on = oc * lax.rsqrt(var + GN_EPS) * lnx_g + lnx_b
    bonus = _dot_ones(r * k * rk_w, seg) * v
    return (on + bonus) * g


def _bdot(a, b, dims=NN):
    return _dot(a.astype(BF16), b.astype(BF16), dims)


def _head_cols(x, hd):
    return x[:, hd * D_HEAD_DIM:(hd + 1) * D_HEAD_DIM]


HEAD_PAIRS = W_D // LANES


def _pair_diag(y, low):
    zero = jnp.zeros_like(y)
    return jnp.concatenate([jnp.where(low, y, zero), jnp.where(low, zero, y)], axis=0)


def _wkv_precompute(qt, rt, bt, kt, bbar, kbar, vm, nc):
    c = WKV_CHUNK
    n = D_HEAD_DIM
    probs = [(ch, pr) for ch in range(nc) for pr in range(HEAD_PAIRS)]
    cut = lambda x, p: x[p[0] * c:(p[0] + 1) * c, p[1] * LANES:(p[1] + 1) * LANES]
    qt, rt, bt, kt, bbar, kbar, vm = (x.astype(BF16) for x in (qt, rt, bt, kt, bbar, kbar, vm))
    ri = lax.broadcasted_iota(jnp.int32, (c, LANES), 0)
    li = lax.broadcasted_iota(jnp.int32, (c, LANES), 1)
    low = li < n
    strict = ri > li % n
    incl = ri >= li % n
    eye = jnp.where(ri == li % n, 1.0, 0.0)
    diag = lambda y: _pair_diag(y, low)
    halves = lambda x: jnp.where(low, x[:n], x[n:])

    qs = {p: cut(qt, p) for p in probs}
    vd = {p: diag(cut(vm, p)) for p in probs}
    aa = {p: _dot(jnp.concatenate([qs[p], cut(rt, p)], axis=0),
                  jnp.concatenate([diag(cut(bt, p)), diag(cut(kt, p))], axis=0), NT) for p in probs}
    lk = {p: jnp.where(strict, aa[p][:c, LANES:], 0.0).astype(BF16) for p in probs}
    ab = {p: jnp.where(incl, aa[p][c:, :LANES], 0.0).astype(BF16) for p in probs}
    ak = {p: jnp.where(incl, aa[p][c:, LANES:], 0.0).astype(BF16) for p in probs}
    npow = {p: jnp.where(strict, -aa[p][:c, :LANES], 0.0).astype(BF16) for p in probs}
    tinv = {p: eye + npow[p].astype(F32) for p in probs}
    lv = {p: _dot(jnp.concatenate([lk[p], ak[p]], axis=0), vd[p]) for p in probs}
    zp = {p: lv[p][:c].astype(BF16) for p in probs}
    npow = {p: _dot(npow[p], diag(npow[p])).astype(BF16) for p in probs}
    for _ in range(int(math.log2(c)) - 2):
        both = {p: _dot(jnp.concatenate([tinv[p].astype(BF16), npow[p]], axis=0), diag(npow[p])) for p in probs}
        tinv = {p: tinv[p] + both[p][:c] for p in probs}
        npow = {p: both[p][c:].astype(BF16) for p in probs}
    tinv = {p: (tinv[p] + _dot(tinv[p].astype(BF16), diag(npow[p]))).astype(BF16) for p in probs}
    tq = {p: _dot(tinv[p], jnp.concatenate([diag(qs[p]), diag(zp[p])], axis=1)) for p in probs}
    qh = {p: tq[p][:, :LANES].astype(BF16) for p in probs}
    zn = {p: (-tq[p][:, LANES:]).astype(BF16) for p in probs}
    abq = {p: _dot(ab[p], jnp.concatenate([diag(qh[p]), diag(zn[p])], axis=1)) for p in probs}
    rh = {p: (cut(rt, p).astype(F32) - abq[p][:, :LANES]).astype(BF16) for p in probs}
    pv = {p: abq[p][:, LANES:] + lv[p][c:] for p in probs}
    gt = {p: halves(-_dot(qh[p], cut(bbar, p), TN)).astype(BF16) for p in probs}
    cst = {p: halves(_dot(jnp.concatenate([cut(vm, p), zn[p]], axis=0),
                          jnp.concatenate([cut(kbar, p), cut(bbar, p)], axis=0), TN)) for p in probs}
    return rh, pv, gt, cst


def _wkv_scan_steps(s_ref, o_ref, pre, gam, nc):
    c = WKV_CHUNK
    rh, pv, gt, cst = pre
    low = lax.broadcasted_iota(jnp.int32, (D_HEAD_DIM, LANES), 1) < D_HEAD_DIM
    for ch in range(nc):
        for pr in range(HEAD_PAIRS):
            p = (ch, pr)
            s0 = s_ref[pr]
            s0b = s0.astype(BF16)
            o_ref[ch * c:(ch + 1) * c, pr * LANES:(pr + 1) * LANES] = _dot(rh[p], _pair_diag(s0b, low), NT) + pv[p]
            s_ref[pr] = (s0 * gam[ch][:, pr * LANES:(pr + 1) * LANES] + _dot(s0b, _pair_diag(gt[p], low))
                         + cst[p])
        yield


def _rwkv_prompt_kernel(pd_ref, mu_ref, w0_ref, w2_ref, a0_ref, a2_ref, g2_ref, kk_ref, ka_ref, rk_ref,
                        lg_ref, lb_ref, q8_ref, ck_ref, cv_ref, yd_ref, shift_ref, wkv_ref, o8_ref,
                        car_ref, s_ref, o_ref, *, tt, bb):
    t = pl.program_id(1)
    nt = pl.num_programs(1)

    @pl.when(t == 0)
    def _():
        car_ref[...] = jnp.zeros_like(car_ref)
        s_ref[...] = jnp.zeros_like(s_ref)

    pd = pd_ref[0]
    ext = jnp.concatenate([car_ref[...], pd], axis=0)
    prev = pltpu.roll(ext, 1, 0)[ROW_CARRY:]
    car_ref[...] = ext[tt:]
    xs = pd + (prev - pd) * mu_ref[...]
    seg = _head_ones(W_D, D_HEAD_DIM)
    r, k, v, kk, a, ld, g = _rwkv_inputs(xs, w0_ref[...], w2_ref[...], a0_ref[...], a2_ref[...],
                                         g2_ref[...], kk_ref[...], ka_ref[...], seg)
    c = WKV_CHUNK
    nc = tt // c
    tri = jnp.where(lax.broadcasted_iota(jnp.int32, (c, c), 0) >= lax.broadcasted_iota(jnp.int32, (c, c), 1),
                    1.0, 0.0).astype(BF16)
    cs_chunks = [_ones_dot(tri, ld[ch * c:(ch + 1) * c]) for ch in range(nc)]
    cs = jnp.concatenate(cs_chunks, axis=0)
    cs_end = [x[c - 1:c] for x in cs_chunks]
    cs_last = jnp.concatenate([jnp.broadcast_to(x, (c, W_D)) for x in cs_end], axis=0)
    e_neg = jnp.exp(-cs)
    e_tail = jnp.exp(cs_last - cs)
    b = kk * a
    pre = _wkv_precompute(kk * jnp.exp(cs - ld), r * jnp.exp(cs), b * e_neg, k * e_neg, b * e_tail, k * e_tail, v, nc)
    _interleave(_wkv_scan_steps(s_ref, o_ref, pre, [jnp.exp(x) for x in cs_end], nc),
                _attn_sample_steps(q8_ref, ck_ref, cv_ref, o8_ref, bb))
    yd_ref[0] = _rwkv_finish(o_ref[...], r, k, v, g, rk_ref[...], lg_ref[...], lb_ref[...], seg).astype(yd_ref.dtype)

    @pl.when(t == nt - 1)
    def _():
        shift_ref[0] = pd[tt - 1:tt]
        for hd in range(D_HEADS):
            wkv_ref[0, hd] = _head_cols(s_ref[hd // 2], hd % 2)


def _rwkv_prompt(pd, params, q, cache_k, cache_v, layer, tt):
    bn, t, _ = pd.shape
    nt = t // tt
    n = q.shape[0]
    bb = n // (bn * nt)
    assert bb * bn * nt == n
    nbytes = 2 * tt * D_PROJ * 4 + 40 * tt * W_D * 4 + 4 * bb * MEM_ROWS * LANES * 4 + 2 * bb * MEM_ROWS * LANES * 2
    req = pl.BlockSpec((bb, SUBLANES, LANES), lambda i, j: (i * nt + j, 0, 0))
    cache_spec = pl.BlockSpec((None, bb, MEM_ROWS, LANES), lambda i, j: (layer, i * nt + j, 0, 0))
    yd, shift, wkv, o8 = pl.pallas_call(
        functools.partial(_rwkv_prompt_kernel, tt=tt, bb=bb),
        out_shape=(jax.ShapeDtypeStruct((bn, t, W_D), BF16),
                   jax.ShapeDtypeStruct((bn, 1, D_PROJ), F32),
                   jax.ShapeDtypeStruct((bn, D_HEADS, D_HEAD_DIM, D_HEAD_DIM), F32),
                   jax.ShapeDtypeStruct((n, SUBLANES, LANES), F32)),
        grid=(bn, nt),
        in_specs=[pl.BlockSpec((1, tt, D_PROJ), lambda i, j: (i, j, 0))] + [_lspec(z, layer) for z in params]
                 + [req, cache_spec, cache_spec],
        out_specs=(pl.BlockSpec((1, tt, W_D), lambda i, j: (i, j, 0)),
                   pl.BlockSpec((1, 1, D_PROJ), lambda i, j: (i, 0, 0)),
                   pl.BlockSpec((1, D_HEADS, D_HEAD_DIM, D_HEAD_DIM), lambda i, j: (i, 0, 0, 0)),
                   req),
        scratch_shapes=[pltpu.VMEM((ROW_CARRY, D_PROJ), F32),
                        pltpu.VMEM((HEAD_PAIRS, D_HEAD_DIM, LANES), F32),
                        pltpu.VMEM((tt, W_D), F32)],
        compiler_params=_params(("arbitrary", "arbitrary"), nbytes),
        name="rwkv_prompt",
    )(pd, *params, _to_pair_rows(q), _cache_rows_view(cache_k), _cache_rows_view(cache_v))
    return yd, shift, wkv, _from_pair_rows(o8)


def _rwkv_sample_kernel(pd_ref, sh_ref, st_ref, mu_ref, w0_ref, w2_ref, a0_ref, a2_ref, g2_ref, kk_ref,
                        ka_ref, rk_ref, lg_ref, lb_ref, yd_ref, so_ref, rows_ref, cols_ref, ot_ref):
    h = pl.program_id(0)
    n = D_HEAD_DIM

    @pl.when(h == 0)
    def _():
        pd = pd_ref[...]
        xs = pd + (sh_ref[...] - pd) * mu_ref[...]
        seg = _head_ones(W_D, D_HEAD_DIM)
        r, k, v, kk, a, ld, g = _rwkv_inputs(xs, w0_ref[...], w2_ref[...], a0_ref[...], a2_ref[...],
                                             g2_ref[...], kk_ref[...], ka_ref[...], seg)
        for j, x in enumerate((r, k, v, g)):
            rows_ref[j] = x
        for j, x in enumerate((kk, jnp.exp(ld), kk * a, k, r, v)):
            cols_ref[j] = x.T

    base = pl.multiple_of(h * n, n)
    kap, dec, bvec, kvec, rvec = (cols_ref[j, pl.ds(base, n), :] for j in range(5))
    for vi in range(n):
        s = st_ref[0, vi]
        u = -jnp.sum(s * kap, axis=0, keepdims=True)
        s = s * dec + u * bvec + cols_ref[5, pl.ds(base + vi, 1), :] * kvec
        so_ref[0, vi] = s
        ot_ref[pl.ds(base + vi, 1), :] = jnp.sum(s * rvec, axis=0, keepdims=True)

    @pl.when(h == pl.num_programs(0) - 1)
    def _():
        seg = _head_ones(W_D, D_HEAD_DIM)
        yd_ref[...] = _rwkv_finish(ot_ref[...].T, rows_ref[0], rows_ref[1], rows_ref[2], rows_ref[3],
                                   rk_ref[...], lg_ref[...], lb_ref[...], seg)


def _rwkv_sample(pd, shift, state, params, layer):
    n = pd.shape[0]
    sblock = (1, D_HEAD_DIM, D_HEAD_DIM, n)
    return pl.pallas_call(
        _rwkv_sample_kernel,
        out_shape=(jax.ShapeDtypeStruct((n, W_D), F32), jax.ShapeDtypeStruct(state.shape[1:], F32)),
        grid=(D_HEADS,),
        in_specs=[pl.BlockSpec((n, D_PROJ), lambda i: (0, 0)), _lspec(shift, layer),
                  pl.BlockSpec((None,) + sblock, lambda i: (layer, i, 0, 0, 0))]
                 + [_lspec(z, layer) for z in params],
        out_specs=(pl.BlockSpec((n, W_D), lambda i: (0, 0)), pl.BlockSpec(sblock, lambda i: (i, 0, 0, 0))),
        scratch_shapes=[pltpu.VMEM((4, n, W_D), F32), pltpu.VMEM((6, W_D, n), F32), pltpu.VMEM((W_D, n), F32)],
        compiler_params=pltpu.CompilerParams(dimension_semantics=("arbitrary",)),
        name="rwkv_sample",
    )(pd, shift, state, *params)


def _block_diag(w):
    gn, n, _ = w.shape
    eye = jnp.eye(gn, dtype=w.dtype)
    return (eye[:, None, :, None] * w[:, :, None, :]).reshape(gn * n, gn * n)


def kernel(x_prompt, x_sample, mem_prompt, cache_mem_k, cache_mem_v, state_pool, state_conv, state_shift, state_wkv,
           w_in, mu_d, ln_v_g, ln_v_b, ws_chunk, b_chunk, w_pool, pool_scale, conv_w,
           rwkv_w0, rwkv_w2, rwkv_a0, rwkv_a2, rwkv_g2, rwkv_k_k, rwkv_k_a, rwkv_r_k, rwkv_lnx_g, rwkv_lnx_b,
           w_out, ln1_g, ln1_b, w_xq, w_xk, w_xv, w_xo, ln2_g, ln2_b, ffn_w1, ffn_w3, ffn_w2, ln3_g, ln3_b):
    bp, t_p, d = x_prompt.shape
    ns, t_s, _ = x_sample.shape
    depth = w_in.shape[0]
    assert d == D_MODEL and t_s == 1 and t_p % CHUNK == 0 and w_in.shape[2] == PROJ
    alpha = (2 * depth) ** 0.25
    mp = bp * t_p
    nw = W_A + W_B + W_C
    row = lambda z: z.reshape(depth, 1, -1)

    w_in_b, w_out_b = w_in, w_out
    w_xq_b, w_xk_b, w_xv_b, w_xo_b = w_xq, w_xk, w_xv, w_xo
    w1_b, w3_b, w2_b = ffn_w1, ffn_w3, ffn_w2
    ws_flat = ws_chunk.reshape(depth, A_HEADS * CHUNK, CHUNK)
    bias_full = jnp.repeat(jnp.swapaxes(b_chunk, 1, 2), A_HEAD_DIM, axis=2)
    ws0 = jnp.repeat(ws_chunk[:, :, 0, 0], A_HEAD_DIM, axis=1).reshape(depth, 1, W_A)
    wpool_bd = jnp.stack([_block_diag(w_pool[l]) for l in range(depth)]).astype(BF16)
    abc_w = (row(ln_v_g), row(ln_v_b))
    abc_w2 = (wpool_bd, row(pool_scale), conv_w)
    rw = (row(mu_d), row(rwkv_w0), rwkv_w2, row(rwkv_a0), rwkv_a2, rwkv_g2,
          row(rwkv_k_k), row(rwkv_k_a), row(rwkv_r_k), row(rwkv_lnx_g), row(rwkv_lnx_b))
    ln1, ln2, ln3 = (row(ln1_g), row(ln1_b)), (row(ln2_g), row(ln2_b)), (row(ln3_g), row(ln3_b))
    pool_view = jnp.swapaxes(state_pool, 1, 2)
    wkv_view = state_wkv.transpose(0, 2, 3, 4, 1)
    conv_view = state_conv.reshape(depth, ns, (CONV_W - 1) * W_C)
    shift_view = state_shift.reshape(depth, ns, D_PROJ)

    hp = x_prompt
    hs = x_sample.reshape(ns, d)
    mem = mem_prompt.reshape(bp * MEM_LEN, d)
    mk_all, mv_all = _mem_kv(mem, w_xk_b, w_xv_b, 512)
    mk_all = mk_all.reshape(depth, bp, MEM_LEN, d)
    mv_all = mv_all.reshape(depth, bp, MEM_LEN, d)
    outs = [[] for _ in range(10)]
    for l in range(depth):
        pd, mix, v_last, pool_p, conv_p = _proj_abc_prompt(hp, w_in_b, *abc_w, ws_flat, bias_full, *abc_w2, l, tt=512)

        pabc_s, pd_s = _proj(hs, w_in_b, l, ns)
        mix_s, v_s, pool_s, conv_s = _abc_sample(pabc_s, pool_view, conv_view, *abc_w, ws0, bias_full, *abc_w2,
                                                 l, pos0=PAST_LEN)
        yd_s, wkv_s = _rwkv_sample(pd_s, shift_view, wkv_view, rw, l)
        hs = _mm_res_ln([mix_s, yd_s], w_out_b, l, hs, *ln1, tm=ns, alpha=alpha, name="out_proj_s")
        q_s = _mm(hs, w_xq_b, l, ns, "q_s")

        yd, shift_p, wkv_p, o_s = _rwkv_prompt(pd, rw, q_s, cache_mem_k, cache_mem_v, l, tt=512)
        hp = _attn_prompt(mix, yd, hp, mk_all, mv_all, w_out_b, *ln1, w_xq_b, w_xo_b, *ln2, l, tq=512, alpha=alpha)
        hp = _ffn(hp.reshape(mp, d), w1_b, w3_b, w2_b, *ln3, l, tm=1024, tf=256, alpha=alpha,
                  name="ffn").reshape(bp, t_p, d)
        for lst, val in zip(outs[:5], (v_last, pool_p, conv_p, shift_p, wkv_p)):
            lst.append(val)

        hs = _mm_res_ln([o_s], w_xo_b, l, hs, *ln2, tm=ns, alpha=alpha, name="xo_s")
        hs = _ffn(hs, w1_b, w3_b, w2_b, *ln3, l, tm=ns, tf=256, alpha=alpha, name="ffn_s")
        for lst, val in zip(outs[5:], (v_s.reshape(ns, 1, W_A), pool_s,
                                       conv_s.reshape(ns, CONV_W - 1, W_C), pd_s.reshape(ns, 1, D_PROJ), wkv_s)):
            lst.append(val)

    stacked = [jnp.stack(o) for o in outs]
    stacked[6] = jnp.swapaxes(stacked[6], 1, 2)
    stacked[9] = stacked[9].transpose(0, 4, 1, 2, 3)
    mem_shape = (depth, bp, MEM_LEN, X_HEADS, X_HEAD_DIM)
    return ((hp, hs.reshape(ns, 1, d)) + tuple(stacked[:5]) + (mk_all.reshape(mem_shape), mv_all.reshape(mem_shape))
            + tuple(stacked[5:]))
```

```python
import functools
import math

import jax
import jax.numpy as jnp
from jax import lax
from jax.experimental import pallas as pl
from jax.experimental.pallas import tpu as pltpu

F32 = jnp.float32
BF16 = jnp.bfloat16

D_MODEL = 1024
W_A = 256
W_B = 256
W_C = 256
W_D = 256
A_HEADS = 4
A_HEAD_DIM = W_A // A_HEADS
CHUNK = 128
POOL_WINDOWS = (2, 4, 8, 16)
POOL_GROUP = W_B // len(POOL_WINDOWS)
POOL_BUF = max(POOL_WINDOWS) - 1
CONV_W = 3
D_HEAD_DIM = 64
D_HEADS = W_D // D_HEAD_DIM
R_DECAY = 32
R_AAA = 32
R_GATE = 64
D_PROJ = 3 * W_D + R_DECAY + R_AAA + R_GATE
N_ABC = 2 * W_A + W_B + 3 * W_C
PROJ = N_ABC + D_PROJ
MEM_LEN = 256
X_HEADS = 4
X_HEAD_DIM = D_MODEL // X_HEADS
D_FF = int(math.ceil(8 * D_MODEL / 3 / 256)) * 256
PAST_LEN = 16384
LN_EPS = 1e-5
GN_EPS = 64e-5

WKV_CHUNK = 64
POOL_CARRY = 24
ROW_CARRY = 8
V7X_VMEM_BYTES = 64 * 1024 * 1024
VMEM_CAP = V7X_VMEM_BYTES - 8 * 1024 * 1024

NN = (((1,), (0,)), ((), ()))
NT = (((1,), (1,)), ((), ()))
TN = (((0,), (0,)), ((), ()))


def _vmem_limit(nbytes):
    return int(min(VMEM_CAP, max(32 * 1024 * 1024, 2 * nbytes)))


def _params(sem, nbytes):
    return pltpu.CompilerParams(dimension_semantics=sem, vmem_limit_bytes=_vmem_limit(nbytes))


def _lspec(arr, layer):
    tail = arr.shape[1:]
    zeros = (0,) * len(tail)
    return pl.BlockSpec((None,) + tail, lambda *_: (layer,) + zeros)


def _dot(a, b, dims=NN):
    return lax.dot_general(a, b, dims, preferred_element_type=F32)


def _split2(a):
    hi = a.astype(BF16)
    lo = (a - hi.astype(F32)).astype(BF16)
    return hi, lo


def _dot_hi(a, b, dims=NN):
    ah, al = _split2(a)
    bh, bl = _split2(b)
    return _dot(ah, bh, dims) + _dot(ah, bl, dims) + _dot(al, bh, dims)


def _dot_ones(x, ones_bf16, dims=NN):
    hi, lo = _split2(x)
    return _dot(hi, ones_bf16, dims) + _dot(lo, ones_bf16, dims)


def _ones_dot(ones_bf16, x):
    hi = x.astype(BF16)
    r1 = x - hi.astype(F32)
    mid = r1.astype(BF16)
    lo = (r1 - mid.astype(F32)).astype(BF16)
    return _dot(ones_bf16, hi) + _dot(ones_bf16, mid) + _dot(ones_bf16, lo)


def _ln(x, g, b, eps=LN_EPS):
    mu = jnp.mean(x, axis=-1, keepdims=True)
    xc = x - mu
    var = jnp.mean(xc * xc, axis=-1, keepdims=True)
    return xc * lax.rsqrt(var + eps) * g + b


def _gelu(x):
    c = math.sqrt(2.0 / math.pi)
    return x * (0.5 * (1.0 + jnp.tanh(c * (x + 0.044715 * (x * x * x)))))


def _sigmoid(x):
    return 1.0 / (1.0 + jnp.exp(-x))


def _softplus(x):
    return jnp.maximum(x, 0.0) + jnp.log(1.0 + jnp.exp(-jnp.abs(x)))


def _interleave(*staged):
    live = list(staged)
    while live:
        for steps in list(live):
            if next(steps, StopIteration) is StopIteration:
                live.remove(steps)


def _head_ones(n, group):
    r = lax.broadcasted_iota(jnp.int32, (n, n), 0) // group
    c = lax.broadcasted_iota(jnp.int32, (n, n), 1) // group
    return jnp.where(r == c, 1.0, 0.0).astype(BF16)


def _proj_kernel(x_ref, w_ref, oabc_ref, od_ref):
    y = _dot(x_ref[...].astype(BF16), w_ref[...].astype(BF16))
    oabc_ref[...] = y[:, :N_ABC]
    od_ref[...] = y[:, N_ABC:]


def _proj(x, w, layer, tm):
    m, k = x.shape
    nbytes = 2 * (tm * k * 4 + k * PROJ * 4 + tm * PROJ * 4) + tm * PROJ * 4 + k * PROJ * 2
    return pl.pallas_call(
        _proj_kernel,
        out_shape=(jax.ShapeDtypeStruct((m, N_ABC), F32), jax.ShapeDtypeStruct((m, D_PROJ), F32)),
        grid=(m // tm,),
        in_specs=[pl.BlockSpec((tm, k), lambda i: (i, 0)), _lspec(w, layer)],
        out_specs=(pl.BlockSpec((tm, N_ABC), lambda i: (i, 0)), pl.BlockSpec((tm, D_PROJ), lambda i: (i, 0))),
        compiler_params=_params(("parallel",), nbytes),
        name="proj",
    )(x, w)


def _mm_kernel(x_ref, w_ref, o_ref):
    o_ref[...] = _dot(x_ref[...].astype(BF16), w_ref[...].astype(BF16))


def _mm(x, w, layer, tm, name):
    m, k = x.shape
    n = w.shape[2]
    nbytes = 2 * (tm * k * 4 + k * n * 4 + tm * n * 4) + tm * n * 4 + k * n * 2
    return pl.pallas_call(
        _mm_kernel,
        out_shape=jax.ShapeDtypeStruct((m, n), F32),
        grid=(m // tm,),
        in_specs=[pl.BlockSpec((tm, k), lambda i: (i, 0)), _lspec(w, layer)],
        out_specs=pl.BlockSpec((tm, n), lambda i: (i, 0)),
        compiler_params=_params(("parallel",), nbytes),
        name=name,
    )(x, w)


def _mem_kv_kernel(x_ref, wk_ref, wv_ref, k_ref, v_ref, kt_ref, vt_ref, wkb_ref, wvb_ref):
    @pl.when(pl.program_id(1) == 0)
    def _():
        wkb_ref[...] = wk_ref[...].astype(BF16)
        wvb_ref[...] = wv_ref[...].astype(BF16)

    xb = x_ref[...].astype(BF16)
    for w_ref, o_ref, t_ref in ((wkb_ref, k_ref, kt_ref), (wvb_ref, v_ref, vt_ref)):
        y = _dot(xb, w_ref[...])
        o_ref[...] = y
        rows = y.shape[0]
        t_ref[...] = jnp.swapaxes(y.reshape(rows, X_HEADS, LANE_TILES, LANES), 1, 2).reshape(rows, SUBLANES, LANES)


def _mem_kv(mem, wk, wv, tm):
    m, k = mem.shape
    nl, _, n = wk.shape
    nbytes = 2 * (tm * k * 4 + 2 * k * n * 4 + 4 * tm * n * 4) + 2 * tm * n * 4 + 2 * k * n * 2
    wspec = pl.BlockSpec((None, k, n), lambda l, i: (l, 0, 0))
    ospec = pl.BlockSpec((None, tm, n), lambda l, i: (l, i, 0))
    tspec = pl.BlockSpec((None, tm, SUBLANES, LANES), lambda l, i: (l, i, 0, 0))
    shape = jax.ShapeDtypeStruct((nl, m, n), F32)
    tshape = jax.ShapeDtypeStruct((nl, m, SUBLANES, LANES), F32)
    return pl.pallas_call(
        _mem_kv_kernel,
        out_shape=(shape, shape, tshape, tshape),
        grid=(nl, m // tm),
        in_specs=[pl.BlockSpec((tm, k), lambda l, i: (i, 0)), wspec, wspec],
        out_specs=(ospec, ospec, tspec, tspec),
        scratch_shapes=[pltpu.VMEM((k, n), BF16), pltpu.VMEM((k, n), BF16)],
        compiler_params=_params(("arbitrary", "arbitrary"), nbytes),
        name="mem_kv",
    )(mem, wk, wv)


def _mm_res_ln_kernel(*refs, n_in, alpha):
    xs = refs[:n_in]
    ws = refs[n_in:2 * n_in]
    h_ref, g_ref, b_ref, o_ref = refs[2 * n_in:]
    y = _dot(xs[0][...].astype(BF16), ws[0][...].astype(BF16))
    for x_ref, w_ref in zip(xs[1:], ws[1:]):
        y = y + _dot(x_ref[...].astype(BF16), w_ref[...].astype(BF16))
    o_ref[...] = _ln(alpha * h_ref[...] + y, g_ref[...], b_ref[...])


def _mm_res_ln(xs, w, layer, h, g, b, tm, alpha, name):
    m, n = h.shape
    nbytes = 2 * sum(tm * x.shape[1] * 4 + x.shape[1] * n * 4 for x in xs) + 5 * tm * n * 4 + w.shape[1] * n * 2
    in_specs = [pl.BlockSpec((tm, x.shape[1]), lambda i: (i, 0)) for x in xs]
    start = 0
    for x in xs:
        width = x.shape[1]
        assert start % width == 0
        in_specs.append(pl.BlockSpec((None, width, n), lambda i, blk=start // width: (layer, blk, 0)))
        start += width
    assert start == w.shape[1]
    in_specs += [pl.BlockSpec((tm, n), lambda i: (i, 0)), _lspec(g, layer), _lspec(b, layer)]
    return pl.pallas_call(
        functools.partial(_mm_res_ln_kernel, n_in=len(xs), alpha=alpha),
        out_shape=jax.ShapeDtypeStruct((m, n), F32),
        grid=(m // tm,),
        in_specs=in_specs,
        out_specs=pl.BlockSpec((tm, n), lambda i: (i, 0)),
        compiler_params=_params(("parallel",), nbytes),
        name=name,
    )(*xs, *([w] * len(xs)), h, g, b)


def _ffn_kernel(x_ref, w1_ref, w3_ref, w2_ref, g_ref, b_ref, o_ref, xb_ref, acc_ref, *, alpha):
    j = pl.program_id(1)

    @pl.when(j == 0)
    def _():
        xb_ref[...] = x_ref[...].astype(BF16)
        acc_ref[...] = jnp.zeros_like(acc_ref)

    xb = xb_ref[...]
    h1 = _dot(xb, w1_ref[...].astype(BF16))
    h3 = _dot(xb, w3_ref[...].astype(BF16))
    a = (h1 * _sigmoid(h1) * h3).astype(BF16)
    acc_ref[...] += _dot(a, w2_ref[...].astype(BF16))

    @pl.when(j == pl.num_programs(1) - 1)
    def _():
        o_ref[...] = _ln(alpha * x_ref[...] + acc_ref[...], g_ref[...], b_ref[...])


def _ffn(x, w1, w3, w2, g, b, layer, tm, tf, alpha, name):
    m, d = x.shape
    nbytes = 4 * tm * d * 4 + tm * d * 2 + tm * d * 4 + 3 * 3 * d * tf * 4 + 3 * tm * tf * 4
    return pl.pallas_call(
        functools.partial(_ffn_kernel, alpha=alpha),
        out_shape=jax.ShapeDtypeStruct((m, d), F32),
        grid=(m // tm, D_FF // tf),
        in_specs=[pl.BlockSpec((tm, d), lambda i, j: (i, 0)),
                  pl.BlockSpec((None, d, tf), lambda i, j: (layer, 0, j)),
                  pl.BlockSpec((None, d, tf), lambda i, j: (layer, 0, j)),
                  pl.BlockSpec((None, tf, d), lambda i, j: (layer, j, 0)),
                  _lspec(g, layer), _lspec(b, layer)],
        out_specs=pl.BlockSpec((tm, d), lambda i, j: (i, 0)),
        scratch_shapes=[pltpu.VMEM((tm, d), BF16), pltpu.VMEM((tm, d), F32)],
        compiler_params=_params(("parallel", "arbitrary"), nbytes),
        name=name,
    )(x, w1, w3, w2, g, b)


def _softmax_rows(s):
    m = jnp.max(s, axis=-1, keepdims=True)
    e = jnp.exp(s - m)
    return e / jnp.sum(e, axis=-1, keepdims=True)


ATTN_PIECES = 2


def _attn_prompt_kernel(mix_ref, yd_ref, h_ref, mk_ref, mv_ref, wma_ref, wmd_ref, g1_ref, b1_ref,
                        wq_ref, wo_ref, g_ref, b_ref, o_ref, ob_ref, wmb_ref, wqb_ref, wob_ref, *, alpha):
    nw = wma_ref.shape[0]

    @pl.when((pl.program_id(0) == 0) & (pl.program_id(1) == 0))
    def _():
        wmb_ref[:nw, :] = wma_ref[...].astype(BF16)
        wmb_ref[nw:, :] = wmd_ref[...].astype(BF16)
        wqb_ref[...] = wq_ref[...].astype(BF16)
        wob_ref[...] = wo_ref[...].astype(BF16)

    rows = h_ref.shape[1] // ATTN_PIECES
    scale = X_HEAD_DIM ** -0.5
    sls = [slice(hd * X_HEAD_DIM, (hd + 1) * X_HEAD_DIM) for hd in range(X_HEADS)]
    kb = [mk_ref[0, :, sl].astype(BF16) for sl in sls]
    vb = [mv_ref[0, :, sl].astype(BF16) for sl in sls]
    y, h, q, sc = {}, {}, {}, {}

    def project(p, rs):
        y[p] = (_dot(mix_ref[0, rs, :].astype(BF16), wmb_ref[:nw, :])
                + _dot(yd_ref[0, rs, :].astype(BF16), wmb_ref[nw:, :]))

    def query(p, rs):
        h[p] = _ln(alpha * h_ref[0, rs, :] + y[p], g1_ref[...], b1_ref[...])
        q[p] = _dot(h[p].astype(BF16), wqb_ref[...]).astype(BF16)

    def scores(p, rs):
        sc[p] = [_dot(q[p][:, sl], k, NT) * scale for sl, k in zip(sls, kb)]

    def values(p, rs):
        for sl, s, v in zip(sls, sc[p], vb):
            ob_ref[rs, sl] = _dot(_softmax_rows(s).astype(BF16), v).astype(BF16)

    def output(p, rs):
        o_ref[0, rs, :] = _ln(alpha * h[p] + _dot(ob_ref[rs, :], wob_ref[...]), g_ref[...], b_ref[...])

    stages = (project, query, scores, values, output)
    for step in range(len(stages) + ATTN_PIECES - 1):
        for p in range(ATTN_PIECES):
            if 0 <= step - p < len(stages):
                stages[step - p](p, slice(p * rows, (p + 1) * rows))


def _attn_prompt(mix, yd, h, mk, mv, w_mix, g1, b1, wq, wo, g, b, layer, tq, alpha):
    bn, t, d = h.shape
    nw = mix.shape[2]
    assert nw % W_D == 0 and w_mix.shape[1] == nw + W_D
    nbytes = (6 * tq * d * 4 + 4 * MEM_LEN * d * 4 + 6 * d * d * 4 + 3 * d * d * 2 + tq * d * 2
              + 3 * tq * d * 4 + 3 * tq * MEM_LEN * 4)
    tile = lambda n: pl.BlockSpec((1, tq, n), lambda i, j: (i, j, 0))
    return pl.pallas_call(
        functools.partial(_attn_prompt_kernel, alpha=alpha),
        out_shape=jax.ShapeDtypeStruct((bn, t, d), F32),
        grid=(bn, t // tq),
        in_specs=[tile(nw), tile(W_D), tile(d),
                  pl.BlockSpec((None, 1, MEM_LEN, d), lambda i, j: (layer, i, 0, 0)),
                  pl.BlockSpec((None, 1, MEM_LEN, d), lambda i, j: (layer, i, 0, 0)),
                  pl.BlockSpec((None, nw, d), lambda i, j: (layer, 0, 0)),
                  pl.BlockSpec((None, W_D, d), lambda i, j: (layer, nw // W_D, 0)),
                  _lspec(g1, layer), _lspec(b1, layer),
                  _lspec(wq, layer), _lspec(wo, layer), _lspec(g, layer), _lspec(b, layer)],
        out_specs=tile(d),
        scratch_shapes=[pltpu.VMEM((tq, d), BF16), pltpu.VMEM((nw + W_D, d), BF16),
                        pltpu.VMEM((d, d), BF16), pltpu.VMEM((d, d), BF16)],
        compiler_params=_params(("arbitrary", "arbitrary"), nbytes),
        name="attn_prompt",
    )(mix, yd, h, mk, mv, w_mix, w_mix, g1, b1, wq, wo, g, b)


LANES = 128
SUBLANES = 8
LANE_TILES = X_HEAD_DIM // LANES
MEM_ROWS = MEM_LEN * LANE_TILES * X_HEADS


def _cache_rows_view(cache):
    nl, n = cache.shape[:2]
    x = cache.reshape(nl, n, MEM_LEN, X_HEADS, LANE_TILES, LANES)
    return x.transpose(0, 1, 2, 4, 3, 5).reshape(nl, n, MEM_ROWS, LANES)


def _cache_from_rows(rows, n):
    nl = rows.shape[0]
    x = rows.reshape(nl, n, MEM_LEN, LANE_TILES, X_HEADS, LANES)
    return x.transpose(0, 1, 2, 4, 3, 5).reshape(nl, n, MEM_LEN, X_HEADS, X_HEAD_DIM)


def _attn_sample_steps(q_ref, k_ref, v_ref, o_ref, bb):
    scale = X_HEAD_DIM ** -0.5
    shape = (SUBLANES, MEM_ROWS)
    rowi = lax.broadcasted_iota(jnp.int32, shape, 0)
    coli = lax.broadcasted_iota(jnp.int32, shape, 1)
    valid = (coli % SUBLANES) == rowi
    raw = [_dot(q_ref[i].astype(BF16), k_ref[i].astype(BF16), NT) for i in range(bb)]
    yield
    probs = []
    for r in raw:
        r = jnp.where(valid, r, 0.0)
        other = pltpu.roll(r, X_HEADS, 0)
        other = jnp.where(rowi < X_HEADS, pltpu.roll(other, MEM_ROWS - X_HEADS, 1), pltpu.roll(other, X_HEADS, 1))
        sc = jnp.where(valid, (r + other) * scale, -jnp.inf)
        m = jnp.max(sc, axis=-1, keepdims=True)
        e = jnp.exp(sc - m)
        probs.append((e / jnp.sum(e, axis=-1, keepdims=True)).astype(BF16))
        yield
    for i in range(bb):
        o_ref[i] = _dot(probs[i], v_ref[i].astype(BF16))
        yield


def _to_pair_rows(q):
    n = q.shape[0]
    return q.reshape(n, X_HEADS, LANE_TILES, LANES).transpose(0, 2, 1, 3).reshape(n, SUBLANES, LANES)


def _from_pair_rows(o8):
    n = o8.shape[0]
    return o8.reshape(n, LANE_TILES, X_HEADS, LANES).transpose(0, 2, 1, 3).reshape(n, D_MODEL)


def _pool_window_sums(ext, tt):
    s2 = ext + pltpu.roll(ext, 1, 0)
    s4 = s2 + pltpu.roll(s2, 2, 0)
    s8 = s4 + pltpu.roll(s4, 4, 0)
    s16 = s8 + pltpu.roll(s8, 8, 0)
    grp = lax.broadcasted_iota(jnp.int32, (tt, W_B), 1) // POOL_GROUP
    lo = ext.shape[0] - tt
    return jnp.where(grp == 0, s2[lo:], jnp.where(grp == 1, s4[lo:], jnp.where(grp == 2, s8[lo:], s16[lo:])))


def _proj_abc_kernel(x_ref, w_ref, lng_ref, lnb_ref, ws_ref, bias_ref, wpool_ref, pscale_ref, cw_ref,
                     pd_ref, mix_ref, vlast_ref, pool_ref, conv_ref, pcar_ref, ccar_ref, wb_ref, *, tt):
    t = pl.program_id(1)
    nt = pl.num_programs(1)

    @pl.when((pl.program_id(0) == 0) & (t == 0))
    def _():
        wb_ref[...] = w_ref[...].astype(BF16)

    @pl.when(t == 0)
    def _():
        pcar_ref[...] = jnp.zeros_like(pcar_ref)
        ccar_ref[...] = jnp.zeros_like(ccar_ref)

    rows = lax.broadcasted_iota(jnp.int32, (A_HEADS * CHUNK, CHUNK), 0) % CHUNK
    cols = lax.broadcasted_iota(jnp.int32, (A_HEADS * CHUNK, CHUNK), 1)
    wsm = jnp.where(rows >= cols, ws_ref[...], 0.0).astype(BF16)
    hid = lax.broadcasted_iota(jnp.int32, (CHUNK, W_A), 1) // A_HEAD_DIM
    win = jnp.left_shift(2, lax.broadcasted_iota(jnp.int32, (CHUNK, W_B), 1) // POOL_GROUP)
    rowi = lax.broadcasted_iota(jnp.int32, (CHUNK, W_B), 0)
    w = wb_ref[...]
    pcar = pcar_ref[...]
    ccar = ccar_ref[...]
    v = None
    for c in range(tt // CHUNK):
        rs = slice(c * CHUNK, (c + 1) * CHUNK)
        y = _dot(x_ref[0, rs, :].astype(BF16), w)
        pd_ref[0, rs, :] = y[:, N_ABC:]

        ga = _gelu(y[:, :2 * W_A])
        u = ga[:, :W_A]
        v = _ln(ga[:, W_A:], lng_ref[...], lnb_ref[...])
        zz = _dot(wsm, v.astype(BF16))
        z = zz[(A_HEADS - 1) * CHUNK:]
        for hd in range(A_HEADS - 2, -1, -1):
            z = jnp.where(hid == hd, zz[hd * CHUNK:(hd + 1) * CHUNK], z)
        mix_ref[0, rs, 0:W_A] = (u * (z + bias_ref[...])).astype(mix_ref.dtype)

        xb = y[:, 2 * W_A:2 * W_A + W_B]
        ext = jnp.concatenate([pcar, xb], axis=0)
        sums = _pool_window_sums(ext, CHUNK)
        pos = t * tt + c * CHUNK + rowi
        cnt = jnp.minimum(win, pos + 1).astype(F32)
        d = sums / cnt - xb
        mix_ref[0, rs, W_A:W_A + W_B] = (_dot(d.astype(BF16), wpool_ref[...]) * pscale_ref[...]).astype(mix_ref.dtype)
        pcar = ext[CHUNK:]

        o = 2 * W_A + W_B
        bg = y[:, o:o + W_C]
        zc = y[:, o + W_C:o + 2 * W_C] * y[:, o + 2 * W_C:o + 3 * W_C]
        extz = jnp.concatenate([ccar, zc], axis=0)
        conv = (cw_ref[0:1, :] * pltpu.roll(extz, 2, 0) + cw_ref[1:2, :] * pltpu.roll(extz, 1, 0)
                + cw_ref[2:3, :] * extz)
        mix_ref[0, rs, W_A + W_B:] = (bg * conv[ROW_CARRY:]).astype(mix_ref.dtype)
        ccar = extz[CHUNK:]
    pcar_ref[...] = pcar
    ccar_ref[...] = ccar

    @pl.when(t == nt - 1)
    def _():
        vlast_ref[0] = v
        pool_ref[0] = pcar_ref[POOL_CARRY - POOL_BUF:, :]
        conv_ref[0] = ccar_ref[ROW_CARRY - (CONV_W - 1):, :]


def _proj_abc_prompt(h, w, lng, lnb, ws, bias, wpool, pscale, cw, layer, tt):
    bn, t, d = h.shape
    nw = W_A + W_B + W_C
    nbytes = (2 * tt * d * 4 + 2 * d * PROJ * 4 + d * PROJ * 2 + 2 * tt * (D_PROJ + nw) * 4 + 3 * CHUNK * PROJ * 4
              + 4 * A_HEADS * CHUNK * CHUNK * 4)
    tile = lambda n: pl.BlockSpec((1, tt, n), lambda i, j: (i, j, 0))
    last = lambda r, n: pl.BlockSpec((1, r, n), lambda i, j: (i, 0, 0))
    return pl.pallas_call(
        functools.partial(_proj_abc_kernel, tt=tt),
        out_shape=(jax.ShapeDtypeStruct((bn, t, D_PROJ), F32),
                   jax.ShapeDtypeStruct((bn, t, nw), BF16),
                   jax.ShapeDtypeStruct((bn, CHUNK, W_A), F32),
                   jax.ShapeDtypeStruct((bn, POOL_BUF, W_B), F32),
                   jax.ShapeDtypeStruct((bn, CONV_W - 1, W_C), F32)),
        grid=(bn, t // tt),
        in_specs=[tile(d), *[_lspec(z, layer) for z in (w, lng, lnb, ws, bias, wpool, pscale, cw)]],
        out_specs=(tile(D_PROJ), tile(nw), last(CHUNK, W_A), last(POOL_BUF, W_B), last(CONV_W - 1, W_C)),
        scratch_shapes=[pltpu.VMEM((POOL_CARRY, W_B), F32), pltpu.VMEM((ROW_CARRY, W_C), F32),
                        pltpu.VMEM((d, PROJ), BF16)],
        compiler_params=_params(("arbitrary", "arbitrary"), nbytes),
        name="proj_abc_prompt",
    )(h, w, lng, lnb, ws, bias, wpool, pscale, cw)


def _abc_sample_kernel(x_ref, pool_ref, conv_ref, lng_ref, lnb_ref, ws0_ref, bias0_ref, wpool_ref,
                       pscale_ref, cw_ref, mix_ref, v_ref, poolo_ref, convo_ref, *, pos0):
    x = x_ref[...]
    n = x.shape[0]
    ga = _gelu(x[:, :2 * W_A])
    u = ga[:, :W_A]
    v = _ln(ga[:, W_A:], lng_ref[...], lnb_ref[...])
    v_ref[...] = v
    mix_ref[:, 0:W_A] = u * (v * ws0_ref[...] + bias0_ref[0:1, :])

    xb = x[:, 2 * W_A:2 * W_A + W_B]
    grp = lax.broadcasted_iota(jnp.int32, (n, W_B), 1) // POOL_GROUP
    run = xb
    sums = jnp.zeros_like(xb)
    for back in range(1, max(POOL_WINDOWS)):
        run = run + pool_ref[POOL_BUF - back]
        for gi, win in enumerate(POOL_WINDOWS):
            if back == win - 1:
                sums = jnp.where(grp == gi, run, sums)
    win = jnp.left_shift(2, grp)
    cnt = jnp.minimum(win, pos0 + 1).astype(F32)
    d = sums / cnt - xb
    mix_ref[:, W_A:W_A + W_B] = _dot(d.astype(BF16), wpool_ref[...]) * pscale_ref[...]
    for r in range(POOL_BUF - 1):
        poolo_ref[r] = pool_ref[r + 1]
    poolo_ref[POOL_BUF - 1] = xb

    o = 2 * W_A + W_B
    bg = x[:, o:o + W_C]
    zc = x[:, o + W_C:o + 2 * W_C] * x[:, o + 2 * W_C:o + 3 * W_C]
    z0 = conv_ref[:, :W_C]
    z1 = conv_ref[:, W_C:]
    y = cw_ref[0:1, :] * z0 + cw_ref[1:2, :] * z1 + cw_ref[2:3, :] * zc
    mix_ref[:, W_A + W_B:] = bg * y
    convo_ref[:, :W_C] = z1
    convo_ref[:, W_C:] = zc


def _abc_sample(pabc, pool, conv, lng, lnb, ws0, bias, wpool, pscale, cw, layer, pos0):
    n = pabc.shape[0]
    nw = W_A + W_B + W_C
    whole = lambda shape: pl.BlockSpec(shape, lambda i: (0,) * len(shape))
    return pl.pallas_call(
        functools.partial(_abc_sample_kernel, pos0=pos0),
        out_shape=(jax.ShapeDtypeStruct((n, nw), F32),
                   jax.ShapeDtypeStruct((n, W_A), F32),
                   jax.ShapeDtypeStruct(pool.shape[1:], F32),
                   jax.ShapeDtypeStruct(conv.shape[1:], F32)),
        grid=(1,),
        in_specs=[whole(pabc.shape)] + [_lspec(z, layer) for z in (pool, conv, lng, lnb, ws0, bias, wpool, pscale, cw)],
        out_specs=(whole((n, nw)), whole((n, W_A)), whole(pool.shape[1:]), whole(conv.shape[1:])),
        name="mixer_abc_sample",
    )(pabc, pool, conv, lng, lnb, ws0, bias, wpool, pscale, cw)


def _rwkv_inputs(xs, w0, w2, a0, a2, g2, kk_w, ka_w, seg):
    r = xs[:, 0:W_D]
    k = xs[:, W_D:2 * W_D]
    v = xs[:, 2 * W_D:3 * W_D]
    o = 3 * W_D
    dw = xs[:, o:o + R_DECAY]
    da = xs[:, o + R_DECAY:o + R_DECAY + R_AAA]
    dg = xs[:, o + R_DECAY + R_AAA:]
    w_log = -_softplus(-(w0 + _dot_hi(jnp.tanh(dw), w2))) - 0.5
    logdecay = -jnp.exp(w_log)
    a = _sigmoid(a0 + _dot_hi(da, a2))
    g = _dot_hi(_sigmoid(dg), g2)
    kk = k * kk_w
    kk = kk * lax.rsqrt(jnp.maximum(_dot_ones(kk * kk, seg), 1e-12))
    k = k * (1.0 + (a - 1.0) * ka_w)
    return r, k, v, kk, a, logdecay, g


def _rwkv_finish(o, r, k, v, g, rk_w, lnx_g, lnx_b, seg):
    inv = 1.0 / D_HEAD_DIM
    mu = _dot_ones(o, seg) * inv
    oc = o - mu
    var = _dot_ones(oc * oc, seg) * inv
    on = oc * lax.rsqrt(var + GN_EPS) * lnx_g + lnx_b
    bonus = _dot_ones(r * k * rk_w, seg) * v
    return (on + bonus) * g


def _bdot(a, b, dims=NN):
    return _dot(a.astype(BF16), b.astype(BF16), dims)


def _head_cols(x, hd):
    return x[:, hd * D_HEAD_DIM:(hd + 1) * D_HEAD_DIM]


HEAD_PAIRS = W_D // LANES


def _pair_diag(y, low):
    zero = jnp.zeros_like(y)
    return jnp.concatenate([jnp.where(low, y, zero), jnp.where(low, zero, y)], axis=0)


def _wkv_precompute(qt, rt, bt, kt, bbar, kbar, vm, nc):
    c = WKV_CHUNK
    n = D_HEAD_DIM
    probs = [(ch, pr) for ch in range(nc) for pr in range(HEAD_PAIRS)]
    cut = lambda x, p: x[p[0] * c:(p[0] + 1) * c, p[1] * LANES:(p[1] + 1) * LANES]
    qt, rt, bt, kt, bbar, kbar, vm = (x.astype(BF16) for x in (qt, rt, bt, kt, bbar, kbar, vm))
    ri = lax.broadcasted_iota(jnp.int32, (c, LANES), 0)
    li = lax.broadcasted_iota(jnp.int32, (c, LANES), 1)
    low = li < n
    strict = ri > li % n
    incl = ri >= li % n
    eye = jnp.where(ri == li % n, 1.0, 0.0)
    diag = lambda y: _pair_diag(y, low)
    halves = lambda x: jnp.where(low, x[:n], x[n:])

    qs = {p: cut(qt, p) for p in probs}
    vd = {p: diag(cut(vm, p)) for p in probs}
    aa = {p: _dot(jnp.concatenate([qs[p], cut(rt, p)], axis=0),
                  jnp.concatenate([diag(cut(bt, p)), diag(cut(kt, p))], axis=0), NT) for p in probs}
    lk = {p: jnp.where(strict, aa[p][:c, LANES:], 0.0).astype(BF16) for p in probs}
    ab = {p: jnp.where(incl, aa[p][c:, :LANES], 0.0).astype(BF16) for p in probs}
    ak = {p: jnp.where(incl, aa[p][c:, LANES:], 0.0).astype(BF16) for p in probs}
    npow = {p: jnp.where(strict, -aa[p][:c, :LANES], 0.0).astype(BF16) for p in probs}
    tinv = {p: eye + npow[p].astype(F32) for p in probs}
    lv = {p: _dot(jnp.concatenate([lk[p], ak[p]], axis=0), vd[p]) for p in probs}
    zp = {p: lv[p][:c].astype(BF16) for p in probs}
    npow = {p: _dot(npow[p], diag(npow[p])).astype(BF16) for p in probs}
    for _ in range(int(math.log2(c)) - 2):
        both = {p: _dot(jnp.concatenate([tinv[p].astype(BF16), npow[p]], axis=0), diag(npow[p])) for p in probs}
        tinv = {p: tinv[p] + both[p][:c] for p in probs}
        npow = {p: both[p][c:].astype(BF16) for p in probs}
    tinv = {p: (tinv[p] + _dot(tinv[p].astype(BF16), diag(npow[p]))).astype(BF16) for p in probs}
    tq = {p: _dot(tinv[p], jnp.concatenate([diag(qs[p]), diag(zp[p])], axis=1)) for p in probs}
    qh = {p: tq[p][:, :LANES].astype(BF16) for p in probs}
    zn = {p: (-tq[p][:, LANES:]).astype(BF16) for p in probs}
    abq = {p: _dot(ab[p], jnp.concatenate([diag(qh[p]), diag(zn[p])], axis=1)) for p in probs}
    rh = {p: (cut(rt, p).astype(F32) - abq[p][:, :LANES]).astype(BF16) for p in probs}
    pv = {p: abq[p][:, LANES:] + lv[p][c:] for p in probs}
    gt = {p: halves(-_dot(qh[p], cut(bbar, p), TN)).astype(BF16) for p in probs}
    cst = {p: halves(_dot(jnp.concatenate([cut(vm, p), zn[p]], axis=0),
                          jnp.concatenate([cut(kbar, p), cut(bbar, p)], axis=0), TN)) for p in probs}
    return rh, pv, gt, cst


def _wkv_scan_steps(s_ref, o_ref, pre, gam, nc):
    c = WKV_CHUNK
    rh, pv, gt, cst = pre
    low = lax.broadcasted_iota(jnp.int32, (D_HEAD_DIM, LANES), 1) < D_HEAD_DIM
    for ch in range(nc):
        for pr in range(HEAD_PAIRS):
            p = (ch, pr)
            s0 = s_ref[pr]
            s0b = s0.astype(BF16)
            o_ref[ch * c:(ch + 1) * c, pr * LANES:(pr + 1) * LANES] = _dot(rh[p], _pair_diag(s0b, low), NT) + pv[p]
            s_ref[pr] = (s0 * gam[ch][:, pr * LANES:(pr + 1) * LANES] + _dot(s0b, _pair_diag(gt[p], low))
                         + cst[p])
        yield


def _rwkv_prompt_kernel(pd_ref, mu_ref, w0_ref, w2_ref, a0_ref, a2_ref, g2_ref, kk_ref, ka_ref, rk_ref,
                        lg_ref, lb_ref, q8_ref, ck_ref, cv_ref, yd_ref, shift_ref, wkv_ref, o8_ref,
                        car_ref, s_ref, o_ref, *, tt, bb):
    t = pl.program_id(1)
    nt = pl.num_programs(1)

    @pl.when(t == 0)
    def _():
        car_ref[...] = jnp.zeros_like(car_ref)
        s_ref[...] = jnp.zeros_like(s_ref)

    pd = pd_ref[0]
    ext = jnp.concatenate([car_ref[...], pd], axis=0)
    prev = pltpu.roll(ext, 1, 0)[ROW_CARRY:]
    car_ref[...] = ext[tt:]
    xs = pd + (prev - pd) * mu_ref[...]
    seg = _head_ones(W_D, D_HEAD_DIM)
    r, k, v, kk, a, ld, g = _rwkv_inputs(xs, w0_ref[...], w2_ref[...], a0_ref[...], a2_ref[...],
                                         g2_ref[...], kk_ref[...], ka_ref[...], seg)
    c = WKV_CHUNK
    nc = tt // c
    tri = jnp.where(lax.broadcasted_iota(jnp.int32, (c, c), 0) >= lax.broadcasted_iota(jnp.int32, (c, c), 1),
                    1.0, 0.0).astype(BF16)
    cs_chunks = [_ones_dot(tri, ld[ch * c:(ch + 1) * c]) for ch in range(nc)]
    cs = jnp.concatenate(cs_chunks, axis=0)
    cs_end = [x[c - 1:c] for x in cs_chunks]
    cs_last = jnp.concatenate([jnp.broadcast_to(x, (c, W_D)) for x in cs_end], axis=0)
    e_neg = jnp.exp(-cs)
    e_tail = jnp.exp(cs_last - cs)
    b = kk * a
    pre = _wkv_precompute(kk * jnp.exp(cs - ld), r * jnp.exp(cs), b * e_neg, k * e_neg, b * e_tail, k * e_tail, v, nc)
    _interleave(_wkv_scan_steps(s_ref, o_ref, pre, [jnp.exp(x) for x in cs_end], nc),
                _attn_sample_steps(q8_ref, ck_ref, cv_ref, o8_ref, bb))
    yd_ref[0] = _rwkv_finish(o_ref[...], r, k, v, g, rk_ref[...], lg_ref[...], lb_ref[...], seg).astype(yd_ref.dtype)

    @pl.when(t == nt - 1)
    def _():
        shift_ref[0] = pd[tt - 1:tt]
        for hd in range(D_HEADS):
            wkv_ref[0, hd] = _head_cols(s_ref[hd // 2], hd % 2)


def _rwkv_prompt(pd, params, q, cache_k, cache_v, layer, tt):
    bn, t, _ = pd.shape
    nt = t // tt
    n = q.shape[0]
    bb = n // (bn * nt)
    assert bb * bn * nt == n
    nbytes = 2 * tt * D_PROJ * 4 + 40 * tt * W_D * 4 + 4 * bb * MEM_ROWS * LANES * 4 + 2 * bb * MEM_ROWS * LANES * 2
    req = pl.BlockSpec((bb, SUBLANES, LANES), lambda i, j: (i * nt + j, 0, 0))
    cache_spec = pl.BlockSpec((None, bb, MEM_ROWS, LANES), lambda i, j: (layer, i * nt + j, 0, 0))
    yd, shift, wkv, o8 = pl.pallas_call(
        functools.partial(_rwkv_prompt_kernel, tt=tt, bb=bb),
        out_shape=(jax.ShapeDtypeStruct((bn, t, W_D), BF16),
                   jax.ShapeDtypeStruct((bn, 1, D_PROJ), F32),
                   jax.ShapeDtypeStruct((bn, D_HEADS, D_HEAD_DIM, D_HEAD_DIM), F32),
                   jax.ShapeDtypeStruct((n, SUBLANES, LANES), F32)),
        grid=(bn, nt),
        in_specs=[pl.BlockSpec((1, tt, D_PROJ), lambda i, j: (i, j, 0))] + [_lspec(z, layer) for z in params]
                 + [req, cache_spec, cache_spec],
        out_specs=(pl.BlockSpec((1, tt, W_D), lambda i, j: (i, j, 0)),
                   pl.BlockSpec((1, 1, D_PROJ), lambda i, j: (i, 0, 0)),
                   pl.BlockSpec((1, D_HEADS, D_HEAD_DIM, D_HEAD_DIM), lambda i, j: (i, 0, 0, 0)),
                   req),
        scratch_shapes=[pltpu.VMEM((ROW_CARRY, D_PROJ), F32),
                        pltpu.VMEM((HEAD_PAIRS, D_HEAD_DIM, LANES), F32),
                        pltpu.VMEM((tt, W_D), F32)],
        compiler_params=_params(("arbitrary", "arbitrary"), nbytes),
        name="rwkv_prompt",
    )(pd, *params, _to_pair_rows(q), _cache_rows_view(cache_k), _cache_rows_view(cache_v))
    return yd, shift, wkv, _from_pair_rows(o8)


def _rwkv_sample_kernel(pd_ref, sh_ref, st_ref, mu_ref, w0_ref, w2_ref, a0_ref, a2_ref, g2_ref, kk_ref,
                        ka_ref, rk_ref, lg_ref, lb_ref, yd_ref, so_ref, rows_ref, cols_ref, ot_ref):
    h = pl.program_id(0)
    n = D_HEAD_DIM

    @pl.when(h == 0)
    def _():
        pd = pd_ref[...]
        xs = pd + (sh_ref[...] - pd) * mu_ref[...]
        seg = _head_ones(W_D, D_HEAD_DIM)
        r, k, v, kk, a, ld, g = _rwkv_inputs(xs, w0_ref[...], w2_ref[...], a0_ref[...], a2_ref[...],
                                             g2_ref[...], kk_ref[...], ka_ref[...], seg)
        for j, x in enumerate((r, k, v, g)):
            rows_ref[j] = x
        for j, x in enumerate((kk, jnp.exp(ld), kk * a, k, r, v)):
            cols_ref[j] = x.T

    base = pl.multiple_of(h * n, n)
    kap, dec, bvec, kvec, rvec = (cols_ref[j, pl.ds(base, n), :] for j in range(5))
    for vi in range(n):
        s = st_ref[0, vi]
        u = -jnp.sum(s * kap, axis=0, keepdims=True)
        s = s * dec + u * bvec + cols_ref[5, pl.ds(base + vi, 1), :] * kvec
        so_ref[0, vi] = s
        ot_ref[pl.ds(base + vi, 1), :] = jnp.sum(s * rvec, axis=0, keepdims=True)

    @pl.when(h == pl.num_programs(0) - 1)
    def _():
        seg = _head_ones(W_D, D_HEAD_DIM)
        yd_ref[...] = _rwkv_finish(ot_ref[...].T, rows_ref[0], rows_ref[1], rows_ref[2], rows_ref[3],
                                   rk_ref[...], lg_ref[...], lb_ref[...], seg)


def _rwkv_sample(pd, shift, state, params, layer):
    n = pd.shape[0]
    sblock = (1, D_HEAD_DIM, D_HEAD_DIM, n)
    return pl.pallas_call(
        _rwkv_sample_kernel,
        out_shape=(jax.ShapeDtypeStruct((n, W_D), F32), jax.ShapeDtypeStruct(state.shape[1:], F32)),
        grid=(D_HEADS,),
        in_specs=[pl.BlockSpec((n, D_PROJ), lambda i: (0, 0)), _lspec(shift, layer),
                  pl.BlockSpec((None,) + sblock, lambda i: (layer, i, 0, 0, 0))]
                 + [_lspec(z, layer) for z in params],
        out_specs=(pl.BlockSpec((n, W_D), lambda i: (0, 0)), pl.BlockSpec(sblock, lambda i: (i, 0, 0, 0))),
        scratch_shapes=[pltpu.VMEM((4, n, W_D), F32), pltpu.VMEM((6, W_D, n), F32), pltpu.VMEM((W_D, n), F32)],
        compiler_params=pltpu.CompilerParams(dimension_semantics=("arbitrary",)),
        name="rwkv_sample",
    )(pd, shift, state, *params)


def _block_diag(w):
    gn, n, _ = w.shape
    eye = jnp.eye(gn, dtype=w.dtype)
    return (eye[:, None, :, None] * w[:, :, None, :]).reshape(gn * n, gn * n)


def kernel(x_prompt, x_sample, mem_prompt, cache_mem_k, cache_mem_v, state_pool, state_conv, state_shift, state_wkv,
           w_in, mu_d, ln_v_g, ln_v_b, ws_chunk, b_chunk, w_pool, pool_scale, conv_w,
           rwkv_w0, rwkv_w2, rwkv_a0, rwkv_a2, rwkv_g2, rwkv_k_k, rwkv_k_a, rwkv_r_k, rwkv_lnx_g, rwkv_lnx_b,
           w_out, ln1_g, ln1_b, w_xq, w_xk, w_xv, w_xo, ln2_g, ln2_b, ffn_w1, ffn_w3, ffn_w2, ln3_g, ln3_b):
    bp, t_p, d = x_prompt.shape
    ns, t_s, _ = x_sample.shape
    depth = w_in.shape[0]
    assert d == D_MODEL and t_s == 1 and t_p % CHUNK == 0 and w_in.shape[2] == PROJ
    alpha = (2 * depth) ** 0.25
    mp = bp * t_p
    nw = W_A + W_B + W_C
    row = lambda z: z.reshape(depth, 1, -1)

    w_in_b, w_out_b = w_in, w_out
    w_xq_b, w_xk_b, w_xv_b, w_xo_b = w_xq, w_xk, w_xv, w_xo
    w1_b, w3_b, w2_b = ffn_w1, ffn_w3, ffn_w2
    ws_flat = ws_chunk.reshape(depth, A_HEADS * CHUNK, CHUNK)
    bias_full = jnp.repeat(jnp.swapaxes(b_chunk, 1, 2), A_HEAD_DIM, axis=2)
    ws0 = jnp.repeat(ws_chunk[:, :, 0, 0], A_HEAD_DIM, axis=1).reshape(depth, 1, W_A)
    wpool_bd = jnp.stack([_block_diag(w_pool[l]) for l in range(depth)]).astype(BF16)
    abc_w = (row(ln_v_g), row(ln_v_b))
    abc_w2 = (wpool_bd, row(pool_scale), conv_w)
    rw = (row(mu_d), row(rwkv_w0), rwkv_w2, row(rwkv_a0), rwkv_a2, rwkv_g2,
          row(rwkv_k_k), row(rwkv_k_a), row(rwkv_r_k), row(rwkv_lnx_g), row(rwkv_lnx_b))
    ln1, ln2, ln3 = (row(ln1_g), row(ln1_b)), (row(ln2_g), row(ln2_b)), (row(ln3_g), row(ln3_b))
    pool_view = jnp.swapaxes(state_pool, 1, 2)
    wkv_view = state_wkv.transpose(0, 2, 3, 4, 1)
    conv_view = state_conv.reshape(depth, ns, (CONV_W - 1) * W_C)
    shift_view = state_shift.reshape(depth, ns, D_PROJ)

    hp = x_prompt
    hs = x_sample.reshape(ns, d)
    mem = mem_prompt.reshape(bp * MEM_LEN, d)
    mk_all, mv_all, mk_rows, mv_rows = _mem_kv(mem, w_xk_b, w_xv_b, 512)
    mk_all = mk_all.reshape(depth, bp, MEM_LEN, d)
    mv_all = mv_all.reshape(depth, bp, MEM_LEN, d)
    outs = [[] for _ in range(10)]
    for l in range(depth):
        pd, mix, v_last, pool_p, conv_p = _proj_abc_prompt(hp, w_in_b, *abc_w, ws_flat, bias_full, *abc_w2, l, tt=1024)

        pabc_s, pd_s = _proj(hs, w_in_b, l, ns)
        mix_s, v_s, pool_s, conv_s = _abc_sample(pabc_s, pool_view, conv_view, *abc_w, ws0, bias_full, *abc_w2,
                                                 l, pos0=PAST_LEN)
        yd_s, wkv_s = _rwkv_sample(pd_s, shift_view, wkv_view, rw, l)
        hs = _mm_res_ln([mix_s, yd_s], w_out_b, l, hs, *ln1, tm=ns, alpha=alpha, name="out_proj_s")
        q_s = _mm(hs, w_xq_b, l, ns, "q_s")

        yd, shift_p, wkv_p, o_s = _rwkv_prompt(pd, rw, q_s, cache_mem_k, cache_mem_v, l, tt=512)
        hp = _attn_prompt(mix, yd, hp, mk_all, mv_all, w_out_b, *ln1, w_xq_b, w_xo_b, *ln2, l, tq=512, alpha=alpha)
        hp = _ffn(hp.reshape(mp, d), w1_b, w3_b, w2_b, *ln3, l, tm=1024, tf=256, alpha=alpha,
                  name="ffn").reshape(bp, t_p, d)
        for lst, val in zip(outs[:5], (v_last, pool_p, conv_p, shift_p, wkv_p)):
            lst.append(val)

        hs = _mm_res_ln([o_s], w_xo_b, l, hs, *ln2, tm=ns, alpha=alpha, name="xo_s")
        hs = _ffn(hs, w1_b, w3_b, w2_b, *ln3, l, tm=ns, tf=256, alpha=alpha, name="ffn_s")
        for lst, val in zip(outs[5:], (v_s.reshape(ns, 1, W_A), pool_s,
                                       conv_s.reshape(ns, CONV_W - 1, W_C), pd_s.reshape(ns, 1, D_PROJ), wkv_s)):
            lst.append(val)

    stacked = [jnp.stack(o) for o in outs]
    stacked[6] = jnp.swapaxes(stacked[6], 1, 2)
    stacked[9] = stacked[9].transpose(0, 4, 1, 2, 3)
    return ((hp, hs.reshape(ns, 1, d)) + tuple(stacked[:5]) + (_cache_from_rows(mk_rows, bp), _cache_from_rows(mv_rows, bp))
            + tuple(stacked[5:]))
```

```python
import functools
import math

import jax
import jax.numpy as jnp
from jax import lax
from jax.experimental import pallas as pl
from jax.experimental.pallas import tpu as pltpu

F32 = jnp.float32
BF16 = jnp.bfloat16

D_MODEL = 1024
W_A = 256
W_B = 256
W_C = 256
W_D = 256
A_HEADS = 4
A_HEAD_DIM = W_A // A_HEADS
CHUNK = 128
POOL_WINDOWS = (2, 4, 8, 16)
POOL_GROUP = W_B // len(POOL_WINDOWS)
POOL_BUF = max(POOL_WINDOWS) - 1
CONV_W = 3
D_HEAD_DIM = 64
D_HEADS = W_D // D_HEAD_DIM
R_DECAY = 32
R_AAA = 32
R_GATE = 64
D_PROJ = 3 * W_D + R_DECAY + R_AAA + R_GATE
N_ABC = 2 * W_A + W_B + 3 * W_C
PROJ = N_ABC + D_PROJ
MEM_LEN = 256
X_HEADS = 4
X_HEAD_DIM = D_MODEL // X_HEADS
D_FF = int(math.ceil(8 * D_MODEL / 3 / 256)) * 256
PAST_LEN = 16384
LN_EPS = 1e-5
GN_EPS = 64e-5

WKV_CHUNK = 64
POOL_CARRY = 24
ROW_CARRY = 8
V7X_VMEM_BYTES = 64 * 1024 * 1024
VMEM_CAP = V7X_VMEM_BYTES - 8 * 1024 * 1024

NN = (((1,), (0,)), ((), ()))
NT = (((1,), (1,)), ((), ()))
TN = (((0,), (0,)), ((), ()))


def _vmem_limit(nbytes):
    return int(min(VMEM_CAP, max(32 * 1024 * 1024, 2 * nbytes)))


def _params(sem, nbytes):
    return pltpu.CompilerParams(dimension_semantics=sem, vmem_limit_bytes=_vmem_limit(nbytes))


def _lspec(arr, layer):
    tail = arr.shape[1:]
    zeros = (0,) * len(tail)
    return pl.BlockSpec((None,) + tail, lambda *_: (layer,) + zeros)


def _dot(a, b, dims=NN):
    return lax.dot_general(a, b, dims, preferred_element_type=F32)


def _split2(a):
    hi = a.astype(BF16)
    lo = (a - hi.astype(F32)).astype(BF16)
    return hi, lo


def _dot_hi(a, b, dims=NN):
    ah, al = _split2(a)
    bh, bl = _split2(b)
    return _dot(ah, bh, dims) + _dot(ah, bl, dims) + _dot(al, bh, dims)


def _dot_ones(x, ones_bf16, dims=NN):
    hi, lo = _split2(x)
    return _dot(hi, ones_bf16, dims) + _dot(lo, ones_bf16, dims)


def _ones_dot(ones_bf16, x):
    hi = x.astype(BF16)
    r1 = x - hi.astype(F32)
    mid = r1.astype(BF16)
    lo = (r1 - mid.astype(F32)).astype(BF16)
    return _dot(ones_bf16, hi) + _dot(ones_bf16, mid) + _dot(ones_bf16, lo)


def _ln(x, g, b, eps=LN_EPS):
    mu = jnp.mean(x, axis=-1, keepdims=True)
    xc = x - mu
    var = jnp.mean(xc * xc, axis=-1, keepdims=True)
    return xc * lax.rsqrt(var + eps) * g + b


def _gelu(x):
    c = math.sqrt(2.0 / math.pi)
    return x * (0.5 * (1.0 + jnp.tanh(c * (x + 0.044715 * (x * x * x)))))


def _sigmoid(x):
    return 1.0 / (1.0 + jnp.exp(-x))


def _softplus(x):
    return jnp.maximum(x, 0.0) + jnp.log(1.0 + jnp.exp(-jnp.abs(x)))


def _interleave(*staged):
    live = list(staged)
    while live:
        for steps in list(live):
            if next(steps, StopIteration) is StopIteration:
                live.remove(steps)


def _head_ones(n, group):
    r = lax.broadcasted_iota(jnp.int32, (n, n), 0) // group
    c = lax.broadcasted_iota(jnp.int32, (n, n), 1) // group
    return jnp.where(r == c, 1.0, 0.0).astype(BF16)


def _proj_kernel(x_ref, w_ref, oabc_ref, od_ref):
    y = _dot(x_ref[...].astype(BF16), w_ref[...].astype(BF16))
    oabc_ref[...] = y[:, :N_ABC]
    od_ref[...] = y[:, N_ABC:]


def _proj(x, w, layer, tm):
    m, k = x.shape
    nbytes = 2 * (tm * k * 4 + k * PROJ * 4 + tm * PROJ * 4) + tm * PROJ * 4 + k * PROJ * 2
    return pl.pallas_call(
        _proj_kernel,
        out_shape=(jax.ShapeDtypeStruct((m, N_ABC), F32), jax.ShapeDtypeStruct((m, D_PROJ), F32)),
        grid=(m // tm,),
        in_specs=[pl.BlockSpec((tm, k), lambda i: (i, 0)), _lspec(w, layer)],
        out_specs=(pl.BlockSpec((tm, N_ABC), lambda i: (i, 0)), pl.BlockSpec((tm, D_PROJ), lambda i: (i, 0))),
        compiler_params=_params(("parallel",), nbytes),
        name="proj",
    )(x, w)


def _mm_kernel(x_ref, w_ref, o_ref):
    o_ref[...] = _dot(x_ref[...].astype(BF16), w_ref[...].astype(BF16))


def _mm(x, w, layer, tm, name):
    m, k = x.shape
    n = w.shape[2]
    nbytes = 2 * (tm * k * 4 + k * n * 4 + tm * n * 4) + tm * n * 4 + k * n * 2
    return pl.pallas_call(
        _mm_kernel,
        out_shape=jax.ShapeDtypeStruct((m, n), F32),
        grid=(m // tm,),
        in_specs=[pl.BlockSpec((tm, k), lambda i: (i, 0)), _lspec(w, layer)],
        out_specs=pl.BlockSpec((tm, n), lambda i: (i, 0)),
        compiler_params=_params(("parallel",), nbytes),
        name=name,
    )(x, w)


def _mem_kv_kernel(x_ref, wk_ref, wv_ref, k_ref, v_ref, kt_ref, vt_ref, wkb_ref, wvb_ref):
    @pl.when(pl.program_id(1) == 0)
    def _():
        wkb_ref[...] = wk_ref[...].astype(BF16)
        wvb_ref[...] = wv_ref[...].astype(BF16)

    xb = x_ref[...].astype(BF16)
    for w_ref, o_ref, t_ref in ((wkb_ref, k_ref, kt_ref), (wvb_ref, v_ref, vt_ref)):
        y = _dot(xb, w_ref[...])
        o_ref[...] = y
        rows = y.shape[0]
        t_ref[...] = jnp.swapaxes(y.reshape(rows, X_HEADS, LANE_TILES, LANES), 1, 2).reshape(rows, SUBLANES, LANES)


def _mem_kv(mem, wk, wv, tm):
    m, k = mem.shape
    nl, _, n = wk.shape
    nbytes = 2 * (tm * k * 4 + 2 * k * n * 4 + 4 * tm * n * 4) + 2 * tm * n * 4 + 2 * k * n * 2
    wspec = pl.BlockSpec((None, k, n), lambda l, i: (l, 0, 0))
    ospec = pl.BlockSpec((None, tm, n), lambda l, i: (l, i, 0))
    tspec = pl.BlockSpec((None, tm, SUBLANES, LANES), lambda l, i: (l, i, 0, 0))
    shape = jax.ShapeDtypeStruct((nl, m, n), F32)
    tshape = jax.ShapeDtypeStruct((nl, m, SUBLANES, LANES), F32)
    return pl.pallas_call(
        _mem_kv_kernel,
        out_shape=(shape, shape, tshape, tshape),
        grid=(nl, m // tm),
        in_specs=[pl.BlockSpec((tm, k), lambda l, i: (i, 0)), wspec, wspec],
        out_specs=(ospec, ospec, tspec, tspec),
        scratch_shapes=[pltpu.VMEM((k, n), BF16), pltpu.VMEM((k, n), BF16)],
        compiler_params=_params(("arbitrary", "arbitrary"), nbytes),
        name="mem_kv",
    )(mem, wk, wv)


def _mm_res_ln_kernel(*refs, n_in, alpha):
    xs = refs[:n_in]
    ws = refs[n_in:2 * n_in]
    h_ref, g_ref, b_ref, o_ref = refs[2 * n_in:]
    y = _dot(xs[0][...].astype(BF16), ws[0][...].astype(BF16))
    for x_ref, w_ref in zip(xs[1:], ws[1:]):
        y = y + _dot(x_ref[...].astype(BF16), w_ref[...].astype(BF16))
    o_ref[...] = _ln(alpha * h_ref[...] + y, g_ref[...], b_ref[...])


def _mm_res_ln(xs, w, layer, h, g, b, tm, alpha, name):
    m, n = h.shape
    nbytes = 2 * sum(tm * x.shape[1] * 4 + x.shape[1] * n * 4 for x in xs) + 5 * tm * n * 4 + w.shape[1] * n * 2
    in_specs = [pl.BlockSpec((tm, x.shape[1]), lambda i: (i, 0)) for x in xs]
    start = 0
    for x in xs:
        width = x.shape[1]
        assert start % width == 0
        in_specs.append(pl.BlockSpec((None, width, n), lambda i, blk=start // width: (layer, blk, 0)))
        start += width
    assert start == w.shape[1]
    in_specs += [pl.BlockSpec((tm, n), lambda i: (i, 0)), _lspec(g, layer), _lspec(b, layer)]
    return pl.pallas_call(
        functools.partial(_mm_res_ln_kernel, n_in=len(xs), alpha=alpha),
        out_shape=jax.ShapeDtypeStruct((m, n), F32),
        grid=(m // tm,),
        in_specs=in_specs,
        out_specs=pl.BlockSpec((tm, n), lambda i: (i, 0)),
        compiler_params=_params(("parallel",), nbytes),
        name=name,
    )(*xs, *([w] * len(xs)), h, g, b)


def _ffn_kernel(x_ref, w1_ref, w3_ref, w2_ref, g_ref, b_ref, o_ref, xb_ref, acc_ref, *, alpha):
    j = pl.program_id(1)

    @pl.when(j == 0)
    def _():
        xb_ref[...] = x_ref[...].astype(BF16)
        acc_ref[...] = jnp.zeros_like(acc_ref)

    xb = xb_ref[...]
    h1 = _dot(xb, w1_ref[...])
    h3 = _dot(xb, w3_ref[...])
    a = (h1 * _sigmoid(h1) * h3).astype(BF16)
    acc_ref[...] += _dot(a, w2_ref[...])

    @pl.when(j == pl.num_programs(1) - 1)
    def _():
        o_ref[...] = _ln(alpha * x_ref[...] + acc_ref[...], g_ref[...], b_ref[...])


def _ffn(x, w1, w3, w2, g, b, layer, tm, tf, alpha, name):
    m, d = x.shape
    nbytes = 4 * tm * d * 4 + tm * d * 2 + tm * d * 4 + 2 * 3 * d * tf * 2 + 3 * tm * tf * 4
    return pl.pallas_call(
        functools.partial(_ffn_kernel, alpha=alpha),
        out_shape=jax.ShapeDtypeStruct((m, d), F32),
        grid=(m // tm, D_FF // tf),
        in_specs=[pl.BlockSpec((tm, d), lambda i, j: (i, 0)),
                  pl.BlockSpec((None, d, tf), lambda i, j: (0, 0, j)),
                  pl.BlockSpec((None, d, tf), lambda i, j: (0, 0, j)),
                  pl.BlockSpec((None, tf, d), lambda i, j: (0, j, 0)),
                  _lspec(g, layer), _lspec(b, layer)],
        out_specs=pl.BlockSpec((tm, d), lambda i, j: (i, 0)),
        scratch_shapes=[pltpu.VMEM((tm, d), BF16), pltpu.VMEM((tm, d), F32)],
        compiler_params=_params(("parallel", "arbitrary"), nbytes),
        name=name,
    )(x, w1, w3, w2, g, b)


def _softmax_rows(s):
    m = jnp.max(s, axis=-1, keepdims=True)
    e = jnp.exp(s - m)
    return e / jnp.sum(e, axis=-1, keepdims=True)


ATTN_PIECES = 2


def _attn_prompt_kernel(mix_ref, yd_ref, h_ref, mk_ref, mv_ref, wma_ref, wmd_ref, g1_ref, b1_ref,
                        wq_ref, wo_ref, g_ref, b_ref, o_ref, ob_ref, wmb_ref, wqb_ref, wob_ref, *, alpha):
    nw = wma_ref.shape[0]

    @pl.when((pl.program_id(0) == 0) & (pl.program_id(1) == 0))
    def _():
        wmb_ref[:nw, :] = wma_ref[...].astype(BF16)
        wmb_ref[nw:, :] = wmd_ref[...].astype(BF16)
        wqb_ref[...] = wq_ref[...].astype(BF16)
        wob_ref[...] = wo_ref[...].astype(BF16)

    rows = h_ref.shape[1] // ATTN_PIECES
    scale = X_HEAD_DIM ** -0.5
    sls = [slice(hd * X_HEAD_DIM, (hd + 1) * X_HEAD_DIM) for hd in range(X_HEADS)]
    kb = [mk_ref[0, :, sl].astype(BF16) for sl in sls]
    vb = [mv_ref[0, :, sl].astype(BF16) for sl in sls]
    y, h, q, sc = {}, {}, {}, {}

    def project(p, rs):
        y[p] = (_dot(mix_ref[0, rs, :].astype(BF16), wmb_ref[:nw, :])
                + _dot(yd_ref[0, rs, :].astype(BF16), wmb_ref[nw:, :]))

    def query(p, rs):
        h[p] = _ln(alpha * h_ref[0, rs, :] + y[p], g1_ref[...], b1_ref[...])
        q[p] = _dot(h[p].astype(BF16), wqb_ref[...]).astype(BF16)

    def scores(p, rs):
        sc[p] = [_dot(q[p][:, sl], k, NT) * scale for sl, k in zip(sls, kb)]

    def values(p, rs):
        for sl, s, v in zip(sls, sc[p], vb):
            ob_ref[rs, sl] = _dot(_softmax_rows(s).astype(BF16), v).astype(BF16)

    def output(p, rs):
        o_ref[0, rs, :] = _ln(alpha * h[p] + _dot(ob_ref[rs, :], wob_ref[...]), g_ref[...], b_ref[...])

    stages = (project, query, scores, values, output)
    for step in range(len(stages) + ATTN_PIECES - 1):
        for p in range(ATTN_PIECES):
            if 0 <= step - p < len(stages):
                stages[step - p](p, slice(p * rows, (p + 1) * rows))


def _attn_prompt(mix, yd, h, mk, mv, w_mix, g1, b1, wq, wo, g, b, layer, tq, alpha):
    bn, t, d = h.shape
    nw = mix.shape[2]
    assert nw % W_D == 0 and w_mix.shape[1] == nw + W_D
    nbytes = (6 * tq * d * 4 + 4 * MEM_LEN * d * 4 + 6 * d * d * 4 + 3 * d * d * 2 + tq * d * 2
              + 3 * tq * d * 4 + 3 * tq * MEM_LEN * 4)
    tile = lambda n: pl.BlockSpec((1, tq, n), lambda i, j: (i, j, 0))
    return pl.pallas_call(
        functools.partial(_attn_prompt_kernel, alpha=alpha),
        out_shape=jax.ShapeDtypeStruct((bn, t, d), F32),
        grid=(bn, t // tq),
        in_specs=[tile(nw), tile(W_D), tile(d),
                  pl.BlockSpec((None, 1, MEM_LEN, d), lambda i, j: (layer, i, 0, 0)),
                  pl.BlockSpec((None, 1, MEM_LEN, d), lambda i, j: (layer, i, 0, 0)),
                  pl.BlockSpec((None, nw, d), lambda i, j: (layer, 0, 0)),
                  pl.BlockSpec((None, W_D, d), lambda i, j: (layer, nw // W_D, 0)),
                  _lspec(g1, layer), _lspec(b1, layer),
                  _lspec(wq, layer), _lspec(wo, layer), _lspec(g, layer), _lspec(b, layer)],
        out_specs=tile(d),
        scratch_shapes=[pltpu.VMEM((tq, d), BF16), pltpu.VMEM((nw + W_D, d), BF16),
                        pltpu.VMEM((d, d), BF16), pltpu.VMEM((d, d), BF16)],
        compiler_params=_params(("arbitrary", "arbitrary"), nbytes),
        name="attn_prompt",
    )(mix, yd, h, mk, mv, w_mix, w_mix, g1, b1, wq, wo, g, b)


LANES = 128
SUBLANES = 8
LANE_TILES = X_HEAD_DIM // LANES
MEM_ROWS = MEM_LEN * LANE_TILES * X_HEADS


def _cache_rows_view(cache):
    nl, n = cache.shape[:2]
    x = cache.reshape(nl, n, MEM_LEN, X_HEADS, LANE_TILES, LANES)
    return x.transpose(0, 1, 2, 4, 3, 5).reshape(nl, n, MEM_ROWS, LANES)


def _cache_from_rows(rows, n):
    nl = rows.shape[0]
    x = rows.reshape(nl, n, MEM_LEN, LANE_TILES, X_HEADS, LANES)
    return x.transpose(0, 1, 2, 4, 3, 5).reshape(nl, n, MEM_LEN, X_HEADS, X_HEAD_DIM)


def _attn_sample_steps(q_ref, k_ref, v_ref, o_ref, bb):
    scale = X_HEAD_DIM ** -0.5
    shape = (SUBLANES, MEM_ROWS)
    rowi = lax.broadcasted_iota(jnp.int32, shape, 0)
    coli = lax.broadcasted_iota(jnp.int32, shape, 1)
    valid = (coli % SUBLANES) == rowi
    raw = [_dot(q_ref[i].astype(BF16), k_ref[i].astype(BF16), NT) for i in range(bb)]
    yield
    probs = []
    for r in raw:
        r = jnp.where(valid, r, 0.0)
        other = pltpu.roll(r, X_HEADS, 0)
        other = jnp.where(rowi < X_HEADS, pltpu.roll(other, MEM_ROWS - X_HEADS, 1), pltpu.roll(other, X_HEADS, 1))
        sc = jnp.where(valid, (r + other) * scale, -jnp.inf)
        m = jnp.max(sc, axis=-1, keepdims=True)
        e = jnp.exp(sc - m)
        probs.append((e / jnp.sum(e, axis=-1, keepdims=True)).astype(BF16))
        yield
    for i in range(bb):
        o_ref[i] = _dot(probs[i], v_ref[i].astype(BF16))
        yield


def _to_pair_rows(q):
    n = q.shape[0]
    return q.reshape(n, X_HEADS, LANE_TILES, LANES).transpose(0, 2, 1, 3).reshape(n, SUBLANES, LANES)


def _from_pair_rows(o8):
    n = o8.shape[0]
    return o8.reshape(n, LANE_TILES, X_HEADS, LANES).transpose(0, 2, 1, 3).reshape(n, D_MODEL)


def _pool_window_sums(ext, tt):
    s2 = ext + pltpu.roll(ext, 1, 0)
    s4 = s2 + pltpu.roll(s2, 2, 0)
    s8 = s4 + pltpu.roll(s4, 4, 0)
    s16 = s8 + pltpu.roll(s8, 8, 0)
    grp = lax.broadcasted_iota(jnp.int32, (tt, W_B), 1) // POOL_GROUP
    lo = ext.shape[0] - tt
    return jnp.where(grp == 0, s2[lo:], jnp.where(grp == 1, s4[lo:], jnp.where(grp == 2, s8[lo:], s16[lo:])))


def _proj_abc_kernel(x_ref, w_ref, lng_ref, lnb_ref, ws_ref, bias_ref, wpool_ref, pscale_ref, cw_ref,
                     f1_ref, f3_ref, f2_ref, pd_ref, mix_ref, vlast_ref, pool_ref, conv_ref,
                     f1b_ref, f3b_ref, f2b_ref, pcar_ref, ccar_ref, wb_ref, *, tt):
    t = pl.program_id(1)
    nt = pl.num_programs(1)

    @pl.when((pl.program_id(0) == 0) & (t == 0))
    def _():
        wb_ref[...] = w_ref[...].astype(BF16)

    @pl.when(t == 0)
    def _():
        pcar_ref[...] = jnp.zeros_like(pcar_ref)
        ccar_ref[...] = jnp.zeros_like(ccar_ref)

    rows = lax.broadcasted_iota(jnp.int32, (A_HEADS * CHUNK, CHUNK), 0) % CHUNK
    cols = lax.broadcasted_iota(jnp.int32, (A_HEADS * CHUNK, CHUNK), 1)
    wsm = jnp.where(rows >= cols, ws_ref[...], 0.0).astype(BF16)
    hid = lax.broadcasted_iota(jnp.int32, (CHUNK, W_A), 1) // A_HEAD_DIM
    win = jnp.left_shift(2, lax.broadcasted_iota(jnp.int32, (CHUNK, W_B), 1) // POOL_GROUP)
    rowi = lax.broadcasted_iota(jnp.int32, (CHUNK, W_B), 0)
    w = wb_ref[...]
    pcar = pcar_ref[...]
    ccar = ccar_ref[...]
    v = None
    for c in range(tt // CHUNK):
        rs = slice(c * CHUNK, (c + 1) * CHUNK)
        y = _dot(x_ref[0, rs, :].astype(BF16), w)
        pd_ref[0, rs, :] = y[:, N_ABC:]

        ga = _gelu(y[:, :2 * W_A])
        u = ga[:, :W_A]
        v = _ln(ga[:, W_A:], lng_ref[...], lnb_ref[...])
        zz = _dot(wsm, v.astype(BF16))
        z = zz[(A_HEADS - 1) * CHUNK:]
        for hd in range(A_HEADS - 2, -1, -1):
            z = jnp.where(hid == hd, zz[hd * CHUNK:(hd + 1) * CHUNK], z)
        mix_ref[0, rs, 0:W_A] = (u * (z + bias_ref[...])).astype(mix_ref.dtype)

        xb = y[:, 2 * W_A:2 * W_A + W_B]
        ext = jnp.concatenate([pcar, xb], axis=0)
        sums = _pool_window_sums(ext, CHUNK)
        pos = t * tt + c * CHUNK + rowi
        cnt = jnp.minimum(win, pos + 1).astype(F32)
        d = sums / cnt - xb
        mix_ref[0, rs, W_A:W_A + W_B] = (_dot(d.astype(BF16), wpool_ref[...]) * pscale_ref[...]).astype(mix_ref.dtype)
        pcar = ext[CHUNK:]

        o = 2 * W_A + W_B
        bg = y[:, o:o + W_C]
        zc = y[:, o + W_C:o + 2 * W_C] * y[:, o + 2 * W_C:o + 3 * W_C]
        extz = jnp.concatenate([ccar, zc], axis=0)
        conv = (cw_ref[0:1, :] * pltpu.roll(extz, 2, 0) + cw_ref[1:2, :] * pltpu.roll(extz, 1, 0)
                + cw_ref[2:3, :] * extz)
        mix_ref[0, rs, W_A + W_B:] = (bg * conv[ROW_CARRY:]).astype(mix_ref.dtype)
        ccar = extz[CHUNK:]
    pcar_ref[...] = pcar
    ccar_ref[...] = ccar
    for src_ref, dst_ref in ((f1_ref, f1b_ref), (f3_ref, f3b_ref), (f2_ref, f2b_ref)):
        dst_ref[...] = src_ref[...].astype(BF16)

    @pl.when(t == nt - 1)
    def _():
        vlast_ref[0] = v
        pool_ref[0] = pcar_ref[POOL_CARRY - POOL_BUF:, :]
        conv_ref[0] = ccar_ref[ROW_CARRY - (CONV_W - 1):, :]


def _proj_abc_prompt(h, w, lng, lnb, ws, bias, wpool, pscale, cw, ffn_ws, layer, tt):
    bn, t, d = h.shape
    nw = W_A + W_B + W_C
    nt = t // tt
    steps = bn * nt
    assert all(z.shape[1] % (steps * 2 * SUBLANES) == 0 for z in ffn_ws)
    f_in = [pl.BlockSpec((None, z.shape[1] // steps, z.shape[2]), lambda i, j: (layer, i * nt + j, 0)) for z in ffn_ws]
    f_out = [pl.BlockSpec((None, z.shape[1] // steps, z.shape[2]), lambda i, j: (0, i * nt + j, 0)) for z in ffn_ws]
    f_shapes = [jax.ShapeDtypeStruct((1,) + z.shape[1:], BF16) for z in ffn_ws]
    nbytes = (2 * tt * d * 4 + 2 * d * PROJ * 4 + d * PROJ * 2 + 2 * tt * (D_PROJ + nw) * 4 + 3 * CHUNK * PROJ * 4
              + 4 * A_HEADS * CHUNK * CHUNK * 4 + 12 * sum(z.shape[1] * z.shape[2] for z in ffn_ws) // steps)
    tile = lambda n: pl.BlockSpec((1, tt, n), lambda i, j: (i, j, 0))
    last = lambda r, n: pl.BlockSpec((1, r, n), lambda i, j: (i, 0, 0))
    return pl.pallas_call(
        functools.partial(_proj_abc_kernel, tt=tt),
        out_shape=(jax.ShapeDtypeStruct((bn, t, D_PROJ), F32),
                   jax.ShapeDtypeStruct((bn, t, nw), BF16),
                   jax.ShapeDtypeStruct((bn, CHUNK, W_A), F32),
                   jax.ShapeDtypeStruct((bn, POOL_BUF, W_B), F32),
                   jax.ShapeDtypeStruct((bn, CONV_W - 1, W_C), F32), *f_shapes),
        grid=(bn, nt),
        in_specs=[tile(d), *[_lspec(z, layer) for z in (w, lng, lnb, ws, bias, wpool, pscale, cw)], *f_in],
        out_specs=(tile(D_PROJ), tile(nw), last(CHUNK, W_A), last(POOL_BUF, W_B), last(CONV_W - 1, W_C), *f_out),
        scratch_shapes=[pltpu.VMEM((POOL_CARRY, W_B), F32), pltpu.VMEM((ROW_CARRY, W_C), F32),
                        pltpu.VMEM((d, PROJ), BF16)],
        compiler_params=_params(("arbitrary", "arbitrary"), nbytes),
        name="proj_abc_prompt",
    )(h, w, lng, lnb, ws, bias, wpool, pscale, cw, *ffn_ws)


def _abc_sample_kernel(x_ref, pool_ref, conv_ref, lng_ref, lnb_ref, ws0_ref, bias0_ref, wpool_ref,
                       pscale_ref, cw_ref, mix_ref, v_ref, poolo_ref, convo_ref, *, pos0):
    x = x_ref[...]
    n = x.shape[0]
    ga = _gelu(x[:, :2 * W_A])
    u = ga[:, :W_A]
    v = _ln(ga[:, W_A:], lng_ref[...], lnb_ref[...])
    v_ref[...] = v
    mix_ref[:, 0:W_A] = u * (v * ws0_ref[...] + bias0_ref[0:1, :])

    xb = x[:, 2 * W_A:2 * W_A + W_B]
    grp = lax.broadcasted_iota(jnp.int32, (n, W_B), 1) // POOL_GROUP
    run = xb
    sums = jnp.zeros_like(xb)
    for back in range(1, max(POOL_WINDOWS)):
        run = run + pool_ref[POOL_BUF - back]
        for gi, win in enumerate(POOL_WINDOWS):
            if back == win - 1:
                sums = jnp.where(grp == gi, run, sums)
    win = jnp.left_shift(2, grp)
    cnt = jnp.minimum(win, pos0 + 1).astype(F32)
    d = sums / cnt - xb
    mix_ref[:, W_A:W_A + W_B] = _dot(d.astype(BF16), wpool_ref[...]) * pscale_ref[...]
    for r in range(POOL_BUF - 1):
        poolo_ref[r] = pool_ref[r + 1]
    poolo_ref[POOL_BUF - 1] = xb

    o = 2 * W_A + W_B
    bg = x[:, o:o + W_C]
    zc = x[:, o + W_C:o + 2 * W_C] * x[:, o + 2 * W_C:o + 3 * W_C]
    z0 = conv_ref[:, :W_C]
    z1 = conv_ref[:, W_C:]
    y = cw_ref[0:1, :] * z0 + cw_ref[1:2, :] * z1 + cw_ref[2:3, :] * zc
    mix_ref[:, W_A + W_B:] = bg * y
    convo_ref[:, :W_C] = z1
    convo_ref[:, W_C:] = zc


def _abc_sample(pabc, pool, conv, lng, lnb, ws0, bias, wpool, pscale, cw, layer, pos0):
    n = pabc.shape[0]
    nw = W_A + W_B + W_C
    whole = lambda shape: pl.BlockSpec(shape, lambda i: (0,) * len(shape))
    return pl.pallas_call(
        functools.partial(_abc_sample_kernel, pos0=pos0),
        out_shape=(jax.ShapeDtypeStruct((n, nw), F32),
                   jax.ShapeDtypeStruct((n, W_A), F32),
                   jax.ShapeDtypeStruct(pool.shape[1:], F32),
                   jax.ShapeDtypeStruct(conv.shape[1:], F32)),
        grid=(1,),
        in_specs=[whole(pabc.shape)] + [_lspec(z, layer) for z in (pool, conv, lng, lnb, ws0, bias, wpool, pscale, cw)],
        out_specs=(whole((n, nw)), whole((n, W_A)), whole(pool.shape[1:]), whole(conv.shape[1:])),
        name="mixer_abc_sample",
    )(pabc, pool, conv, lng, lnb, ws0, bias, wpool, pscale, cw)


def _rwkv_inputs(xs, w0, w2, a0, a2, g2, kk_w, ka_w, seg):
    r = xs[:, 0:W_D]
    k = xs[:, W_D:2 * W_D]
    v = xs[:, 2 * W_D:3 * W_D]
    o = 3 * W_D
    dw = xs[:, o:o + R_DECAY]
    da = xs[:, o + R_DECAY:o + R_DECAY + R_AAA]
    dg = xs[:, o + R_DECAY + R_AAA:]
    w_log = -_softplus(-(w0 + _dot_hi(jnp.tanh(dw), w2))) - 0.5
    logdecay = -jnp.exp(w_log)
    a = _sigmoid(a0 + _dot_hi(da, a2))
    g = _dot_hi(_sigmoid(dg), g2)
    kk = k * kk_w
    kk = kk * lax.rsqrt(jnp.maximum(_dot_ones(kk * kk, seg), 1e-12))
    k = k * (1.0 + (a - 1.0) * ka_w)
    return r, k, v, kk, a, logdecay, g


def _rwkv_finish(o, r, k, v, g, rk_w, lnx_g, lnx_b, seg):
    inv = 1.0 / D_HEAD_DIM
    mu = _dot_ones(o, seg) * inv
    oc = o - mu
    var = _dot_ones(oc * oc, seg) * inv
    on = oc * lax.rsqrt(var + GN_EPS) * lnx_g + lnx_b
    bonus = _dot_ones(r * k * rk_w, seg) * v
    return (on + bonus) * g


def _bdot(a, b, dims=NN):
    return _dot(a.astype(BF16), b.astype(BF16), dims)


def _head_cols(x, hd):
    return x[:, hd * D_HEAD_DIM:(hd + 1) * D_HEAD_DIM]


HEAD_PAIRS = W_D // LANES


def _pair_diag(y, low):
    zero = jnp.zeros_like(y)
    return jnp.concatenate([jnp.where(low, y, zero), jnp.where(low, zero, y)], axis=0)


def _wkv_precompute(qt, rt, bt, kt, bbar, kbar, vm, nc):
    c = WKV_CHUNK
    n = D_HEAD_DIM
    probs = [(ch, pr) for ch in range(nc) for pr in range(HEAD_PAIRS)]
    cut = lambda x, p: x[p[0] * c:(p[0] + 1) * c, p[1] * LANES:(p[1] + 1) * LANES]
    qt, rt, bt, kt, bbar, kbar, vm = (x.astype(BF16) for x in (qt, rt, bt, kt, bbar, kbar, vm))
    ri = lax.broadcasted_iota(jnp.int32, (c, LANES), 0)
    li = lax.broadcasted_iota(jnp.int32, (c, LANES), 1)
    low = li < n
    strict = ri > li % n
    incl = ri >= li % n
    eye = jnp.where(ri == li % n, 1.0, 0.0)
    diag = lambda y: _pair_diag(y, low)
    halves = lambda x: jnp.where(low, x[:n], x[n:])

    qs = {p: cut(qt, p) for p in probs}
    vd = {p: diag(cut(vm, p)) for p in probs}
    aa = {p: _dot(jnp.concatenate([qs[p], cut(rt, p)], axis=0),
                  jnp.concatenate([diag(cut(bt, p)), diag(cut(kt, p))], axis=0), NT) for p in probs}
    lk = {p: jnp.where(strict, aa[p][:c, LANES:], 0.0).astype(BF16) for p in probs}
    ab = {p: jnp.where(incl, aa[p][c:, :LANES], 0.0).astype(BF16) for p in probs}
    ak = {p: jnp.where(incl, aa[p][c:, LANES:], 0.0).astype(BF16) for p in probs}
    npow = {p: jnp.where(strict, -aa[p][:c, :LANES], 0.0).astype(BF16) for p in probs}
    tinv = {p: eye + npow[p].astype(F32) for p in probs}
    lv = {p: _dot(jnp.concatenate([lk[p], ak[p]], axis=0), vd[p]) for p in probs}
    zp = {p: lv[p][:c].astype(BF16) for p in probs}
    npow = {p: _dot(npow[p], diag(npow[p])).astype(BF16) for p in probs}
    for _ in range(int(math.log2(c)) - 2):
        both = {p: _dot(jnp.concatenate([tinv[p].astype(BF16), npow[p]], axis=0), diag(npow[p])) for p in probs}
        tinv = {p: tinv[p] + both[p][:c] for p in probs}
        npow = {p: both[p][c:].astype(BF16) for p in probs}
    tinv = {p: (tinv[p] + _dot(tinv[p].astype(BF16), diag(npow[p]))).astype(BF16) for p in probs}
    tq = {p: _dot(tinv[p], jnp.concatenate([diag(qs[p]), diag(zp[p])], axis=1)) for p in probs}
    qh = {p: tq[p][:, :LANES].astype(BF16) for p in probs}
    zn = {p: (-tq[p][:, LANES:]).astype(BF16) for p in probs}
    abq = {p: _dot(ab[p], jnp.concatenate([diag(qh[p]), diag(zn[p])], axis=1)) for p in probs}
    rh = {p: (cut(rt, p).astype(F32) - abq[p][:, :LANES]).astype(BF16) for p in probs}
    pv = {p: abq[p][:, LANES:] + lv[p][c:] for p in probs}
    gt = {p: halves(-_dot(qh[p], cut(bbar, p), TN)).astype(BF16) for p in probs}
    cst = {p: halves(_dot(jnp.concatenate([cut(vm, p), zn[p]], axis=0),
                          jnp.concatenate([cut(kbar, p), cut(bbar, p)], axis=0), TN)) for p in probs}
    return rh, pv, gt, cst


def _wkv_scan_steps(s_ref, o_ref, pre, gam, nc):
    c = WKV_CHUNK
    rh, pv, gt, cst = pre
    low = lax.broadcasted_iota(jnp.int32, (D_HEAD_DIM, LANES), 1) < D_HEAD_DIM
    for ch in range(nc):
        for pr in range(HEAD_PAIRS):
            p = (ch, pr)
            s0 = s_ref[pr]
            s0b = s0.astype(BF16)
            o_ref[ch * c:(ch + 1) * c, pr * LANES:(pr + 1) * LANES] = _dot(rh[p], _pair_diag(s0b, low), NT) + pv[p]
            s_ref[pr] = (s0 * gam[ch][:, pr * LANES:(pr + 1) * LANES] + _dot(s0b, _pair_diag(gt[p], low))
                         + cst[p])
        yield


def _rwkv_prompt_kernel(pd_ref, mu_ref, w0_ref, w2_ref, a0_ref, a2_ref, g2_ref, kk_ref, ka_ref, rk_ref,
                        lg_ref, lb_ref, q8_ref, ck_ref, cv_ref, yd_ref, shift_ref, wkv_ref, o8_ref,
                        car_ref, s_ref, o_ref, *, tt, bb):
    t = pl.program_id(1)
    nt = pl.num_programs(1)

    @pl.when(t == 0)
    def _():
        car_ref[...] = jnp.zeros_like(car_ref)
        s_ref[...] = jnp.zeros_like(s_ref)

    pd = pd_ref[0]
    ext = jnp.concatenate([car_ref[...], pd], axis=0)
    prev = pltpu.roll(ext, 1, 0)[ROW_CARRY:]
    car_ref[...] = ext[tt:]
    xs = pd + (prev - pd) * mu_ref[...]
    seg = _head_ones(W_D, D_HEAD_DIM)
    r, k, v, kk, a, ld, g = _rwkv_inputs(xs, w0_ref[...], w2_ref[...], a0_ref[...], a2_ref[...],
                                         g2_ref[...], kk_ref[...], ka_ref[...], seg)
    c = WKV_CHUNK
    nc = tt // c
    tri = jnp.where(lax.broadcasted_iota(jnp.int32, (c, c), 0) >= lax.broadcasted_iota(jnp.int32, (c, c), 1),
                    1.0, 0.0).astype(BF16)
    cs_chunks = [_ones_dot(tri, ld[ch * c:(ch + 1) * c]) for ch in range(nc)]
    cs = jnp.concatenate(cs_chunks, axis=0)
    cs_end = [x[c - 1:c] for x in cs_chunks]
    cs_last = jnp.concatenate([jnp.broadcast_to(x, (c, W_D)) for x in cs_end], axis=0)
    e_neg = jnp.exp(-cs)
    e_tail = jnp.exp(cs_last - cs)
    b = kk * a
    pre = _wkv_precompute(kk * jnp.exp(cs - ld), r * jnp.exp(cs), b * e_neg, k * e_neg, b * e_tail, k * e_tail, v, nc)
    _interleave(_wkv_scan_steps(s_ref, o_ref, pre, [jnp.exp(x) for x in cs_end], nc),
                _attn_sample_steps(q8_ref, ck_ref, cv_ref, o8_ref, bb))
    yd_ref[0] = _rwkv_finish(o_ref[...], r, k, v, g, rk_ref[...], lg_ref[...], lb_ref[...], seg).astype(yd_ref.dtype)

    @pl.when(t == nt - 1)
    def _():
        shift_ref[0] = pd[tt - 1:tt]
        for hd in range(D_HEADS):
            wkv_ref[0, hd] = _head_cols(s_ref[hd // 2], hd % 2)


def _rwkv_prompt(pd, params, q, cache_k, cache_v, layer, tt):
    bn, t, _ = pd.shape
    nt = t // tt
    n = q.shape[0]
    bb = n // (bn * nt)
    assert bb * bn * nt == n
    nbytes = 2 * tt * D_PROJ * 4 + 40 * tt * W_D * 4 + 4 * bb * MEM_ROWS * LANES * 4 + 2 * bb * MEM_ROWS * LANES * 2
    req = pl.BlockSpec((bb, SUBLANES, LANES), lambda i, j: (i * nt + j, 0, 0))
    cache_spec = pl.BlockSpec((None, bb, MEM_ROWS, LANES), lambda i, j: (layer, i * nt + j, 0, 0))
    yd, shift, wkv, o8 = pl.pallas_call(
        functools.partial(_rwkv_prompt_kernel, tt=tt, bb=bb),
        out_shape=(jax.ShapeDtypeStruct((bn, t, W_D), BF16),
                   jax.ShapeDtypeStruct((bn, 1, D_PROJ), F32),
                   jax.ShapeDtypeStruct((bn, D_HEADS, D_HEAD_DIM, D_HEAD_DIM), F32),
                   jax.ShapeDtypeStruct((n, SUBLANES, LANES), F32)),
        grid=(bn, nt),
        in_specs=[pl.BlockSpec((1, tt, D_PROJ), lambda i, j: (i, j, 0))] + [_lspec(z, layer) for z in params]
                 + [req, cache_spec, cache_spec],
        out_specs=(pl.BlockSpec((1, tt, W_D), lambda i, j: (i, j, 0)),
                   pl.BlockSpec((1, 1, D_PROJ), lambda i, j: (i, 0, 0)),
                   pl.BlockSpec((1, D_HEADS, D_HEAD_DIM, D_HEAD_DIM), lambda i, j: (i, 0, 0, 0)),
                   req),
        scratch_shapes=[pltpu.VMEM((ROW_CARRY, D_PROJ), F32),
                        pltpu.VMEM((HEAD_PAIRS, D_HEAD_DIM, LANES), F32),
                        pltpu.VMEM((tt, W_D), F32)],
        compiler_params=_params(("arbitrary", "arbitrary"), nbytes),
        name="rwkv_prompt",
    )(pd, *params, _to_pair_rows(q), _cache_rows_view(cache_k), _cache_rows_view(cache_v))
    return yd, shift, wkv, _from_pair_rows(o8)


def _rwkv_sample_kernel(pd_ref, sh_ref, st_ref, mu_ref, w0_ref, w2_ref, a0_ref, a2_ref, g2_ref, kk_ref,
                        ka_ref, rk_ref, lg_ref, lb_ref, yd_ref, so_ref, rows_ref, cols_ref, ot_ref):
    h = pl.program_id(0)
    n = D_HEAD_DIM

    @pl.when(h == 0)
    def _():
        pd = pd_ref[...]
        xs = pd + (sh_ref[...] - pd) * mu_ref[...]
        seg = _head_ones(W_D, D_HEAD_DIM)
        r, k, v, kk, a, ld, g = _rwkv_inputs(xs, w0_ref[...], w2_ref[...], a0_ref[...], a2_ref[...],
                                             g2_ref[...], kk_ref[...], ka_ref[...], seg)
        for j, x in enumerate((r, k, v, g)):
            rows_ref[j] = x
        for j, x in enumerate((kk, jnp.exp(ld), kk * a, k, r, v)):
            cols_ref[j] = x.T

    base = pl.multiple_of(h * n, n)
    kap, dec, bvec, kvec, rvec = (cols_ref[j, pl.ds(base, n), :] for j in range(5))
    for vi in range(n):
        s = st_ref[0, vi]
        u = -jnp.sum(s * kap, axis=0, keepdims=True)
        s = s * dec + u * bvec + cols_ref[5, pl.ds(base + vi, 1), :] * kvec
        so_ref[0, vi] = s
        ot_ref[pl.ds(base + vi, 1), :] = jnp.sum(s * rvec, axis=0, keepdims=True)

    @pl.when(h == pl.num_programs(0) - 1)
    def _():
        seg = _head_ones(W_D, D_HEAD_DIM)
        yd_ref[...] = _rwkv_finish(ot_ref[...].T, rows_ref[0], rows_ref[1], rows_ref[2], rows_ref[3],
                                   rk_ref[...], lg_ref[...], lb_ref[...], seg)


def _rwkv_sample(pd, shift, state, params, layer):
    n = pd.shape[0]
    sblock = (1, D_HEAD_DIM, D_HEAD_DIM, n)
    return pl.pallas_call(
        _rwkv_sample_kernel,
        out_shape=(jax.ShapeDtypeStruct((n, W_D), F32), jax.ShapeDtypeStruct(state.shape[1:], F32)),
        grid=(D_HEADS,),
        in_specs=[pl.BlockSpec((n, D_PROJ), lambda i: (0, 0)), _lspec(shift, layer),
                  pl.BlockSpec((None,) + sblock, lambda i: (layer, i, 0, 0, 0))]
                 + [_lspec(z, layer) for z in params],
        out_specs=(pl.BlockSpec((n, W_D), lambda i: (0, 0)), pl.BlockSpec(sblock, lambda i: (i, 0, 0, 0))),
        scratch_shapes=[pltpu.VMEM((4, n, W_D), F32), pltpu.VMEM((6, W_D, n), F32), pltpu.VMEM((W_D, n), F32)],
        compiler_params=pltpu.CompilerParams(dimension_semantics=("arbitrary",)),
        name="rwkv_sample",
    )(pd, shift, state, *params)


def _block_diag(w):
    gn, n, _ = w.shape
    eye = jnp.eye(gn, dtype=w.dtype)
    return (eye[:, None, :, None] * w[:, :, None, :]).reshape(gn * n, gn * n)


def kernel(x_prompt, x_sample, mem_prompt, cache_mem_k, cache_mem_v, state_pool, state_conv, state_shift, state_wkv,
           w_in, mu_d, ln_v_g, ln_v_b, ws_chunk, b_chunk, w_pool, pool_scale, conv_w,
           rwkv_w0, rwkv_w2, rwkv_a0, rwkv_a2, rwkv_g2, rwkv_k_k, rwkv_k_a, rwkv_r_k, rwkv_lnx_g, rwkv_lnx_b,
           w_out, ln1_g, ln1_b, w_xq, w_xk, w_xv, w_xo, ln2_g, ln2_b, ffn_w1, ffn_w3, ffn_w2, ln3_g, ln3_b):
    bp, t_p, d = x_prompt.shape
    ns, t_s, _ = x_sample.shape
    depth = w_in.shape[0]
    assert d == D_MODEL and t_s == 1 and t_p % CHUNK == 0 and w_in.shape[2] == PROJ
    alpha = (2 * depth) ** 0.25
    mp = bp * t_p
    nw = W_A + W_B + W_C
    row = lambda z: z.reshape(depth, 1, -1)

    w_in_b, w_out_b = w_in, w_out
    w_xq_b, w_xk_b, w_xv_b, w_xo_b = w_xq, w_xk, w_xv, w_xo
    w1_b, w3_b, w2_b = ffn_w1, ffn_w3, ffn_w2
    ws_flat = ws_chunk.reshape(depth, A_HEADS * CHUNK, CHUNK)
    bias_full = jnp.repeat(jnp.swapaxes(b_chunk, 1, 2), A_HEAD_DIM, axis=2)
    ws0 = jnp.repeat(ws_chunk[:, :, 0, 0], A_HEAD_DIM, axis=1).reshape(depth, 1, W_A)
    wpool_bd = jnp.stack([_block_diag(w_pool[l]) for l in range(depth)]).astype(BF16)
    abc_w = (row(ln_v_g), row(ln_v_b))
    abc_w2 = (wpool_bd, row(pool_scale), conv_w)
    rw = (row(mu_d), row(rwkv_w0), rwkv_w2, row(rwkv_a0), rwkv_a2, rwkv_g2,
          row(rwkv_k_k), row(rwkv_k_a), row(rwkv_r_k), row(rwkv_lnx_g), row(rwkv_lnx_b))
    ln1, ln2, ln3 = (row(ln1_g), row(ln1_b)), (row(ln2_g), row(ln2_b)), (row(ln3_g), row(ln3_b))
    pool_view = jnp.swapaxes(state_pool, 1, 2)
    wkv_view = state_wkv.transpose(0, 2, 3, 4, 1)
    conv_view = state_conv.reshape(depth, ns, (CONV_W - 1) * W_C)
    shift_view = state_shift.reshape(depth, ns, D_PROJ)

    hp = x_prompt
    hs = x_sample.reshape(ns, d)
    mem = mem_prompt.reshape(bp * MEM_LEN, d)
    mk_all, mv_all, mk_rows, mv_rows = _mem_kv(mem, w_xk_b, w_xv_b, 512)
    mk_all = mk_all.reshape(depth, bp, MEM_LEN, d)
    mv_all = mv_all.reshape(depth, bp, MEM_LEN, d)
    outs = [[] for _ in range(10)]
    for l in range(depth):
        pd, mix, v_last, pool_p, conv_p, *ffn_b = _proj_abc_prompt(hp, w_in_b, *abc_w, ws_flat, bias_full, *abc_w2,
                                                                   (w1_b, w3_b, w2_b), l, tt=1024)

        pabc_s, pd_s = _proj(hs, w_in_b, l, ns)
        mix_s, v_s, pool_s, conv_s = _abc_sample(pabc_s, pool_view, conv_view, *abc_w, ws0, bias_full, *abc_w2,
                                                 l, pos0=PAST_LEN)
        yd_s, wkv_s = _rwkv_sample(pd_s, shift_view, wkv_view, rw, l)
        hs = _mm_res_ln([mix_s, yd_s], w_out_b, l, hs, *ln1, tm=ns, alpha=alpha, name="out_proj_s")
        q_s = _mm(hs, w_xq_b, l, ns, "q_s")

        yd, shift_p, wkv_p, o_s = _rwkv_prompt(pd, rw, q_s, cache_mem_k, cache_mem_v, l, tt=512)
        hp = _attn_prompt(mix, yd, hp, mk_all, mv_all, w_out_b, *ln1, w_xq_b, w_xo_b, *ln2, l, tq=512, alpha=alpha)
        hp = _ffn(hp.reshape(mp, d), *ffn_b, *ln3, l, tm=1024, tf=256, alpha=alpha, name="ffn").reshape(bp, t_p, d)
        for lst, val in zip(outs[:5], (v_last, pool_p, conv_p, shift_p, wkv_p)):
            lst.append(val)

        hs = _mm_res_ln([o_s], w_xo_b, l, hs, *ln2, tm=ns, alpha=alpha, name="xo_s")
        hs = _ffn(hs, *ffn_b, *ln3, l, tm=ns, tf=256, alpha=alpha, name="ffn_s")
        for lst, val in zip(outs[5:], (v_s.reshape(ns, 1, W_A), pool_s,
                                       conv_s.reshape(ns, CONV_W - 1, W_C), pd_s.reshape(ns, 1, D_PROJ), wkv_s)):
            lst.append(val)

    stacked = [jnp.stack(o) for o in outs]
    stacked[6] = jnp.swapaxes(stacked[6], 1, 2)
    stacked[9] = stacked[9].transpose(0, 4, 1, 2, 3)
    return ((hp, hs.reshape(ns, 1, d)) + tuple(stacked[:5]) + (_cache_from_rows(mk_rows, bp), _cache_from_rows(mv_rows, bp))
            + tuple(stacked[5:]))
```

```python
import functools
import math

import jax
import jax.numpy as jnp
from jax import lax
from jax.experimental import pallas as pl
from jax.experimental.pallas import tpu as pltpu

F32 = jnp.float32
BF16 = jnp.bfloat16

D_MODEL = 1024
W_A = 256
W_B = 256
W_C = 256
W_D = 256
A_HEADS = 4
A_HEAD_DIM = W_A // A_HEADS
CHUNK = 128
POOL_WINDOWS = (2, 4, 8, 16)
POOL_GROUP = W_B // len(POOL_WINDOWS)
POOL_BUF = max(POOL_WINDOWS) - 1
CONV_W = 3
D_HEAD_DIM = 64
D_HEADS = W_D // D_HEAD_DIM
R_DECAY = 32
R_AAA = 32
R_GATE = 64
D_PROJ = 3 * W_D + R_DECAY + R_AAA + R_GATE
N_ABC = 2 * W_A + W_B + 3 * W_C
PROJ = N_ABC + D_PROJ
MEM_LEN = 256
X_HEADS = 4
X_HEAD_DIM = D_MODEL // X_HEADS
D_FF = int(math.ceil(8 * D_MODEL / 3 / 256)) * 256
PAST_LEN = 16384
LN_EPS = 1e-5
GN_EPS = 64e-5

WKV_CHUNK = 64
POOL_CARRY = 24
ROW_CARRY = 8
V7X_VMEM_BYTES = 64 * 1024 * 1024
VMEM_CAP = V7X_VMEM_BYTES - 8 * 1024 * 1024

NN = (((1,), (0,)), ((), ()))
NT = (((1,), (1,)), ((), ()))
TN = (((0,), (0,)), ((), ()))


def _vmem_limit(nbytes):
    return int(min(VMEM_CAP, max(32 * 1024 * 1024, 2 * nbytes)))


def _params(sem, nbytes):
    return pltpu.CompilerParams(dimension_semantics=sem, vmem_limit_bytes=_vmem_limit(nbytes))


def _lspec(arr, layer):
    tail = arr.shape[1:]
    zeros = (0,) * len(tail)
    return pl.BlockSpec((None,) + tail, lambda *_: (layer,) + zeros)


def _dot(a, b, dims=NN):
    return lax.dot_general(a, b, dims, preferred_element_type=F32)


def _split2(a):
    hi = a.astype(BF16)
    lo = (a - hi.astype(F32)).astype(BF16)
    return hi, lo


def _dot_hi(a, b, dims=NN):
    ah, al = _split2(a)
    bh, bl = _split2(b)
    return _dot(ah, bh, dims) + _dot(ah, bl, dims) + _dot(al, bh, dims)


def _dot_ones(x, ones_bf16, dims=NN):
    hi, lo = _split2(x)
    return _dot(hi, ones_bf16, dims) + _dot(lo, ones_bf16, dims)


def _ones_dot(ones_bf16, x):
    hi, lo = _split2(x)
    return _dot(ones_bf16, hi) + _dot(ones_bf16, lo)


def _ln(x, g, b, eps=LN_EPS):
    mu = jnp.mean(x, axis=-1, keepdims=True)
    xc = x - mu
    var = jnp.mean(xc * xc, axis=-1, keepdims=True)
    return xc * lax.rsqrt(var + eps) * g + b


def _gelu(x):
    c = math.sqrt(2.0 / math.pi)
    return x * (0.5 * (1.0 + jnp.tanh(c * (x + 0.044715 * (x * x * x)))))


def _sigmoid(x):
    return 1.0 / (1.0 + jnp.exp(-x))


def _softplus(x):
    return jnp.maximum(x, 0.0) + jnp.log(1.0 + jnp.exp(-jnp.abs(x)))


def _interleave(*staged):
    live = list(staged)
    while live:
        for steps in list(live):
            if next(steps, StopIteration) is StopIteration:
                live.remove(steps)


def _head_ones(n, group):
    r = lax.broadcasted_iota(jnp.int32, (n, n), 0) // group
    c = lax.broadcasted_iota(jnp.int32, (n, n), 1) // group
    return jnp.where(r == c, 1.0, 0.0).astype(BF16)


def _proj_kernel(x_ref, w_ref, oabc_ref, od_ref):
    y = _dot(x_ref[...].astype(BF16), w_ref[...].astype(BF16))
    oabc_ref[...] = y[:, :N_ABC]
    od_ref[...] = y[:, N_ABC:]


def _proj(x, w, layer, tm):
    m, k = x.shape
    nbytes = 2 * (tm * k * 4 + k * PROJ * 4 + tm * PROJ * 4) + tm * PROJ * 4 + k * PROJ * 2
    return pl.pallas_call(
        _proj_kernel,
        out_shape=(jax.ShapeDtypeStruct((m, N_ABC), F32), jax.ShapeDtypeStruct((m, D_PROJ), F32)),
        grid=(m // tm,),
        in_specs=[pl.BlockSpec((tm, k), lambda i: (i, 0)), _lspec(w, layer)],
        out_specs=(pl.BlockSpec((tm, N_ABC), lambda i: (i, 0)), pl.BlockSpec((tm, D_PROJ), lambda i: (i, 0))),
        compiler_params=_params(("parallel",), nbytes),
        name="proj",
    )(x, w)


def _mm_kernel(x_ref, w_ref, o_ref):
    o_ref[...] = _dot(x_ref[...].astype(BF16), w_ref[...].astype(BF16))


def _mm(x, w, layer, tm, name):
    m, k = x.shape
    n = w.shape[2]
    nbytes = 2 * (tm * k * 4 + k * n * 4 + tm * n * 4) + tm * n * 4 + k * n * 2
    return pl.pallas_call(
        _mm_kernel,
        out_shape=jax.ShapeDtypeStruct((m, n), F32),
        grid=(m // tm,),
        in_specs=[pl.BlockSpec((tm, k), lambda i: (i, 0)), _lspec(w, layer)],
        out_specs=pl.BlockSpec((tm, n), lambda i: (i, 0)),
        compiler_params=_params(("parallel",), nbytes),
        name=name,
    )(x, w)


def _mem_kv_kernel(x_ref, wk_ref, wv_ref, k_ref, v_ref, kt_ref, vt_ref, wkb_ref, wvb_ref):
    @pl.when(pl.program_id(1) == 0)
    def _():
        wkb_ref[...] = wk_ref[...].astype(BF16)
        wvb_ref[...] = wv_ref[...].astype(BF16)

    xb = x_ref[...].astype(BF16)
    for w_ref, o_ref, t_ref in ((wkb_ref, k_ref, kt_ref), (wvb_ref, v_ref, vt_ref)):
        y = _dot(xb, w_ref[...])
        o_ref[...] = y.astype(o_ref.dtype)
        rows = y.shape[0]
        t_ref[...] = jnp.swapaxes(y.reshape(rows, X_HEADS, LANE_TILES, LANES), 1, 2).reshape(rows, SUBLANES, LANES)


def _mem_kv(mem, wk, wv, tm):
    m, k = mem.shape
    nl, _, n = wk.shape
    nbytes = 2 * (tm * k * 4 + 2 * k * n * 4 + 4 * tm * n * 4) + 2 * tm * n * 4 + 2 * k * n * 2
    wspec = pl.BlockSpec((None, k, n), lambda l, i: (l, 0, 0))
    ospec = pl.BlockSpec((None, tm, n), lambda l, i: (l, i, 0))
    tspec = pl.BlockSpec((None, tm, SUBLANES, LANES), lambda l, i: (l, i, 0, 0))
    shape = jax.ShapeDtypeStruct((nl, m, n), BF16)
    tshape = jax.ShapeDtypeStruct((nl, m, SUBLANES, LANES), F32)
    return pl.pallas_call(
        _mem_kv_kernel,
        out_shape=(shape, shape, tshape, tshape),
        grid=(nl, m // tm),
        in_specs=[pl.BlockSpec((tm, k), lambda l, i: (i, 0)), wspec, wspec],
        out_specs=(ospec, ospec, tspec, tspec),
        scratch_shapes=[pltpu.VMEM((k, n), BF16), pltpu.VMEM((k, n), BF16)],
        compiler_params=_params(("arbitrary", "arbitrary"), nbytes),
        name="mem_kv",
    )(mem, wk, wv)


def _mm_res_ln_kernel(*refs, n_in, alpha):
    xs = refs[:n_in]
    ws = refs[n_in:2 * n_in]
    h_ref, g_ref, b_ref, o_ref = refs[2 * n_in:]
    y = _dot(xs[0][...].astype(BF16), ws[0][...].astype(BF16))
    for x_ref, w_ref in zip(xs[1:], ws[1:]):
        y = y + _dot(x_ref[...].astype(BF16), w_ref[...].astype(BF16))
    o_ref[...] = _ln(alpha * h_ref[...] + y, g_ref[...], b_ref[...])


def _mm_res_ln(xs, w, layer, h, g, b, tm, alpha, name):
    m, n = h.shape
    nbytes = 2 * sum(tm * x.shape[1] * 4 + x.shape[1] * n * 4 for x in xs) + 5 * tm * n * 4 + w.shape[1] * n * 2
    in_specs = [pl.BlockSpec((tm, x.shape[1]), lambda i: (i, 0)) for x in xs]
    start = 0
    for x in xs:
        width = x.shape[1]
        assert start % width == 0
        in_specs.append(pl.BlockSpec((None, width, n), lambda i, blk=start // width: (layer, blk, 0)))
        start += width
    assert start == w.shape[1]
    in_specs += [pl.BlockSpec((tm, n), lambda i: (i, 0)), _lspec(g, layer), _lspec(b, layer)]
    return pl.pallas_call(
        functools.partial(_mm_res_ln_kernel, n_in=len(xs), alpha=alpha),
        out_shape=jax.ShapeDtypeStruct((m, n), F32),
        grid=(m // tm,),
        in_specs=in_specs,
        out_specs=pl.BlockSpec((tm, n), lambda i: (i, 0)),
        compiler_params=_params(("parallel",), nbytes),
        name=name,
    )(*xs, *([w] * len(xs)), h, g, b)


def _ffn_kernel(x_ref, w1_ref, w3_ref, w2_ref, g_ref, b_ref, o_ref, xb_ref, acc_ref, *, alpha):
    j = pl.program_id(1)

    @pl.when(j == 0)
    def _():
        xb_ref[...] = x_ref[...].astype(BF16)
        acc_ref[...] = jnp.zeros_like(acc_ref)

    xb = xb_ref[...]
    h1 = _dot(xb, w1_ref[...])
    h3 = _dot(xb, w3_ref[...])
    a = (h1 * _sigmoid(h1) * h3).astype(BF16)
    acc_ref[...] += _dot(a, w2_ref[...])

    @pl.when(j == pl.num_programs(1) - 1)
    def _():
        o_ref[...] = _ln(alpha * x_ref[...] + acc_ref[...], g_ref[...], b_ref[...])


def _ffn(x, w1, w3, w2, g, b, layer, tm, tf, alpha, name):
    m, d = x.shape
    nbytes = 4 * tm * d * 4 + tm * d * 2 + tm * d * 4 + 2 * 3 * d * tf * 2 + 3 * tm * tf * 4
    return pl.pallas_call(
        functools.partial(_ffn_kernel, alpha=alpha),
        out_shape=jax.ShapeDtypeStruct((m, d), F32),
        grid=(m // tm, D_FF // tf),
        in_specs=[pl.BlockSpec((tm, d), lambda i, j: (i, 0)),
                  pl.BlockSpec((None, d, tf), lambda i, j: (0, 0, j)),
                  pl.BlockSpec((None, d, tf), lambda i, j: (0, 0, j)),
                  pl.BlockSpec((None, tf, d), lambda i, j: (0, j, 0)),
                  _lspec(g, layer), _lspec(b, layer)],
        out_specs=pl.BlockSpec((tm, d), lambda i, j: (i, 0)),
        scratch_shapes=[pltpu.VMEM((tm, d), BF16), pltpu.VMEM((tm, d), F32)],
        compiler_params=_params(("parallel", "arbitrary"), nbytes),
        name=name,
    )(x, w1, w3, w2, g, b)


def _softmax_rows(s):
    m = jnp.max(s, axis=-1, keepdims=True)
    e = jnp.exp(s - m)
    return e * (1.0 / jnp.sum(e, axis=-1, keepdims=True))


ATTN_PIECES = 2


def _attn_prompt_kernel(mix_ref, yd_ref, h_ref, mk_ref, mv_ref, wma_ref, wmd_ref, g1_ref, b1_ref,
                        wq_ref, wo_ref, g_ref, b_ref, o_ref, ob_ref, wmb_ref, wqb_ref, wob_ref, *, alpha):
    nw = wma_ref.shape[0]

    @pl.when((pl.program_id(0) == 0) & (pl.program_id(1) == 0))
    def _():
        wmb_ref[:nw, :] = wma_ref[...].astype(BF16)
        wmb_ref[nw:, :] = wmd_ref[...].astype(BF16)
        wqb_ref[...] = wq_ref[...].astype(BF16)
        wob_ref[...] = wo_ref[...].astype(BF16)

    rows = h_ref.shape[1] // ATTN_PIECES
    scale = X_HEAD_DIM ** -0.5
    sls = [slice(hd * X_HEAD_DIM, (hd + 1) * X_HEAD_DIM) for hd in range(X_HEADS)]
    kb = [mk_ref[0, :, sl] for sl in sls]
    vb = [mv_ref[0, :, sl] for sl in sls]
    y, h, q, sc = {}, {}, {}, {}

    def project(p, rs):
        y[p] = (_dot(mix_ref[0, rs, :].astype(BF16), wmb_ref[:nw, :])
                + _dot(yd_ref[0, rs, :].astype(BF16), wmb_ref[nw:, :]))

    def query(p, rs):
        h[p] = _ln(alpha * h_ref[0, rs, :] + y[p], g1_ref[...], b1_ref[...])
        q[p] = _dot(h[p].astype(BF16), wqb_ref[...]).astype(BF16)

    def scores(p, rs):
        sc[p] = [_dot(q[p][:, sl], k, NT) * scale for sl, k in zip(sls, kb)]

    def values(p, rs):
        for sl, s, v in zip(sls, sc[p], vb):
            ob_ref[rs, sl] = _dot(_softmax_rows(s).astype(BF16), v).astype(BF16)

    def output(p, rs):
        o_ref[0, rs, :] = _ln(alpha * h[p] + _dot(ob_ref[rs, :], wob_ref[...]), g_ref[...], b_ref[...])

    stages = (project, query, scores, values, output)
    for step in range(len(stages) + ATTN_PIECES - 1):
        for p in range(ATTN_PIECES):
            if 0 <= step - p < len(stages):
                stages[step - p](p, slice(p * rows, (p + 1) * rows))


def _attn_prompt(mix, yd, h, mk, mv, w_mix, g1, b1, wq, wo, g, b, layer, tq, alpha):
    bn, t, d = h.shape
    nw = mix.shape[2]
    assert nw % W_D == 0 and w_mix.shape[1] == nw + W_D
    nbytes = (6 * tq * d * 4 + 4 * MEM_LEN * d * 4 + 6 * d * d * 4 + 3 * d * d * 2 + tq * d * 2
              + 3 * tq * d * 4 + 3 * tq * MEM_LEN * 4)
    tile = lambda n: pl.BlockSpec((1, tq, n), lambda i, j: (i, j, 0))
    return pl.pallas_call(
        functools.partial(_attn_prompt_kernel, alpha=alpha),
        out_shape=jax.ShapeDtypeStruct((bn, t, d), F32),
        grid=(bn, t // tq),
        in_specs=[tile(nw), tile(W_D), tile(d),
                  pl.BlockSpec((None, 1, MEM_LEN, d), lambda i, j: (layer, i, 0, 0)),
                  pl.BlockSpec((None, 1, MEM_LEN, d), lambda i, j: (layer, i, 0, 0)),
                  pl.BlockSpec((None, nw, d), lambda i, j: (layer, 0, 0)),
                  pl.BlockSpec((None, W_D, d), lambda i, j: (layer, nw // W_D, 0)),
                  _lspec(g1, layer), _lspec(b1, layer),
                  _lspec(wq, layer), _lspec(wo, layer), _lspec(g, layer), _lspec(b, layer)],
        out_specs=tile(d),
        scratch_shapes=[pltpu.VMEM((tq, d), BF16), pltpu.VMEM((nw + W_D, d), BF16),
                        pltpu.VMEM((d, d), BF16), pltpu.VMEM((d, d), BF16)],
        compiler_params=_params(("arbitrary", "arbitrary"), nbytes),
        name="attn_prompt",
    )(mix, yd, h, mk, mv, w_mix, w_mix, g1, b1, wq, wo, g, b)


LANES = 128
SUBLANES = 8
LANE_TILES = X_HEAD_DIM // LANES
MEM_ROWS = MEM_LEN * LANE_TILES * X_HEADS


def _cache_rows_view(cache):
    nl, n = cache.shape[:2]
    x = cache.reshape(nl, n, MEM_LEN, X_HEADS, LANE_TILES, LANES)
    return x.transpose(0, 1, 2, 4, 3, 5).reshape(nl, n, MEM_ROWS, LANES)


def _cache_from_rows(rows, n):
    nl = rows.shape[0]
    x = rows.reshape(nl, n, MEM_LEN, LANE_TILES, X_HEADS, LANES)
    return x.transpose(0, 1, 2, 4, 3, 5).reshape(nl, n, MEM_LEN, X_HEADS, X_HEAD_DIM)


def _attn_sample_steps(q_ref, k_ref, v_ref, o_ref, bb):
    scale = X_HEAD_DIM ** -0.5
    shape = (SUBLANES, MEM_ROWS)
    rowi = lax.broadcasted_iota(jnp.int32, shape, 0)
    coli = lax.broadcasted_iota(jnp.int32, shape, 1)
    valid = (coli % SUBLANES) == rowi
    raw = [_dot(q_ref[i].astype(BF16), k_ref[i].astype(BF16), NT) for i in range(bb)]
    yield
    probs = []
    for r in raw:
        r = jnp.where(valid, r, 0.0)
        other = pltpu.roll(r, X_HEADS, 0)
        other = jnp.where(rowi < X_HEADS, pltpu.roll(other, MEM_ROWS - X_HEADS, 1), pltpu.roll(other, X_HEADS, 1))
        sc = jnp.where(valid, (r + other) * scale, -jnp.inf)
        probs.append(_softmax_rows(sc).astype(BF16))
        yield
    for i in range(bb):
        o_ref[i] = _dot(probs[i], v_ref[i].astype(BF16))
        yield


def _to_pair_rows(q):
    n = q.shape[0]
    return q.reshape(n, X_HEADS, LANE_TILES, LANES).transpose(0, 2, 1, 3).reshape(n, SUBLANES, LANES)


def _from_pair_rows(o8):
    n = o8.shape[0]
    return o8.reshape(n, LANE_TILES, X_HEADS, LANES).transpose(0, 2, 1, 3).reshape(n, D_MODEL)


def _pool_window_sums(ext, tt):
    s2 = ext + pltpu.roll(ext, 1, 0)
    s4 = s2 + pltpu.roll(s2, 2, 0)
    s8 = s4 + pltpu.roll(s4, 4, 0)
    s16 = s8 + pltpu.roll(s8, 8, 0)
    grp = lax.broadcasted_iota(jnp.int32, (tt, W_B), 1) // POOL_GROUP
    lo = ext.shape[0] - tt
    return jnp.where(grp == 0, s2[lo:], jnp.where(grp == 1, s4[lo:], jnp.where(grp == 2, s8[lo:], s16[lo:])))


def _proj_abc_kernel(x_ref, w_ref, lng_ref, lnb_ref, ws_ref, bias_ref, wpool_ref, pscale_ref, cw_ref,
                     f1_ref, f3_ref, f2_ref, pd_ref, mix_ref, vlast_ref, pool_ref, conv_ref,
                     f1b_ref, f3b_ref, f2b_ref, pcar_ref, ccar_ref, wb_ref, *, tt):
    t = pl.program_id(1)
    nt = pl.num_programs(1)

    @pl.when((pl.program_id(0) == 0) & (t == 0))
    def _():
        wb_ref[...] = w_ref[...].astype(BF16)

    @pl.when(t == 0)
    def _():
        pcar_ref[...] = jnp.zeros_like(pcar_ref)
        ccar_ref[...] = jnp.zeros_like(ccar_ref)

    rows = lax.broadcasted_iota(jnp.int32, (A_HEADS * CHUNK, CHUNK), 0) % CHUNK
    cols = lax.broadcasted_iota(jnp.int32, (A_HEADS * CHUNK, CHUNK), 1)
    wsm = jnp.where(rows >= cols, ws_ref[...], 0.0).astype(BF16)
    hid = lax.broadcasted_iota(jnp.int32, (CHUNK, W_A), 1) // A_HEAD_DIM
    win = jnp.left_shift(2, lax.broadcasted_iota(jnp.int32, (CHUNK, W_B), 1) // POOL_GROUP)
    rowi = lax.broadcasted_iota(jnp.int32, (CHUNK, W_B), 0)
    w = wb_ref[...]
    pcar = pcar_ref[...]
    ccar = ccar_ref[...]
    v = None
    for c in range(tt // CHUNK):
        rs = slice(c * CHUNK, (c + 1) * CHUNK)
        y = _dot(x_ref[0, rs, :].astype(BF16), w)
        pd_ref[0, rs, :] = y[:, N_ABC:]

        ga = _gelu(y[:, :2 * W_A])
        u = ga[:, :W_A]
        v = _ln(ga[:, W_A:], lng_ref[...], lnb_ref[...])
        zz = _dot(wsm, v.astype(BF16))
        z = zz[(A_HEADS - 1) * CHUNK:]
        for hd in range(A_HEADS - 2, -1, -1):
            z = jnp.where(hid == hd, zz[hd * CHUNK:(hd + 1) * CHUNK], z)
        mix_ref[0, rs, 0:W_A] = (u * (z + bias_ref[...])).astype(mix_ref.dtype)

        xb = y[:, 2 * W_A:2 * W_A + W_B]
        ext = jnp.concatenate([pcar, xb], axis=0)
        sums = _pool_window_sums(ext, CHUNK)
        pos = t * tt + c * CHUNK + rowi
        cnt = jnp.minimum(win, pos + 1).astype(F32)
        d = sums / cnt - xb
        mix_ref[0, rs, W_A:W_A + W_B] = (_dot(d.astype(BF16), wpool_ref[...]) * pscale_ref[...]).astype(mix_ref.dtype)
        pcar = ext[CHUNK:]

        o = 2 * W_A + W_B
        bg = y[:, o:o + W_C]
        zc = y[:, o + W_C:o + 2 * W_C] * y[:, o + 2 * W_C:o + 3 * W_C]
        extz = jnp.concatenate([ccar, zc], axis=0)
        conv = (cw_ref[0:1, :] * pltpu.roll(extz, 2, 0) + cw_ref[1:2, :] * pltpu.roll(extz, 1, 0)
                + cw_ref[2:3, :] * extz)
        mix_ref[0, rs, W_A + W_B:] = (bg * conv[ROW_CARRY:]).astype(mix_ref.dtype)
        ccar = extz[CHUNK:]
    pcar_ref[...] = pcar
    ccar_ref[...] = ccar
    for src_ref, dst_ref in ((f1_ref, f1b_ref), (f3_ref, f3b_ref), (f2_ref, f2b_ref)):
        dst_ref[...] = src_ref[...].astype(BF16)

    @pl.when(t == nt - 1)
    def _():
        vlast_ref[0] = v
        pool_ref[0] = pcar_ref[POOL_CARRY - POOL_BUF:, :]
        conv_ref[0] = ccar_ref[ROW_CARRY - (CONV_W - 1):, :]


def _proj_abc_prompt(h, w, lng, lnb, ws, bias, wpool, pscale, cw, ffn_ws, layer, tt):
    bn, t, d = h.shape
    nw = W_A + W_B + W_C
    nt = t // tt
    steps = bn * nt
    assert all(z.shape[1] % (steps * 2 * SUBLANES) == 0 for z in ffn_ws)
    f_in = [pl.BlockSpec((None, z.shape[1] // steps, z.shape[2]), lambda i, j: (layer, i * nt + j, 0)) for z in ffn_ws]
    f_out = [pl.BlockSpec((None, z.shape[1] // steps, z.shape[2]), lambda i, j: (0, i * nt + j, 0)) for z in ffn_ws]
    f_shapes = [jax.ShapeDtypeStruct((1,) + z.shape[1:], BF16) for z in ffn_ws]
    nbytes = (2 * tt * d * 4 + 2 * d * PROJ * 4 + d * PROJ * 2 + 2 * tt * (D_PROJ + nw) * 4 + 3 * CHUNK * PROJ * 4
              + 4 * A_HEADS * CHUNK * CHUNK * 4 + 12 * sum(z.shape[1] * z.shape[2] for z in ffn_ws) // steps)
    tile = lambda n: pl.BlockSpec((1, tt, n), lambda i, j: (i, j, 0))
    last = lambda r, n: pl.BlockSpec((1, r, n), lambda i, j: (i, 0, 0))
    return pl.pallas_call(
        functools.partial(_proj_abc_kernel, tt=tt),
        out_shape=(jax.ShapeDtypeStruct((bn, t, D_PROJ), F32),
                   jax.ShapeDtypeStruct((bn, t, nw), BF16),
                   jax.ShapeDtypeStruct((bn, CHUNK, W_A), F32),
                   jax.ShapeDtypeStruct((bn, POOL_BUF, W_B), F32),
                   jax.ShapeDtypeStruct((bn, CONV_W - 1, W_C), F32), *f_shapes),
        grid=(bn, nt),
        in_specs=[tile(d), *[_lspec(z, layer) for z in (w, lng, lnb, ws, bias, wpool, pscale, cw)], *f_in],
        out_specs=(tile(D_PROJ), tile(nw), last(CHUNK, W_A), last(POOL_BUF, W_B), last(CONV_W - 1, W_C), *f_out),
        scratch_shapes=[pltpu.VMEM((POOL_CARRY, W_B), F32), pltpu.VMEM((ROW_CARRY, W_C), F32),
                        pltpu.VMEM((d, PROJ), BF16)],
        compiler_params=_params(("arbitrary", "arbitrary"), nbytes),
        name="proj_abc_prompt",
    )(h, w, lng, lnb, ws, bias, wpool, pscale, cw, *ffn_ws)


def _abc_sample_kernel(x_ref, pool_ref, conv_ref, lng_ref, lnb_ref, ws0_ref, bias0_ref, wpool_ref,
                       pscale_ref, cw_ref, mix_ref, v_ref, poolo_ref, convo_ref, *, pos0):
    x = x_ref[...]
    n = x.shape[0]
    ga = _gelu(x[:, :2 * W_A])
    u = ga[:, :W_A]
    v = _ln(ga[:, W_A:], lng_ref[...], lnb_ref[...])
    v_ref[...] = v
    mix_ref[:, 0:W_A] = u * (v * ws0_ref[...] + bias0_ref[0:1, :])

    xb = x[:, 2 * W_A:2 * W_A + W_B]
    grp = lax.broadcasted_iota(jnp.int32, (n, W_B), 1) // POOL_GROUP
    run = xb
    sums = jnp.zeros_like(xb)
    for back in range(1, max(POOL_WINDOWS)):
        run = run + pool_ref[POOL_BUF - back]
        for gi, win in enumerate(POOL_WINDOWS):
            if back == win - 1:
                sums = jnp.where(grp == gi, run, sums)
    win = jnp.left_shift(2, grp)
    cnt = jnp.minimum(win, pos0 + 1).astype(F32)
    d = sums / cnt - xb
    mix_ref[:, W_A:W_A + W_B] = _dot(d.astype(BF16), wpool_ref[...]) * pscale_ref[...]
    for r in range(POOL_BUF - 1):
        poolo_ref[r] = pool_ref[r + 1]
    poolo_ref[POOL_BUF - 1] = xb

    o = 2 * W_A + W_B
    bg = x[:, o:o + W_C]
    zc = x[:, o + W_C:o + 2 * W_C] * x[:, o + 2 * W_C:o + 3 * W_C]
    z0 = conv_ref[:, :W_C]
    z1 = conv_ref[:, W_C:]
    y = cw_ref[0:1, :] * z0 + cw_ref[1:2, :] * z1 + cw_ref[2:3, :] * zc
    mix_ref[:, W_A + W_B:] = bg * y
    convo_ref[:, :W_C] = z1
    convo_ref[:, W_C:] = zc


def _abc_sample(pabc, pool, conv, lng, lnb, ws0, bias, wpool, pscale, cw, layer, pos0):
    n = pabc.shape[0]
    nw = W_A + W_B + W_C
    whole = lambda shape: pl.BlockSpec(shape, lambda i: (0,) * len(shape))
    return pl.pallas_call(
        functools.partial(_abc_sample_kernel, pos0=pos0),
        out_shape=(jax.ShapeDtypeStruct((n, nw), F32),
                   jax.ShapeDtypeStruct((n, W_A), F32),
                   jax.ShapeDtypeStruct(pool.shape[1:], F32),
                   jax.ShapeDtypeStruct(conv.shape[1:], F32)),
        grid=(1,),
        in_specs=[whole(pabc.shape)] + [_lspec(z, layer) for z in (pool, conv, lng, lnb, ws0, bias, wpool, pscale, cw)],
        out_specs=(whole((n, nw)), whole((n, W_A)), whole(pool.shape[1:]), whole(conv.shape[1:])),
        name="mixer_abc_sample",
    )(pabc, pool, conv, lng, lnb, ws0, bias, wpool, pscale, cw)


def _rwkv_inputs(xs, w0, w2, a0, a2, g2, kk_w, ka_w, seg):
    r = xs[:, 0:W_D]
    k = xs[:, W_D:2 * W_D]
    v = xs[:, 2 * W_D:3 * W_D]
    o = 3 * W_D
    dw = xs[:, o:o + R_DECAY]
    da = xs[:, o + R_DECAY:o + R_DECAY + R_AAA]
    dg = xs[:, o + R_DECAY + R_AAA:]
    w_log = -_softplus(-(w0 + _dot_hi(jnp.tanh(dw), w2))) - 0.5
    logdecay = -jnp.exp(w_log)
    a = _sigmoid(a0 + _dot_hi(da, a2))
    g = _dot_hi(_sigmoid(dg), g2)
    kk = k * kk_w
    kk = kk * lax.rsqrt(jnp.maximum(_dot_ones(kk * kk, seg), 1e-12))
    k = k * (1.0 + (a - 1.0) * ka_w)
    return r, k, v, kk, a, logdecay, g


def _rwkv_finish(o, r, k, v, g, rk_w, lnx_g, lnx_b, seg):
    inv = 1.0 / D_HEAD_DIM
    mu = _dot_ones(o, seg) * inv
    oc = o - mu
    var = _dot_ones(oc * oc, seg) * inv
    on = oc * lax.rsqrt(var + GN_EPS) * lnx_g + lnx_b
    bonus = _dot_ones(r * k * rk_w, seg) * v
    return (on + bonus) * g


def _bdot(a, b, dims=NN):
    return _dot(a.astype(BF16), b.astype(BF16), dims)


def _head_cols(x, hd):
    return x[:, hd * D_HEAD_DIM:(hd + 1) * D_HEAD_DIM]


HEAD_PAIRS = W_D // LANES


def _pair_diag(y, low):
    zero = jnp.zeros_like(y)
    return jnp.concatenate([jnp.where(low, y, zero), jnp.where(low, zero, y)], axis=0)


def _wkv_precompute(qt, rt, bt, kt, bbar, kbar, vm, nc):
    c = WKV_CHUNK
    n = D_HEAD_DIM
    probs = [(ch, pr) for ch in range(nc) for pr in range(HEAD_PAIRS)]
    cut = lambda x, p: x[p[0] * c:(p[0] + 1) * c, p[1] * LANES:(p[1] + 1) * LANES]
    qt, rt, bt, kt, bbar, kbar, vm = (x.astype(BF16) for x in (qt, rt, bt, kt, bbar, kbar, vm))
    ri = lax.broadcasted_iota(jnp.int32, (c, LANES), 0)
    li = lax.broadcasted_iota(jnp.int32, (c, LANES), 1)
    low = li < n
    strict = ri > li % n
    incl = ri >= li % n
    eye = jnp.where(ri == li % n, 1.0, 0.0)
    diag = lambda y: _pair_diag(y, low)
    halves = lambda x: jnp.where(low, x[:n], x[n:])

    qs = {p: cut(qt, p) for p in probs}
    vd = {p: diag(cut(vm, p)) for p in probs}
    aa = {p: _dot(jnp.concatenate([qs[p], cut(rt, p)], axis=0),
                  jnp.concatenate([diag(cut(bt, p)), diag(cut(kt, p))], axis=0), NT) for p in probs}
    lk = {p: jnp.where(strict, aa[p][:c, LANES:], 0.0).astype(BF16) for p in probs}
    ab = {p: jnp.where(incl, aa[p][c:, :LANES], 0.0).astype(BF16) for p in probs}
    ak = {p: jnp.where(incl, aa[p][c:, LANES:], 0.0).astype(BF16) for p in probs}
    npow = {p: jnp.where(strict, -aa[p][:c, :LANES], 0.0).astype(BF16) for p in probs}
    tinv = {p: eye + npow[p].astype(F32) for p in probs}
    lv = {p: _dot(jnp.concatenate([lk[p], ak[p]], axis=0), vd[p]) for p in probs}
    zp = {p: lv[p][:c].astype(BF16) for p in probs}
    npow = {p: _dot(npow[p], diag(npow[p])).astype(BF16) for p in probs}
    for _ in range(int(math.log2(c)) - 2):
        both = {p: _dot(jnp.concatenate([tinv[p].astype(BF16), npow[p]], axis=0), diag(npow[p])) for p in probs}
        tinv = {p: tinv[p] + both[p][:c] for p in probs}
        npow = {p: both[p][c:].astype(BF16) for p in probs}
    tinv = {p: (tinv[p] + _dot(tinv[p].astype(BF16), diag(npow[p]))).astype(BF16) for p in probs}
    tq = {p: _dot(tinv[p], jnp.concatenate([diag(qs[p]), diag(zp[p])], axis=1)) for p in probs}
    qh = {p: tq[p][:, :LANES].astype(BF16) for p in probs}
    zn = {p: (-tq[p][:, LANES:]).astype(BF16) for p in probs}
    abq = {p: _dot(ab[p], jnp.concatenate([diag(qh[p]), diag(zn[p])], axis=1)) for p in probs}
    rh = {p: (cut(rt, p).astype(F32) - abq[p][:, :LANES]).astype(BF16) for p in probs}
    pv = {p: abq[p][:, LANES:] + lv[p][c:] for p in probs}
    gt = {p: halves(-_dot(qh[p], cut(bbar, p), TN)).astype(BF16) for p in probs}
    cst = {p: halves(_dot(jnp.concatenate([cut(vm, p), zn[p]], axis=0),
                          jnp.concatenate([cut(kbar, p), cut(bbar, p)], axis=0), TN)) for p in probs}
    return rh, pv, gt, cst


def _wkv_scan_steps(s_ref, o_ref, pre, gam, nc):
    c = WKV_CHUNK
    rh, pv, gt, cst = pre
    low = lax.broadcasted_iota(jnp.int32, (D_HEAD_DIM, LANES), 1) < D_HEAD_DIM
    for ch in range(nc):
        for pr in range(HEAD_PAIRS):
            p = (ch, pr)
            s0 = s_ref[pr]
            s0b = s0.astype(BF16)
            o_ref[ch * c:(ch + 1) * c, pr * LANES:(pr + 1) * LANES] = _dot(rh[p], _pair_diag(s0b, low), NT) + pv[p]
            s_ref[pr] = (s0 * gam[ch][:, pr * LANES:(pr + 1) * LANES] + _dot(s0b, _pair_diag(gt[p], low))
                         + cst[p])
        yield


def _rwkv_prompt_kernel(pd_ref, mu_ref, w0_ref, w2_ref, a0_ref, a2_ref, g2_ref, kk_ref, ka_ref, rk_ref,
                        lg_ref, lb_ref, q8_ref, ck_ref, cv_ref, yd_ref, shift_ref, wkv_ref, o8_ref,
                        car_ref, s_ref, o_ref, *, tt, bb):
    t = pl.program_id(1)
    nt = pl.num_programs(1)

    @pl.when(t == 0)
    def _():
        car_ref[...] = jnp.zeros_like(car_ref)
        s_ref[...] = jnp.zeros_like(s_ref)

    pd = pd_ref[0]
    ext = jnp.concatenate([car_ref[...], pd], axis=0)
    prev = pltpu.roll(ext, 1, 0)[ROW_CARRY:]
    car_ref[...] = ext[tt:]
    xs = pd + (prev - pd) * mu_ref[...]
    seg = _head_ones(W_D, D_HEAD_DIM)
    r, k, v, kk, a, ld, g = _rwkv_inputs(xs, w0_ref[...], w2_ref[...], a0_ref[...], a2_ref[...],
                                         g2_ref[...], kk_ref[...], ka_ref[...], seg)
    c = WKV_CHUNK
    nc = tt // c
    tri = jnp.where(lax.broadcasted_iota(jnp.int32, (c, c), 0) >= lax.broadcasted_iota(jnp.int32, (c, c), 1),
                    1.0, 0.0).astype(BF16)
    cs_chunks = [_ones_dot(tri, ld[ch * c:(ch + 1) * c]) for ch in range(nc)]
    cs = jnp.concatenate(cs_chunks, axis=0)
    cs_end = [x[c - 1:c] for x in cs_chunks]
    cs_last = jnp.concatenate([jnp.broadcast_to(x, (c, W_D)) for x in cs_end], axis=0)
    e_neg = jnp.exp(-cs)
    e_tail = jnp.exp(cs_last - cs)
    b = kk * a
    pre = _wkv_precompute(kk * jnp.exp(cs - ld), r * jnp.exp(cs), b * e_neg, k * e_neg, b * e_tail, k * e_tail, v, nc)
    _interleave(_wkv_scan_steps(s_ref, o_ref, pre, [jnp.exp(x) for x in cs_end], nc),
                _attn_sample_steps(q8_ref, ck_ref, cv_ref, o8_ref, bb))
    yd_ref[0] = _rwkv_finish(o_ref[...], r, k, v, g, rk_ref[...], lg_ref[...], lb_ref[...], seg).astype(yd_ref.dtype)

    @pl.when(t == nt - 1)
    def _():
        shift_ref[0] = pd[tt - 1:tt]
        for hd in range(D_HEADS):
            wkv_ref[0, hd] = _head_cols(s_ref[hd // 2], hd % 2)


def _rwkv_prompt(pd, params, q, cache_k, cache_v, layer, tt):
    bn, t, _ = pd.shape
    nt = t // tt
    n = q.shape[0]
    bb = n // (bn * nt)
    assert bb * bn * nt == n
    nbytes = 2 * tt * D_PROJ * 4 + 40 * tt * W_D * 4 + 4 * bb * MEM_ROWS * LANES * 4 + 2 * bb * MEM_ROWS * LANES * 2
    req = pl.BlockSpec((bb, SUBLANES, LANES), lambda i, j: (i * nt + j, 0, 0))
    cache_spec = pl.BlockSpec((None, bb, MEM_ROWS, LANES), lambda i, j: (layer, i * nt + j, 0, 0))
    yd, shift, wkv, o8 = pl.pallas_call(
        functools.partial(_rwkv_prompt_kernel, tt=tt, bb=bb),
        out_shape=(jax.ShapeDtypeStruct((bn, t, W_D), BF16),
                   jax.ShapeDtypeStruct((bn, 1, D_PROJ), F32),
                   jax.ShapeDtypeStruct((bn, D_HEADS, D_HEAD_DIM, D_HEAD_DIM), F32),
                   jax.ShapeDtypeStruct((n, SUBLANES, LANES), F32)),
        grid=(bn, nt),
        in_specs=[pl.BlockSpec((1, tt, D_PROJ), lambda i, j: (i, j, 0))] + [_lspec(z, layer) for z in params]
                 + [req, cache_spec, cache_spec],
        out_specs=(pl.BlockSpec((1, tt, W_D), lambda i, j: (i, j, 0)),
                   pl.BlockSpec((1, 1, D_PROJ), lambda i, j: (i, 0, 0)),
                   pl.BlockSpec((1, D_HEADS, D_HEAD_DIM, D_HEAD_DIM), lambda i, j: (i, 0, 0, 0)),
                   req),
        scratch_shapes=[pltpu.VMEM((ROW_CARRY, D_PROJ), F32),
                        pltpu.VMEM((HEAD_PAIRS, D_HEAD_DIM, LANES), F32),
                        pltpu.VMEM((tt, W_D), F32)],
        compiler_params=_params(("arbitrary", "arbitrary"), nbytes),
        name="rwkv_prompt",
    )(pd, *params, _to_pair_rows(q), _cache_rows_view(cache_k), _cache_rows_view(cache_v))
    return yd, shift, wkv, _from_pair_rows(o8)


def _rwkv_sample_kernel(pd_ref, sh_ref, st_ref, mu_ref, w0_ref, w2_ref, a0_ref, a2_ref, g2_ref, kk_ref,
                        ka_ref, rk_ref, lg_ref, lb_ref, yd_ref, so_ref, rows_ref, cols_ref, ot_ref):
    h = pl.program_id(0)
    n = D_HEAD_DIM

    @pl.when(h == 0)
    def _():
        pd = pd_ref[...]
        xs = pd + (sh_ref[...] - pd) * mu_ref[...]
        seg = _head_ones(W_D, D_HEAD_DIM)
        r, k, v, kk, a, ld, g = _rwkv_inputs(xs, w0_ref[...], w2_ref[...], a0_ref[...], a2_ref[...],
                                             g2_ref[...], kk_ref[...], ka_ref[...], seg)
        for j, x in enumerate((r, k, v, g)):
            rows_ref[j] = x
        for j, x in enumerate((kk, jnp.exp(ld), kk * a, k, r, v)):
            cols_ref[j] = x.T

    base = pl.multiple_of(h * n, n)
    kap, dec, bvec, kvec, rvec = (cols_ref[j, pl.ds(base, n), :] for j in range(5))
    for vi in range(n):
        s = st_ref[0, vi]
        u = -jnp.sum(s * kap, axis=0, keepdims=True)
        s = s * dec + u * bvec + cols_ref[5, pl.ds(base + vi, 1), :] * kvec
        so_ref[0, vi] = s
        ot_ref[pl.ds(base + vi, 1), :] = jnp.sum(s * rvec, axis=0, keepdims=True)

    @pl.when(h == pl.num_programs(0) - 1)
    def _():
        seg = _head_ones(W_D, D_HEAD_DIM)
        yd_ref[...] = _rwkv_finish(ot_ref[...].T, rows_ref[0], rows_ref[1], rows_ref[2], rows_ref[3],
                                   rk_ref[...], lg_ref[...], lb_ref[...], seg)


def _rwkv_sample(pd, shift, state, params, layer):
    n = pd.shape[0]
    sblock = (1, D_HEAD_DIM, D_HEAD_DIM, n)
    return pl.pallas_call(
        _rwkv_sample_kernel,
        out_shape=(jax.ShapeDtypeStruct((n, W_D), F32), jax.ShapeDtypeStruct(state.shape[1:], F32)),
        grid=(D_HEADS,),
        in_specs=[pl.BlockSpec((n, D_PROJ), lambda i: (0, 0)), _lspec(shift, layer),
                  pl.BlockSpec((None,) + sblock, lambda i: (layer, i, 0, 0, 0))]
                 + [_lspec(z, layer) for z in params],
        out_specs=(pl.BlockSpec((n, W_D), lambda i: (0, 0)), pl.BlockSpec(sblock, lambda i: (i, 0, 0, 0))),
        scratch_shapes=[pltpu.VMEM((4, n, W_D), F32), pltpu.VMEM((6, W_D, n), F32), pltpu.VMEM((W_D, n), F32)],
        compiler_params=pltpu.CompilerParams(dimension_semantics=("arbitrary",)),
        name="rwkv_sample",
    )(pd, shift, state, *params)


def _block_diag(w):
    gn, n, _ = w.shape
    eye = jnp.eye(gn, dtype=w.dtype)
    return (eye[:, None, :, None] * w[:, :, None, :]).reshape(gn * n, gn * n)


def kernel(x_prompt, x_sample, mem_prompt, cache_mem_k, cache_mem_v, state_pool, state_conv, state_shift, state_wkv,
           w_in, mu_d, ln_v_g, ln_v_b, ws_chunk, b_chunk, w_pool, pool_scale, conv_w,
           rwkv_w0, rwkv_w2, rwkv_a0, rwkv_a2, rwkv_g2, rwkv_k_k, rwkv_k_a, rwkv_r_k, rwkv_lnx_g, rwkv_lnx_b,
           w_out, ln1_g, ln1_b, w_xq, w_xk, w_xv, w_xo, ln2_g, ln2_b, ffn_w1, ffn_w3, ffn_w2, ln3_g, ln3_b):
    bp, t_p, d = x_prompt.shape
    ns, t_s, _ = x_sample.shape
    depth = w_in.shape[0]
    assert d == D_MODEL and t_s == 1 and t_p % CHUNK == 0 and w_in.shape[2] == PROJ
    alpha = (2 * depth) ** 0.25
    mp = bp * t_p
    nw = W_A + W_B + W_C
    row = lambda z: z.reshape(depth, 1, -1)

    w_in_b, w_out_b = w_in, w_out
    w_xq_b, w_xk_b, w_xv_b, w_xo_b = w_xq, w_xk, w_xv, w_xo
    w1_b, w3_b, w2_b = ffn_w1, ffn_w3, ffn_w2
    ws_flat = ws_chunk.reshape(depth, A_HEADS * CHUNK, CHUNK)
    bias_full = jnp.repeat(jnp.swapaxes(b_chunk, 1, 2), A_HEAD_DIM, axis=2)
    ws0 = jnp.repeat(ws_chunk[:, :, 0, 0], A_HEAD_DIM, axis=1).reshape(depth, 1, W_A)
    wpool_bd = jnp.stack([_block_diag(w_pool[l]) for l in range(depth)]).astype(BF16)
    abc_w = (row(ln_v_g), row(ln_v_b))
    abc_w2 = (wpool_bd, row(pool_scale), conv_w)
    rw = (row(mu_d), row(rwkv_w0), rwkv_w2, row(rwkv_a0), rwkv_a2, rwkv_g2,
          row(rwkv_k_k), row(rwkv_k_a), row(rwkv_r_k), row(rwkv_lnx_g), row(rwkv_lnx_b))
    ln1, ln2, ln3 = (row(ln1_g), row(ln1_b)), (row(ln2_g), row(ln2_b)), (row(ln3_g), row(ln3_b))
    pool_view = jnp.swapaxes(state_pool, 1, 2)
    wkv_view = state_wkv.transpose(0, 2, 3, 4, 1)
    conv_view = state_conv.reshape(depth, ns, (CONV_W - 1) * W_C)
    shift_view = state_shift.reshape(depth, ns, D_PROJ)

    hp = x_prompt
    hs = x_sample.reshape(ns, d)
    mem = mem_prompt.reshape(bp * MEM_LEN, d)
    mk_all, mv_all, mk_rows, mv_rows = _mem_kv(mem, w_xk_b, w_xv_b, 512)
    mk_all = mk_all.reshape(depth, bp, MEM_LEN, d)
    mv_all = mv_all.reshape(depth, bp, MEM_LEN, d)
    outs = [[] for _ in range(10)]
    for l in range(depth):
        pd, mix, v_last, pool_p, conv_p, *ffn_b = _proj_abc_prompt(hp, w_in_b, *abc_w, ws_flat, bias_full, *abc_w2,
                                                                   (w1_b, w3_b, w2_b), l, tt=1024)

        pabc_s, pd_s = _proj(hs, w_in_b, l, ns)
        mix_s, v_s, pool_s, conv_s = _abc_sample(pabc_s, pool_view, conv_view, *abc_w, ws0, bias_full, *abc_w2,
                                                 l, pos0=PAST_LEN)
        yd_s, wkv_s = _rwkv_sample(pd_s, shift_view, wkv_view, rw, l)
        hs = _mm_res_ln([mix_s, yd_s], w_out_b, l, hs, *ln1, tm=ns, alpha=alpha, name="out_proj_s")
        q_s = _mm(hs, w_xq_b, l, ns, "q_s")

        yd, shift_p, wkv_p, o_s = _rwkv_prompt(pd, rw, q_s, cache_mem_k, cache_mem_v, l, tt=512)
        hp = _attn_prompt(mix, yd, hp, mk_all, mv_all, w_out_b, *ln1, w_xq_b, w_xo_b, *ln2, l, tq=512, alpha=alpha)
        hp = _ffn(hp.reshape(mp, d), *ffn_b, *ln3, l, tm=1024, tf=256, alpha=alpha, name="ffn").reshape(bp, t_p, d)
        for lst, val in zip(outs[:5], (v_last, pool_p, conv_p, shift_p, wkv_p)):
            lst.append(val)

        hs = _mm_res_ln([o_s], w_xo_b, l, hs, *ln2, tm=ns, alpha=alpha, name="xo_s")
        hs = _ffn(hs, *ffn_b, *ln3, l, tm=ns, tf=256, alpha=alpha, name="ffn_s")
        for lst, val in zip(outs[5:], (v_s.reshape(ns, 1, W_A), pool_s,
                                       conv_s.reshape(ns, CONV_W - 1, W_C), pd_s.reshape(ns, 1, D_PROJ), wkv_s)):
            lst.append(val)

    stacked = [jnp.stack(o) for o in outs]
    stacked[6] = jnp.swapaxes(stacked[6], 1, 2)
    stacked[9] = stacked[9].transpose(0, 4, 1, 2, 3)
    return ((hp, hs.reshape(ns, 1, d)) + tuple(stacked[:5]) + (_cache_from_rows(mk_rows, bp), _cache_from_rows(mv_rows, bp))
            + tuple(stacked[5:]))
```

```python
import functools
import math

import jax
import jax.numpy as jnp
from jax import lax
from jax.experimental import pallas as pl
from jax.experimental.pallas import tpu as pltpu

F32 = jnp.float32
BF16 = jnp.bfloat16

D_MODEL = 1024
W_A = 256
W_B = 256
W_C = 256
W_D = 256
A_HEADS = 4
A_HEAD_DIM = W_A // A_HEADS
CHUNK = 128
POOL_WINDOWS = (2, 4, 8, 16)
POOL_GROUP = W_B // len(POOL_WINDOWS)
POOL_BUF = max(POOL_WINDOWS) - 1
CONV_W = 3
D_HEAD_DIM = 64
D_HEADS = W_D // D_HEAD_DIM
R_DECAY = 32
R_AAA = 32
R_GATE = 64
D_PROJ = 3 * W_D + R_DECAY + R_AAA + R_GATE
N_ABC = 2 * W_A + W_B + 3 * W_C
PROJ = N_ABC + D_PROJ
MEM_LEN = 256
X_HEADS = 4
X_HEAD_DIM = D_MODEL // X_HEADS
D_FF = int(math.ceil(8 * D_MODEL / 3 / 256)) * 256
PAST_LEN = 16384
LN_EPS = 1e-5
GN_EPS = 64e-5

WKV_CHUNK = 64
POOL_CARRY = 24
ROW_CARRY = 8
V7X_VMEM_BYTES = 64 * 1024 * 1024
VMEM_CAP = V7X_VMEM_BYTES - 8 * 1024 * 1024

NN = (((1,), (0,)), ((), ()))
NT = (((1,), (1,)), ((), ()))
TN = (((0,), (0,)), ((), ()))


def _vmem_limit(nbytes):
    return int(min(VMEM_CAP, max(32 * 1024 * 1024, 2 * nbytes)))


def _params(sem, nbytes):
    return pltpu.CompilerParams(dimension_semantics=sem, vmem_limit_bytes=_vmem_limit(nbytes))


def _lspec(arr, layer):
    tail = arr.shape[1:]
    zeros = (0,) * len(tail)
    return pl.BlockSpec((None,) + tail, lambda *_: (layer,) + zeros)


def _dot(a, b, dims=NN):
    return lax.dot_general(a, b, dims, preferred_element_type=F32)


def _split2(a):
    hi = a.astype(BF16)
    lo = (a - hi.astype(F32)).astype(BF16)
    return hi, lo


def _dot_hi(a, b, dims=NN):
    ah, al = _split2(a)
    bh, bl = _split2(b)
    return _dot(ah, bh, dims) + _dot(ah, bl, dims) + _dot(al, bh, dims)


def _dot_ones(x, ones_bf16, dims=NN):
    hi, lo = _split2(x)
    return _dot(hi, ones_bf16, dims) + _dot(lo, ones_bf16, dims)


def _ones_dot(ones_bf16, x):
    hi = x.astype(BF16)
    r1 = x - hi.astype(F32)
    mid = r1.astype(BF16)
    lo = (r1 - mid.astype(F32)).astype(BF16)
    return _dot(ones_bf16, hi) + _dot(ones_bf16, mid) + _dot(ones_bf16, lo)


def _ln(x, g, b, eps=LN_EPS):
    mu = jnp.mean(x, axis=-1, keepdims=True)
    xc = x - mu
    var = jnp.mean(xc * xc, axis=-1, keepdims=True)
    return xc * lax.rsqrt(var + eps) * g + b


def _gelu(x):
    c = math.sqrt(2.0 / math.pi)
    return x * (0.5 * (1.0 + jnp.tanh(c * (x + 0.044715 * (x * x * x)))))


def _sigmoid(x):
    return 1.0 / (1.0 + jnp.exp(-x))


def _softplus(x):
    return jnp.maximum(x, 0.0) + jnp.log(1.0 + jnp.exp(-jnp.abs(x)))


def _interleave(*staged):
    live = list(staged)
    while live:
        for steps in list(live):
            if next(steps, StopIteration) is StopIteration:
                live.remove(steps)


def _head_ones(n, group):
    r = lax.broadcasted_iota(jnp.int32, (n, n), 0) // group
    c = lax.broadcasted_iota(jnp.int32, (n, n), 1) // group
    return jnp.where(r == c, 1.0, 0.0).astype(BF16)


def _proj_kernel(x_ref, w_ref, oabc_ref, od_ref):
    y = _dot(x_ref[...].astype(BF16), w_ref[...].astype(BF16))
    oabc_ref[...] = y[:, :N_ABC]
    od_ref[...] = y[:, N_ABC:]


def _proj(x, w, layer, tm):
    m, k = x.shape
    nbytes = 2 * (tm * k * 4 + k * PROJ * 4 + tm * PROJ * 4) + tm * PROJ * 4 + k * PROJ * 2
    return pl.pallas_call(
        _proj_kernel,
        out_shape=(jax.ShapeDtypeStruct((m, N_ABC), F32), jax.ShapeDtypeStruct((m, D_PROJ), F32)),
        grid=(m // tm,),
        in_specs=[pl.BlockSpec((tm, k), lambda i: (i, 0)), _lspec(w, layer)],
        out_specs=(pl.BlockSpec((tm, N_ABC), lambda i: (i, 0)), pl.BlockSpec((tm, D_PROJ), lambda i: (i, 0))),
        compiler_params=_params(("parallel",), nbytes),
        name="proj",
    )(x, w)


def _mm_kernel(x_ref, w_ref, o_ref):
    o_ref[...] = _dot(x_ref[...].astype(BF16), w_ref[...].astype(BF16))


def _mm(x, w, layer, tm, name):
    m, k = x.shape
    n = w.shape[2]
    nbytes = 2 * (tm * k * 4 + k * n * 4 + tm * n * 4) + tm * n * 4 + k * n * 2
    return pl.pallas_call(
        _mm_kernel,
        out_shape=jax.ShapeDtypeStruct((m, n), F32),
        grid=(m // tm,),
        in_specs=[pl.BlockSpec((tm, k), lambda i: (i, 0)), _lspec(w, layer)],
        out_specs=pl.BlockSpec((tm, n), lambda i: (i, 0)),
        compiler_params=_params(("parallel",), nbytes),
        name=name,
    )(x, w)


def _mem_kv_kernel(x_ref, wk_ref, wv_ref, k_ref, v_ref, kt_ref, vt_ref, wkb_ref, wvb_ref):
    @pl.when(pl.program_id(1) == 0)
    def _():
        wkb_ref[...] = wk_ref[...].astype(BF16)
        wvb_ref[...] = wv_ref[...].astype(BF16)

    xb = x_ref[...].astype(BF16)
    for w_ref, o_ref, t_ref in ((wkb_ref, k_ref, kt_ref), (wvb_ref, v_ref, vt_ref)):
        y = _dot(xb, w_ref[...])
        o_ref[...] = y
        rows = y.shape[0]
        t_ref[...] = jnp.swapaxes(y.reshape(rows, X_HEADS, LANE_TILES, LANES), 1, 2).reshape(rows, SUBLANES, LANES)


def _mem_kv(mem, wk, wv, tm):
    m, k = mem.shape
    nl, _, n = wk.shape
    nbytes = 2 * (tm * k * 4 + 2 * k * n * 4 + 4 * tm * n * 4) + 2 * tm * n * 4 + 2 * k * n * 2
    wspec = pl.BlockSpec((None, k, n), lambda l, i: (l, 0, 0))
    ospec = pl.BlockSpec((None, tm, n), lambda l, i: (l, i, 0))
    tspec = pl.BlockSpec((None, tm, SUBLANES, LANES), lambda l, i: (l, i, 0, 0))
    shape = jax.ShapeDtypeStruct((nl, m, n), F32)
    tshape = jax.ShapeDtypeStruct((nl, m, SUBLANES, LANES), F32)
    return pl.pallas_call(
        _mem_kv_kernel,
        out_shape=(shape, shape, tshape, tshape),
        grid=(nl, m // tm),
        in_specs=[pl.BlockSpec((tm, k), lambda l, i: (i, 0)), wspec, wspec],
        out_specs=(ospec, ospec, tspec, tspec),
        scratch_shapes=[pltpu.VMEM((k, n), BF16), pltpu.VMEM((k, n), BF16)],
        compiler_params=_params(("arbitrary", "arbitrary"), nbytes),
        name="mem_kv",
    )(mem, wk, wv)


def _mm_res_ln_kernel(*refs, n_in, alpha):
    xs = refs[:n_in]
    ws = refs[n_in:2 * n_in]
    h_ref, g_ref, b_ref, o_ref = refs[2 * n_in:]
    y = _dot(xs[0][...].astype(BF16), ws[0][...].astype(BF16))
    for x_ref, w_ref in zip(xs[1:], ws[1:]):
        y = y + _dot(x_ref[...].astype(BF16), w_ref[...].astype(BF16))
    o_ref[...] = _ln(alpha * h_ref[...] + y, g_ref[...], b_ref[...])


def _mm_res_ln(xs, w, layer, h, g, b, tm, alpha, name):
    m, n = h.shape
    nbytes = 2 * sum(tm * x.shape[1] * 4 + x.shape[1] * n * 4 for x in xs) + 5 * tm * n * 4 + w.shape[1] * n * 2
    in_specs = [pl.BlockSpec((tm, x.shape[1]), lambda i: (i, 0)) for x in xs]
    start = 0
    for x in xs:
        width = x.shape[1]
        assert start % width == 0
        in_specs.append(pl.BlockSpec((None, width, n), lambda i, blk=start // width: (layer, blk, 0)))
        start += width
    assert start == w.shape[1]
    in_specs += [pl.BlockSpec((tm, n), lambda i: (i, 0)), _lspec(g, layer), _lspec(b, layer)]
    return pl.pallas_call(
        functools.partial(_mm_res_ln_kernel, n_in=len(xs), alpha=alpha),
        out_shape=jax.ShapeDtypeStruct((m, n), F32),
        grid=(m // tm,),
        in_specs=in_specs,
        out_specs=pl.BlockSpec((tm, n), lambda i: (i, 0)),
        compiler_params=_params(("parallel",), nbytes),
        name=name,
    )(*xs, *([w] * len(xs)), h, g, b)


def _ffn_kernel(xb_ref, x_ref, w1_ref, w3_ref, w2_ref, g_ref, b_ref, o_ref, acc_ref, *, alpha):
    j = pl.program_id(1)

    @pl.when(j == 0)
    def _():
        acc_ref[...] = jnp.zeros_like(acc_ref)

    xb = xb_ref[...].astype(BF16)
    h1 = _dot(xb, w1_ref[...])
    h3 = _dot(xb, w3_ref[...])
    a = (h1 * _sigmoid(h1) * h3).astype(BF16)
    acc_ref[...] += _dot(a, w2_ref[...])

    @pl.when(j == pl.num_programs(1) - 1)
    def _():
        o_ref[...] = _ln(alpha * x_ref[...] + acc_ref[...], g_ref[...], b_ref[...])


def _ffn(xb, x, w1, w3, w2, g, b, layer, tm, tf, alpha, name):
    m, d = x.shape
    last = D_FF // tf - 1
    nbytes = (2 * tm * d * xb.dtype.itemsize + 4 * tm * d * 4 + tm * d * 4 + 2 * 3 * d * tf * 2 + 3 * tm * tf * 4
              + tm * d * 2)
    return pl.pallas_call(
        functools.partial(_ffn_kernel, alpha=alpha),
        out_shape=jax.ShapeDtypeStruct((m, d), F32),
        grid=(m // tm, D_FF // tf),
        in_specs=[pl.BlockSpec((tm, d), lambda i, j: (i, 0)),
                  pl.BlockSpec((tm, d), lambda i, j: (jnp.where(j == last, i, jnp.maximum(i - 1, 0)), 0)),
                  pl.BlockSpec((None, d, tf), lambda i, j: (0, 0, j)),
                  pl.BlockSpec((None, d, tf), lambda i, j: (0, 0, j)),
                  pl.BlockSpec((None, tf, d), lambda i, j: (0, j, 0)),
                  _lspec(g, layer), _lspec(b, layer)],
        out_specs=pl.BlockSpec((tm, d), lambda i, j: (i, 0)),
        scratch_shapes=[pltpu.VMEM((tm, d), F32)],
        compiler_params=_params(("parallel", "arbitrary"), nbytes),
        name=name,
    )(xb, x, w1, w3, w2, g, b)


def _softmax_rows(s):
    m = jnp.max(s, axis=-1, keepdims=True)
    e = jnp.exp(s - m)
    return e / jnp.sum(e, axis=-1, keepdims=True)


ATTN_PIECES = 2


def _attn_prompt_kernel(mix_ref, yd_ref, h_ref, mk_ref, mv_ref, wma_ref, wmd_ref, g1_ref, b1_ref,
                        wq_ref, wo_ref, g_ref, b_ref, o_ref, o16_ref, ob_ref, wmb_ref, wqb_ref, wob_ref, *, alpha):
    nw = wma_ref.shape[0]

    @pl.when((pl.program_id(0) == 0) & (pl.program_id(1) == 0))
    def _():
        wmb_ref[:nw, :] = wma_ref[...].astype(BF16)
        wmb_ref[nw:, :] = wmd_ref[...].astype(BF16)
        wqb_ref[...] = wq_ref[...].astype(BF16)
        wob_ref[...] = wo_ref[...].astype(BF16)

    rows = h_ref.shape[1] // ATTN_PIECES
    scale = X_HEAD_DIM ** -0.5
    sls = [slice(hd * X_HEAD_DIM, (hd + 1) * X_HEAD_DIM) for hd in range(X_HEADS)]
    kb = [mk_ref[0, :, sl].astype(BF16) for sl in sls]
    vb = [mv_ref[0, :, sl].astype(BF16) for sl in sls]
    y, h, q, sc = {}, {}, {}, {}

    def project(p, rs):
        y[p] = (_dot(mix_ref[0, rs, :].astype(BF16), wmb_ref[:nw, :])
                + _dot(yd_ref[0, rs, :].astype(BF16), wmb_ref[nw:, :]))

    def query(p, rs):
        h[p] = _ln(alpha * h_ref[0, rs, :] + y[p], g1_ref[...], b1_ref[...])
        q[p] = _dot(h[p].astype(BF16), wqb_ref[...]).astype(BF16)

    def scores(p, rs):
        sc[p] = [_dot(q[p][:, sl], k, NT) * scale for sl, k in zip(sls, kb)]

    def values(p, rs):
        for sl, s, v in zip(sls, sc[p], vb):
            ob_ref[rs, sl] = _dot(_softmax_rows(s).astype(BF16), v).astype(BF16)

    def output(p, rs):
        out = _ln(alpha * h[p] + _dot(ob_ref[rs, :], wob_ref[...]), g_ref[...], b_ref[...])
        o_ref[0, rs, :] = out
        o16_ref[0, rs, :] = out.astype(BF16)

    stages = (project, query, scores, values, output)
    for step in range(len(stages) + ATTN_PIECES - 1):
        for p in range(ATTN_PIECES):
            if 0 <= step - p < len(stages):
                stages[step - p](p, slice(p * rows, (p + 1) * rows))


def _attn_prompt(mix, yd, h, mk, mv, w_mix, g1, b1, wq, wo, g, b, layer, tq, alpha):
    bn, t, d = h.shape
    nw = mix.shape[2]
    assert nw % W_D == 0 and w_mix.shape[1] == nw + W_D
    nbytes = (6 * tq * d * 4 + 4 * MEM_LEN * d * 4 + 6 * d * d * 4 + 3 * d * d * 2 + tq * d * 2
              + 3 * tq * d * 4 + 3 * tq * MEM_LEN * 4)
    tile = lambda n: pl.BlockSpec((1, tq, n), lambda i, j: (i, j, 0))
    return pl.pallas_call(
        functools.partial(_attn_prompt_kernel, alpha=alpha),
        out_shape=(jax.ShapeDtypeStruct((bn, t, d), F32), jax.ShapeDtypeStruct((bn, t, d), BF16)),
        grid=(bn, t // tq),
        in_specs=[tile(nw), tile(W_D), tile(d),
                  pl.BlockSpec((None, 1, MEM_LEN, d), lambda i, j: (layer, i, 0, 0)),
                  pl.BlockSpec((None, 1, MEM_LEN, d), lambda i, j: (layer, i, 0, 0)),
                  pl.BlockSpec((None, nw, d), lambda i, j: (layer, 0, 0)),
                  pl.BlockSpec((None, W_D, d), lambda i, j: (layer, nw // W_D, 0)),
                  _lspec(g1, layer), _lspec(b1, layer),
                  _lspec(wq, layer), _lspec(wo, layer), _lspec(g, layer), _lspec(b, layer)],
        out_specs=(tile(d), tile(d)),
        scratch_shapes=[pltpu.VMEM((tq, d), BF16), pltpu.VMEM((nw + W_D, d), BF16),
                        pltpu.VMEM((d, d), BF16), pltpu.VMEM((d, d), BF16)],
        compiler_params=_params(("arbitrary", "arbitrary"), nbytes),
        name="attn_prompt",
    )(mix, yd, h, mk, mv, w_mix, w_mix, g1, b1, wq, wo, g, b)


LANES = 128
SUBLANES = 8
LANE_TILES = X_HEAD_DIM // LANES
MEM_ROWS = MEM_LEN * LANE_TILES * X_HEADS


def _cache_rows_view(cache):
    nl, n = cache.shape[:2]
    x = cache.reshape(nl, n, MEM_LEN, X_HEADS, LANE_TILES, LANES)
    return x.transpose(0, 1, 2, 4, 3, 5).reshape(nl, n, MEM_ROWS, LANES)


def _cache_from_rows(rows, n):
    nl = rows.shape[0]
    x = rows.reshape(nl, n, MEM_LEN, LANE_TILES, X_HEADS, LANES)
    return x.transpose(0, 1, 2, 4, 3, 5).reshape(nl, n, MEM_LEN, X_HEADS, X_HEAD_DIM)


def _attn_sample_steps(q_ref, k_ref, v_ref, o_ref, bb):
    scale = X_HEAD_DIM ** -0.5
    shape = (SUBLANES, MEM_ROWS)
    rowi = lax.broadcasted_iota(jnp.int32, shape, 0)
    coli = lax.broadcasted_iota(jnp.int32, shape, 1)
    valid = (coli % SUBLANES) == rowi
    raw = [_dot(q_ref[i].astype(BF16), k_ref[i].astype(BF16), NT) for i in range(bb)]
    yield
    probs = []
    for r in raw:
        r = jnp.where(valid, r, 0.0)
        other = pltpu.roll(r, X_HEADS, 0)
        other = jnp.where(rowi < X_HEADS, pltpu.roll(other, MEM_ROWS - X_HEADS, 1), pltpu.roll(other, X_HEADS, 1))
        sc = jnp.where(valid, (r + other) * scale, -jnp.inf)
        m = jnp.max(sc, axis=-1, keepdims=True)
        e = jnp.exp(sc - m)
        probs.append((e / jnp.sum(e, axis=-1, keepdims=True)).astype(BF16))
        yield
    for i in range(bb):
        o_ref[i] = _dot(probs[i], v_ref[i].astype(BF16))
        yield


def _to_pair_rows(q):
    n = q.shape[0]
    return q.reshape(n, X_HEADS, LANE_TILES, LANES).transpose(0, 2, 1, 3).reshape(n, SUBLANES, LANES)


def _from_pair_rows(o8):
    n = o8.shape[0]
    return o8.reshape(n, LANE_TILES, X_HEADS, LANES).transpose(0, 2, 1, 3).reshape(n, D_MODEL)


def _pool_window_sums(ext, tt):
    s2 = ext + pltpu.roll(ext, 1, 0)
    s4 = s2 + pltpu.roll(s2, 2, 0)
    s8 = s4 + pltpu.roll(s4, 4, 0)
    s16 = s8 + pltpu.roll(s8, 8, 0)
    grp = lax.broadcasted_iota(jnp.int32, (tt, W_B), 1) // POOL_GROUP
    lo = ext.shape[0] - tt
    return jnp.where(grp == 0, s2[lo:], jnp.where(grp == 1, s4[lo:], jnp.where(grp == 2, s8[lo:], s16[lo:])))


def _proj_abc_kernel(x_ref, w_ref, lng_ref, lnb_ref, ws_ref, bias_ref, wpool_ref, pscale_ref, cw_ref,
                     f1_ref, f3_ref, f2_ref, pd_ref, mix_ref, vlast_ref, pool_ref, conv_ref,
                     f1b_ref, f3b_ref, f2b_ref, pcar_ref, ccar_ref, wb_ref, *, tt):
    t = pl.program_id(1)
    nt = pl.num_programs(1)

    @pl.when((pl.program_id(0) == 0) & (t == 0))
    def _():
        wb_ref[...] = w_ref[...].astype(BF16)

    @pl.when(t == 0)
    def _():
        pcar_ref[...] = jnp.zeros_like(pcar_ref)
        ccar_ref[...] = jnp.zeros_like(ccar_ref)

    rows = lax.broadcasted_iota(jnp.int32, (A_HEADS * CHUNK, CHUNK), 0) % CHUNK
    cols = lax.broadcasted_iota(jnp.int32, (A_HEADS * CHUNK, CHUNK), 1)
    wsm = jnp.where(rows >= cols, ws_ref[...], 0.0).astype(BF16)
    hid = lax.broadcasted_iota(jnp.int32, (CHUNK, W_A), 1) // A_HEAD_DIM
    win = jnp.left_shift(2, lax.broadcasted_iota(jnp.int32, (CHUNK, W_B), 1) // POOL_GROUP)
    rowi = lax.broadcasted_iota(jnp.int32, (CHUNK, W_B), 0)
    w = wb_ref[...]
    pcar = pcar_ref[...]
    ccar = ccar_ref[...]
    v = None
    for c in range(tt // CHUNK):
        rs = slice(c * CHUNK, (c + 1) * CHUNK)
        y = _dot(x_ref[0, rs, :].astype(BF16), w)
        pd_ref[0, rs, :] = y[:, N_ABC:]

        ga = _gelu(y[:, :2 * W_A])
        u = ga[:, :W_A]
        v = _ln(ga[:, W_A:], lng_ref[...], lnb_ref[...])
        zz = _dot(wsm, v.astype(BF16))
        z = zz[(A_HEADS - 1) * CHUNK:]
        for hd in range(A_HEADS - 2, -1, -1):
            z = jnp.where(hid == hd, zz[hd * CHUNK:(hd + 1) * CHUNK], z)
        mix_ref[0, rs, 0:W_A] = (u * (z + bias_ref[...])).astype(mix_ref.dtype)

        xb = y[:, 2 * W_A:2 * W_A + W_B]
        ext = jnp.concatenate([pcar, xb], axis=0)
        sums = _pool_window_sums(ext, CHUNK)
        pos = t * tt + c * CHUNK + rowi
        cnt = jnp.minimum(win, pos + 1).astype(F32)
        d = sums / cnt - xb
        mix_ref[0, rs, W_A:W_A + W_B] = (_dot(d.astype(BF16), wpool_ref[...]) * pscale_ref[...]).astype(mix_ref.dtype)
        pcar = ext[CHUNK:]

        o = 2 * W_A + W_B
        bg = y[:, o:o + W_C]
        zc = y[:, o + W_C:o + 2 * W_C] * y[:, o + 2 * W_C:o + 3 * W_C]
        extz = jnp.concatenate([ccar, zc], axis=0)
        conv = (cw_ref[0:1, :] * pltpu.roll(extz, 2, 0) + cw_ref[1:2, :] * pltpu.roll(extz, 1, 0)
                + cw_ref[2:3, :] * extz)
        mix_ref[0, rs, W_A + W_B:] = (bg * conv[ROW_CARRY:]).astype(mix_ref.dtype)
        ccar = extz[CHUNK:]
    pcar_ref[...] = pcar
    ccar_ref[...] = ccar
    for src_ref, dst_ref in ((f1_ref, f1b_ref), (f3_ref, f3b_ref), (f2_ref, f2b_ref)):
        dst_ref[...] = src_ref[...].astype(BF16)

    @pl.when(t == nt - 1)
    def _():
        vlast_ref[0] = v
        pool_ref[0] = pcar_ref[POOL_CARRY - POOL_BUF:, :]
        conv_ref[0] = ccar_ref[ROW_CARRY - (CONV_W - 1):, :]


def _proj_abc_prompt(h, w, lng, lnb, ws, bias, wpool, pscale, cw, ffn_ws, layer, tt):
    bn, t, d = h.shape
    nw = W_A + W_B + W_C
    nt = t // tt
    steps = bn * nt
    assert all(z.shape[1] % (steps * 2 * SUBLANES) == 0 for z in ffn_ws)
    f_in = [pl.BlockSpec((None, z.shape[1] // steps, z.shape[2]), lambda i, j: (layer, i * nt + j, 0)) for z in ffn_ws]
    f_out = [pl.BlockSpec((None, z.shape[1] // steps, z.shape[2]), lambda i, j: (0, i * nt + j, 0)) for z in ffn_ws]
    f_shapes = [jax.ShapeDtypeStruct((1,) + z.shape[1:], BF16) for z in ffn_ws]
    nbytes = (2 * tt * d * 4 + 2 * d * PROJ * 4 + d * PROJ * 2 + 2 * tt * (D_PROJ + nw) * 4 + 3 * CHUNK * PROJ * 4
              + 4 * A_HEADS * CHUNK * CHUNK * 4 + 12 * sum(z.shape[1] * z.shape[2] for z in ffn_ws) // steps)
    tile = lambda n: pl.BlockSpec((1, tt, n), lambda i, j: (i, j, 0))
    last = lambda r, n: pl.BlockSpec((1, r, n), lambda i, j: (i, 0, 0))
    return pl.pallas_call(
        functools.partial(_proj_abc_kernel, tt=tt),
        out_shape=(jax.ShapeDtypeStruct((bn, t, D_PROJ), F32),
                   jax.ShapeDtypeStruct((bn, t, nw), BF16),
                   jax.ShapeDtypeStruct((bn, CHUNK, W_A), F32),
                   jax.ShapeDtypeStruct((bn, POOL_BUF, W_B), F32),
                   jax.ShapeDtypeStruct((bn, CONV_W - 1, W_C), F32), *f_shapes),
        grid=(bn, nt),
        in_specs=[tile(d), *[_lspec(z, layer) for z in (w, lng, lnb, ws, bias, wpool, pscale, cw)], *f_in],
        out_specs=(tile(D_PROJ), tile(nw), last(CHUNK, W_A), last(POOL_BUF, W_B), last(CONV_W - 1, W_C), *f_out),
        scratch_shapes=[pltpu.VMEM((POOL_CARRY, W_B), F32), pltpu.VMEM((ROW_CARRY, W_C), F32),
                        pltpu.VMEM((d, PROJ), BF16)],
        compiler_params=_params(("arbitrary", "arbitrary"), nbytes),
        name="proj_abc_prompt",
    )(h, w, lng, lnb, ws, bias, wpool, pscale, cw, *ffn_ws)


def _abc_sample_kernel(x_ref, pool_ref, conv_ref, lng_ref, lnb_ref, ws0_ref, bias0_ref, wpool_ref,
                       pscale_ref, cw_ref, mix_ref, v_ref, poolo_ref, convo_ref, *, pos0):
    x = x_ref[...]
    n = x.shape[0]
    ga = _gelu(x[:, :2 * W_A])
    u = ga[:, :W_A]
    v = _ln(ga[:, W_A:], lng_ref[...], lnb_ref[...])
    v_ref[...] = v
    mix_ref[:, 0:W_A] = u * (v * ws0_ref[...] + bias0_ref[0:1, :])

    xb = x[:, 2 * W_A:2 * W_A + W_B]
    grp = lax.broadcasted_iota(jnp.int32, (n, W_B), 1) // POOL_GROUP
    run = xb
    sums = jnp.zeros_like(xb)
    for back in range(1, max(POOL_WINDOWS)):
        run = run + pool_ref[POOL_BUF - back]
        for gi, win in enumerate(POOL_WINDOWS):
            if back == win - 1:
                sums = jnp.where(grp == gi, run, sums)
    win = jnp.left_shift(2, grp)
    cnt = jnp.minimum(win, pos0 + 1).astype(F32)
    d = sums / cnt - xb
    mix_ref[:, W_A:W_A + W_B] = _dot(d.astype(BF16), wpool_ref[...]) * pscale_ref[...]
    for r in range(POOL_BUF - 1):
        poolo_ref[r] = pool_ref[r + 1]
    poolo_ref[POOL_BUF - 1] = xb

    o = 2 * W_A + W_B
    bg = x[:, o:o + W_C]
    zc = x[:, o + W_C:o + 2 * W_C] * x[:, o + 2 * W_C:o + 3 * W_C]
    z0 = conv_ref[:, :W_C]
    z1 = conv_ref[:, W_C:]
    y = cw_ref[0:1, :] * z0 + cw_ref[1:2, :] * z1 + cw_ref[2:3, :] * zc
    mix_ref[:, W_A + W_B:] = bg * y
    convo_ref[:, :W_C] = z1
    convo_ref[:, W_C:] = zc


def _abc_sample(pabc, pool, conv, lng, lnb, ws0, bias, wpool, pscale, cw, layer, pos0):
    n = pabc.shape[0]
    nw = W_A + W_B + W_C
    whole = lambda shape: pl.BlockSpec(shape, lambda i: (0,) * len(shape))
    return pl.pallas_call(
        functools.partial(_abc_sample_kernel, pos0=pos0),
        out_shape=(jax.ShapeDtypeStruct((n, nw), F32),
                   jax.ShapeDtypeStruct((n, W_A), F32),
                   jax.ShapeDtypeStruct(pool.shape[1:], F32),
                   jax.ShapeDtypeStruct(conv.shape[1:], F32)),
        grid=(1,),
        in_specs=[whole(pabc.shape)] + [_lspec(z, layer) for z in (pool, conv, lng, lnb, ws0, bias, wpool, pscale, cw)],
        out_specs=(whole((n, nw)), whole((n, W_A)), whole(pool.shape[1:]), whole(conv.shape[1:])),
        name="mixer_abc_sample",
    )(pabc, pool, conv, lng, lnb, ws0, bias, wpool, pscale, cw)


def _rwkv_inputs(xs, w0, w2, a0, a2, g2, kk_w, ka_w, seg):
    r = xs[:, 0:W_D]
    k = xs[:, W_D:2 * W_D]
    v = xs[:, 2 * W_D:3 * W_D]
    o = 3 * W_D
    dw = xs[:, o:o + R_DECAY]
    da = xs[:, o + R_DECAY:o + R_DECAY + R_AAA]
    dg = xs[:, o + R_DECAY + R_AAA:]
    w_log = -_softplus(-(w0 + _dot_hi(jnp.tanh(dw), w2))) - 0.5
    logdecay = -jnp.exp(w_log)
    a = _sigmoid(a0 + _dot_hi(da, a2))
    g = _dot_hi(_sigmoid(dg), g2)
    kk = k * kk_w
    kk = kk * lax.rsqrt(jnp.maximum(_dot_ones(kk * kk, seg), 1e-12))
    k = k * (1.0 + (a - 1.0) * ka_w)
    return r, k, v, kk, a, logdecay, g


def _rwkv_finish(o, r, k, v, g, rk_w, lnx_g, lnx_b, seg):
    inv = 1.0 / D_HEAD_DIM
    mu = _dot_ones(o, seg) * inv
    oc = o - mu
    var = _dot_ones(oc * oc, seg) * inv
    on = oc * lax.rsqrt(var + GN_EPS) * lnx_g + lnx_b
    bonus = _dot_ones(r * k * rk_w, seg) * v
    return (on + bonus) * g


def _bdot(a, b, dims=NN):
    return _dot(a.astype(BF16), b.astype(BF16), dims)


def _head_cols(x, hd):
    return x[:, hd * D_HEAD_DIM:(hd + 1) * D_HEAD_DIM]


HEAD_PAIRS = W_D // LANES


def _pair_diag(y, low):
    zero = jnp.zeros_like(y)
    return jnp.concatenate([jnp.where(low, y, zero), jnp.where(low, zero, y)], axis=0)


def _wkv_precompute(qt, rt, bt, kt, bbar, kbar, vm, nc):
    c = WKV_CHUNK
    n = D_HEAD_DIM
    probs = [(ch, pr) for ch in range(nc) for pr in range(HEAD_PAIRS)]
    cut = lambda x, p: x[p[0] * c:(p[0] + 1) * c, p[1] * LANES:(p[1] + 1) * LANES]
    qt, rt, bt, kt, bbar, kbar, vm = (x.astype(BF16) for x in (qt, rt, bt, kt, bbar, kbar, vm))
    ri = lax.broadcasted_iota(jnp.int32, (c, LANES), 0)
    li = lax.broadcasted_iota(jnp.int32, (c, LANES), 1)
    low = li < n
    strict = ri > li % n
    incl = ri >= li % n
    eye = jnp.where(ri == li % n, 1.0, 0.0)
    diag = lambda y: _pair_diag(y, low)
    halves = lambda x: jnp.where(low, x[:n], x[n:])

    qs = {p: cut(qt, p) for p in probs}
    vd = {p: diag(cut(vm, p)) for p in probs}
    aa = {p: _dot(jnp.concatenate([qs[p], cut(rt, p)], axis=0),
                  jnp.concatenate([diag(cut(bt, p)), diag(cut(kt, p))], axis=0), NT) for p in probs}
    lk = {p: jnp.where(strict, aa[p][:c, LANES:], 0.0).astype(BF16) for p in probs}
    ab = {p: jnp.where(incl, aa[p][c:, :LANES], 0.0).astype(BF16) for p in probs}
    ak = {p: jnp.where(incl, aa[p][c:, LANES:], 0.0).astype(BF16) for p in probs}
    npow = {p: jnp.where(strict, -aa[p][:c, :LANES], 0.0).astype(BF16) for p in probs}
    tinv = {p: eye + npow[p].astype(F32) for p in probs}
    lv = {p: _dot(jnp.concatenate([lk[p], ak[p]], axis=0), vd[p]) for p in probs}
    zp = {p: lv[p][:c].astype(BF16) for p in probs}
    npow = {p: _dot(npow[p], diag(npow[p])).astype(BF16) for p in probs}
    for _ in range(int(math.log2(c)) - 2):
        both = {p: _dot(jnp.concatenate([tinv[p].astype(BF16), npow[p]], axis=0), diag(npow[p])) for p in probs}
        tinv = {p: tinv[p] + both[p][:c] for p in probs}
        npow = {p: both[p][c:].astype(BF16) for p in probs}
    tinv = {p: (tinv[p] + _dot(tinv[p].astype(BF16), diag(npow[p]))).astype(BF16) for p in probs}
    tq = {p: _dot(tinv[p], jnp.concatenate([diag(qs[p]), diag(zp[p])], axis=1)) for p in probs}
    qh = {p: tq[p][:, :LANES].astype(BF16) for p in probs}
    zn = {p: (-tq[p][:, LANES:]).astype(BF16) for p in probs}
    abq = {p: _dot(ab[p], jnp.concatenate([diag(qh[p]), diag(zn[p])], axis=1)) for p in probs}
    rh = {p: (cut(rt, p).astype(F32) - abq[p][:, :LANES]).astype(BF16) for p in probs}
    pv = {p: abq[p][:, LANES:] + lv[p][c:] for p in probs}
    gt = {p: halves(-_dot(qh[p], cut(bbar, p), TN)).astype(BF16) for p in probs}
    cst = {p: halves(_dot(jnp.concatenate([cut(vm, p), zn[p]], axis=0),
                          jnp.concatenate([cut(kbar, p), cut(bbar, p)], axis=0), TN)) for p in probs}
    return rh, pv, gt, cst


def _wkv_scan_steps(s_ref, o_ref, pre, gam, nc):
    c = WKV_CHUNK
    rh, pv, gt, cst = pre
    low = lax.broadcasted_iota(jnp.int32, (D_HEAD_DIM, LANES), 1) < D_HEAD_DIM
    for ch in range(nc):
        for pr in range(HEAD_PAIRS):
            p = (ch, pr)
            s0 = s_ref[pr]
            s0b = s0.astype(BF16)
            o_ref[ch * c:(ch + 1) * c, pr * LANES:(pr + 1) * LANES] = _dot(rh[p], _pair_diag(s0b, low), NT) + pv[p]
            s_ref[pr] = (s0 * gam[ch][:, pr * LANES:(pr + 1) * LANES] + _dot(s0b, _pair_diag(gt[p], low))
                         + cst[p])
        yield


def _rwkv_prompt_kernel(pd_ref, mu_ref, w0_ref, w2_ref, a0_ref, a2_ref, g2_ref, kk_ref, ka_ref, rk_ref,
                        lg_ref, lb_ref, q8_ref, ck_ref, cv_ref, yd_ref, shift_ref, wkv_ref, o8_ref,
                        car_ref, s_ref, o_ref, *, tt, bb):
    t = pl.program_id(1)
    nt = pl.num_programs(1)

    @pl.when(t == 0)
    def _():
        car_ref[...] = jnp.zeros_like(car_ref)
        s_ref[...] = jnp.zeros_like(s_ref)

    pd = pd_ref[0]
    ext = jnp.concatenate([car_ref[...], pd], axis=0)
    prev = pltpu.roll(ext, 1, 0)[ROW_CARRY:]
    car_ref[...] = ext[tt:]
    xs = pd + (prev - pd) * mu_ref[...]
    seg = _head_ones(W_D, D_HEAD_DIM)
    r, k, v, kk, a, ld, g = _rwkv_inputs(xs, w0_ref[...], w2_ref[...], a0_ref[...], a2_ref[...],
                                         g2_ref[...], kk_ref[...], ka_ref[...], seg)
    c = WKV_CHUNK
    nc = tt // c
    tri = jnp.where(lax.broadcasted_iota(jnp.int32, (c, c), 0) >= lax.broadcasted_iota(jnp.int32, (c, c), 1),
                    1.0, 0.0).astype(BF16)
    cs_chunks = [_ones_dot(tri, ld[ch * c:(ch + 1) * c]) for ch in range(nc)]
    cs = jnp.concatenate(cs_chunks, axis=0)
    cs_end = [x[c - 1:c] for x in cs_chunks]
    cs_last = jnp.concatenate([jnp.broadcast_to(x, (c, W_D)) for x in cs_end], axis=0)
    e_neg = jnp.exp(-cs)
    e_tail = jnp.exp(cs_last - cs)
    b = kk * a
    pre = _wkv_precompute(kk * jnp.exp(cs - ld), r * jnp.exp(cs), b * e_neg, k * e_neg, b * e_tail, k * e_tail, v, nc)
    _interleave(_wkv_scan_steps(s_ref, o_ref, pre, [jnp.exp(x) for x in cs_end], nc),
                _attn_sample_steps(q8_ref, ck_ref, cv_ref, o8_ref, bb))
    yd_ref[0] = _rwkv_finish(o_ref[...], r, k, v, g, rk_ref[...], lg_ref[...], lb_ref[...], seg).astype(yd_ref.dtype)

    @pl.when(t == nt - 1)
    def _():
        shift_ref[0] = pd[tt - 1:tt]
        for hd in range(D_HEADS):
            wkv_ref[0, hd] = _head_cols(s_ref[hd // 2], hd % 2)


def _rwkv_prompt(pd, params, q, cache_k, cache_v, layer, tt):
    bn, t, _ = pd.shape
    nt = t // tt
    n = q.shape[0]
    bb = n // (bn * nt)
    assert bb * bn * nt == n
    nbytes = 2 * tt * D_PROJ * 4 + 40 * tt * W_D * 4 + 4 * bb * MEM_ROWS * LANES * 4 + 2 * bb * MEM_ROWS * LANES * 2
    req = pl.BlockSpec((bb, SUBLANES, LANES), lambda i, j: (i * nt + j, 0, 0))
    cache_spec = pl.BlockSpec((None, bb, MEM_ROWS, LANES), lambda i, j: (layer, i * nt + j, 0, 0))
    yd, shift, wkv, o8 = pl.pallas_call(
        functools.partial(_rwkv_prompt_kernel, tt=tt, bb=bb),
        out_shape=(jax.ShapeDtypeStruct((bn, t, W_D), BF16),
                   jax.ShapeDtypeStruct((bn, 1, D_PROJ), F32),
                   jax.ShapeDtypeStruct((bn, D_HEADS, D_HEAD_DIM, D_HEAD_DIM), F32),
                   jax.ShapeDtypeStruct((n, SUBLANES, LANES), F32)),
        grid=(bn, nt),
        in_specs=[pl.BlockSpec((1, tt, D_PROJ), lambda i, j: (i, j, 0))] + [_lspec(z, layer) for z in params]
                 + [req, cache_spec, cache_spec],
        out_specs=(pl.BlockSpec((1, tt, W_D), lambda i, j: (i, j, 0)),
                   pl.BlockSpec((1, 1, D_PROJ), lambda i, j: (i, 0, 0)),
                   pl.BlockSpec((1, D_HEADS, D_HEAD_DIM, D_HEAD_DIM), lambda i, j: (i, 0, 0, 0)),
                   req),
        scratch_shapes=[pltpu.VMEM((ROW_CARRY, D_PROJ), F32),
                        pltpu.VMEM((HEAD_PAIRS, D_HEAD_DIM, LANES), F32),
                        pltpu.VMEM((tt, W_D), F32)],
        compiler_params=_params(("arbitrary", "arbitrary"), nbytes),
        name="rwkv_prompt",
    )(pd, *params, _to_pair_rows(q), _cache_rows_view(cache_k), _cache_rows_view(cache_v))
    return yd, shift, wkv, _from_pair_rows(o8)


def _rwkv_sample_kernel(pd_ref, sh_ref, st_ref, mu_ref, w0_ref, w2_ref, a0_ref, a2_ref, g2_ref, kk_ref,
                        ka_ref, rk_ref, lg_ref, lb_ref, yd_ref, so_ref, rows_ref, cols_ref, ot_ref):
    h = pl.program_id(0)
    n = D_HEAD_DIM

    @pl.when(h == 0)
    def _():
        pd = pd_ref[...]
        xs = pd + (sh_ref[...] - pd) * mu_ref[...]
        seg = _head_ones(W_D, D_HEAD_DIM)
        r, k, v, kk, a, ld, g = _rwkv_inputs(xs, w0_ref[...], w2_ref[...], a0_ref[...], a2_ref[...],
                                             g2_ref[...], kk_ref[...], ka_ref[...], seg)
        for j, x in enumerate((r, k, v, g)):
            rows_ref[j] = x
        for j, x in enumerate((kk, jnp.exp(ld), kk * a, k, r, v)):
            cols_ref[j] = x.T

    base = pl.multiple_of(h * n, n)
    kap, dec, bvec, kvec, rvec = (cols_ref[j, pl.ds(base, n), :] for j in range(5))
    for vi in range(n):
        s = st_ref[0, vi]
        u = -jnp.sum(s * kap, axis=0, keepdims=True)
        s = s * dec + u * bvec + cols_ref[5, pl.ds(base + vi, 1), :] * kvec
        so_ref[0, vi] = s
        ot_ref[pl.ds(base + vi, 1), :] = jnp.sum(s * rvec, axis=0, keepdims=True)

    @pl.when(h == pl.num_programs(0) - 1)
    def _():
        seg = _head_ones(W_D, D_HEAD_DIM)
        yd_ref[...] = _rwkv_finish(ot_ref[...].T, rows_ref[0], rows_ref[1], rows_ref[2], rows_ref[3],
                                   rk_ref[...], lg_ref[...], lb_ref[...], seg)


def _rwkv_sample(pd, shift, state, params, layer):
    n = pd.shape[0]
    sblock = (1, D_HEAD_DIM, D_HEAD_DIM, n)
    return pl.pallas_call(
        _rwkv_sample_kernel,
        out_shape=(jax.ShapeDtypeStruct((n, W_D), F32), jax.ShapeDtypeStruct(state.shape[1:], F32)),
        grid=(D_HEADS,),
        in_specs=[pl.BlockSpec((n, D_PROJ), lambda i: (0, 0)), _lspec(shift, layer),
                  pl.BlockSpec((None,) + sblock, lambda i: (layer, i, 0, 0, 0))]
                 + [_lspec(z, layer) for z in params],
        out_specs=(pl.BlockSpec((n, W_D), lambda i: (0, 0)), pl.BlockSpec(sblock, lambda i: (i, 0, 0, 0))),
        scratch_shapes=[pltpu.VMEM((4, n, W_D), F32), pltpu.VMEM((6, W_D, n), F32), pltpu.VMEM((W_D, n), F32)],
        compiler_params=pltpu.CompilerParams(dimension_semantics=("arbitrary",)),
        name="rwkv_sample",
    )(pd, shift, state, *params)


def _block_diag(w):
    gn, n, _ = w.shape
    eye = jnp.eye(gn, dtype=w.dtype)
    return (eye[:, None, :, None] * w[:, :, None, :]).reshape(gn * n, gn * n)


def kernel(x_prompt, x_sample, mem_prompt, cache_mem_k, cache_mem_v, state_pool, state_conv, state_shift, state_wkv,
           w_in, mu_d, ln_v_g, ln_v_b, ws_chunk, b_chunk, w_pool, pool_scale, conv_w,
           rwkv_w0, rwkv_w2, rwkv_a0, rwkv_a2, rwkv_g2, rwkv_k_k, rwkv_k_a, rwkv_r_k, rwkv_lnx_g, rwkv_lnx_b,
           w_out, ln1_g, ln1_b, w_xq, w_xk, w_xv, w_xo, ln2_g, ln2_b, ffn_w1, ffn_w3, ffn_w2, ln3_g, ln3_b):
    bp, t_p, d = x_prompt.shape
    ns, t_s, _ = x_sample.shape
    depth = w_in.shape[0]
    assert d == D_MODEL and t_s == 1 and t_p % CHUNK == 0 and w_in.shape[2] == PROJ
    alpha = (2 * depth) ** 0.25
    mp = bp * t_p
    nw = W_A + W_B + W_C
    row = lambda z: z.reshape(depth, 1, -1)

    w_in_b, w_out_b = w_in, w_out
    w_xq_b, w_xk_b, w_xv_b, w_xo_b = w_xq, w_xk, w_xv, w_xo
    w1_b, w3_b, w2_b = ffn_w1, ffn_w3, ffn_w2
    ws_flat = ws_chunk.reshape(depth, A_HEADS * CHUNK, CHUNK)
    bias_full = jnp.repeat(jnp.swapaxes(b_chunk, 1, 2), A_HEAD_DIM, axis=2)
    ws0 = jnp.repeat(ws_chunk[:, :, 0, 0], A_HEAD_DIM, axis=1).reshape(depth, 1, W_A)
    wpool_bd = jnp.stack([_block_diag(w_pool[l]) for l in range(depth)]).astype(BF16)
    abc_w = (row(ln_v_g), row(ln_v_b))
    abc_w2 = (wpool_bd, row(pool_scale), conv_w)
    rw = (row(mu_d), row(rwkv_w0), rwkv_w2, row(rwkv_a0), rwkv_a2, rwkv_g2,
          row(rwkv_k_k), row(rwkv_k_a), row(rwkv_r_k), row(rwkv_lnx_g), row(rwkv_lnx_b))
    ln1, ln2, ln3 = (row(ln1_g), row(ln1_b)), (row(ln2_g), row(ln2_b)), (row(ln3_g), row(ln3_b))
    pool_view = jnp.swapaxes(state_pool, 1, 2)
    wkv_view = state_wkv.transpose(0, 2, 3, 4, 1)
    conv_view = state_conv.reshape(depth, ns, (CONV_W - 1) * W_C)
    shift_view = state_shift.reshape(depth, ns, D_PROJ)

    hp = x_prompt
    hs = x_sample.reshape(ns, d)
    mem = mem_prompt.reshape(bp * MEM_LEN, d)
    mk_all, mv_all, mk_rows, mv_rows = _mem_kv(mem, w_xk_b, w_xv_b, 512)
    mk_all = mk_all.reshape(depth, bp, MEM_LEN, d)
    mv_all = mv_all.reshape(depth, bp, MEM_LEN, d)
    outs = [[] for _ in range(10)]
    for l in range(depth):
        pd, mix, v_last, pool_p, conv_p, *ffn_b = _proj_abc_prompt(hp, w_in_b, *abc_w, ws_flat, bias_full, *abc_w2,
                                                                   (w1_b, w3_b, w2_b), l, tt=1024)

        pabc_s, pd_s = _proj(hs, w_in_b, l, ns)
        mix_s, v_s, pool_s, conv_s = _abc_sample(pabc_s, pool_view, conv_view, *abc_w, ws0, bias_full, *abc_w2,
                                                 l, pos0=PAST_LEN)
        yd_s, wkv_s = _rwkv_sample(pd_s, shift_view, wkv_view, rw, l)
        hs = _mm_res_ln([mix_s, yd_s], w_out_b, l, hs, *ln1, tm=ns, alpha=alpha, name="out_proj_s")
        q_s = _mm(hs, w_xq_b, l, ns, "q_s")

        yd, shift_p, wkv_p, o_s = _rwkv_prompt(pd, rw, q_s, cache_mem_k, cache_mem_v, l, tt=512)
        hp, hp16 = _attn_prompt(mix, yd, hp, mk_all, mv_all, w_out_b, *ln1, w_xq_b, w_xo_b, *ln2, l, tq=512,
                                alpha=alpha)
        hp = _ffn(hp16.reshape(mp, d), hp.reshape(mp, d), *ffn_b, *ln3, l, tm=1024, tf=256, alpha=alpha,
                  name="ffn").reshape(bp, t_p, d)
        for lst, val in zip(outs[:5], (v_last, pool_p, conv_p, shift_p, wkv_p)):
            lst.append(val)

        hs = _mm_res_ln([o_s], w_xo_b, l, hs, *ln2, tm=ns, alpha=alpha, name="xo_s")
        hs = _ffn(hs, hs, *ffn_b, *ln3, l, tm=ns, tf=256, alpha=alpha, name="ffn_s")
        for lst, val in zip(outs[5:], (v_s.reshape(ns, 1, W_A), pool_s,
                                       conv_s.reshape(ns, CONV_W - 1, W_C), pd_s.reshape(ns, 1, D_PROJ), wkv_s)):
            lst.append(val)

    stacked = [jnp.stack(o) for o in outs]
    stacked[6] = jnp.swapaxes(stacked[6], 1, 2)
    stacked[9] = stacked[9].transpose(0, 4, 1, 2, 3)
    return ((hp, hs.reshape(ns, 1, d)) + tuple(stacked[:5]) + (_cache_from_rows(mk_rows, bp), _cache_from_rows(mv_rows, bp))
            + tuple(stacked[5:]))
```

```python
import functools
import math

import jax
import jax.numpy as jnp
from jax import lax
from jax.experimental import pallas as pl
from jax.experimental.pallas import tpu as pltpu

F32 = jnp.float32
BF16 = jnp.bfloat16

D_MODEL = 1024
W_A = 256
W_B = 256
W_C = 256
W_D = 256
A_HEADS = 4
A_HEAD_DIM = W_A // A_HEADS
CHUNK = 128
POOL_WINDOWS = (2, 4, 8, 16)
POOL_GROUP = W_B // len(POOL_WINDOWS)
POOL_BUF = max(POOL_WINDOWS) - 1
CONV_W = 3
D_HEAD_DIM = 64
D_HEADS = W_D // D_HEAD_DIM
R_DECAY = 32
R_AAA = 32
R_GATE = 64
D_PROJ = 3 * W_D + R_DECAY + R_AAA + R_GATE
N_ABC = 2 * W_A + W_B + 3 * W_C
PROJ = N_ABC + D_PROJ
MEM_LEN = 256
X_HEADS = 4
X_HEAD_DIM = D_MODEL // X_HEADS
D_FF = int(math.ceil(8 * D_MODEL / 3 / 256)) * 256
PAST_LEN = 16384
LN_EPS = 1e-5
GN_EPS = 64e-5

WKV_CHUNK = 64
POOL_CARRY = 24
ROW_CARRY = 8
V7X_VMEM_BYTES = 64 * 1024 * 1024
VMEM_CAP = V7X_VMEM_BYTES - 8 * 1024 * 1024

NN = (((1,), (0,)), ((), ()))
NT = (((1,), (1,)), ((), ()))
TN = (((0,), (0,)), ((), ()))


def _vmem_limit(nbytes):
    return int(min(VMEM_CAP, max(32 * 1024 * 1024, 2 * nbytes)))


def _params(sem, nbytes):
    return pltpu.CompilerParams(dimension_semantics=sem, vmem_limit_bytes=_vmem_limit(nbytes))


def _lspec(arr, layer):
    tail = arr.shape[1:]
    zeros = (0,) * len(tail)
    return pl.BlockSpec((None,) + tail, lambda *_: (layer,) + zeros)


def _dot(a, b, dims=NN):
    return lax.dot_general(a, b, dims, preferred_element_type=F32)


def _split2(a):
    hi = a.astype(BF16)
    lo = (a - hi.astype(F32)).astype(BF16)
    return hi, lo


def _dot_hi(a, b, dims=NN):
    ah, al = _split2(a)
    bh, bl = _split2(b)
    return _dot(ah, bh, dims) + _dot(ah, bl, dims) + _dot(al, bh, dims)


def _dot_ones(x, ones_bf16, dims=NN):
    hi, lo = _split2(x)
    return _dot(hi, ones_bf16, dims) + _dot(lo, ones_bf16, dims)


def _ones_dot(ones_bf16, x):
    hi = x.astype(BF16)
    r1 = x - hi.astype(F32)
    mid = r1.astype(BF16)
    lo = (r1 - mid.astype(F32)).astype(BF16)
    return _dot(ones_bf16, hi) + _dot(ones_bf16, mid) + _dot(ones_bf16, lo)


def _ln(x, g, b, eps=LN_EPS):
    mu = jnp.mean(x, axis=-1, keepdims=True)
    xc = x - mu
    var = jnp.mean(xc * xc, axis=-1, keepdims=True)
    return xc * lax.rsqrt(var + eps) * g + b


def _gelu(x):
    c = math.sqrt(2.0 / math.pi)
    return x * (0.5 * (1.0 + jnp.tanh(c * (x + 0.044715 * (x * x * x)))))


def _sigmoid(x):
    return 1.0 / (1.0 + jnp.exp(-x))


def _softplus(x):
    return jnp.maximum(x, 0.0) + jnp.log(1.0 + jnp.exp(-jnp.abs(x)))


def _interleave(*staged):
    live = list(staged)
    while live:
        for steps in list(live):
            if next(steps, StopIteration) is StopIteration:
                live.remove(steps)


def _head_ones(n, group):
    r = lax.broadcasted_iota(jnp.int32, (n, n), 0) // group
    c = lax.broadcasted_iota(jnp.int32, (n, n), 1) // group
    return jnp.where(r == c, 1.0, 0.0).astype(BF16)


def _proj_kernel(x_ref, w_ref, oabc_ref, od_ref):
    y = _dot(x_ref[...].astype(BF16), w_ref[...].astype(BF16))
    oabc_ref[...] = y[:, :N_ABC]
    od_ref[...] = y[:, N_ABC:]


def _proj(x, w, layer, tm):
    m, k = x.shape
    nbytes = 2 * (tm * k * 4 + k * PROJ * 4 + tm * PROJ * 4) + tm * PROJ * 4 + k * PROJ * 2
    return pl.pallas_call(
        _proj_kernel,
        out_shape=(jax.ShapeDtypeStruct((m, N_ABC), F32), jax.ShapeDtypeStruct((m, D_PROJ), F32)),
        grid=(m // tm,),
        in_specs=[pl.BlockSpec((tm, k), lambda i: (i, 0)), _lspec(w, layer)],
        out_specs=(pl.BlockSpec((tm, N_ABC), lambda i: (i, 0)), pl.BlockSpec((tm, D_PROJ), lambda i: (i, 0))),
        compiler_params=_params(("parallel",), nbytes),
        name="proj",
    )(x, w)


def _mm_kernel(x_ref, w_ref, o_ref):
    o_ref[...] = _dot(x_ref[...].astype(BF16), w_ref[...].astype(BF16))


def _mm(x, w, layer, tm, name):
    m, k = x.shape
    n = w.shape[2]
    nbytes = 2 * (tm * k * 4 + k * n * 4 + tm * n * 4) + tm * n * 4 + k * n * 2
    return pl.pallas_call(
        _mm_kernel,
        out_shape=jax.ShapeDtypeStruct((m, n), F32),
        grid=(m // tm,),
        in_specs=[pl.BlockSpec((tm, k), lambda i: (i, 0)), _lspec(w, layer)],
        out_specs=pl.BlockSpec((tm, n), lambda i: (i, 0)),
        compiler_params=_params(("parallel",), nbytes),
        name=name,
    )(x, w)


def _mem_kv_kernel(x_ref, wk_ref, wv_ref, k_ref, v_ref, kt_ref, vt_ref, wkb_ref, wvb_ref):
    @pl.when(pl.program_id(1) == 0)
    def _():
        wkb_ref[...] = wk_ref[...].astype(BF16)
        wvb_ref[...] = wv_ref[...].astype(BF16)

    xb = x_ref[...].astype(BF16)
    for w_ref, o_ref, t_ref in ((wkb_ref, k_ref, kt_ref), (wvb_ref, v_ref, vt_ref)):
        y = _dot(xb, w_ref[...])
        o_ref[...] = y
        rows = y.shape[0]
        t_ref[...] = jnp.swapaxes(y.reshape(rows, X_HEADS, LANE_TILES, LANES), 1, 2).reshape(rows, SUBLANES, LANES)


def _mem_kv(mem, wk, wv, tm):
    m, k = mem.shape
    nl, _, n = wk.shape
    nbytes = 2 * (tm * k * 4 + 2 * k * n * 4 + 4 * tm * n * 4) + 2 * tm * n * 4 + 2 * k * n * 2
    wspec = pl.BlockSpec((None, k, n), lambda l, i: (l, 0, 0))
    ospec = pl.BlockSpec((None, tm, n), lambda l, i: (l, i, 0))
    tspec = pl.BlockSpec((None, tm, SUBLANES, LANES), lambda l, i: (l, i, 0, 0))
    shape = jax.ShapeDtypeStruct((nl, m, n), F32)
    tshape = jax.ShapeDtypeStruct((nl, m, SUBLANES, LANES), F32)
    return pl.pallas_call(
        _mem_kv_kernel,
        out_shape=(shape, shape, tshape, tshape),
        grid=(nl, m // tm),
        in_specs=[pl.BlockSpec((tm, k), lambda l, i: (i, 0)), wspec, wspec],
        out_specs=(ospec, ospec, tspec, tspec),
        scratch_shapes=[pltpu.VMEM((k, n), BF16), pltpu.VMEM((k, n), BF16)],
        compiler_params=_params(("arbitrary", "arbitrary"), nbytes),
        name="mem_kv",
    )(mem, wk, wv)


def _mm_res_ln_kernel(*refs, n_in, alpha):
    xs = refs[:n_in]
    ws = refs[n_in:2 * n_in]
    h_ref, g_ref, b_ref, o_ref = refs[2 * n_in:]
    y = _dot(xs[0][...].astype(BF16), ws[0][...].astype(BF16))
    for x_ref, w_ref in zip(xs[1:], ws[1:]):
        y = y + _dot(x_ref[...].astype(BF16), w_ref[...].astype(BF16))
    o_ref[...] = _ln(alpha * h_ref[...] + y, g_ref[...], b_ref[...])


def _mm_res_ln(xs, w, layer, h, g, b, tm, alpha, name):
    m, n = h.shape
    nbytes = 2 * sum(tm * x.shape[1] * 4 + x.shape[1] * n * 4 for x in xs) + 5 * tm * n * 4 + w.shape[1] * n * 2
    in_specs = [pl.BlockSpec((tm, x.shape[1]), lambda i: (i, 0)) for x in xs]
    start = 0
    for x in xs:
        width = x.shape[1]
        assert start % width == 0
        in_specs.append(pl.BlockSpec((None, width, n), lambda i, blk=start // width: (layer, blk, 0)))
        start += width
    assert start == w.shape[1]
    in_specs += [pl.BlockSpec((tm, n), lambda i: (i, 0)), _lspec(g, layer), _lspec(b, layer)]
    return pl.pallas_call(
        functools.partial(_mm_res_ln_kernel, n_in=len(xs), alpha=alpha),
        out_shape=jax.ShapeDtypeStruct((m, n), F32),
        grid=(m // tm,),
        in_specs=in_specs,
        out_specs=pl.BlockSpec((tm, n), lambda i: (i, 0)),
        compiler_params=_params(("parallel",), nbytes),
        name=name,
    )(*xs, *([w] * len(xs)), h, g, b)


PREFETCH_STEP = 1


def _ffn_kernel(x_hbm, w1_ref, w3_ref, w2_ref, g_ref, b_ref, o_ref, xs_ref, xb_ref, acc_ref, sem, *, alpha, n_tiles):
    i = pl.program_id(0)
    j = pl.program_id(1)
    tm = xb_ref.shape[0]
    slot = i % 2

    def rows_copy(tile, where):
        start = pl.multiple_of(tile * tm, tm)
        return pltpu.make_async_copy(x_hbm.at[pl.ds(start, tm), :], xs_ref.at[where], sem.at[where])

    @pl.when((i == 0) & (j == 0))
    def _():
        rows_copy(0, 0).start()

    @pl.when(j == 0)
    def _():
        rows_copy(i, slot).wait()
        xb_ref[...] = xs_ref[slot].astype(BF16)
        acc_ref[...] = jnp.zeros_like(acc_ref)

    @pl.when((j == PREFETCH_STEP) & (i + 1 < n_tiles))
    def _():
        rows_copy(i + 1, 1 - slot).start()

    xb = xb_ref[...]
    h1 = _dot(xb, w1_ref[...])
    h3 = _dot(xb, w3_ref[...])
    a = (h1 * _sigmoid(h1) * h3).astype(BF16)
    acc_ref[...] += _dot(a, w2_ref[...])

    @pl.when(j == pl.num_programs(1) - 1)
    def _():
        o_ref[...] = _ln(alpha * xs_ref[slot] + acc_ref[...], g_ref[...], b_ref[...])


def _ffn(x, w1, w3, w2, g, b, layer, tm, tf, alpha, name):
    m, d = x.shape
    assert D_FF // tf > PREFETCH_STEP
    nbytes = 4 * tm * d * 4 + tm * d * 2 + tm * d * 4 + 2 * 3 * d * tf * 2 + 3 * tm * tf * 4
    return pl.pallas_call(
        functools.partial(_ffn_kernel, alpha=alpha, n_tiles=m // tm),
        out_shape=jax.ShapeDtypeStruct((m, d), F32),
        grid=(m // tm, D_FF // tf),
        in_specs=[pl.BlockSpec(memory_space=pl.ANY),
                  pl.BlockSpec((None, d, tf), lambda i, j: (0, 0, j)),
                  pl.BlockSpec((None, d, tf), lambda i, j: (0, 0, j)),
                  pl.BlockSpec((None, tf, d), lambda i, j: (0, j, 0)),
                  _lspec(g, layer), _lspec(b, layer)],
        out_specs=pl.BlockSpec((tm, d), lambda i, j: (i, 0)),
        scratch_shapes=[pltpu.VMEM((2, tm, d), F32), pltpu.VMEM((tm, d), BF16), pltpu.VMEM((tm, d), F32),
                        pltpu.SemaphoreType.DMA((2,))],
        compiler_params=_params(("arbitrary", "arbitrary"), nbytes),
        name=name,
    )(x, w1, w3, w2, g, b)


def _softmax_rows(s):
    m = jnp.max(s, axis=-1, keepdims=True)
    e = jnp.exp(s - m)
    return e / jnp.sum(e, axis=-1, keepdims=True)


ATTN_PIECES = 2


def _attn_prompt_kernel(mix_ref, yd_ref, h_ref, mk_ref, mv_ref, wma_ref, wmd_ref, g1_ref, b1_ref,
                        wq_ref, wo_ref, g_ref, b_ref, o_ref, ob_ref, wmb_ref, wqb_ref, wob_ref, *, alpha):
    nw = wma_ref.shape[0]

    @pl.when((pl.program_id(0) == 0) & (pl.program_id(1) == 0))
    def _():
        wmb_ref[:nw, :] = wma_ref[...].astype(BF16)
        wmb_ref[nw:, :] = wmd_ref[...].astype(BF16)
        wqb_ref[...] = wq_ref[...].astype(BF16)
        wob_ref[...] = wo_ref[...].astype(BF16)

    rows = h_ref.shape[1] // ATTN_PIECES
    scale = X_HEAD_DIM ** -0.5
    sls = [slice(hd * X_HEAD_DIM, (hd + 1) * X_HEAD_DIM) for hd in range(X_HEADS)]
    kb = [mk_ref[0, :, sl].astype(BF16) for sl in sls]
    vb = [mv_ref[0, :, sl].astype(BF16) for sl in sls]
    y, h, q, sc = {}, {}, {}, {}

    def project(p, rs):
        y[p] = (_dot(mix_ref[0, rs, :].astype(BF16), wmb_ref[:nw, :])
                + _dot(yd_ref[0, rs, :].astype(BF16), wmb_ref[nw:, :]))

    def query(p, rs):
        h[p] = _ln(alpha * h_ref[0, rs, :] + y[p], g1_ref[...], b1_ref[...])
        q[p] = _dot(h[p].astype(BF16), wqb_ref[...]).astype(BF16)

    def scores(p, rs):
        sc[p] = [_dot(q[p][:, sl], k, NT) * scale for sl, k in zip(sls, kb)]

    def values(p, rs):
        for sl, s, v in zip(sls, sc[p], vb):
            ob_ref[rs, sl] = _dot(_softmax_rows(s).astype(BF16), v).astype(BF16)

    def output(p, rs):
        o_ref[0, rs, :] = _ln(alpha * h[p] + _dot(ob_ref[rs, :], wob_ref[...]), g_ref[...], b_ref[...])

    stages = (project, query, scores, values, output)
    for step in range(len(stages) + ATTN_PIECES - 1):
        for p in range(ATTN_PIECES):
            if 0 <= step - p < len(stages):
                stages[step - p](p, slice(p * rows, (p + 1) * rows))


def _attn_prompt(mix, yd, h, mk, mv, w_mix, g1, b1, wq, wo, g, b, layer, tq, alpha):
    bn, t, d = h.shape
    nw = mix.shape[2]
    assert nw % W_D == 0 and w_mix.shape[1] == nw + W_D
    nbytes = (6 * tq * d * 4 + 4 * MEM_LEN * d * 4 + 6 * d * d * 4 + 3 * d * d * 2 + tq * d * 2
              + 3 * tq * d * 4 + 3 * tq * MEM_LEN * 4)
    tile = lambda n: pl.BlockSpec((1, tq, n), lambda i, j: (i, j, 0))
    return pl.pallas_call(
        functools.partial(_attn_prompt_kernel, alpha=alpha),
        out_shape=jax.ShapeDtypeStruct((bn, t, d), F32),
        grid=(bn, t // tq),
        in_specs=[tile(nw), tile(W_D), tile(d),
                  pl.BlockSpec((None, 1, MEM_LEN, d), lambda i, j: (layer, i, 0, 0)),
                  pl.BlockSpec((None, 1, MEM_LEN, d), lambda i, j: (layer, i, 0, 0)),
                  pl.BlockSpec((None, nw, d), lambda i, j: (layer, 0, 0)),
                  pl.BlockSpec((None, W_D, d), lambda i, j: (layer, nw // W_D, 0)),
                  _lspec(g1, layer), _lspec(b1, layer),
                  _lspec(wq, layer), _lspec(wo, layer), _lspec(g, layer), _lspec(b, layer)],
        out_specs=tile(d),
        scratch_shapes=[pltpu.VMEM((tq, d), BF16), pltpu.VMEM((nw + W_D, d), BF16),
                        pltpu.VMEM((d, d), BF16), pltpu.VMEM((d, d), BF16)],
        compiler_params=_params(("arbitrary", "arbitrary"), nbytes),
        name="attn_prompt",
    )(mix, yd, h, mk, mv, w_mix, w_mix, g1, b1, wq, wo, g, b)


LANES = 128
SUBLANES = 8
LANE_TILES = X_HEAD_DIM // LANES
MEM_ROWS = MEM_LEN * LANE_TILES * X_HEADS


def _cache_rows_view(cache):
    nl, n = cache.shape[:2]
    x = cache.reshape(nl, n, MEM_LEN, X_HEADS, LANE_TILES, LANES)
    return x.transpose(0, 1, 2, 4, 3, 5).reshape(nl, n, MEM_ROWS, LANES)


def _cache_from_rows(rows, n):
    nl = rows.shape[0]
    x = rows.reshape(nl, n, MEM_LEN, LANE_TILES, X_HEADS, LANES)
    return x.transpose(0, 1, 2, 4, 3, 5).reshape(nl, n, MEM_LEN, X_HEADS, X_HEAD_DIM)


def _attn_sample_steps(q_ref, k_ref, v_ref, o_ref, bb):
    scale = X_HEAD_DIM ** -0.5
    shape = (SUBLANES, MEM_ROWS)
    rowi = lax.broadcasted_iota(jnp.int32, shape, 0)
    coli = lax.broadcasted_iota(jnp.int32, shape, 1)
    valid = (coli % SUBLANES) == rowi
    raw = [_dot(q_ref[i].astype(BF16), k_ref[i].astype(BF16), NT) for i in range(bb)]
    yield
    probs = []
    for r in raw:
        r = jnp.where(valid, r, 0.0)
        other = pltpu.roll(r, X_HEADS, 0)
        other = jnp.where(rowi < X_HEADS, pltpu.roll(other, MEM_ROWS - X_HEADS, 1), pltpu.roll(other, X_HEADS, 1))
        sc = jnp.where(valid, (r + other) * scale, -jnp.inf)
        m = jnp.max(sc, axis=-1, keepdims=True)
        e = jnp.exp(sc - m)
        probs.append((e / jnp.sum(e, axis=-1, keepdims=True)).astype(BF16))
        yield
    for i in range(bb):
        o_ref[i] = _dot(probs[i], v_ref[i].astype(BF16))
        yield


def _to_pair_rows(q):
    n = q.shape[0]
    return q.reshape(n, X_HEADS, LANE_TILES, LANES).transpose(0, 2, 1, 3).reshape(n, SUBLANES, LANES)


def _from_pair_rows(o8):
    n = o8.shape[0]
    return o8.reshape(n, LANE_TILES, X_HEADS, LANES).transpose(0, 2, 1, 3).reshape(n, D_MODEL)


def _pool_window_sums(ext, tt):
    s2 = ext + pltpu.roll(ext, 1, 0)
    s4 = s2 + pltpu.roll(s2, 2, 0)
    s8 = s4 + pltpu.roll(s4, 4, 0)
    s16 = s8 + pltpu.roll(s8, 8, 0)
    grp = lax.broadcasted_iota(jnp.int32, (tt, W_B), 1) // POOL_GROUP
    lo = ext.shape[0] - tt
    return jnp.where(grp == 0, s2[lo:], jnp.where(grp == 1, s4[lo:], jnp.where(grp == 2, s8[lo:], s16[lo:])))


def _proj_abc_kernel(x_ref, w_ref, lng_ref, lnb_ref, ws_ref, bias_ref, wpool_ref, pscale_ref, cw_ref,
                     f1_ref, f3_ref, f2_ref, pd_ref, mix_ref, vlast_ref, pool_ref, conv_ref,
                     f1b_ref, f3b_ref, f2b_ref, pcar_ref, ccar_ref, wb_ref, *, tt):
    t = pl.program_id(1)
    nt = pl.num_programs(1)

    @pl.when((pl.program_id(0) == 0) & (t == 0))
    def _():
        wb_ref[...] = w_ref[...].astype(BF16)

    @pl.when(t == 0)
    def _():
        pcar_ref[...] = jnp.zeros_like(pcar_ref)
        ccar_ref[...] = jnp.zeros_like(ccar_ref)

    rows = lax.broadcasted_iota(jnp.int32, (A_HEADS * CHUNK, CHUNK), 0) % CHUNK
    cols = lax.broadcasted_iota(jnp.int32, (A_HEADS * CHUNK, CHUNK), 1)
    wsm = jnp.where(rows >= cols, ws_ref[...], 0.0).astype(BF16)
    hid = lax.broadcasted_iota(jnp.int32, (CHUNK, W_A), 1) // A_HEAD_DIM
    win = jnp.left_shift(2, lax.broadcasted_iota(jnp.int32, (CHUNK, W_B), 1) // POOL_GROUP)
    rowi = lax.broadcasted_iota(jnp.int32, (CHUNK, W_B), 0)
    w = wb_ref[...]
    pcar = pcar_ref[...]
    ccar = ccar_ref[...]
    v = None
    for c in range(tt // CHUNK):
        rs = slice(c * CHUNK, (c + 1) * CHUNK)
        y = _dot(x_ref[0, rs, :].astype(BF16), w)
        pd_ref[0, rs, :] = y[:, N_ABC:]

        ga = _gelu(y[:, :2 * W_A])
        u = ga[:, :W_A]
        v = _ln(ga[:, W_A:], lng_ref[...], lnb_ref[...])
        zz = _dot(wsm, v.astype(BF16))
        z = zz[(A_HEADS - 1) * CHUNK:]
        for hd in range(A_HEADS - 2, -1, -1):
            z = jnp.where(hid == hd, zz[hd * CHUNK:(hd + 1) * CHUNK], z)
        mix_ref[0, rs, 0:W_A] = (u * (z + bias_ref[...])).astype(mix_ref.dtype)

        xb = y[:, 2 * W_A:2 * W_A + W_B]
        ext = jnp.concatenate([pcar, xb], axis=0)
        sums = _pool_window_sums(ext, CHUNK)
        pos = t * tt + c * CHUNK + rowi
        cnt = jnp.minimum(win, pos + 1).astype(F32)
        d = sums / cnt - xb
        mix_ref[0, rs, W_A:W_A + W_B] = (_dot(d.astype(BF16), wpool_ref[...]) * pscale_ref[...]).astype(mix_ref.dtype)
        pcar = ext[CHUNK:]

        o = 2 * W_A + W_B
        bg = y[:, o:o + W_C]
        zc = y[:, o + W_C:o + 2 * W_C] * y[:, o + 2 * W_C:o + 3 * W_C]
        extz = jnp.concatenate([ccar, zc], axis=0)
        conv = (cw_ref[0:1, :] * pltpu.roll(extz, 2, 0) + cw_ref[1:2, :] * pltpu.roll(extz, 1, 0)
                + cw_ref[2:3, :] * extz)
        mix_ref[0, rs, W_A + W_B:] = (bg * conv[ROW_CARRY:]).astype(mix_ref.dtype)
        ccar = extz[CHUNK:]
    pcar_ref[...] = pcar
    ccar_ref[...] = ccar
    for src_ref, dst_ref in ((f1_ref, f1b_ref), (f3_ref, f3b_ref), (f2_ref, f2b_ref)):
        dst_ref[...] = src_ref[...].astype(BF16)

    @pl.when(t == nt - 1)
    def _():
        vlast_ref[0] = v
        pool_ref[0] = pcar_ref[POOL_CARRY - POOL_BUF:, :]
        conv_ref[0] = ccar_ref[ROW_CARRY - (CONV_W - 1):, :]


def _proj_abc_prompt(h, w, lng, lnb, ws, bias, wpool, pscale, cw, ffn_ws, layer, tt):
    bn, t, d = h.shape
    nw = W_A + W_B + W_C
    nt = t // tt
    steps = bn * nt
    assert all(z.shape[1] % (steps * 2 * SUBLANES) == 0 for z in ffn_ws)
    f_in = [pl.BlockSpec((None, z.shape[1] // steps, z.shape[2]), lambda i, j: (layer, i * nt + j, 0)) for z in ffn_ws]
    f_out = [pl.BlockSpec((None, z.shape[1] // steps, z.shape[2]), lambda i, j: (0, i * nt + j, 0)) for z in ffn_ws]
    f_shapes = [jax.ShapeDtypeStruct((1,) + z.shape[1:], BF16) for z in ffn_ws]
    nbytes = (2 * tt * d * 4 + 2 * d * PROJ * 4 + d * PROJ * 2 + 2 * tt * (D_PROJ + nw) * 4 + 3 * CHUNK * PROJ * 4
              + 4 * A_HEADS * CHUNK * CHUNK * 4 + 12 * sum(z.shape[1] * z.shape[2] for z in ffn_ws) // steps)
    tile = lambda n: pl.BlockSpec((1, tt, n), lambda i, j: (i, j, 0))
    last = lambda r, n: pl.BlockSpec((1, r, n), lambda i, j: (i, 0, 0))
    return pl.pallas_call(
        functools.partial(_proj_abc_kernel, tt=tt),
        out_shape=(jax.ShapeDtypeStruct((bn, t, D_PROJ), F32),
                   jax.ShapeDtypeStruct((bn, t, nw), BF16),
                   jax.ShapeDtypeStruct((bn, CHUNK, W_A), F32),
                   jax.ShapeDtypeStruct((bn, POOL_BUF, W_B), F32),
                   jax.ShapeDtypeStruct((bn, CONV_W - 1, W_C), F32), *f_shapes),
        grid=(bn, nt),
        in_specs=[tile(d), *[_lspec(z, layer) for z in (w, lng, lnb, ws, bias, wpool, pscale, cw)], *f_in],
        out_specs=(tile(D_PROJ), tile(nw), last(CHUNK, W_A), last(POOL_BUF, W_B), last(CONV_W - 1, W_C), *f_out),
        scratch_shapes=[pltpu.VMEM((POOL_CARRY, W_B), F32), pltpu.VMEM((ROW_CARRY, W_C), F32),
                        pltpu.VMEM((d, PROJ), BF16)],
        compiler_params=_params(("arbitrary", "arbitrary"), nbytes),
        name="proj_abc_prompt",
    )(h, w, lng, lnb, ws, bias, wpool, pscale, cw, *ffn_ws)


def _abc_sample_kernel(x_ref, pool_ref, conv_ref, lng_ref, lnb_ref, ws0_ref, bias0_ref, wpool_ref,
                       pscale_ref, cw_ref, mix_ref, v_ref, poolo_ref, convo_ref, *, pos0):
    x = x_ref[...]
    n = x.shape[0]
    ga = _gelu(x[:, :2 * W_A])
    u = ga[:, :W_A]
    v = _ln(ga[:, W_A:], lng_ref[...], lnb_ref[...])
    v_ref[...] = v
    mix_ref[:, 0:W_A] = u * (v * ws0_ref[...] + bias0_ref[0:1, :])

    xb = x[:, 2 * W_A:2 * W_A + W_B]
    grp = lax.broadcasted_iota(jnp.int32, (n, W_B), 1) // POOL_GROUP
    run = xb
    sums = jnp.zeros_like(xb)
    for back in range(1, max(POOL_WINDOWS)):
        run = run + pool_ref[POOL_BUF - back]
        for gi, win in enumerate(POOL_WINDOWS):
            if back == win - 1:
                sums = jnp.where(grp == gi, run, sums)
    win = jnp.left_shift(2, grp)
    cnt = jnp.minimum(win, pos0 + 1).astype(F32)
    d = sums / cnt - xb
    mix_ref[:, W_A:W_A + W_B] = _dot(d.astype(BF16), wpool_ref[...]) * pscale_ref[...]
    for r in range(POOL_BUF - 1):
        poolo_ref[r] = pool_ref[r + 1]
    poolo_ref[POOL_BUF - 1] = xb

    o = 2 * W_A + W_B
    bg = x[:, o:o + W_C]
    zc = x[:, o + W_C:o + 2 * W_C] * x[:, o + 2 * W_C:o + 3 * W_C]
    z0 = conv_ref[:, :W_C]
    z1 = conv_ref[:, W_C:]
    y = cw_ref[0:1, :] * z0 + cw_ref[1:2, :] * z1 + cw_ref[2:3, :] * zc
    mix_ref[:, W_A + W_B:] = bg * y
    convo_ref[:, :W_C] = z1
    convo_ref[:, W_C:] = zc


def _abc_sample(pabc, pool, conv, lng, lnb, ws0, bias, wpool, pscale, cw, layer, pos0):
    n = pabc.shape[0]
    nw = W_A + W_B + W_C
    whole = lambda shape: pl.BlockSpec(shape, lambda i: (0,) * len(shape))
    return pl.pallas_call(
        functools.partial(_abc_sample_kernel, pos0=pos0),
        out_shape=(jax.ShapeDtypeStruct((n, nw), F32),
                   jax.ShapeDtypeStruct((n, W_A), F32),
                   jax.ShapeDtypeStruct(pool.shape[1:], F32),
                   jax.ShapeDtypeStruct(conv.shape[1:], F32)),
        grid=(1,),
        in_specs=[whole(pabc.shape)] + [_lspec(z, layer) for z in (pool, conv, lng, lnb, ws0, bias, wpool, pscale, cw)],
        out_specs=(whole((n, nw)), whole((n, W_A)), whole(pool.shape[1:]), whole(conv.shape[1:])),
        name="mixer_abc_sample",
    )(pabc, pool, conv, lng, lnb, ws0, bias, wpool, pscale, cw)


def _rwkv_inputs(xs, w0, w2, a0, a2, g2, kk_w, ka_w, seg):
    r = xs[:, 0:W_D]
    k = xs[:, W_D:2 * W_D]
    v = xs[:, 2 * W_D:3 * W_D]
    o = 3 * W_D
    dw = xs[:, o:o + R_DECAY]
    da = xs[:, o + R_DECAY:o + R_DECAY + R_AAA]
    dg = xs[:, o + R_DECAY + R_AAA:]
    w_log = -_softplus(-(w0 + _dot_hi(jnp.tanh(dw), w2))) - 0.5
    logdecay = -jnp.exp(w_log)
    a = _sigmoid(a0 + _dot_hi(da, a2))
    g = _dot_hi(_sigmoid(dg), g2)
    kk = k * kk_w
    kk = kk * lax.rsqrt(jnp.maximum(_dot_ones(kk * kk, seg), 1e-12))
    k = k * (1.0 + (a - 1.0) * ka_w)
    return r, k, v, kk, a, logdecay, g


def _rwkv_finish(o, r, k, v, g, rk_w, lnx_g, lnx_b, seg):
    inv = 1.0 / D_HEAD_DIM
    mu = _dot_ones(o, seg) * inv
    oc = o - mu
    var = _dot_ones(oc * oc, seg) * inv
    on = oc * lax.rsqrt(var + GN_EPS) * lnx_g + lnx_b
    bonus = _dot_ones(r * k * rk_w, seg) * v
    return (on + bonus) * g


def _bdot(a, b, dims=NN):
    return _dot(a.astype(BF16), b.astype(BF16), dims)


def _head_cols(x, hd):
    return x[:, hd * D_HEAD_DIM:(hd + 1) * D_HEAD_DIM]


HEAD_PAIRS = W_D // LANES


def _pair_diag(y, low):
    zero = jnp.zeros_like(y)
    return jnp.concatenate([jnp.where(low, y, zero), jnp.where(low, zero, y)], axis=0)


def _wkv_precompute(qt, rt, bt, kt, bbar, kbar, vm, nc):
    c = WKV_CHUNK
    n = D_HEAD_DIM
    probs = [(ch, pr) for ch in range(nc) for pr in range(HEAD_PAIRS)]
    cut = lambda x, p: x[p[0] * c:(p[0] + 1) * c, p[1] * LANES:(p[1] + 1) * LANES]
    qt, rt, bt, kt, bbar, kbar, vm = (x.astype(BF16) for x in (qt, rt, bt, kt, bbar, kbar, vm))
    ri = lax.broadcasted_iota(jnp.int32, (c, LANES), 0)
    li = lax.broadcasted_iota(jnp.int32, (c, LANES), 1)
    low = li < n
    strict = ri > li % n
    incl = ri >= li % n
    eye = jnp.where(ri == li % n, 1.0, 0.0)
    diag = lambda y: _pair_diag(y, low)
    halves = lambda x: jnp.where(low, x[:n], x[n:])

    qs = {p: cut(qt, p) for p in probs}
    vd = {p: diag(cut(vm, p)) for p in probs}
    aa = {p: _dot(jnp.concatenate([qs[p], cut(rt, p)], axis=0),
                  jnp.concatenate([diag(cut(bt, p)), diag(cut(kt, p))], axis=0), NT) for p in probs}
    lk = {p: jnp.where(strict, aa[p][:c, LANES:], 0.0).astype(BF16) for p in probs}
    ab = {p: jnp.where(incl, aa[p][c:, :LANES], 0.0).astype(BF16) for p in probs}
    ak = {p: jnp.where(incl, aa[p][c:, LANES:], 0.0).astype(BF16) for p in probs}
    npow = {p: jnp.where(strict, -aa[p][:c, :LANES], 0.0).astype(BF16) for p in probs}
    tinv = {p: eye + npow[p].astype(F32) for p in probs}
    lv = {p: _dot(jnp.concatenate([lk[p], ak[p]], axis=0), vd[p]) for p in probs}
    zp = {p: lv[p][:c].astype(BF16) for p in probs}
    npow = {p: _dot(npow[p], diag(npow[p])).astype(BF16) for p in probs}
    for _ in range(int(math.log2(c)) - 2):
        both = {p: _dot(jnp.concatenate([tinv[p].astype(BF16), npow[p]], axis=0), diag(npow[p])) for p in probs}
        tinv = {p: tinv[p] + both[p][:c] for p in probs}
        npow = {p: both[p][c:].astype(BF16) for p in probs}
    tinv = {p: (tinv[p] + _dot(tinv[p].astype(BF16), diag(npow[p]))).astype(BF16) for p in probs}
    tq = {p: _dot(tinv[p], jnp.concatenate([diag(qs[p]), diag(zp[p])], axis=1)) for p in probs}
    qh = {p: tq[p][:, :LANES].astype(BF16) for p in probs}
    zn = {p: (-tq[p][:, LANES:]).astype(BF16) for p in probs}
    abq = {p: _dot(ab[p], jnp.concatenate([diag(qh[p]), diag(zn[p])], axis=1)) for p in probs}
    rh = {p: (cut(rt, p).astype(F32) - abq[p][:, :LANES]).astype(BF16) for p in probs}
    pv = {p: abq[p][:, LANES:] + lv[p][c:] for p in probs}
    gt = {p: halves(-_dot(qh[p], cut(bbar, p), TN)).astype(BF16) for p in probs}
    cst = {p: halves(_dot(jnp.concatenate([cut(vm, p), zn[p]], axis=0),
                          jnp.concatenate([cut(kbar, p), cut(bbar, p)], axis=0), TN)) for p in probs}
    return rh, pv, gt, cst


def _wkv_scan_steps(s_ref, o_ref, pre, gam, nc):
    c = WKV_CHUNK
    rh, pv, gt, cst = pre
    low = lax.broadcasted_iota(jnp.int32, (D_HEAD_DIM, LANES), 1) < D_HEAD_DIM
    for ch in range(nc):
        for pr in range(HEAD_PAIRS):
            p = (ch, pr)
            s0 = s_ref[pr]
            s0b = s0.astype(BF16)
            o_ref[ch * c:(ch + 1) * c, pr * LANES:(pr + 1) * LANES] = _dot(rh[p], _pair_diag(s0b, low), NT) + pv[p]
            s_ref[pr] = (s0 * gam[ch][:, pr * LANES:(pr + 1) * LANES] + _dot(s0b, _pair_diag(gt[p], low))
                         + cst[p])
        yield


def _rwkv_prompt_kernel(pd_ref, mu_ref, w0_ref, w2_ref, a0_ref, a2_ref, g2_ref, kk_ref, ka_ref, rk_ref,
                        lg_ref, lb_ref, q8_ref, ck_ref, cv_ref, yd_ref, shift_ref, wkv_ref, o8_ref,
                        car_ref, s_ref, o_ref, *, tt, bb):
    t = pl.program_id(1)
    nt = pl.num_programs(1)

    @pl.when(t == 0)
    def _():
        car_ref[...] = jnp.zeros_like(car_ref)
        s_ref[...] = jnp.zeros_like(s_ref)

    pd = pd_ref[0]
    ext = jnp.concatenate([car_ref[...], pd], axis=0)
    prev = pltpu.roll(ext, 1, 0)[ROW_CARRY:]
    car_ref[...] = ext[tt:]
    xs = pd + (prev - pd) * mu_ref[...]
    seg = _head_ones(W_D, D_HEAD_DIM)
    r, k, v, kk, a, ld, g = _rwkv_inputs(xs, w0_ref[...], w2_ref[...], a0_ref[...], a2_ref[...],
                                         g2_ref[...], kk_ref[...], ka_ref[...], seg)
    c = WKV_CHUNK
    nc = tt // c
    tri = jnp.where(lax.broadcasted_iota(jnp.int32, (c, c), 0) >= lax.broadcasted_iota(jnp.int32, (c, c), 1),
                    1.0, 0.0).astype(BF16)
    cs_chunks = [_ones_dot(tri, ld[ch * c:(ch + 1) * c]) for ch in range(nc)]
    cs = jnp.concatenate(cs_chunks, axis=0)
    cs_end = [x[c - 1:c] for x in cs_chunks]
    cs_last = jnp.concatenate([jnp.broadcast_to(x, (c, W_D)) for x in cs_end], axis=0)
    e_neg = jnp.exp(-cs)
    e_tail = jnp.exp(cs_last - cs)
    b = kk * a
    pre = _wkv_precompute(kk * jnp.exp(cs - ld), r * jnp.exp(cs), b * e_neg, k * e_neg, b * e_tail, k * e_tail, v, nc)
    _interleave(_wkv_scan_steps(s_ref, o_ref, pre, [jnp.exp(x) for x in cs_end], nc),
                _attn_sample_steps(q8_ref, ck_ref, cv_ref, o8_ref, bb))
    yd_ref[0] = _rwkv_finish(o_ref[...], r, k, v, g, rk_ref[...], lg_ref[...], lb_ref[...], seg).astype(yd_ref.dtype)

    @pl.when(t == nt - 1)
    def _():
        shift_ref[0] = pd[tt - 1:tt]
        for hd in range(D_HEADS):
            wkv_ref[0, hd] = _head_cols(s_ref[hd // 2], hd % 2)


def _rwkv_prompt(pd, params, q, cache_k, cache_v, layer, tt):
    bn, t, _ = pd.shape
    nt = t // tt
    n = q.shape[0]
    bb = n // (bn * nt)
    assert bb * bn * nt == n
    nbytes = 2 * tt * D_PROJ * 4 + 40 * tt * W_D * 4 + 4 * bb * MEM_ROWS * LANES * 4 + 2 * bb * MEM_ROWS * LANES * 2
    req = pl.BlockSpec((bb, SUBLANES, LANES), lambda i, j: (i * nt + j, 0, 0))
    cache_spec = pl.BlockSpec((None, bb, MEM_ROWS, LANES), lambda i, j: (layer, i * nt + j, 0, 0))
    yd, shift, wkv, o8 = pl.pallas_call(
        functools.partial(_rwkv_prompt_kernel, tt=tt, bb=bb),
        out_shape=(jax.ShapeDtypeStruct((bn, t, W_D), BF16),
                   jax.ShapeDtypeStruct((bn, 1, D_PROJ), F32),
                   jax.ShapeDtypeStruct((bn, D_HEADS, D_HEAD_DIM, D_HEAD_DIM), F32),
                   jax.ShapeDtypeStruct((n, SUBLANES, LANES), F32)),
        grid=(bn, nt),
        in_specs=[pl.BlockSpec((1, tt, D_PROJ), lambda i, j: (i, j, 0))] + [_lspec(z, layer) for z in params]
                 + [req, cache_spec, cache_spec],
        out_specs=(pl.BlockSpec((1, tt, W_D), lambda i, j: (i, j, 0)),
                   pl.BlockSpec((1, 1, D_PROJ), lambda i, j: (i, 0, 0)),
                   pl.BlockSpec((1, D_HEADS, D_HEAD_DIM, D_HEAD_DIM), lambda i, j: (i, 0, 0, 0)),
                   req),
        scratch_shapes=[pltpu.VMEM((ROW_CARRY, D_PROJ), F32),
                        pltpu.VMEM((HEAD_PAIRS, D_HEAD_DIM, LANES), F32),
                        pltpu.VMEM((tt, W_D), F32)],
        compiler_params=_params(("arbitrary", "arbitrary"), nbytes),
        name="rwkv_prompt",
    )(pd, *params, _to_pair_rows(q), _cache_rows_view(cache_k), _cache_rows_view(cache_v))
    return yd, shift, wkv, _from_pair_rows(o8)


def _rwkv_sample_kernel(pd_ref, sh_ref, st_ref, mu_ref, w0_ref, w2_ref, a0_ref, a2_ref, g2_ref, kk_ref,
                        ka_ref, rk_ref, lg_ref, lb_ref, yd_ref, so_ref, rows_ref, cols_ref, ot_ref):
    h = pl.program_id(0)
    n = D_HEAD_DIM

    @pl.when(h == 0)
    def _():
        pd = pd_ref[...]
        xs = pd + (sh_ref[...] - pd) * mu_ref[...]
        seg = _head_ones(W_D, D_HEAD_DIM)
        r, k, v, kk, a, ld, g = _rwkv_inputs(xs, w0_ref[...], w2_ref[...], a0_ref[...], a2_ref[...],
                                             g2_ref[...], kk_ref[...], ka_ref[...], seg)
        for j, x in enumerate((r, k, v, g)):
            rows_ref[j] = x
        for j, x in enumerate((kk, jnp.exp(ld), kk * a, k, r, v)):
            cols_ref[j] = x.T

    base = pl.multiple_of(h * n, n)
    kap, dec, bvec, kvec, rvec = (cols_ref[j, pl.ds(base, n), :] for j in range(5))
    for vi in range(n):
        s = st_ref[0, vi]
        u = -jnp.sum(s * kap, axis=0, keepdims=True)
        s = s * dec + u * bvec + cols_ref[5, pl.ds(base + vi, 1), :] * kvec
        so_ref[0, vi] = s
        ot_ref[pl.ds(base + vi, 1), :] = jnp.sum(s * rvec, axis=0, keepdims=True)

    @pl.when(h == pl.num_programs(0) - 1)
    def _():
        seg = _head_ones(W_D, D_HEAD_DIM)
        yd_ref[...] = _rwkv_finish(ot_ref[...].T, rows_ref[0], rows_ref[1], rows_ref[2], rows_ref[3],
                                   rk_ref[...], lg_ref[...], lb_ref[...], seg)


def _rwkv_sample(pd, shift, state, params, layer):
    n = pd.shape[0]
    sblock = (1, D_HEAD_DIM, D_HEAD_DIM, n)
    return pl.pallas_call(
        _rwkv_sample_kernel,
        out_shape=(jax.ShapeDtypeStruct((n, W_D), F32), jax.ShapeDtypeStruct(state.shape[1:], F32)),
        grid=(D_HEADS,),
        in_specs=[pl.BlockSpec((n, D_PROJ), lambda i: (0, 0)), _lspec(shift, layer),
                  pl.BlockSpec((None,) + sblock, lambda i: (layer, i, 0, 0, 0))]
                 + [_lspec(z, layer) for z in params],
        out_specs=(pl.BlockSpec((n, W_D), lambda i: (0, 0)), pl.BlockSpec(sblock, lambda i: (i, 0, 0, 0))),
        scratch_shapes=[pltpu.VMEM((4, n, W_D), F32), pltpu.VMEM((6, W_D, n), F32), pltpu.VMEM((W_D, n), F32)],
        compiler_params=pltpu.CompilerParams(dimension_semantics=("arbitrary",)),
        name="rwkv_sample",
    )(pd, shift, state, *params)


def _block_diag(w):
    gn, n, _ = w.shape
    eye = jnp.eye(gn, dtype=w.dtype)
    return (eye[:, None, :, None] * w[:, :, None, :]).reshape(gn * n, gn * n)


def kernel(x_prompt, x_sample, mem_prompt, cache_mem_k, cache_mem_v, state_pool, state_conv, state_shift, state_wkv,
           w_in, mu_d, ln_v_g, ln_v_b, ws_chunk, b_chunk, w_pool, pool_scale, conv_w,
           rwkv_w0, rwkv_w2, rwkv_a0, rwkv_a2, rwkv_g2, rwkv_k_k, rwkv_k_a, rwkv_r_k, rwkv_lnx_g, rwkv_lnx_b,
           w_out, ln1_g, ln1_b, w_xq, w_xk, w_xv, w_xo, ln2_g, ln2_b, ffn_w1, ffn_w3, ffn_w2, ln3_g, ln3_b):
    bp, t_p, d = x_prompt.shape
    ns, t_s, _ = x_sample.shape
    depth = w_in.shape[0]
    assert d == D_MODEL and t_s == 1 and t_p % CHUNK == 0 and w_in.shape[2] == PROJ
    alpha = (2 * depth) ** 0.25
    mp = bp * t_p
    nw = W_A + W_B + W_C
    row = lambda z: z.reshape(depth, 1, -1)

    w_in_b, w_out_b = w_in, w_out
    w_xq_b, w_xk_b, w_xv_b, w_xo_b = w_xq, w_xk, w_xv, w_xo
    w1_b, w3_b, w2_b = ffn_w1, ffn_w3, ffn_w2
    ws_flat = ws_chunk.reshape(depth, A_HEADS * CHUNK, CHUNK)
    bias_full = jnp.repeat(jnp.swapaxes(b_chunk, 1, 2), A_HEAD_DIM, axis=2)
    ws0 = jnp.repeat(ws_chunk[:, :, 0, 0], A_HEAD_DIM, axis=1).reshape(depth, 1, W_A)
    wpool_bd = jnp.stack([_block_diag(w_pool[l]) for l in range(depth)]).astype(BF16)
    abc_w = (row(ln_v_g), row(ln_v_b))
    abc_w2 = (wpool_bd, row(pool_scale), conv_w)
    rw = (row(mu_d), row(rwkv_w0), rwkv_w2, row(rwkv_a0), rwkv_a2, rwkv_g2,
          row(rwkv_k_k), row(rwkv_k_a), row(rwkv_r_k), row(rwkv_lnx_g), row(rwkv_lnx_b))
    ln1, ln2, ln3 = (row(ln1_g), row(ln1_b)), (row(ln2_g), row(ln2_b)), (row(ln3_g), row(ln3_b))
    pool_view = jnp.swapaxes(state_pool, 1, 2)
    wkv_view = state_wkv.transpose(0, 2, 3, 4, 1)
    conv_view = state_conv.reshape(depth, ns, (CONV_W - 1) * W_C)
    shift_view = state_shift.reshape(depth, ns, D_PROJ)

    hp = x_prompt
    hs = x_sample.reshape(ns, d)
    mem = mem_prompt.reshape(bp * MEM_LEN, d)
    mk_all, mv_all, mk_rows, mv_rows = _mem_kv(mem, w_xk_b, w_xv_b, 512)
    mk_all = mk_all.reshape(depth, bp, MEM_LEN, d)
    mv_all = mv_all.reshape(depth, bp, MEM_LEN, d)
    outs = [[] for _ in range(10)]
    for l in range(depth):
        pd, mix, v_last, pool_p, conv_p, *ffn_b = _proj_abc_prompt(hp, w_in_b, *abc_w, ws_flat, bias_full, *abc_w2,
                                                                   (w1_b, w3_b, w2_b), l, tt=1024)

        pabc_s, pd_s = _proj(hs, w_in_b, l, ns)
        mix_s, v_s, pool_s, conv_s = _abc_sample(pabc_s, pool_view, conv_view, *abc_w, ws0, bias_full, *abc_w2,
                                                 l, pos0=PAST_LEN)
        yd_s, wkv_s = _rwkv_sample(pd_s, shift_view, wkv_view, rw, l)
        hs = _mm_res_ln([mix_s, yd_s], w_out_b, l, hs, *ln1, tm=ns, alpha=alpha, name="out_proj_s")
        q_s = _mm(hs, w_xq_b, l, ns, "q_s")

        yd, shift_p, wkv_p, o_s = _rwkv_prompt(pd, rw, q_s, cache_mem_k, cache_mem_v, l, tt=512)
        hp = _attn_prompt(mix, yd, hp, mk_all, mv_all, w_out_b, *ln1, w_xq_b, w_xo_b, *ln2, l, tq=512, alpha=alpha)
        hp = _ffn(hp.reshape(mp, d), *ffn_b, *ln3, l, tm=1024, tf=256, alpha=alpha, name="ffn").reshape(bp, t_p, d)
        for lst, val in zip(outs[:5], (v_last, pool_p, conv_p, shift_p, wkv_p)):
            lst.append(val)

        hs = _mm_res_ln([o_s], w_xo_b, l, hs, *ln2, tm=ns, alpha=alpha, name="xo_s")
        hs = _ffn(hs, *ffn_b, *ln3, l, tm=ns, tf=256, alpha=alpha, name="ffn_s")
        for lst, val in zip(outs[5:], (v_s.reshape(ns, 1, W_A), pool_s,
                                       conv_s.reshape(ns, CONV_W - 1, W_C), pd_s.reshape(ns, 1, D_PROJ), wkv_s)):
            lst.append(val)

    stacked = [jnp.stack(o) for o in outs]
    stacked[6] = jnp.swapaxes(stacked[6], 1, 2)
    stacked[9] = stacked[9].transpose(0, 4, 1, 2, 3)
    return ((hp, hs.reshape(ns, 1, d)) + tuple(stacked[:5]) + (_cache_from_rows(mk_rows, bp), _cache_from_rows(mv_rows, bp))
            + tuple(stacked[5:]))
```

```python
import functools
import math
from typing import NamedTuple

import jax
import jax.numpy as jnp
from jax import lax
from jax.experimental import pallas as pl
from jax.experimental.pallas import tpu as pltpu

F32 = jnp.float32
BF16 = jnp.bfloat16

D_MODEL = 1024
W_A = 256
W_B = 256
W_C = 256
W_D = 256
A_HEADS = 4
A_HEAD_DIM = W_A // A_HEADS
CHUNK = 128
POOL_WINDOWS = (2, 4, 8, 16)
POOL_GROUP = W_B // len(POOL_WINDOWS)
POOL_BUF = max(POOL_WINDOWS) - 1
CONV_W = 3
D_HEAD_DIM = 64
D_HEADS = W_D // D_HEAD_DIM
R_DECAY = 32
R_AAA = 32
R_GATE = 64
D_PROJ = 3 * W_D + R_DECAY + R_AAA + R_GATE
N_ABC = 2 * W_A + W_B + 3 * W_C
PROJ = N_ABC + D_PROJ
MEM_LEN = 256
X_HEADS = 4
X_HEAD_DIM = D_MODEL // X_HEADS
D_FF = int(math.ceil(8 * D_MODEL / 3 / 256)) * 256
PAST_LEN = 16384
LN_EPS = 1e-5
GN_EPS = 64e-5

WKV_CHUNK = 64
POOL_CARRY = 24
ROW_CARRY = 8
V7X_VMEM_BYTES = 64 * 1024 * 1024
VMEM_CAP = V7X_VMEM_BYTES - 8 * 1024 * 1024

NN = (((1,), (0,)), ((), ()))
NT = (((1,), (1,)), ((), ()))
TN = (((0,), (0,)), ((), ()))


def _vmem_limit(nbytes):
    return int(min(VMEM_CAP, max(32 * 1024 * 1024, 2 * nbytes)))


def _params(sem, nbytes):
    return pltpu.CompilerParams(dimension_semantics=sem, vmem_limit_bytes=_vmem_limit(nbytes))


class _Row(NamedTuple):
    arr: jax.Array
    start: int
    width: int


def _pack_rows(vectors):
    names = sorted(vectors, key=lambda k: -vectors[k].shape[1])
    pieces, where, start = [], {}, 0
    for name in names:
        v = vectors[name]
        width = v.shape[1]
        gap = -start % width
        if gap:
            pieces.append(jnp.zeros((v.shape[0], gap), v.dtype))
        where[name] = (start + gap, width)
        pieces.append(v)
        start += gap + width
    packed = jnp.concatenate(pieces, axis=1).reshape(pieces[0].shape[0], 1, start)
    return {name: _Row(packed, *where[name]) for name in names}


def _operand(z):
    return z.arr if isinstance(z, _Row) else z


def _lspec(arr, layer):
    if isinstance(arr, _Row):
        assert arr.start % arr.width == 0
        return pl.BlockSpec((None, 1, arr.width), lambda *_: (layer, 0, arr.start // arr.width))
    tail = arr.shape[1:]
    zeros = (0,) * len(tail)
    return pl.BlockSpec((None,) + tail, lambda *_: (layer,) + zeros)


def _dot(a, b, dims=NN):
    return lax.dot_general(a, b, dims, preferred_element_type=F32)


def _split2(a):
    hi = a.astype(BF16)
    lo = (a - hi.astype(F32)).astype(BF16)
    return hi, lo


def _dot_hi(a, b, dims=NN):
    ah, al = _split2(a)
    bh, bl = _split2(b)
    return _dot(ah, bh, dims) + _dot(ah, bl, dims) + _dot(al, bh, dims)


def _dot_ones(x, ones_bf16, dims=NN):
    hi, lo = _split2(x)
    return _dot(hi, ones_bf16, dims) + _dot(lo, ones_bf16, dims)


def _ones_dot(ones_bf16, x):
    hi = x.astype(BF16)
    r1 = x - hi.astype(F32)
    mid = r1.astype(BF16)
    lo = (r1 - mid.astype(F32)).astype(BF16)
    return _dot(ones_bf16, hi) + _dot(ones_bf16, mid) + _dot(ones_bf16, lo)


def _ln(x, g, b, eps=LN_EPS):
    mu = jnp.mean(x, axis=-1, keepdims=True)
    xc = x - mu
    var = jnp.mean(xc * xc, axis=-1, keepdims=True)
    return xc * lax.rsqrt(var + eps) * g + b


def _gelu(x):
    c = math.sqrt(2.0 / math.pi)
    return x * (0.5 * (1.0 + jnp.tanh(c * (x + 0.044715 * (x * x * x)))))


def _sigmoid(x):
    return 1.0 / (1.0 + jnp.exp(-x))


def _softplus(x):
    return jnp.maximum(x, 0.0) + jnp.log(1.0 + jnp.exp(-jnp.abs(x)))


def _interleave(*staged):
    live = list(staged)
    while live:
        for steps in list(live):
            if next(steps, StopIteration) is StopIteration:
                live.remove(steps)


def _head_ones(n, group):
    r = lax.broadcasted_iota(jnp.int32, (n, n), 0) // group
    c = lax.broadcasted_iota(jnp.int32, (n, n), 1) // group
    return jnp.where(r == c, 1.0, 0.0).astype(BF16)


def _proj_kernel(x_ref, w_ref, oabc_ref, od_ref):
    y = _dot(x_ref[...].astype(BF16), w_ref[...].astype(BF16))
    oabc_ref[...] = y[:, :N_ABC]
    od_ref[...] = y[:, N_ABC:]


def _proj(x, w, layer, tm):
    m, k = x.shape
    nbytes = 2 * (tm * k * 4 + k * PROJ * 4 + tm * PROJ * 4) + tm * PROJ * 4 + k * PROJ * 2
    return pl.pallas_call(
        _proj_kernel,
        out_shape=(jax.ShapeDtypeStruct((m, N_ABC), F32), jax.ShapeDtypeStruct((m, D_PROJ), F32)),
        grid=(m // tm,),
        in_specs=[pl.BlockSpec((tm, k), lambda i: (i, 0)), _lspec(w, layer)],
        out_specs=(pl.BlockSpec((tm, N_ABC), lambda i: (i, 0)), pl.BlockSpec((tm, D_PROJ), lambda i: (i, 0))),
        compiler_params=_params(("parallel",), nbytes),
        name="proj",
    )(x, w)


def _mm_kernel(x_ref, w_ref, o_ref):
    o_ref[...] = _dot(x_ref[...].astype(BF16), w_ref[...].astype(BF16))


def _mm(x, w, layer, tm, name):
    m, k = x.shape
    n = w.shape[2]
    nbytes = 2 * (tm * k * 4 + k * n * 4 + tm * n * 4) + tm * n * 4 + k * n * 2
    return pl.pallas_call(
        _mm_kernel,
        out_shape=jax.ShapeDtypeStruct((m, n), F32),
        grid=(m // tm,),
        in_specs=[pl.BlockSpec((tm, k), lambda i: (i, 0)), _lspec(w, layer)],
        out_specs=pl.BlockSpec((tm, n), lambda i: (i, 0)),
        compiler_params=_params(("parallel",), nbytes),
        name=name,
    )(x, w)


def _mem_kv_kernel(x_ref, wk_ref, wv_ref, k_ref, v_ref, kt_ref, vt_ref, wkb_ref, wvb_ref):
    @pl.when(pl.program_id(1) == 0)
    def _():
        wkb_ref[...] = wk_ref[...].astype(BF16)
        wvb_ref[...] = wv_ref[...].astype(BF16)

    xb = x_ref[...].astype(BF16)
    for w_ref, o_ref, t_ref in ((wkb_ref, k_ref, kt_ref), (wvb_ref, v_ref, vt_ref)):
        y = _dot(xb, w_ref[...])
        o_ref[...] = y
        rows = y.shape[0]
        t_ref[...] = jnp.swapaxes(y.reshape(rows, X_HEADS, LANE_TILES, LANES), 1, 2).reshape(rows, SUBLANES, LANES)


def _mem_kv(mem, wk, wv, tm):
    m, k = mem.shape
    nl, _, n = wk.shape
    nbytes = 2 * (tm * k * 4 + 2 * k * n * 4 + 4 * tm * n * 4) + 2 * tm * n * 4 + 2 * k * n * 2
    wspec = pl.BlockSpec((None, k, n), lambda l, i: (l, 0, 0))
    ospec = pl.BlockSpec((None, tm, n), lambda l, i: (l, i, 0))
    tspec = pl.BlockSpec((None, tm, SUBLANES, LANES), lambda l, i: (l, i, 0, 0))
    shape = jax.ShapeDtypeStruct((nl, m, n), F32)
    tshape = jax.ShapeDtypeStruct((nl, m, SUBLANES, LANES), F32)
    return pl.pallas_call(
        _mem_kv_kernel,
        out_shape=(shape, shape, tshape, tshape),
        grid=(nl, m // tm),
        in_specs=[pl.BlockSpec((tm, k), lambda l, i: (i, 0)), wspec, wspec],
        out_specs=(ospec, ospec, tspec, tspec),
        scratch_shapes=[pltpu.VMEM((k, n), BF16), pltpu.VMEM((k, n), BF16)],
        compiler_params=_params(("arbitrary", "arbitrary"), nbytes),
        name="mem_kv",
    )(mem, wk, wv)


def _mm_res_ln_kernel(*refs, n_in, alpha):
    xs = refs[:n_in]
    ws = refs[n_in:2 * n_in]
    h_ref, g_ref, b_ref, o_ref = refs[2 * n_in:]
    y = _dot(xs[0][...].astype(BF16), ws[0][...].astype(BF16))
    for x_ref, w_ref in zip(xs[1:], ws[1:]):
        y = y + _dot(x_ref[...].astype(BF16), w_ref[...].astype(BF16))
    o_ref[...] = _ln(alpha * h_ref[...] + y, g_ref[...], b_ref[...])


def _mm_res_ln(xs, w, layer, h, g, b, tm, alpha, name):
    m, n = h.shape
    nbytes = 2 * sum(tm * x.shape[1] * 4 + x.shape[1] * n * 4 for x in xs) + 5 * tm * n * 4 + w.shape[1] * n * 2
    in_specs = [pl.BlockSpec((tm, x.shape[1]), lambda i: (i, 0)) for x in xs]
    start = 0
    for x in xs:
        width = x.shape[1]
        assert start % width == 0
        in_specs.append(pl.BlockSpec((None, width, n), lambda i, blk=start // width: (layer, blk, 0)))
        start += width
    assert start == w.shape[1]
    in_specs += [pl.BlockSpec((tm, n), lambda i: (i, 0)), _lspec(g, layer), _lspec(b, layer)]
    return pl.pallas_call(
        functools.partial(_mm_res_ln_kernel, n_in=len(xs), alpha=alpha),
        out_shape=jax.ShapeDtypeStruct((m, n), F32),
        grid=(m // tm,),
        in_specs=in_specs,
        out_specs=pl.BlockSpec((tm, n), lambda i: (i, 0)),
        compiler_params=_params(("parallel",), nbytes),
        name=name,
    )(*xs, *([w] * len(xs)), h, _operand(g), _operand(b))


def _ffn_kernel(x_ref, w1_ref, w3_ref, w2_ref, g_ref, b_ref, o_ref, xb_ref, acc_ref, *, alpha):
    j = pl.program_id(1)

    @pl.when(j == 0)
    def _():
        xb_ref[...] = x_ref[...].astype(BF16)
        acc_ref[...] = jnp.zeros_like(acc_ref)

    xb = xb_ref[...]
    h1 = _dot(xb, w1_ref[...])
    h3 = _dot(xb, w3_ref[...])
    a = (h1 * _sigmoid(h1) * h3).astype(BF16)
    acc_ref[...] += _dot(a, w2_ref[...])

    @pl.when(j == pl.num_programs(1) - 1)
    def _():
        o_ref[...] = _ln(alpha * x_ref[...] + acc_ref[...], g_ref[...], b_ref[...])


def _ffn(x, w1, w3, w2, g, b, layer, tm, tf, alpha, name):
    m, d = x.shape
    nbytes = 4 * tm * d * 4 + tm * d * 2 + tm * d * 4 + 2 * 3 * d * tf * 2 + 3 * tm * tf * 4
    return pl.pallas_call(
        functools.partial(_ffn_kernel, alpha=alpha),
        out_shape=jax.ShapeDtypeStruct((m, d), F32),
        grid=(m // tm, D_FF // tf),
        in_specs=[pl.BlockSpec((tm, d), lambda i, j: (i, 0)),
                  pl.BlockSpec((None, d, tf), lambda i, j: (0, 0, j)),
                  pl.BlockSpec((None, d, tf), lambda i, j: (0, 0, j)),
                  pl.BlockSpec((None, tf, d), lambda i, j: (0, j, 0)),
                  _lspec(g, layer), _lspec(b, layer)],
        out_specs=pl.BlockSpec((tm, d), lambda i, j: (i, 0)),
        scratch_shapes=[pltpu.VMEM((tm, d), BF16), pltpu.VMEM((tm, d), F32)],
        compiler_params=_params(("parallel", "arbitrary"), nbytes),
        name=name,
    )(x, w1, w3, w2, _operand(g), _operand(b))


def _softmax_rows(s):
    m = jnp.max(s, axis=-1, keepdims=True)
    e = jnp.exp(s - m)
    return e / jnp.sum(e, axis=-1, keepdims=True)


ATTN_PIECES = 2


def _attn_prompt_kernel(mix_ref, yd_ref, h_ref, mk_ref, mv_ref, wma_ref, wmd_ref, g1_ref, b1_ref,
                        wq_ref, wo_ref, g_ref, b_ref, o_ref, ob_ref, wmb_ref, wqb_ref, wob_ref, *, alpha):
    nw = wma_ref.shape[0]

    @pl.when((pl.program_id(0) == 0) & (pl.program_id(1) == 0))
    def _():
        wmb_ref[:nw, :] = wma_ref[...].astype(BF16)
        wmb_ref[nw:, :] = wmd_ref[...].astype(BF16)
        wqb_ref[...] = wq_ref[...].astype(BF16)
        wob_ref[...] = wo_ref[...].astype(BF16)

    rows = h_ref.shape[1] // ATTN_PIECES
    scale = X_HEAD_DIM ** -0.5
    sls = [slice(hd * X_HEAD_DIM, (hd + 1) * X_HEAD_DIM) for hd in range(X_HEADS)]
    kb = [mk_ref[0, :, sl].astype(BF16) for sl in sls]
    vb = [mv_ref[0, :, sl].astype(BF16) for sl in sls]
    y, h, q, sc = {}, {}, {}, {}

    def project(p, rs):
        y[p] = (_dot(mix_ref[0, rs, :].astype(BF16), wmb_ref[:nw, :])
                + _dot(yd_ref[0, rs, :].astype(BF16), wmb_ref[nw:, :]))

    def query(p, rs):
        h[p] = _ln(alpha * h_ref[0, rs, :] + y[p], g1_ref[...], b1_ref[...])
        q[p] = _dot(h[p].astype(BF16), wqb_ref[...]).astype(BF16)

    def scores(p, rs):
        sc[p] = [_dot(q[p][:, sl], k, NT) * scale for sl, k in zip(sls, kb)]

    def values(p, rs):
        for sl, s, v in zip(sls, sc[p], vb):
            ob_ref[rs, sl] = _dot(_softmax_rows(s).astype(BF16), v).astype(BF16)

    def output(p, rs):
        o_ref[0, rs, :] = _ln(alpha * h[p] + _dot(ob_ref[rs, :], wob_ref[...]), g_ref[...], b_ref[...])

    stages = (project, query, scores, values, output)
    for step in range(len(stages) + ATTN_PIECES - 1):
        for p in range(ATTN_PIECES):
            if 0 <= step - p < len(stages):
                stages[step - p](p, slice(p * rows, (p + 1) * rows))


def _attn_prompt(mix, yd, h, mk, mv, w_mix, g1, b1, wq, wo, g, b, layer, tq, alpha):
    bn, t, d = h.shape
    nw = mix.shape[2]
    assert nw % W_D == 0 and w_mix.shape[1] == nw + W_D
    nbytes = (6 * tq * d * 4 + 4 * MEM_LEN * d * 4 + 6 * d * d * 4 + 3 * d * d * 2 + tq * d * 2
              + 3 * tq * d * 4 + 3 * tq * MEM_LEN * 4)
    tile = lambda n: pl.BlockSpec((1, tq, n), lambda i, j: (i, j, 0))
    return pl.pallas_call(
        functools.partial(_attn_prompt_kernel, alpha=alpha),
        out_shape=jax.ShapeDtypeStruct((bn, t, d), F32),
        grid=(bn, t // tq),
        in_specs=[tile(nw), tile(W_D), tile(d),
                  pl.BlockSpec((None, 1, MEM_LEN, d), lambda i, j: (layer, i, 0, 0)),
                  pl.BlockSpec((None, 1, MEM_LEN, d), lambda i, j: (layer, i, 0, 0)),
                  pl.BlockSpec((None, nw, d), lambda i, j: (layer, 0, 0)),
                  pl.BlockSpec((None, W_D, d), lambda i, j: (layer, nw // W_D, 0)),
                  _lspec(g1, layer), _lspec(b1, layer),
                  _lspec(wq, layer), _lspec(wo, layer), _lspec(g, layer), _lspec(b, layer)],
        out_specs=tile(d),
        scratch_shapes=[pltpu.VMEM((tq, d), BF16), pltpu.VMEM((nw + W_D, d), BF16),
                        pltpu.VMEM((d, d), BF16), pltpu.VMEM((d, d), BF16)],
        compiler_params=_params(("arbitrary", "arbitrary"), nbytes),
        name="attn_prompt",
    )(mix, yd, h, mk, mv, w_mix, w_mix, _operand(g1), _operand(b1), wq, wo, _operand(g), _operand(b))


LANES = 128
SUBLANES = 8
LANE_TILES = X_HEAD_DIM // LANES
MEM_ROWS = MEM_LEN * LANE_TILES * X_HEADS


def _cache_rows_view(cache):
    nl, n = cache.shape[:2]
    x = cache.reshape(nl, n, MEM_LEN, X_HEADS, LANE_TILES, LANES)
    return x.transpose(0, 1, 2, 4, 3, 5).reshape(nl, n, MEM_ROWS, LANES)


def _cache_from_rows(rows, n):
    nl = rows.shape[0]
    x = rows.reshape(nl, n, MEM_LEN, LANE_TILES, X_HEADS, LANES)
    return x.transpose(0, 1, 2, 4, 3, 5).reshape(nl, n, MEM_LEN, X_HEADS, X_HEAD_DIM)


def _attn_sample_steps(q_ref, k_ref, v_ref, o_ref, bb):
    scale = X_HEAD_DIM ** -0.5
    shape = (SUBLANES, MEM_ROWS)
    rowi = lax.broadcasted_iota(jnp.int32, shape, 0)
    coli = lax.broadcasted_iota(jnp.int32, shape, 1)
    valid = (coli % SUBLANES) == rowi
    raw = [_dot(q_ref[i].astype(BF16), k_ref[i].astype(BF16), NT) for i in range(bb)]
    yield
    probs = []
    for r in raw:
        r = jnp.where(valid, r, 0.0)
        other = pltpu.roll(r, X_HEADS, 0)
        other = jnp.where(rowi < X_HEADS, pltpu.roll(other, MEM_ROWS - X_HEADS, 1), pltpu.roll(other, X_HEADS, 1))
        sc = jnp.where(valid, (r + other) * scale, -jnp.inf)
        m = jnp.max(sc, axis=-1, keepdims=True)
        e = jnp.exp(sc - m)
        probs.append((e / jnp.sum(e, axis=-1, keepdims=True)).astype(BF16))
        yield
    for i in range(bb):
        o_ref[i] = _dot(probs[i], v_ref[i].astype(BF16))
        yield


def _to_pair_rows(q):
    n = q.shape[0]
    return q.reshape(n, X_HEADS, LANE_TILES, LANES).transpose(0, 2, 1, 3).reshape(n, SUBLANES, LANES)


def _from_pair_rows(o8):
    n = o8.shape[0]
    return o8.reshape(n, LANE_TILES, X_HEADS, LANES).transpose(0, 2, 1, 3).reshape(n, D_MODEL)


def _pool_window_sums(ext, tt):
    s2 = ext + pltpu.roll(ext, 1, 0)
    s4 = s2 + pltpu.roll(s2, 2, 0)
    s8 = s4 + pltpu.roll(s4, 4, 0)
    s16 = s8 + pltpu.roll(s8, 8, 0)
    grp = lax.broadcasted_iota(jnp.int32, (tt, W_B), 1) // POOL_GROUP
    lo = ext.shape[0] - tt
    return jnp.where(grp == 0, s2[lo:], jnp.where(grp == 1, s4[lo:], jnp.where(grp == 2, s8[lo:], s16[lo:])))


def _proj_abc_kernel(x_ref, w_ref, lng_ref, lnb_ref, ws_ref, bias_ref, wpool_ref, pscale_ref, cw_ref,
                     f1_ref, f3_ref, f2_ref, pd_ref, mix_ref, vlast_ref, pool_ref, conv_ref,
                     f1b_ref, f3b_ref, f2b_ref, pcar_ref, ccar_ref, wb_ref, *, tt):
    t = pl.program_id(1)
    nt = pl.num_programs(1)

    @pl.when((pl.program_id(0) == 0) & (t == 0))
    def _():
        wb_ref[...] = w_ref[...].astype(BF16)

    @pl.when(t == 0)
    def _():
        pcar_ref[...] = jnp.zeros_like(pcar_ref)
        ccar_ref[...] = jnp.zeros_like(ccar_ref)

    rows = lax.broadcasted_iota(jnp.int32, (A_HEADS * CHUNK, CHUNK), 0) % CHUNK
    cols = lax.broadcasted_iota(jnp.int32, (A_HEADS * CHUNK, CHUNK), 1)
    wsm = jnp.where(rows >= cols, ws_ref[...], 0.0).astype(BF16)
    hid = lax.broadcasted_iota(jnp.int32, (CHUNK, W_A), 1) // A_HEAD_DIM
    win = jnp.left_shift(2, lax.broadcasted_iota(jnp.int32, (CHUNK, W_B), 1) // POOL_GROUP)
    rowi = lax.broadcasted_iota(jnp.int32, (CHUNK, W_B), 0)
    w = wb_ref[...]
    pcar = pcar_ref[...]
    ccar = ccar_ref[...]
    v = None
    for c in range(tt // CHUNK):
        rs = slice(c * CHUNK, (c + 1) * CHUNK)
        y = _dot(x_ref[0, rs, :].astype(BF16), w)
        pd_ref[0, rs, :] = y[:, N_ABC:]

        ga = _gelu(y[:, :2 * W_A])
        u = ga[:, :W_A]
        v = _ln(ga[:, W_A:], lng_ref[...], lnb_ref[...])
        zz = _dot(wsm, v.astype(BF16))
        z = zz[(A_HEADS - 1) * CHUNK:]
        for hd in range(A_HEADS - 2, -1, -1):
            z = jnp.where(hid == hd, zz[hd * CHUNK:(hd + 1) * CHUNK], z)
        mix_ref[0, rs, 0:W_A] = (u * (z + bias_ref[...])).astype(mix_ref.dtype)

        xb = y[:, 2 * W_A:2 * W_A + W_B]
        ext = jnp.concatenate([pcar, xb], axis=0)
        sums = _pool_window_sums(ext, CHUNK)
        pos = t * tt + c * CHUNK + rowi
        cnt = jnp.minimum(win, pos + 1).astype(F32)
        d = sums / cnt - xb
        mix_ref[0, rs, W_A:W_A + W_B] = (_dot(d.astype(BF16), wpool_ref[...]) * pscale_ref[...]).astype(mix_ref.dtype)
        pcar = ext[CHUNK:]

        o = 2 * W_A + W_B
        bg = y[:, o:o + W_C]
        zc = y[:, o + W_C:o + 2 * W_C] * y[:, o + 2 * W_C:o + 3 * W_C]
        extz = jnp.concatenate([ccar, zc], axis=0)
        conv = (cw_ref[0:1, :] * pltpu.roll(extz, 2, 0) + cw_ref[1:2, :] * pltpu.roll(extz, 1, 0)
                + cw_ref[2:3, :] * extz)
        mix_ref[0, rs, W_A + W_B:] = (bg * conv[ROW_CARRY:]).astype(mix_ref.dtype)
        ccar = extz[CHUNK:]
    pcar_ref[...] = pcar
    ccar_ref[...] = ccar
    for src_ref, dst_ref in ((f1_ref, f1b_ref), (f3_ref, f3b_ref), (f2_ref, f2b_ref)):
        dst_ref[...] = src_ref[...].astype(BF16)

    @pl.when(t == nt - 1)
    def _():
        vlast_ref[0] = v
        pool_ref[0] = pcar_ref[POOL_CARRY - POOL_BUF:, :]
        conv_ref[0] = ccar_ref[ROW_CARRY - (CONV_W - 1):, :]


def _proj_abc_prompt(h, w, lng, lnb, ws, bias, wpool, pscale, cw, ffn_ws, layer, tt):
    bn, t, d = h.shape
    nw = W_A + W_B + W_C
    nt = t // tt
    steps = bn * nt
    assert all(z.shape[1] % (steps * 2 * SUBLANES) == 0 for z in ffn_ws)
    f_in = [pl.BlockSpec((None, z.shape[1] // steps, z.shape[2]), lambda i, j: (layer, i * nt + j, 0)) for z in ffn_ws]
    f_out = [pl.BlockSpec((None, z.shape[1] // steps, z.shape[2]), lambda i, j: (0, i * nt + j, 0)) for z in ffn_ws]
    f_shapes = [jax.ShapeDtypeStruct((1,) + z.shape[1:], BF16) for z in ffn_ws]
    nbytes = (2 * tt * d * 4 + 2 * d * PROJ * 4 + d * PROJ * 2 + 2 * tt * (D_PROJ + nw) * 4 + 3 * CHUNK * PROJ * 4
              + 4 * A_HEADS * CHUNK * CHUNK * 4 + 12 * sum(z.shape[1] * z.shape[2] for z in ffn_ws) // steps)
    tile = lambda n: pl.BlockSpec((1, tt, n), lambda i, j: (i, j, 0))
    last = lambda r, n: pl.BlockSpec((1, r, n), lambda i, j: (i, 0, 0))
    return pl.pallas_call(
        functools.partial(_proj_abc_kernel, tt=tt),
        out_shape=(jax.ShapeDtypeStruct((bn, t, D_PROJ), F32),
                   jax.ShapeDtypeStruct((bn, t, nw), BF16),
                   jax.ShapeDtypeStruct((bn, CHUNK, W_A), F32),
                   jax.ShapeDtypeStruct((bn, POOL_BUF, W_B), F32),
                   jax.ShapeDtypeStruct((bn, CONV_W - 1, W_C), F32), *f_shapes),
        grid=(bn, nt),
        in_specs=[tile(d), *[_lspec(z, layer) for z in (w, lng, lnb, ws, bias, wpool, pscale, cw)], *f_in],
        out_specs=(tile(D_PROJ), tile(nw), last(CHUNK, W_A), last(POOL_BUF, W_B), last(CONV_W - 1, W_C), *f_out),
        scratch_shapes=[pltpu.VMEM((POOL_CARRY, W_B), F32), pltpu.VMEM((ROW_CARRY, W_C), F32),
                        pltpu.VMEM((d, PROJ), BF16)],
        compiler_params=_params(("arbitrary", "arbitrary"), nbytes),
        name="proj_abc_prompt",
    )(*map(_operand, (h, w, lng, lnb, ws, bias, wpool, pscale, cw, *ffn_ws)))


def _abc_sample_kernel(x_ref, pool_ref, conv_ref, lng_ref, lnb_ref, ws0_ref, bias0_ref, wpool_ref,
                       pscale_ref, cw_ref, mix_ref, v_ref, poolo_ref, convo_ref, *, pos0):
    x = x_ref[...]
    n = x.shape[0]
    ga = _gelu(x[:, :2 * W_A])
    u = ga[:, :W_A]
    v = _ln(ga[:, W_A:], lng_ref[...], lnb_ref[...])
    v_ref[...] = v
    mix_ref[:, 0:W_A] = u * (v * ws0_ref[...] + bias0_ref[0:1, :])

    xb = x[:, 2 * W_A:2 * W_A + W_B]
    grp = lax.broadcasted_iota(jnp.int32, (n, W_B), 1) // POOL_GROUP
    run = xb
    sums = jnp.zeros_like(xb)
    for back in range(1, max(POOL_WINDOWS)):
        run = run + pool_ref[POOL_BUF - back]
        for gi, win in enumerate(POOL_WINDOWS):
            if back == win - 1:
                sums = jnp.where(grp == gi, run, sums)
    win = jnp.left_shift(2, grp)
    cnt = jnp.minimum(win, pos0 + 1).astype(F32)
    d = sums / cnt - xb
    mix_ref[:, W_A:W_A + W_B] = _dot(d.astype(BF16), wpool_ref[...]) * pscale_ref[...]
    for r in range(POOL_BUF - 1):
        poolo_ref[r] = pool_ref[r + 1]
    poolo_ref[POOL_BUF - 1] = xb

    o = 2 * W_A + W_B
    bg = x[:, o:o + W_C]
    zc = x[:, o + W_C:o + 2 * W_C] * x[:, o + 2 * W_C:o + 3 * W_C]
    z0 = conv_ref[:, :W_C]
    z1 = conv_ref[:, W_C:]
    y = cw_ref[0:1, :] * z0 + cw_ref[1:2, :] * z1 + cw_ref[2:3, :] * zc
    mix_ref[:, W_A + W_B:] = bg * y
    convo_ref[:, :W_C] = z1
    convo_ref[:, W_C:] = zc


def _abc_sample(pabc, pool, conv, lng, lnb, ws0, bias, wpool, pscale, cw, layer, pos0):
    n = pabc.shape[0]
    nw = W_A + W_B + W_C
    whole = lambda shape: pl.BlockSpec(shape, lambda i: (0,) * len(shape))
    return pl.pallas_call(
        functools.partial(_abc_sample_kernel, pos0=pos0),
        out_shape=(jax.ShapeDtypeStruct((n, nw), F32),
                   jax.ShapeDtypeStruct((n, W_A), F32),
                   jax.ShapeDtypeStruct(pool.shape[1:], F32),
                   jax.ShapeDtypeStruct(conv.shape[1:], F32)),
        grid=(1,),
        in_specs=[whole(pabc.shape)] + [_lspec(z, layer) for z in (pool, conv, lng, lnb, ws0, bias, wpool, pscale, cw)],
        out_specs=(whole((n, nw)), whole((n, W_A)), whole(pool.shape[1:]), whole(conv.shape[1:])),
        name="mixer_abc_sample",
    )(*map(_operand, (pabc, pool, conv, lng, lnb, ws0, bias, wpool, pscale, cw)))


def _rwkv_inputs(xs, w0, w2, a0, a2, g2, kk_w, ka_w, seg):
    r = xs[:, 0:W_D]
    k = xs[:, W_D:2 * W_D]
    v = xs[:, 2 * W_D:3 * W_D]
    o = 3 * W_D
    dw = xs[:, o:o + R_DECAY]
    da = xs[:, o + R_DECAY:o + R_DECAY + R_AAA]
    dg = xs[:, o + R_DECAY + R_AAA:]
    w_log = -_softplus(-(w0 + _dot_hi(jnp.tanh(dw), w2))) - 0.5
    logdecay = -jnp.exp(w_log)
    a = _sigmoid(a0 + _dot_hi(da, a2))
    g = _dot_hi(_sigmoid(dg), g2)
    kk = k * kk_w
    kk = kk * lax.rsqrt(jnp.maximum(_dot_ones(kk * kk, seg), 1e-12))
    k = k * (1.0 + (a - 1.0) * ka_w)
    return r, k, v, kk, a, logdecay, g


def _rwkv_finish(o, r, k, v, g, rk_w, lnx_g, lnx_b, seg):
    inv = 1.0 / D_HEAD_DIM
    mu = _dot_ones(o, seg) * inv
    oc = o - mu
    var = _dot_ones(oc * oc, seg) * inv
    on = oc * lax.rsqrt(var + GN_EPS) * lnx_g + lnx_b
    bonus = _dot_ones(r * k * rk_w, seg) * v
    return (on + bonus) * g


def _bdot(a, b, dims=NN):
    return _dot(a.astype(BF16), b.astype(BF16), dims)


def _head_cols(x, hd):
    return x[:, hd * D_HEAD_DIM:(hd + 1) * D_HEAD_DIM]


HEAD_PAIRS = W_D // LANES


def _pair_diag(y, low):
    zero = jnp.zeros_like(y)
    return jnp.concatenate([jnp.where(low, y, zero), jnp.where(low, zero, y)], axis=0)


def _wkv_precompute(qt, rt, bt, kt, bbar, kbar, vm, nc):
    c = WKV_CHUNK
    n = D_HEAD_DIM
    probs = [(ch, pr) for ch in range(nc) for pr in range(HEAD_PAIRS)]
    cut = lambda x, p: x[p[0] * c:(p[0] + 1) * c, p[1] * LANES:(p[1] + 1) * LANES]
    qt, rt, bt, kt, bbar, kbar, vm = (x.astype(BF16) for x in (qt, rt, bt, kt, bbar, kbar, vm))
    ri = lax.broadcasted_iota(jnp.int32, (c, LANES), 0)
    li = lax.broadcasted_iota(jnp.int32, (c, LANES), 1)
    low = li < n
    strict = ri > li % n
    incl = ri >= li % n
    eye = jnp.where(ri == li % n, 1.0, 0.0)
    diag = lambda y: _pair_diag(y, low)
    halves = lambda x: jnp.where(low, x[:n], x[n:])

    qs = {p: cut(qt, p) for p in probs}
    vd = {p: diag(cut(vm, p)) for p in probs}
    aa = {p: _dot(jnp.concatenate([qs[p], cut(rt, p)], axis=0),
                  jnp.concatenate([diag(cut(bt, p)), diag(cut(kt, p))], axis=0), NT) for p in probs}
    lk = {p: jnp.where(strict, aa[p][:c, LANES:], 0.0).astype(BF16) for p in probs}
    ab = {p: jnp.where(incl, aa[p][c:, :LANES], 0.0).astype(BF16) for p in probs}
    ak = {p: jnp.where(incl, aa[p][c:, LANES:], 0.0).astype(BF16) for p in probs}
    npow = {p: jnp.where(strict, -aa[p][:c, :LANES], 0.0).astype(BF16) for p in probs}
    tinv = {p: eye + npow[p].astype(F32) for p in probs}
    lv = {p: _dot(jnp.concatenate([lk[p], ak[p]], axis=0), vd[p]) for p in probs}
    zp = {p: lv[p][:c].astype(BF16) for p in probs}
    npow = {p: _dot(npow[p], diag(npow[p])).astype(BF16) for p in probs}
    for _ in range(int(math.log2(c)) - 2):
        both = {p: _dot(jnp.concatenate([tinv[p].astype(BF16), npow[p]], axis=0), diag(npow[p])) for p in probs}
        tinv = {p: tinv[p] + both[p][:c] for p in probs}
        npow = {p: both[p][c:].astype(BF16) for p in probs}
    tinv = {p: (tinv[p] + _dot(tinv[p].astype(BF16), diag(npow[p]))).astype(BF16) for p in probs}
    tq = {p: _dot(tinv[p], jnp.concatenate([diag(qs[p]), diag(zp[p])], axis=1)) for p in probs}
    qh = {p: tq[p][:, :LANES].astype(BF16) for p in probs}
    zn = {p: (-tq[p][:, LANES:]).astype(BF16) for p in probs}
    abq = {p: _dot(ab[p], jnp.concatenate([diag(qh[p]), diag(zn[p])], axis=1)) for p in probs}
    rh = {p: (cut(rt, p).astype(F32) - abq[p][:, :LANES]).astype(BF16) for p in probs}
    pv = {p: abq[p][:, LANES:] + lv[p][c:] for p in probs}
    gt = {p: halves(-_dot(qh[p], cut(bbar, p), TN)).astype(BF16) for p in probs}
    cst = {p: halves(_dot(jnp.concatenate([cut(vm, p), zn[p]], axis=0),
                          jnp.concatenate([cut(kbar, p), cut(bbar, p)], axis=0), TN)) for p in probs}
    return rh, pv, gt, cst


def _wkv_scan_steps(s_ref, o_ref, pre, gam, nc):
    c = WKV_CHUNK
    rh, pv, gt, cst = pre
    low = lax.broadcasted_iota(jnp.int32, (D_HEAD_DIM, LANES), 1) < D_HEAD_DIM
    for ch in range(nc):
        for pr in range(HEAD_PAIRS):
            p = (ch, pr)
            s0 = s_ref[pr]
            s0b = s0.astype(BF16)
            o_ref[ch * c:(ch + 1) * c, pr * LANES:(pr + 1) * LANES] = _dot(rh[p], _pair_diag(s0b, low), NT) + pv[p]
            s_ref[pr] = (s0 * gam[ch][:, pr * LANES:(pr + 1) * LANES] + _dot(s0b, _pair_diag(gt[p], low))
                         + cst[p])
        yield


def _rwkv_prompt_kernel(pd_ref, mu_ref, w0_ref, w2_ref, a0_ref, a2_ref, g2_ref, kk_ref, ka_ref, rk_ref,
                        lg_ref, lb_ref, q8_ref, ck_ref, cv_ref, yd_ref, shift_ref, wkv_ref, o8_ref,
                        car_ref, s_ref, o_ref, *, tt, bb):
    t = pl.program_id(1)
    nt = pl.num_programs(1)

    @pl.when(t == 0)
    def _():
        car_ref[...] = jnp.zeros_like(car_ref)
        s_ref[...] = jnp.zeros_like(s_ref)

    pd = pd_ref[0]
    ext = jnp.concatenate([car_ref[...], pd], axis=0)
    prev = pltpu.roll(ext, 1, 0)[ROW_CARRY:]
    car_ref[...] = ext[tt:]
    xs = pd + (prev - pd) * mu_ref[...]
    seg = _head_ones(W_D, D_HEAD_DIM)
    r, k, v, kk, a, ld, g = _rwkv_inputs(xs, w0_ref[...], w2_ref[...], a0_ref[...], a2_ref[...],
                                         g2_ref[...], kk_ref[...], ka_ref[...], seg)
    c = WKV_CHUNK
    nc = tt // c
    tri = jnp.where(lax.broadcasted_iota(jnp.int32, (c, c), 0) >= lax.broadcasted_iota(jnp.int32, (c, c), 1),
                    1.0, 0.0).astype(BF16)
    cs_chunks = [_ones_dot(tri, ld[ch * c:(ch + 1) * c]) for ch in range(nc)]
    cs = jnp.concatenate(cs_chunks, axis=0)
    cs_end = [x[c - 1:c] for x in cs_chunks]
    cs_last = jnp.concatenate([jnp.broadcast_to(x, (c, W_D)) for x in cs_end], axis=0)
    e_neg = jnp.exp(-cs)
    e_tail = jnp.exp(cs_last - cs)
    b = kk * a
    pre = _wkv_precompute(kk * jnp.exp(cs - ld), r * jnp.exp(cs), b * e_neg, k * e_neg, b * e_tail, k * e_tail, v, nc)
    _interleave(_wkv_scan_steps(s_ref, o_ref, pre, [jnp.exp(x) for x in cs_end], nc),
                _attn_sample_steps(q8_ref, ck_ref, cv_ref, o8_ref, bb))
    yd_ref[0] = _rwkv_finish(o_ref[...], r, k, v, g, rk_ref[...], lg_ref[...], lb_ref[...], seg).astype(yd_ref.dtype)

    @pl.when(t == nt - 1)
    def _():
        shift_ref[0] = pd[tt - 1:tt]
        for hd in range(D_HEADS):
            wkv_ref[0, hd] = _head_cols(s_ref[hd // 2], hd % 2)


def _rwkv_prompt(pd, params, q, cache_k, cache_v, layer, tt):
    bn, t, _ = pd.shape
    nt = t // tt
    n = q.shape[0]
    bb = n // (bn * nt)
    assert bb * bn * nt == n
    nbytes = 2 * tt * D_PROJ * 4 + 40 * tt * W_D * 4 + 4 * bb * MEM_ROWS * LANES * 4 + 2 * bb * MEM_ROWS * LANES * 2
    req = pl.BlockSpec((bb, SUBLANES, LANES), lambda i, j: (i * nt + j, 0, 0))
    cache_spec = pl.BlockSpec((None, bb, MEM_ROWS, LANES), lambda i, j: (layer, i * nt + j, 0, 0))
    yd, shift, wkv, o8 = pl.pallas_call(
        functools.partial(_rwkv_prompt_kernel, tt=tt, bb=bb),
        out_shape=(jax.ShapeDtypeStruct((bn, t, W_D), BF16),
                   jax.ShapeDtypeStruct((bn, 1, D_PROJ), F32),
                   jax.ShapeDtypeStruct((bn, D_HEADS, D_HEAD_DIM, D_HEAD_DIM), F32),
                   jax.ShapeDtypeStruct((n, SUBLANES, LANES), F32)),
        grid=(bn, nt),
        in_specs=[pl.BlockSpec((1, tt, D_PROJ), lambda i, j: (i, j, 0))] + [_lspec(z, layer) for z in params]
                 + [req, cache_spec, cache_spec],
        out_specs=(pl.BlockSpec((1, tt, W_D), lambda i, j: (i, j, 0)),
                   pl.BlockSpec((1, 1, D_PROJ), lambda i, j: (i, 0, 0)),
                   pl.BlockSpec((1, D_HEADS, D_HEAD_DIM, D_HEAD_DIM), lambda i, j: (i, 0, 0, 0)),
                   req),
        scratch_shapes=[pltpu.VMEM((ROW_CARRY, D_PROJ), F32),
                        pltpu.VMEM((HEAD_PAIRS, D_HEAD_DIM, LANES), F32),
                        pltpu.VMEM((tt, W_D), F32)],
        compiler_params=_params(("arbitrary", "arbitrary"), nbytes),
        name="rwkv_prompt",
    )(pd, *map(_operand, params), _to_pair_rows(q), _cache_rows_view(cache_k), _cache_rows_view(cache_v))
    return yd, shift, wkv, _from_pair_rows(o8)


def _rwkv_sample_kernel(pd_ref, sh_ref, st_ref, mu_ref, w0_ref, w2_ref, a0_ref, a2_ref, g2_ref, kk_ref,
                        ka_ref, rk_ref, lg_ref, lb_ref, yd_ref, so_ref, rows_ref, cols_ref, ot_ref):
    h = pl.program_id(0)
    n = D_HEAD_DIM

    @pl.when(h == 0)
    def _():
        pd = pd_ref[...]
        xs = pd + (sh_ref[...] - pd) * mu_ref[...]
        seg = _head_ones(W_D, D_HEAD_DIM)
        r, k, v, kk, a, ld, g = _rwkv_inputs(xs, w0_ref[...], w2_ref[...], a0_ref[...], a2_ref[...],
                                             g2_ref[...], kk_ref[...], ka_ref[...], seg)
        for j, x in enumerate((r, k, v, g)):
            rows_ref[j] = x
        for j, x in enumerate((kk, jnp.exp(ld), kk * a, k, r, v)):
            cols_ref[j] = x.T

    base = pl.multiple_of(h * n, n)
    kap, dec, bvec, kvec, rvec = (cols_ref[j, pl.ds(base, n), :] for j in range(5))
    for vi in range(n):
        s = st_ref[0, vi]
        u = -jnp.sum(s * kap, axis=0, keepdims=True)
        s = s * dec + u * bvec + cols_ref[5, pl.ds(base + vi, 1), :] * kvec
        so_ref[0, vi] = s
        ot_ref[pl.ds(base + vi, 1), :] = jnp.sum(s * rvec, axis=0, keepdims=True)

    @pl.when(h == pl.num_programs(0) - 1)
    def _():
        seg = _head_ones(W_D, D_HEAD_DIM)
        yd_ref[...] = _rwkv_finish(ot_ref[...].T, rows_ref[0], rows_ref[1], rows_ref[2], rows_ref[3],
                                   rk_ref[...], lg_ref[...], lb_ref[...], seg)


def _rwkv_sample(pd, shift, state, params, layer):
    n = pd.shape[0]
    sblock = (1, D_HEAD_DIM, D_HEAD_DIM, n)
    return pl.pallas_call(
        _rwkv_sample_kernel,
        out_shape=(jax.ShapeDtypeStruct((n, W_D), F32), jax.ShapeDtypeStruct(state.shape[1:], F32)),
        grid=(D_HEADS,),
        in_specs=[pl.BlockSpec((n, D_PROJ), lambda i: (0, 0)), _lspec(shift, layer),
                  pl.BlockSpec((None,) + sblock, lambda i: (layer, i, 0, 0, 0))]
                 + [_lspec(z, layer) for z in params],
        out_specs=(pl.BlockSpec((n, W_D), lambda i: (0, 0)), pl.BlockSpec(sblock, lambda i: (i, 0, 0, 0))),
        scratch_shapes=[pltpu.VMEM((4, n, W_D), F32), pltpu.VMEM((6, W_D, n), F32), pltpu.VMEM((W_D, n), F32)],
        compiler_params=pltpu.CompilerParams(dimension_semantics=("arbitrary",)),
        name="rwkv_sample",
    )(pd, shift, state, *map(_operand, params))


def _block_diag(w):
    gn, n, _ = w.shape
    eye = jnp.eye(gn, dtype=w.dtype)
    return (eye[:, None, :, None] * w[:, :, None, :]).reshape(gn * n, gn * n)


def kernel(x_prompt, x_sample, mem_prompt, cache_mem_k, cache_mem_v, state_pool, state_conv, state_shift, state_wkv,
           w_in, mu_d, ln_v_g, ln_v_b, ws_chunk, b_chunk, w_pool, pool_scale, conv_w,
           rwkv_w0, rwkv_w2, rwkv_a0, rwkv_a2, rwkv_g2, rwkv_k_k, rwkv_k_a, rwkv_r_k, rwkv_lnx_g, rwkv_lnx_b,
           w_out, ln1_g, ln1_b, w_xq, w_xk, w_xv, w_xo, ln2_g, ln2_b, ffn_w1, ffn_w3, ffn_w2, ln3_g, ln3_b):
    bp, t_p, d = x_prompt.shape
    ns, t_s, _ = x_sample.shape
    depth = w_in.shape[0]
    assert d == D_MODEL and t_s == 1 and t_p % CHUNK == 0 and w_in.shape[2] == PROJ
    alpha = (2 * depth) ** 0.25
    mp = bp * t_p
    nw = W_A + W_B + W_C
    row = _pack_rows(dict(
        ln_v_g=ln_v_g, ln_v_b=ln_v_b, pool_scale=pool_scale, mu_d=mu_d, rwkv_w0=rwkv_w0, rwkv_a0=rwkv_a0,
        rwkv_k_k=rwkv_k_k, rwkv_k_a=rwkv_k_a, rwkv_r_k=rwkv_r_k.reshape(depth, W_D), rwkv_lnx_g=rwkv_lnx_g,
        rwkv_lnx_b=rwkv_lnx_b, ln1_g=ln1_g, ln1_b=ln1_b, ln2_g=ln2_g, ln2_b=ln2_b, ln3_g=ln3_g, ln3_b=ln3_b))

    w_in_b, w_out_b = w_in, w_out
    w_xq_b, w_xk_b, w_xv_b, w_xo_b = w_xq, w_xk, w_xv, w_xo
    w1_b, w3_b, w2_b = ffn_w1, ffn_w3, ffn_w2
    ws_flat = ws_chunk.reshape(depth, A_HEADS * CHUNK, CHUNK)
    bias_full = jnp.repeat(jnp.swapaxes(b_chunk, 1, 2), A_HEAD_DIM, axis=2)
    ws0 = jnp.repeat(ws_chunk[:, :, 0, 0], A_HEAD_DIM, axis=1).reshape(depth, 1, W_A)
    wpool_bd = jnp.stack([_block_diag(w_pool[l]) for l in range(depth)]).astype(BF16)
    abc_w = (row['ln_v_g'], row['ln_v_b'])
    abc_w2 = (wpool_bd, row['pool_scale'], conv_w)
    rw = (row['mu_d'], row['rwkv_w0'], rwkv_w2, row['rwkv_a0'], rwkv_a2, rwkv_g2,
          row['rwkv_k_k'], row['rwkv_k_a'], row['rwkv_r_k'], row['rwkv_lnx_g'], row['rwkv_lnx_b'])
    ln1, ln2, ln3 = (row['ln1_g'], row['ln1_b']), (row['ln2_g'], row['ln2_b']), (row['ln3_g'], row['ln3_b'])
    pool_view = jnp.swapaxes(state_pool, 1, 2)
    wkv_view = state_wkv.transpose(0, 2, 3, 4, 1)
    conv_view = state_conv.reshape(depth, ns, (CONV_W - 1) * W_C)
    shift_view = state_shift.reshape(depth, ns, D_PROJ)

    hp = x_prompt
    hs = x_sample.reshape(ns, d)
    mem = mem_prompt.reshape(bp * MEM_LEN, d)
    mk_all, mv_all, mk_rows, mv_rows = _mem_kv(mem, w_xk_b, w_xv_b, 512)
    mk_all = mk_all.reshape(depth, bp, MEM_LEN, d)
    mv_all = mv_all.reshape(depth, bp, MEM_LEN, d)
    outs = [[] for _ in range(10)]
    for l in range(depth):
        pd, mix, v_last, pool_p, conv_p, *ffn_b = _proj_abc_prompt(hp, w_in_b, *abc_w, ws_flat, bias_full, *abc_w2,
                                                                   (w1_b, w3_b, w2_b), l, tt=1024)

        pabc_s, pd_s = _proj(hs, w_in_b, l, ns)
        mix_s, v_s, pool_s, conv_s = _abc_sample(pabc_s, pool_view, conv_view, *abc_w, ws0, bias_full, *abc_w2,
                                                 l, pos0=PAST_LEN)
        yd_s, wkv_s = _rwkv_sample(pd_s, shift_view, wkv_view, rw, l)
        hs = _mm_res_ln([mix_s, yd_s], w_out_b, l, hs, *ln1, tm=ns, alpha=alpha, name="out_proj_s")
        q_s = _mm(hs, w_xq_b, l, ns, "q_s")

        yd, shift_p, wkv_p, o_s = _rwkv_prompt(pd, rw, q_s, cache_mem_k, cache_mem_v, l, tt=512)
        hp = _attn_prompt(mix, yd, hp, mk_all, mv_all, w_out_b, *ln1, w_xq_b, w_xo_b, *ln2, l, tq=512, alpha=alpha)
        hp = _ffn(hp.reshape(mp, d), *ffn_b, *ln3, l, tm=1024, tf=256, alpha=alpha, name="ffn").reshape(bp, t_p, d)
        for lst, val in zip(outs[:5], (v_last, pool_p, conv_p, shift_p, wkv_p)):
            lst.append(val)

        hs = _mm_res_ln([o_s], w_xo_b, l, hs, *ln2, tm=ns, alpha=alpha, name="xo_s")
        hs = _ffn(hs, *ffn_b, *ln3, l, tm=ns, tf=256, alpha=alpha, name="ffn_s")
        for lst, val in zip(outs[5:], (v_s.reshape(ns, 1, W_A), pool_s,
                                       conv_s.reshape(ns, CONV_W - 1, W_C), pd_s.reshape(ns, 1, D_PROJ), wkv_s)):
            lst.append(val)

    stacked = [jnp.stack(o) for o in outs]
    stacked[6] = jnp.swapaxes(stacked[6], 1, 2)
    stacked[9] = stacked[9].transpose(0, 4, 1, 2, 3)
    return ((hp, hs.reshape(ns, 1, d)) + tuple(stacked[:5]) + (_cache_from_rows(mk_rows, bp), _cache_from_rows(mv_rows, bp))
            + tuple(stacked[5:]))
```

```python
import functools
import math
from typing import NamedTuple

import jax
import jax.numpy as jnp
from jax import lax
from jax.experimental import pallas as pl
from jax.experimental.pallas import tpu as pltpu

F32 = jnp.float32
BF16 = jnp.bfloat16

D_MODEL = 1024
W_A = 256
W_B = 256
W_C = 256
W_D = 256
A_HEADS = 4
A_HEAD_DIM = W_A // A_HEADS
CHUNK = 128
POOL_WINDOWS = (2, 4, 8, 16)
POOL_GROUP = W_B // len(POOL_WINDOWS)
POOL_BUF = max(POOL_WINDOWS) - 1
CONV_W = 3
D_HEAD_DIM = 64
D_HEADS = W_D // D_HEAD_DIM
R_DECAY = 32
R_AAA = 32
R_GATE = 64
D_PROJ = 3 * W_D + R_DECAY + R_AAA + R_GATE
N_ABC = 2 * W_A + W_B + 3 * W_C
PROJ = N_ABC + D_PROJ
MEM_LEN = 256
X_HEADS = 4
X_HEAD_DIM = D_MODEL // X_HEADS
D_FF = int(math.ceil(8 * D_MODEL / 3 / 256)) * 256
PAST_LEN = 16384
LN_EPS = 1e-5
GN_EPS = 64e-5

WKV_CHUNK = 64
POOL_CARRY = 24
ROW_CARRY = 8
V7X_VMEM_BYTES = 64 * 1024 * 1024
VMEM_CAP = V7X_VMEM_BYTES - 8 * 1024 * 1024

NN = (((1,), (0,)), ((), ()))
NT = (((1,), (1,)), ((), ()))
TN = (((0,), (0,)), ((), ()))


def _vmem_limit(nbytes):
    return int(min(VMEM_CAP, max(32 * 1024 * 1024, 2 * nbytes)))


def _params(sem, nbytes):
    return pltpu.CompilerParams(dimension_semantics=sem, vmem_limit_bytes=_vmem_limit(nbytes))


class _Row(NamedTuple):
    arr: jax.Array
    start: int
    width: int


def _pack_rows(vectors):
    names = sorted(vectors, key=lambda k: -vectors[k].shape[1])
    pieces, where, start = [], {}, 0
    for name in names:
        v = vectors[name]
        width = v.shape[1]
        gap = -start % width
        if gap:
            pieces.append(jnp.zeros((v.shape[0], gap), v.dtype))
        where[name] = (start + gap, width)
        pieces.append(v)
        start += gap + width
    packed = jnp.concatenate(pieces, axis=1).reshape(pieces[0].shape[0], 1, start)
    return {name: _Row(packed, *where[name]) for name in names}


def _operand(z):
    return z.arr if isinstance(z, _Row) else z


def _lspec(arr, layer):
    if isinstance(arr, _Row):
        assert arr.start % arr.width == 0
        return pl.BlockSpec((None, 1, arr.width), lambda *_: (layer, 0, arr.start // arr.width))
    tail = arr.shape[1:]
    zeros = (0,) * len(tail)
    return pl.BlockSpec((None,) + tail, lambda *_: (layer,) + zeros)


def _dot(a, b, dims=NN):
    return lax.dot_general(a, b, dims, preferred_element_type=F32)


def _split2(a):
    hi = a.astype(BF16)
    lo = (a - hi.astype(F32)).astype(BF16)
    return hi, lo


def _dot_hi(a, b, dims=NN):
    ah, al = _split2(a)
    bh, bl = _split2(b)
    return _dot(ah, bh, dims) + _dot(ah, bl, dims) + _dot(al, bh, dims)


def _dot_ones(x, ones_bf16, dims=NN):
    hi, lo = _split2(x)
    return _dot(hi, ones_bf16, dims) + _dot(lo, ones_bf16, dims)


def _ones_dot(ones_bf16, x):
    hi = x.astype(BF16)
    r1 = x - hi.astype(F32)
    mid = r1.astype(BF16)
    lo = (r1 - mid.astype(F32)).astype(BF16)
    return _dot(ones_bf16, hi) + _dot(ones_bf16, mid) + _dot(ones_bf16, lo)


def _ln(x, g, b, eps=LN_EPS):
    mu = jnp.mean(x, axis=-1, keepdims=True)
    xc = x - mu
    var = jnp.mean(xc * xc, axis=-1, keepdims=True)
    return xc * lax.rsqrt(var + eps) * g + b


def _gelu(x):
    c = math.sqrt(2.0 / math.pi)
    return x * (0.5 * (1.0 + jnp.tanh(c * (x + 0.044715 * (x * x * x)))))


def _sigmoid(x):
    return 1.0 / (1.0 + jnp.exp(-x))


def _softplus(x):
    return jnp.maximum(x, 0.0) + jnp.log(1.0 + jnp.exp(-jnp.abs(x)))


def _interleave(*staged):
    live = list(staged)
    while live:
        for steps in list(live):
            if next(steps, StopIteration) is StopIteration:
                live.remove(steps)


def _head_ones(n, group):
    r = lax.broadcasted_iota(jnp.int32, (n, n), 0) // group
    c = lax.broadcasted_iota(jnp.int32, (n, n), 1) // group
    return jnp.where(r == c, 1.0, 0.0).astype(BF16)


def _proj_kernel(x_ref, w_ref, oabc_ref, od_ref):
    y = _dot(x_ref[...].astype(BF16), w_ref[...].astype(BF16))
    oabc_ref[...] = y[:, :N_ABC]
    od_ref[...] = y[:, N_ABC:]


def _proj(x, w, layer, tm):
    m, k = x.shape
    nbytes = 2 * (tm * k * 4 + k * PROJ * 4 + tm * PROJ * 4) + tm * PROJ * 4 + k * PROJ * 2
    return pl.pallas_call(
        _proj_kernel,
        out_shape=(jax.ShapeDtypeStruct((m, N_ABC), F32), jax.ShapeDtypeStruct((m, D_PROJ), F32)),
        grid=(m // tm,),
        in_specs=[pl.BlockSpec((tm, k), lambda i: (i, 0)), _lspec(w, layer)],
        out_specs=(pl.BlockSpec((tm, N_ABC), lambda i: (i, 0)), pl.BlockSpec((tm, D_PROJ), lambda i: (i, 0))),
        compiler_params=_params(("parallel",), nbytes),
        name="proj",
    )(x, w)


def _mm_kernel(x_ref, w_ref, o_ref):
    o_ref[...] = _dot(x_ref[...].astype(BF16), w_ref[...].astype(BF16))


def _mm(x, w, layer, tm, name):
    m, k = x.shape
    n = w.shape[2]
    nbytes = 2 * (tm * k * 4 + k * n * 4 + tm * n * 4) + tm * n * 4 + k * n * 2
    return pl.pallas_call(
        _mm_kernel,
        out_shape=jax.ShapeDtypeStruct((m, n), F32),
        grid=(m // tm,),
        in_specs=[pl.BlockSpec((tm, k), lambda i: (i, 0)), _lspec(w, layer)],
        out_specs=pl.BlockSpec((tm, n), lambda i: (i, 0)),
        compiler_params=_params(("parallel",), nbytes),
        name=name,
    )(x, w)


def _mem_kv_kernel(x_ref, wk_ref, wv_ref, k_ref, v_ref, kt_ref, vt_ref, wkb_ref, wvb_ref):
    @pl.when(pl.program_id(1) == 0)
    def _():
        wkb_ref[...] = wk_ref[...].astype(BF16)
        wvb_ref[...] = wv_ref[...].astype(BF16)

    xb = x_ref[...].astype(BF16)
    for w_ref, o_ref, t_ref in ((wkb_ref, k_ref, kt_ref), (wvb_ref, v_ref, vt_ref)):
        y = _dot(xb, w_ref[...])
        o_ref[...] = y
        rows = y.shape[0]
        t_ref[...] = jnp.swapaxes(y.reshape(rows, X_HEADS, LANE_TILES, LANES), 1, 2).reshape(rows, SUBLANES, LANES)


def _mem_kv(mem, wk, wv, tm):
    m, k = mem.shape
    nl, _, n = wk.shape
    nbytes = 2 * (tm * k * 4 + 2 * k * n * 4 + 4 * tm * n * 4) + 2 * tm * n * 4 + 2 * k * n * 2
    wspec = pl.BlockSpec((None, k, n), lambda l, i: (l, 0, 0))
    ospec = pl.BlockSpec((None, tm, n), lambda l, i: (l, i, 0))
    tspec = pl.BlockSpec((None, tm, SUBLANES, LANES), lambda l, i: (l, i, 0, 0))
    shape = jax.ShapeDtypeStruct((nl, m, n), F32)
    tshape = jax.ShapeDtypeStruct((nl, m, SUBLANES, LANES), F32)
    return pl.pallas_call(
        _mem_kv_kernel,
        out_shape=(shape, shape, tshape, tshape),
        grid=(nl, m // tm),
        in_specs=[pl.BlockSpec((tm, k), lambda l, i: (i, 0)), wspec, wspec],
        out_specs=(ospec, ospec, tspec, tspec),
        scratch_shapes=[pltpu.VMEM((k, n), BF16), pltpu.VMEM((k, n), BF16)],
        compiler_params=_params(("arbitrary", "arbitrary"), nbytes),
        name="mem_kv",
    )(mem, wk, wv)


def _mm_res_ln_kernel(*refs, n_in, alpha):
    xs = refs[:n_in]
    ws = refs[n_in:2 * n_in]
    h_ref, g_ref, b_ref, o_ref = refs[2 * n_in:]
    y = _dot(xs[0][...].astype(BF16), ws[0][...].astype(BF16))
    for x_ref, w_ref in zip(xs[1:], ws[1:]):
        y = y + _dot(x_ref[...].astype(BF16), w_ref[...].astype(BF16))
    o_ref[...] = _ln(alpha * h_ref[...] + y, g_ref[...], b_ref[...])


def _mm_res_ln(xs, w, layer, h, g, b, tm, alpha, name, w_layer=None):
    m, n = h.shape
    w_layer = layer if w_layer is None else w_layer
    nbytes = 2 * sum(tm * x.shape[1] * 4 + x.shape[1] * n * 4 for x in xs) + 5 * tm * n * 4 + w.shape[1] * n * 2
    in_specs = [pl.BlockSpec((tm, x.shape[1]), lambda i: (i, 0)) for x in xs]
    start = 0
    for x in xs:
        width = x.shape[1]
        assert start % width == 0
        in_specs.append(pl.BlockSpec((None, width, n), lambda i, blk=start // width: (w_layer, blk, 0)))
        start += width
    assert start == w.shape[1]
    in_specs += [pl.BlockSpec((tm, n), lambda i: (i, 0)), _lspec(g, layer), _lspec(b, layer)]
    return pl.pallas_call(
        functools.partial(_mm_res_ln_kernel, n_in=len(xs), alpha=alpha),
        out_shape=jax.ShapeDtypeStruct((m, n), F32),
        grid=(m // tm,),
        in_specs=in_specs,
        out_specs=pl.BlockSpec((tm, n), lambda i: (i, 0)),
        compiler_params=_params(("parallel",), nbytes),
        name=name,
    )(*xs, *([w] * len(xs)), h, _operand(g), _operand(b))


def _ffn_kernel(x_ref, w1_ref, w3_ref, w2_ref, g_ref, b_ref, o_ref, xb_ref, acc_ref, *, alpha):
    j = pl.program_id(1)

    @pl.when(j == 0)
    def _():
        xb_ref[...] = x_ref[...].astype(BF16)
        acc_ref[...] = jnp.zeros_like(acc_ref)

    xb = xb_ref[...]
    h1 = _dot(xb, w1_ref[...])
    h3 = _dot(xb, w3_ref[...])
    a = (h1 * _sigmoid(h1) * h3).astype(BF16)
    acc_ref[...] += _dot(a, w2_ref[...])

    @pl.when(j == pl.num_programs(1) - 1)
    def _():
        o_ref[...] = _ln(alpha * x_ref[...] + acc_ref[...], g_ref[...], b_ref[...])


def _ffn(x, w1, w3, w2, g, b, layer, tm, tf, alpha, name):
    m, d = x.shape
    nbytes = 4 * tm * d * 4 + tm * d * 2 + tm * d * 4 + 2 * 3 * d * tf * 2 + 3 * tm * tf * 4
    return pl.pallas_call(
        functools.partial(_ffn_kernel, alpha=alpha),
        out_shape=jax.ShapeDtypeStruct((m, d), F32),
        grid=(m // tm, D_FF // tf),
        in_specs=[pl.BlockSpec((tm, d), lambda i, j: (i, 0)),
                  pl.BlockSpec((None, d, tf), lambda i, j: (0, 0, j)),
                  pl.BlockSpec((None, d, tf), lambda i, j: (0, 0, j)),
                  pl.BlockSpec((None, tf, d), lambda i, j: (0, j, 0)),
                  _lspec(g, layer), _lspec(b, layer)],
        out_specs=pl.BlockSpec((tm, d), lambda i, j: (i, 0)),
        scratch_shapes=[pltpu.VMEM((tm, d), BF16), pltpu.VMEM((tm, d), F32)],
        compiler_params=_params(("parallel", "arbitrary"), nbytes),
        name=name,
    )(x, w1, w3, w2, _operand(g), _operand(b))


def _softmax_rows(s):
    m = jnp.max(s, axis=-1, keepdims=True)
    e = jnp.exp(s - m)
    return e / jnp.sum(e, axis=-1, keepdims=True)


ATTN_PIECES = 2


def _attn_prompt_kernel(mix_ref, yd_ref, h_ref, mk_ref, mv_ref, wma_ref, wmd_ref, g1_ref, b1_ref,
                        wq_ref, wo_ref, g_ref, b_ref, o_ref, ob_ref, wmb_ref, wqb_ref, wob_ref, *, alpha):
    nw = wma_ref.shape[0]

    @pl.when((pl.program_id(0) == 0) & (pl.program_id(1) == 0))
    def _():
        wmb_ref[:nw, :] = wma_ref[...].astype(BF16)
        wmb_ref[nw:, :] = wmd_ref[...].astype(BF16)
        wqb_ref[...] = wq_ref[...].astype(BF16)
        wob_ref[...] = wo_ref[...].astype(BF16)

    rows = h_ref.shape[1] // ATTN_PIECES
    scale = X_HEAD_DIM ** -0.5
    sls = [slice(hd * X_HEAD_DIM, (hd + 1) * X_HEAD_DIM) for hd in range(X_HEADS)]
    kb = [mk_ref[0, :, sl].astype(BF16) for sl in sls]
    vb = [mv_ref[0, :, sl].astype(BF16) for sl in sls]
    y, h, q, sc = {}, {}, {}, {}

    def project(p, rs):
        y[p] = (_dot(mix_ref[0, rs, :].astype(BF16), wmb_ref[:nw, :])
                + _dot(yd_ref[0, rs, :].astype(BF16), wmb_ref[nw:, :]))

    def query(p, rs):
        h[p] = _ln(alpha * h_ref[0, rs, :] + y[p], g1_ref[...], b1_ref[...])
        q[p] = _dot(h[p].astype(BF16), wqb_ref[...]).astype(BF16)

    def scores(p, rs):
        sc[p] = [_dot(q[p][:, sl], k, NT) * scale for sl, k in zip(sls, kb)]

    def values(p, rs):
        for sl, s, v in zip(sls, sc[p], vb):
            ob_ref[rs, sl] = _dot(_softmax_rows(s).astype(BF16), v).astype(BF16)

    def output(p, rs):
        o_ref[0, rs, :] = _ln(alpha * h[p] + _dot(ob_ref[rs, :], wob_ref[...]), g_ref[...], b_ref[...])

    stages = (project, query, scores, values, output)
    for step in range(len(stages) + ATTN_PIECES - 1):
        for p in range(ATTN_PIECES):
            if 0 <= step - p < len(stages):
                stages[step - p](p, slice(p * rows, (p + 1) * rows))


def _attn_prompt(mix, yd, h, mk, mv, w_mix, g1, b1, wq, wo, g, b, layer, tq, alpha):
    bn, t, d = h.shape
    nw = mix.shape[2]
    assert nw % W_D == 0 and w_mix.shape[1] == nw + W_D
    nbytes = (6 * tq * d * 4 + 4 * MEM_LEN * d * 4 + 6 * d * d * 4 + 3 * d * d * 2 + tq * d * 2
              + 3 * tq * d * 4 + 3 * tq * MEM_LEN * 4)
    tile = lambda n: pl.BlockSpec((1, tq, n), lambda i, j: (i, j, 0))
    return pl.pallas_call(
        functools.partial(_attn_prompt_kernel, alpha=alpha),
        out_shape=jax.ShapeDtypeStruct((bn, t, d), F32),
        grid=(bn, t // tq),
        in_specs=[tile(nw), tile(W_D), tile(d),
                  pl.BlockSpec((None, 1, MEM_LEN, d), lambda i, j: (layer, i, 0, 0)),
                  pl.BlockSpec((None, 1, MEM_LEN, d), lambda i, j: (layer, i, 0, 0)),
                  pl.BlockSpec((None, nw, d), lambda i, j: (layer, 0, 0)),
                  pl.BlockSpec((None, W_D, d), lambda i, j: (layer, nw // W_D, 0)),
                  _lspec(g1, layer), _lspec(b1, layer),
                  _lspec(wq, layer), _lspec(wo, layer), _lspec(g, layer), _lspec(b, layer)],
        out_specs=tile(d),
        scratch_shapes=[pltpu.VMEM((tq, d), BF16), pltpu.VMEM((nw + W_D, d), BF16),
                        pltpu.VMEM((d, d), BF16), pltpu.VMEM((d, d), BF16)],
        compiler_params=_params(("arbitrary", "arbitrary"), nbytes),
        name="attn_prompt",
    )(mix, yd, h, mk, mv, w_mix, w_mix, _operand(g1), _operand(b1), wq, wo, _operand(g), _operand(b))


LANES = 128
SUBLANES = 8
LANE_TILES = X_HEAD_DIM // LANES
MEM_ROWS = MEM_LEN * LANE_TILES * X_HEADS


def _cache_rows_view(cache):
    nl, n = cache.shape[:2]
    x = cache.reshape(nl, n, MEM_LEN, X_HEADS, LANE_TILES, LANES)
    return x.transpose(0, 1, 2, 4, 3, 5).reshape(nl, n, MEM_ROWS, LANES)


def _cache_from_rows(rows, n):
    nl = rows.shape[0]
    x = rows.reshape(nl, n, MEM_LEN, LANE_TILES, X_HEADS, LANES)
    return x.transpose(0, 1, 2, 4, 3, 5).reshape(nl, n, MEM_LEN, X_HEADS, X_HEAD_DIM)


def _attn_sample_steps(q_ref, k_ref, v_ref, o_ref, bb):
    scale = X_HEAD_DIM ** -0.5
    shape = (SUBLANES, MEM_ROWS)
    rowi = lax.broadcasted_iota(jnp.int32, shape, 0)
    coli = lax.broadcasted_iota(jnp.int32, shape, 1)
    valid = (coli % SUBLANES) == rowi
    raw = [_dot(q_ref[i].astype(BF16), k_ref[i].astype(BF16), NT) for i in range(bb)]
    yield
    probs = []
    for r in raw:
        r = jnp.where(valid, r, 0.0)
        other = pltpu.roll(r, X_HEADS, 0)
        other = jnp.where(rowi < X_HEADS, pltpu.roll(other, MEM_ROWS - X_HEADS, 1), pltpu.roll(other, X_HEADS, 1))
        sc = jnp.where(valid, (r + other) * scale, -jnp.inf)
        m = jnp.max(sc, axis=-1, keepdims=True)
        e = jnp.exp(sc - m)
        probs.append((e / jnp.sum(e, axis=-1, keepdims=True)).astype(BF16))
        yield
    for i in range(bb):
        o_ref[i] = _dot(probs[i], v_ref[i].astype(BF16))
        yield


def _to_pair_rows(q):
    n = q.shape[0]
    return q.reshape(n, X_HEADS, LANE_TILES, LANES).transpose(0, 2, 1, 3).reshape(n, SUBLANES, LANES)


def _from_pair_rows(o8):
    n = o8.shape[0]
    return o8.reshape(n, LANE_TILES, X_HEADS, LANES).transpose(0, 2, 1, 3).reshape(n, D_MODEL)


def _pool_window_sums(ext, tt):
    s2 = ext + pltpu.roll(ext, 1, 0)
    s4 = s2 + pltpu.roll(s2, 2, 0)
    s8 = s4 + pltpu.roll(s4, 4, 0)
    s16 = s8 + pltpu.roll(s8, 8, 0)
    grp = lax.broadcasted_iota(jnp.int32, (tt, W_B), 1) // POOL_GROUP
    lo = ext.shape[0] - tt
    return jnp.where(grp == 0, s2[lo:], jnp.where(grp == 1, s4[lo:], jnp.where(grp == 2, s8[lo:], s16[lo:])))


def _proj_abc_kernel(x_ref, w_ref, lng_ref, lnb_ref, ws_ref, bias_ref, wpool_ref, pscale_ref, cw_ref,
                     *refs, tt, n_round):
    f_refs, refs = refs[:n_round], refs[n_round:]
    pd_ref, mix_ref, vlast_ref, pool_ref, conv_ref = refs[:5]
    fb_refs = refs[5:5 + n_round]
    pcar_ref, ccar_ref, wb_ref = refs[5 + n_round:]
    t = pl.program_id(1)
    nt = pl.num_programs(1)

    @pl.when((pl.program_id(0) == 0) & (t == 0))
    def _():
        wb_ref[...] = w_ref[...].astype(BF16)

    @pl.when(t == 0)
    def _():
        pcar_ref[...] = jnp.zeros_like(pcar_ref)
        ccar_ref[...] = jnp.zeros_like(ccar_ref)

    rows = lax.broadcasted_iota(jnp.int32, (A_HEADS * CHUNK, CHUNK), 0) % CHUNK
    cols = lax.broadcasted_iota(jnp.int32, (A_HEADS * CHUNK, CHUNK), 1)
    wsm = jnp.where(rows >= cols, ws_ref[...], 0.0).astype(BF16)
    hid = lax.broadcasted_iota(jnp.int32, (CHUNK, W_A), 1) // A_HEAD_DIM
    win = jnp.left_shift(2, lax.broadcasted_iota(jnp.int32, (CHUNK, W_B), 1) // POOL_GROUP)
    rowi = lax.broadcasted_iota(jnp.int32, (CHUNK, W_B), 0)
    w = wb_ref[...]
    pcar = pcar_ref[...]
    ccar = ccar_ref[...]
    v = None
    for c in range(tt // CHUNK):
        rs = slice(c * CHUNK, (c + 1) * CHUNK)
        y = _dot(x_ref[0, rs, :].astype(BF16), w)
        pd_ref[0, rs, :] = y[:, N_ABC:]

        ga = _gelu(y[:, :2 * W_A])
        u = ga[:, :W_A]
        v = _ln(ga[:, W_A:], lng_ref[...], lnb_ref[...])
        zz = _dot(wsm, v.astype(BF16))
        z = zz[(A_HEADS - 1) * CHUNK:]
        for hd in range(A_HEADS - 2, -1, -1):
            z = jnp.where(hid == hd, zz[hd * CHUNK:(hd + 1) * CHUNK], z)
        mix_ref[0, rs, 0:W_A] = (u * (z + bias_ref[...])).astype(mix_ref.dtype)

        xb = y[:, 2 * W_A:2 * W_A + W_B]
        ext = jnp.concatenate([pcar, xb], axis=0)
        sums = _pool_window_sums(ext, CHUNK)
        pos = t * tt + c * CHUNK + rowi
        cnt = jnp.minimum(win, pos + 1).astype(F32)
        d = sums / cnt - xb
        mix_ref[0, rs, W_A:W_A + W_B] = (_dot(d.astype(BF16), wpool_ref[...]) * pscale_ref[...]).astype(mix_ref.dtype)
        pcar = ext[CHUNK:]

        o = 2 * W_A + W_B
        bg = y[:, o:o + W_C]
        zc = y[:, o + W_C:o + 2 * W_C] * y[:, o + 2 * W_C:o + 3 * W_C]
        extz = jnp.concatenate([ccar, zc], axis=0)
        conv = (cw_ref[0:1, :] * pltpu.roll(extz, 2, 0) + cw_ref[1:2, :] * pltpu.roll(extz, 1, 0)
                + cw_ref[2:3, :] * extz)
        mix_ref[0, rs, W_A + W_B:] = (bg * conv[ROW_CARRY:]).astype(mix_ref.dtype)
        ccar = extz[CHUNK:]
    pcar_ref[...] = pcar
    ccar_ref[...] = ccar
    for src_ref, dst_ref in zip(f_refs, fb_refs):
        dst_ref[...] = src_ref[...].astype(BF16)

    @pl.when(t == nt - 1)
    def _():
        vlast_ref[0] = v
        pool_ref[0] = pcar_ref[POOL_CARRY - POOL_BUF:, :]
        conv_ref[0] = ccar_ref[ROW_CARRY - (CONV_W - 1):, :]


def _proj_abc_prompt(h, w, lng, lnb, ws, bias, wpool, pscale, cw, ffn_ws, layer, tt):
    bn, t, d = h.shape
    nw = W_A + W_B + W_C
    nt = t // tt
    steps = bn * nt
    assert all(z.shape[1] % (steps * 2 * SUBLANES) == 0 for z in ffn_ws)
    f_in = [pl.BlockSpec((None, z.shape[1] // steps, z.shape[2]), lambda i, j: (layer, i * nt + j, 0)) for z in ffn_ws]
    f_out = [pl.BlockSpec((None, z.shape[1] // steps, z.shape[2]), lambda i, j: (0, i * nt + j, 0)) for z in ffn_ws]
    f_shapes = [jax.ShapeDtypeStruct((1,) + z.shape[1:], BF16) for z in ffn_ws]
    nbytes = (2 * tt * d * 4 + 2 * d * PROJ * 4 + d * PROJ * 2 + 2 * tt * (D_PROJ + nw) * 4 + 3 * CHUNK * PROJ * 4
              + 4 * A_HEADS * CHUNK * CHUNK * 4 + 12 * sum(z.shape[1] * z.shape[2] for z in ffn_ws) // steps)
    tile = lambda n: pl.BlockSpec((1, tt, n), lambda i, j: (i, j, 0))
    last = lambda r, n: pl.BlockSpec((1, r, n), lambda i, j: (i, 0, 0))
    return pl.pallas_call(
        functools.partial(_proj_abc_kernel, tt=tt, n_round=len(ffn_ws)),
        out_shape=(jax.ShapeDtypeStruct((bn, t, D_PROJ), F32),
                   jax.ShapeDtypeStruct((bn, t, nw), BF16),
                   jax.ShapeDtypeStruct((bn, CHUNK, W_A), F32),
                   jax.ShapeDtypeStruct((bn, POOL_BUF, W_B), F32),
                   jax.ShapeDtypeStruct((bn, CONV_W - 1, W_C), F32), *f_shapes),
        grid=(bn, nt),
        in_specs=[tile(d), *[_lspec(z, layer) for z in (w, lng, lnb, ws, bias, wpool, pscale, cw)], *f_in],
        out_specs=(tile(D_PROJ), tile(nw), last(CHUNK, W_A), last(POOL_BUF, W_B), last(CONV_W - 1, W_C), *f_out),
        scratch_shapes=[pltpu.VMEM((POOL_CARRY, W_B), F32), pltpu.VMEM((ROW_CARRY, W_C), F32),
                        pltpu.VMEM((d, PROJ), BF16)],
        compiler_params=_params(("arbitrary", "arbitrary"), nbytes),
        name="proj_abc_prompt",
    )(*map(_operand, (h, w, lng, lnb, ws, bias, wpool, pscale, cw, *ffn_ws)))


def _abc_sample_kernel(x_ref, pool_ref, conv_ref, lng_ref, lnb_ref, ws0_ref, bias0_ref, wpool_ref,
                       pscale_ref, cw_ref, mix_ref, v_ref, poolo_ref, convo_ref, *, pos0):
    x = x_ref[...]
    n = x.shape[0]
    ga = _gelu(x[:, :2 * W_A])
    u = ga[:, :W_A]
    v = _ln(ga[:, W_A:], lng_ref[...], lnb_ref[...])
    v_ref[...] = v
    mix_ref[:, 0:W_A] = u * (v * ws0_ref[...] + bias0_ref[0:1, :])

    xb = x[:, 2 * W_A:2 * W_A + W_B]
    grp = lax.broadcasted_iota(jnp.int32, (n, W_B), 1) // POOL_GROUP
    run = xb
    sums = jnp.zeros_like(xb)
    for back in range(1, max(POOL_WINDOWS)):
        run = run + pool_ref[POOL_BUF - back]
        for gi, win in enumerate(POOL_WINDOWS):
            if back == win - 1:
                sums = jnp.where(grp == gi, run, sums)
    win = jnp.left_shift(2, grp)
    cnt = jnp.minimum(win, pos0 + 1).astype(F32)
    d = sums / cnt - xb
    mix_ref[:, W_A:W_A + W_B] = _dot(d.astype(BF16), wpool_ref[...]) * pscale_ref[...]
    for r in range(POOL_BUF - 1):
        poolo_ref[r] = pool_ref[r + 1]
    poolo_ref[POOL_BUF - 1] = xb

    o = 2 * W_A + W_B
    bg = x[:, o:o + W_C]
    zc = x[:, o + W_C:o + 2 * W_C] * x[:, o + 2 * W_C:o + 3 * W_C]
    z0 = conv_ref[:, :W_C]
    z1 = conv_ref[:, W_C:]
    y = cw_ref[0:1, :] * z0 + cw_ref[1:2, :] * z1 + cw_ref[2:3, :] * zc
    mix_ref[:, W_A + W_B:] = bg * y
    convo_ref[:, :W_C] = z1
    convo_ref[:, W_C:] = zc


def _abc_sample(pabc, pool, conv, lng, lnb, ws0, bias, wpool, pscale, cw, layer, pos0):
    n = pabc.shape[0]
    nw = W_A + W_B + W_C
    whole = lambda shape: pl.BlockSpec(shape, lambda i: (0,) * len(shape))
    return pl.pallas_call(
        functools.partial(_abc_sample_kernel, pos0=pos0),
        out_shape=(jax.ShapeDtypeStruct((n, nw), F32),
                   jax.ShapeDtypeStruct((n, W_A), F32),
                   jax.ShapeDtypeStruct(pool.shape[1:], F32),
                   jax.ShapeDtypeStruct(conv.shape[1:], F32)),
        grid=(1,),
        in_specs=[whole(pabc.shape)] + [_lspec(z, layer) for z in (pool, conv, lng, lnb, ws0, bias, wpool, pscale, cw)],
        out_specs=(whole((n, nw)), whole((n, W_A)), whole(pool.shape[1:]), whole(conv.shape[1:])),
        name="mixer_abc_sample",
    )(*map(_operand, (pabc, pool, conv, lng, lnb, ws0, bias, wpool, pscale, cw)))


def _rwkv_inputs(xs, w0, w2, a0, a2, g2, kk_w, ka_w, seg):
    r = xs[:, 0:W_D]
    k = xs[:, W_D:2 * W_D]
    v = xs[:, 2 * W_D:3 * W_D]
    o = 3 * W_D
    dw = xs[:, o:o + R_DECAY]
    da = xs[:, o + R_DECAY:o + R_DECAY + R_AAA]
    dg = xs[:, o + R_DECAY + R_AAA:]
    w_log = -_softplus(-(w0 + _dot_hi(jnp.tanh(dw), w2))) - 0.5
    logdecay = -jnp.exp(w_log)
    a = _sigmoid(a0 + _dot_hi(da, a2))
    g = _dot_hi(_sigmoid(dg), g2)
    kk = k * kk_w
    kk = kk * lax.rsqrt(jnp.maximum(_dot_ones(kk * kk, seg), 1e-12))
    k = k * (1.0 + (a - 1.0) * ka_w)
    return r, k, v, kk, a, logdecay, g


def _rwkv_finish(o, r, k, v, g, rk_w, lnx_g, lnx_b, seg):
    inv = 1.0 / D_HEAD_DIM
    mu = _dot_ones(o, seg) * inv
    oc = o - mu
    var = _dot_ones(oc * oc, seg) * inv
    on = oc * lax.rsqrt(var + GN_EPS) * lnx_g + lnx_b
    bonus = _dot_ones(r * k * rk_w, seg) * v
    return (on + bonus) * g


def _bdot(a, b, dims=NN):
    return _dot(a.astype(BF16), b.astype(BF16), dims)


def _head_cols(x, hd):
    return x[:, hd * D_HEAD_DIM:(hd + 1) * D_HEAD_DIM]


HEAD_PAIRS = W_D // LANES


def _pair_diag(y, low):
    zero = jnp.zeros_like(y)
    return jnp.concatenate([jnp.where(low, y, zero), jnp.where(low, zero, y)], axis=0)


def _wkv_precompute(qt, rt, bt, kt, bbar, kbar, vm, nc):
    c = WKV_CHUNK
    n = D_HEAD_DIM
    probs = [(ch, pr) for ch in range(nc) for pr in range(HEAD_PAIRS)]
    cut = lambda x, p: x[p[0] * c:(p[0] + 1) * c, p[1] * LANES:(p[1] + 1) * LANES]
    qt, rt, bt, kt, bbar, kbar, vm = (x.astype(BF16) for x in (qt, rt, bt, kt, bbar, kbar, vm))
    ri = lax.broadcasted_iota(jnp.int32, (c, LANES), 0)
    li = lax.broadcasted_iota(jnp.int32, (c, LANES), 1)
    low = li < n
    strict = ri > li % n
    incl = ri >= li % n
    eye = jnp.where(ri == li % n, 1.0, 0.0)
    diag = lambda y: _pair_diag(y, low)
    halves = lambda x: jnp.where(low, x[:n], x[n:])

    qs = {p: cut(qt, p) for p in probs}
    vd = {p: diag(cut(vm, p)) for p in probs}
    aa = {p: _dot(jnp.concatenate([qs[p], cut(rt, p)], axis=0),
                  jnp.concatenate([diag(cut(bt, p)), diag(cut(kt, p))], axis=0), NT) for p in probs}
    lk = {p: jnp.where(strict, aa[p][:c, LANES:], 0.0).astype(BF16) for p in probs}
    ab = {p: jnp.where(incl, aa[p][c:, :LANES], 0.0).astype(BF16) for p in probs}
    ak = {p: jnp.where(incl, aa[p][c:, LANES:], 0.0).astype(BF16) for p in probs}
    npow = {p: jnp.where(strict, -aa[p][:c, :LANES], 0.0).astype(BF16) for p in probs}
    tinv = {p: eye + npow[p].astype(F32) for p in probs}
    lv = {p: _dot(jnp.concatenate([lk[p], ak[p]], axis=0), vd[p]) for p in probs}
    zp = {p: lv[p][:c].astype(BF16) for p in probs}
    npow = {p: _dot(npow[p], diag(npow[p])).astype(BF16) for p in probs}
    for _ in range(int(math.log2(c)) - 2):
        both = {p: _dot(jnp.concatenate([tinv[p].astype(BF16), npow[p]], axis=0), diag(npow[p])) for p in probs}
        tinv = {p: tinv[p] + both[p][:c] for p in probs}
        npow = {p: both[p][c:].astype(BF16) for p in probs}
    tinv = {p: (tinv[p] + _dot(tinv[p].astype(BF16), diag(npow[p]))).astype(BF16) for p in probs}
    tq = {p: _dot(tinv[p], jnp.concatenate([diag(qs[p]), diag(zp[p])], axis=1)) for p in probs}
    qh = {p: tq[p][:, :LANES].astype(BF16) for p in probs}
    zn = {p: (-tq[p][:, LANES:]).astype(BF16) for p in probs}
    abq = {p: _dot(ab[p], jnp.concatenate([diag(qh[p]), diag(zn[p])], axis=1)) for p in probs}
    rh = {p: (cut(rt, p).astype(F32) - abq[p][:, :LANES]).astype(BF16) for p in probs}
    pv = {p: abq[p][:, LANES:] + lv[p][c:] for p in probs}
    gt = {p: halves(-_dot(qh[p], cut(bbar, p), TN)).astype(BF16) for p in probs}
    cst = {p: halves(_dot(jnp.concatenate([cut(vm, p), zn[p]], axis=0),
                          jnp.concatenate([cut(kbar, p), cut(bbar, p)], axis=0), TN)) for p in probs}
    return rh, pv, gt, cst


def _wkv_scan_steps(s_ref, o_ref, pre, gam, nc):
    c = WKV_CHUNK
    rh, pv, gt, cst = pre
    low = lax.broadcasted_iota(jnp.int32, (D_HEAD_DIM, LANES), 1) < D_HEAD_DIM
    for ch in range(nc):
        for pr in range(HEAD_PAIRS):
            p = (ch, pr)
            s0 = s_ref[pr]
            s0b = s0.astype(BF16)
            o_ref[ch * c:(ch + 1) * c, pr * LANES:(pr + 1) * LANES] = _dot(rh[p], _pair_diag(s0b, low), NT) + pv[p]
            s_ref[pr] = (s0 * gam[ch][:, pr * LANES:(pr + 1) * LANES] + _dot(s0b, _pair_diag(gt[p], low))
                         + cst[p])
        yield


def _rwkv_prompt_kernel(pd_ref, mu_ref, w0_ref, w2_ref, a0_ref, a2_ref, g2_ref, kk_ref, ka_ref, rk_ref,
                        lg_ref, lb_ref, q8_ref, ck_ref, cv_ref, yd_ref, shift_ref, wkv_ref, o8_ref,
                        car_ref, s_ref, o_ref, *, tt, bb):
    t = pl.program_id(1)
    nt = pl.num_programs(1)

    @pl.when(t == 0)
    def _():
        car_ref[...] = jnp.zeros_like(car_ref)
        s_ref[...] = jnp.zeros_like(s_ref)

    pd = pd_ref[0]
    ext = jnp.concatenate([car_ref[...], pd], axis=0)
    prev = pltpu.roll(ext, 1, 0)[ROW_CARRY:]
    car_ref[...] = ext[tt:]
    xs = pd + (prev - pd) * mu_ref[...]
    seg = _head_ones(W_D, D_HEAD_DIM)
    r, k, v, kk, a, ld, g = _rwkv_inputs(xs, w0_ref[...], w2_ref[...], a0_ref[...], a2_ref[...],
                                         g2_ref[...], kk_ref[...], ka_ref[...], seg)
    c = WKV_CHUNK
    nc = tt // c
    tri = jnp.where(lax.broadcasted_iota(jnp.int32, (c, c), 0) >= lax.broadcasted_iota(jnp.int32, (c, c), 1),
                    1.0, 0.0).astype(BF16)
    cs_chunks = [_ones_dot(tri, ld[ch * c:(ch + 1) * c]) for ch in range(nc)]
    cs = jnp.concatenate(cs_chunks, axis=0)
    cs_end = [x[c - 1:c] for x in cs_chunks]
    cs_last = jnp.concatenate([jnp.broadcast_to(x, (c, W_D)) for x in cs_end], axis=0)
    e_neg = jnp.exp(-cs)
    e_tail = jnp.exp(cs_last - cs)
    b = kk * a
    pre = _wkv_precompute(kk * jnp.exp(cs - ld), r * jnp.exp(cs), b * e_neg, k * e_neg, b * e_tail, k * e_tail, v, nc)
    _interleave(_wkv_scan_steps(s_ref, o_ref, pre, [jnp.exp(x) for x in cs_end], nc),
                _attn_sample_steps(q8_ref, ck_ref, cv_ref, o8_ref, bb))
    yd_ref[0] = _rwkv_finish(o_ref[...], r, k, v, g, rk_ref[...], lg_ref[...], lb_ref[...], seg).astype(yd_ref.dtype)

    @pl.when(t == nt - 1)
    def _():
        shift_ref[0] = pd[tt - 1:tt]
        for hd in range(D_HEADS):
            wkv_ref[0, hd] = _head_cols(s_ref[hd // 2], hd % 2)


def _rwkv_prompt(pd, params, q, cache_k, cache_v, layer, tt):
    bn, t, _ = pd.shape
    nt = t // tt
    n = q.shape[0]
    bb = n // (bn * nt)
    assert bb * bn * nt == n
    nbytes = 2 * tt * D_PROJ * 4 + 40 * tt * W_D * 4 + 4 * bb * MEM_ROWS * LANES * 4 + 2 * bb * MEM_ROWS * LANES * 2
    req = pl.BlockSpec((bb, SUBLANES, LANES), lambda i, j: (i * nt + j, 0, 0))
    cache_spec = pl.BlockSpec((None, bb, MEM_ROWS, LANES), lambda i, j: (layer, i * nt + j, 0, 0))
    yd, shift, wkv, o8 = pl.pallas_call(
        functools.partial(_rwkv_prompt_kernel, tt=tt, bb=bb),
        out_shape=(jax.ShapeDtypeStruct((bn, t, W_D), BF16),
                   jax.ShapeDtypeStruct((bn, 1, D_PROJ), F32),
                   jax.ShapeDtypeStruct((bn, D_HEADS, D_HEAD_DIM, D_HEAD_DIM), F32),
                   jax.ShapeDtypeStruct((n, SUBLANES, LANES), F32)),
        grid=(bn, nt),
        in_specs=[pl.BlockSpec((1, tt, D_PROJ), lambda i, j: (i, j, 0))] + [_lspec(z, layer) for z in params]
                 + [req, cache_spec, cache_spec],
        out_specs=(pl.BlockSpec((1, tt, W_D), lambda i, j: (i, j, 0)),
                   pl.BlockSpec((1, 1, D_PROJ), lambda i, j: (i, 0, 0)),
                   pl.BlockSpec((1, D_HEADS, D_HEAD_DIM, D_HEAD_DIM), lambda i, j: (i, 0, 0, 0)),
                   req),
        scratch_shapes=[pltpu.VMEM((ROW_CARRY, D_PROJ), F32),
                        pltpu.VMEM((HEAD_PAIRS, D_HEAD_DIM, LANES), F32),
                        pltpu.VMEM((tt, W_D), F32)],
        compiler_params=_params(("arbitrary", "arbitrary"), nbytes),
        name="rwkv_prompt",
    )(pd, *map(_operand, params), _to_pair_rows(q), _cache_rows_view(cache_k), _cache_rows_view(cache_v))
    return yd, shift, wkv, _from_pair_rows(o8)


def _rwkv_sample_kernel(pd_ref, sh_ref, st_ref, mu_ref, w0_ref, w2_ref, a0_ref, a2_ref, g2_ref, kk_ref,
                        ka_ref, rk_ref, lg_ref, lb_ref, yd_ref, so_ref, rows_ref, cols_ref, ot_ref):
    h = pl.program_id(0)
    n = D_HEAD_DIM

    @pl.when(h == 0)
    def _():
        pd = pd_ref[...]
        xs = pd + (sh_ref[...] - pd) * mu_ref[...]
        seg = _head_ones(W_D, D_HEAD_DIM)
        r, k, v, kk, a, ld, g = _rwkv_inputs(xs, w0_ref[...], w2_ref[...], a0_ref[...], a2_ref[...],
                                             g2_ref[...], kk_ref[...], ka_ref[...], seg)
        for j, x in enumerate((r, k, v, g)):
            rows_ref[j] = x
        for j, x in enumerate((kk, jnp.exp(ld), kk * a, k, r, v)):
            cols_ref[j] = x.T

    base = pl.multiple_of(h * n, n)
    kap, dec, bvec, kvec, rvec = (cols_ref[j, pl.ds(base, n), :] for j in range(5))
    for vi in range(n):
        s = st_ref[0, vi]
        u = -jnp.sum(s * kap, axis=0, keepdims=True)
        s = s * dec + u * bvec + cols_ref[5, pl.ds(base + vi, 1), :] * kvec
        so_ref[0, vi] = s
        ot_ref[pl.ds(base + vi, 1), :] = jnp.sum(s * rvec, axis=0, keepdims=True)

    @pl.when(h == pl.num_programs(0) - 1)
    def _():
        seg = _head_ones(W_D, D_HEAD_DIM)
        yd_ref[...] = _rwkv_finish(ot_ref[...].T, rows_ref[0], rows_ref[1], rows_ref[2], rows_ref[3],
                                   rk_ref[...], lg_ref[...], lb_ref[...], seg)


def _rwkv_sample(pd, shift, state, params, layer):
    n = pd.shape[0]
    sblock = (1, D_HEAD_DIM, D_HEAD_DIM, n)
    return pl.pallas_call(
        _rwkv_sample_kernel,
        out_shape=(jax.ShapeDtypeStruct((n, W_D), F32), jax.ShapeDtypeStruct(state.shape[1:], F32)),
        grid=(D_HEADS,),
        in_specs=[pl.BlockSpec((n, D_PROJ), lambda i: (0, 0)), _lspec(shift, layer),
                  pl.BlockSpec((None,) + sblock, lambda i: (layer, i, 0, 0, 0))]
                 + [_lspec(z, layer) for z in params],
        out_specs=(pl.BlockSpec((n, W_D), lambda i: (0, 0)), pl.BlockSpec(sblock, lambda i: (i, 0, 0, 0))),
        scratch_shapes=[pltpu.VMEM((4, n, W_D), F32), pltpu.VMEM((6, W_D, n), F32), pltpu.VMEM((W_D, n), F32)],
        compiler_params=pltpu.CompilerParams(dimension_semantics=("arbitrary",)),
        name="rwkv_sample",
    )(pd, shift, state, *map(_operand, params))


def _block_diag(w):
    gn, n, _ = w.shape
    eye = jnp.eye(gn, dtype=w.dtype)
    return (eye[:, None, :, None] * w[:, :, None, :]).reshape(gn * n, gn * n)


def kernel(x_prompt, x_sample, mem_prompt, cache_mem_k, cache_mem_v, state_pool, state_conv, state_shift, state_wkv,
           w_in, mu_d, ln_v_g, ln_v_b, ws_chunk, b_chunk, w_pool, pool_scale, conv_w,
           rwkv_w0, rwkv_w2, rwkv_a0, rwkv_a2, rwkv_g2, rwkv_k_k, rwkv_k_a, rwkv_r_k, rwkv_lnx_g, rwkv_lnx_b,
           w_out, ln1_g, ln1_b, w_xq, w_xk, w_xv, w_xo, ln2_g, ln2_b, ffn_w1, ffn_w3, ffn_w2, ln3_g, ln3_b):
    bp, t_p, d = x_prompt.shape
    ns, t_s, _ = x_sample.shape
    depth = w_in.shape[0]
    assert d == D_MODEL and t_s == 1 and t_p % CHUNK == 0 and w_in.shape[2] == PROJ
    alpha = (2 * depth) ** 0.25
    mp = bp * t_p
    nw = W_A + W_B + W_C
    row = _pack_rows(dict(
        ln_v_g=ln_v_g, ln_v_b=ln_v_b, pool_scale=pool_scale, mu_d=mu_d, rwkv_w0=rwkv_w0, rwkv_a0=rwkv_a0,
        rwkv_k_k=rwkv_k_k, rwkv_k_a=rwkv_k_a, rwkv_r_k=rwkv_r_k.reshape(depth, W_D), rwkv_lnx_g=rwkv_lnx_g,
        rwkv_lnx_b=rwkv_lnx_b, ln1_g=ln1_g, ln1_b=ln1_b, ln2_g=ln2_g, ln2_b=ln2_b, ln3_g=ln3_g, ln3_b=ln3_b))

    w_in_b, w_out_b = w_in, w_out
    w_xq_b, w_xk_b, w_xv_b, w_xo_b = w_xq, w_xk, w_xv, w_xo
    w1_b, w3_b, w2_b = ffn_w1, ffn_w3, ffn_w2
    ws_flat = ws_chunk.reshape(depth, A_HEADS * CHUNK, CHUNK)
    bias_full = jnp.repeat(jnp.swapaxes(b_chunk, 1, 2), A_HEAD_DIM, axis=2)
    ws0 = jnp.repeat(ws_chunk[:, :, 0, 0], A_HEAD_DIM, axis=1).reshape(depth, 1, W_A)
    wpool_bd = jnp.stack([_block_diag(w_pool[l]) for l in range(depth)]).astype(BF16)
    abc_w = (row['ln_v_g'], row['ln_v_b'])
    abc_w2 = (wpool_bd, row['pool_scale'], conv_w)
    rw = (row['mu_d'], row['rwkv_w0'], rwkv_w2, row['rwkv_a0'], rwkv_a2, rwkv_g2,
          row['rwkv_k_k'], row['rwkv_k_a'], row['rwkv_r_k'], row['rwkv_lnx_g'], row['rwkv_lnx_b'])
    ln1, ln2, ln3 = (row['ln1_g'], row['ln1_b']), (row['ln2_g'], row['ln2_b']), (row['ln3_g'], row['ln3_b'])
    pool_view = jnp.swapaxes(state_pool, 1, 2)
    wkv_view = state_wkv.transpose(0, 2, 3, 4, 1)
    conv_view = state_conv.reshape(depth, ns, (CONV_W - 1) * W_C)
    shift_view = state_shift.reshape(depth, ns, D_PROJ)

    hp = x_prompt
    hs = x_sample.reshape(ns, d)
    mem = mem_prompt.reshape(bp * MEM_LEN, d)
    mk_all, mv_all, mk_rows, mv_rows = _mem_kv(mem, w_xk_b, w_xv_b, 512)
    mk_all = mk_all.reshape(depth, bp, MEM_LEN, d)
    mv_all = mv_all.reshape(depth, bp, MEM_LEN, d)
    outs = [[] for _ in range(10)]
    for l in range(depth):
        pd, mix, v_last, pool_p, conv_p, *rounded = _proj_abc_prompt(hp, w_in_b, *abc_w, ws_flat, bias_full, *abc_w2,
                                                                     (w1_b, w3_b, w2_b, w_out_b, w_xq_b, w_xo_b), l,
                                                                     tt=1024)
        ffn_b, (w_out_s, w_xq_s, w_xo_s) = rounded[:3], rounded[3:]

        pabc_s, pd_s = _proj(hs, w_in_b, l, ns)
        mix_s, v_s, pool_s, conv_s = _abc_sample(pabc_s, pool_view, conv_view, *abc_w, ws0, bias_full, *abc_w2,
                                                 l, pos0=PAST_LEN)
        yd_s, wkv_s = _rwkv_sample(pd_s, shift_view, wkv_view, rw, l)
        hs = _mm_res_ln([mix_s, yd_s], w_out_s, l, hs, *ln1, tm=ns, alpha=alpha, name="out_proj_s", w_layer=0)
        q_s = _mm(hs, w_xq_s, 0, ns, "q_s")

        yd, shift_p, wkv_p, o_s = _rwkv_prompt(pd, rw, q_s, cache_mem_k, cache_mem_v, l, tt=512)
        hp = _attn_prompt(mix, yd, hp, mk_all, mv_all, w_out_b, *ln1, w_xq_b, w_xo_b, *ln2, l, tq=512, alpha=alpha)
        hp = _ffn(hp.reshape(mp, d), *ffn_b, *ln3, l, tm=1024, tf=256, alpha=alpha, name="ffn").reshape(bp, t_p, d)
        for lst, val in zip(outs[:5], (v_last, pool_p, conv_p, shift_p, wkv_p)):
            lst.append(val)

        hs = _mm_res_ln([o_s], w_xo_s, l, hs, *ln2, tm=ns, alpha=alpha, name="xo_s", w_layer=0)
        hs = _ffn(hs, *ffn_b, *ln3, l, tm=ns, tf=256, alpha=alpha, name="ffn_s")
        for lst, val in zip(outs[5:], (v_s.reshape(ns, 1, W_A), pool_s,
                                       conv_s.reshape(ns, CONV_W - 1, W_C), pd_s.reshape(ns, 1, D_PROJ), wkv_s)):
            lst.append(val)

    stacked = [jnp.stack(o) for o in outs]
    stacked[6] = jnp.swapaxes(stacked[6], 1, 2)
    stacked[9] = stacked[9].transpose(0, 4, 1, 2, 3)
    return ((hp, hs.reshape(ns, 1, d)) + tuple(stacked[:5]) + (_cache_from_rows(mk_rows, bp), _cache_from_rows(mv_rows, bp))
            + tuple(stacked[5:]))
```

```python
import functools
import math
from typing import NamedTuple

import jax
import jax.numpy as jnp
from jax import lax
from jax.experimental import pallas as pl
from jax.experimental.pallas import tpu as pltpu

F32 = jnp.float32
BF16 = jnp.bfloat16

D_MODEL = 1024
W_A = 256
W_B = 256
W_C = 256
W_D = 256
A_HEADS = 4
A_HEAD_DIM = W_A // A_HEADS
CHUNK = 128
POOL_WINDOWS = (2, 4, 8, 16)
POOL_GROUP = W_B // len(POOL_WINDOWS)
POOL_BUF = max(POOL_WINDOWS) - 1
CONV_W = 3
D_HEAD_DIM = 64
D_HEADS = W_D // D_HEAD_DIM
R_DECAY = 32
R_AAA = 32
R_GATE = 64
D_PROJ = 3 * W_D + R_DECAY + R_AAA + R_GATE
N_ABC = 2 * W_A + W_B + 3 * W_C
PROJ = N_ABC + D_PROJ
MEM_LEN = 256
X_HEADS = 4
X_HEAD_DIM = D_MODEL // X_HEADS
D_FF = int(math.ceil(8 * D_MODEL / 3 / 256)) * 256
PAST_LEN = 16384
LN_EPS = 1e-5
GN_EPS = 64e-5

WKV_CHUNK = 64
POOL_CARRY = 24
ROW_CARRY = 8
V7X_VMEM_BYTES = 64 * 1024 * 1024
VMEM_CAP = V7X_VMEM_BYTES - 8 * 1024 * 1024

NN = (((1,), (0,)), ((), ()))
NT = (((1,), (1,)), ((), ()))
TN = (((0,), (0,)), ((), ()))


def _vmem_limit(nbytes):
    return int(min(VMEM_CAP, max(32 * 1024 * 1024, 2 * nbytes)))


def _params(sem, nbytes):
    return pltpu.CompilerParams(dimension_semantics=sem, vmem_limit_bytes=_vmem_limit(nbytes))


class _Row(NamedTuple):
    arr: jax.Array
    start: int
    width: int


def _pack_rows(vectors):
    names = sorted(vectors, key=lambda k: -vectors[k].shape[1])
    pieces, where, start = [], {}, 0
    for name in names:
        v = vectors[name]
        width = v.shape[1]
        gap = -start % width
        if gap:
            pieces.append(jnp.zeros((v.shape[0], gap), v.dtype))
        where[name] = (start + gap, width)
        pieces.append(v)
        start += gap + width
    packed = jnp.concatenate(pieces, axis=1).reshape(pieces[0].shape[0], 1, start)
    return {name: _Row(packed, *where[name]) for name in names}


def _operand(z):
    return z.arr if isinstance(z, _Row) else z


def _lspec(arr, layer):
    if isinstance(arr, _Row):
        assert arr.start % arr.width == 0
        return pl.BlockSpec((None, 1, arr.width), lambda *_: (layer, 0, arr.start // arr.width))
    tail = arr.shape[1:]
    zeros = (0,) * len(tail)
    return pl.BlockSpec((None,) + tail, lambda *_: (layer,) + zeros)


def _dot(a, b, dims=NN):
    return lax.dot_general(a, b, dims, preferred_element_type=F32)


def _split2(a):
    hi = a.astype(BF16)
    lo = (a - hi.astype(F32)).astype(BF16)
    return hi, lo


def _dot_hi(a, b, dims=NN):
    ah, al = _split2(a)
    bh, bl = _split2(b)
    return _dot(ah, bh, dims) + _dot(ah, bl, dims) + _dot(al, bh, dims)


def _dot_ones(x, ones_bf16, dims=NN):
    hi, lo = _split2(x)
    return _dot(hi, ones_bf16, dims) + _dot(lo, ones_bf16, dims)


def _ones_dot(ones_bf16, x):
    hi = x.astype(BF16)
    r1 = x - hi.astype(F32)
    mid = r1.astype(BF16)
    lo = (r1 - mid.astype(F32)).astype(BF16)
    return _dot(ones_bf16, hi) + _dot(ones_bf16, mid) + _dot(ones_bf16, lo)


def _ln(x, g, b, eps=LN_EPS):
    mu = jnp.mean(x, axis=-1, keepdims=True)
    xc = x - mu
    var = jnp.mean(xc * xc, axis=-1, keepdims=True)
    return xc * lax.rsqrt(var + eps) * g + b


def _gelu(x):
    c = math.sqrt(2.0 / math.pi)
    return x * (0.5 * (1.0 + jnp.tanh(c * (x + 0.044715 * (x * x * x)))))


def _sigmoid(x):
    return 1.0 / (1.0 + jnp.exp(-x))


def _softplus(x):
    return jnp.maximum(x, 0.0) + jnp.log(1.0 + jnp.exp(-jnp.abs(x)))


def _interleave(*staged):
    live = list(staged)
    while live:
        for steps in list(live):
            if next(steps, StopIteration) is StopIteration:
                live.remove(steps)


def _head_ones(n, group):
    r = lax.broadcasted_iota(jnp.int32, (n, n), 0) // group
    c = lax.broadcasted_iota(jnp.int32, (n, n), 1) // group
    return jnp.where(r == c, 1.0, 0.0).astype(BF16)


def _proj_kernel(x_ref, w_ref, oabc_ref, od_ref):
    y = _dot(x_ref[...].astype(BF16), w_ref[...].astype(BF16))
    oabc_ref[...] = y[:, :N_ABC]
    od_ref[...] = y[:, N_ABC:]


def _proj(x, w, layer, tm):
    m, k = x.shape
    nbytes = 2 * (tm * k * 4 + k * PROJ * 4 + tm * PROJ * 4) + tm * PROJ * 4 + k * PROJ * 2
    return pl.pallas_call(
        _proj_kernel,
        out_shape=(jax.ShapeDtypeStruct((m, N_ABC), F32), jax.ShapeDtypeStruct((m, D_PROJ), F32)),
        grid=(m // tm,),
        in_specs=[pl.BlockSpec((tm, k), lambda i: (i, 0)), _lspec(w, layer)],
        out_specs=(pl.BlockSpec((tm, N_ABC), lambda i: (i, 0)), pl.BlockSpec((tm, D_PROJ), lambda i: (i, 0))),
        compiler_params=_params(("parallel",), nbytes),
        name="proj",
    )(x, w)


def _mm_kernel(x_ref, w_ref, o_ref):
    o_ref[...] = _dot(x_ref[...].astype(BF16), w_ref[...].astype(BF16))


def _mm(x, w, layer, tm, name):
    m, k = x.shape
    n = w.shape[2]
    nbytes = 2 * (tm * k * 4 + k * n * 4 + tm * n * 4) + tm * n * 4 + k * n * 2
    return pl.pallas_call(
        _mm_kernel,
        out_shape=jax.ShapeDtypeStruct((m, n), F32),
        grid=(m // tm,),
        in_specs=[pl.BlockSpec((tm, k), lambda i: (i, 0)), _lspec(w, layer)],
        out_specs=pl.BlockSpec((tm, n), lambda i: (i, 0)),
        compiler_params=_params(("parallel",), nbytes),
        name=name,
    )(x, w)


def _mem_kv_kernel(x_ref, wk_ref, wv_ref, k_ref, v_ref, kt_ref, vt_ref, wkb_ref, wvb_ref):
    @pl.when(pl.program_id(1) == 0)
    def _():
        wkb_ref[...] = wk_ref[...].astype(BF16)
        wvb_ref[...] = wv_ref[...].astype(BF16)

    xb = x_ref[...].astype(BF16)
    for w_ref, o_ref, t_ref in ((wkb_ref, k_ref, kt_ref), (wvb_ref, v_ref, vt_ref)):
        y = _dot(xb, w_ref[...])
        o_ref[...] = y
        rows = y.shape[0]
        t_ref[...] = jnp.swapaxes(y.reshape(rows, X_HEADS, LANE_TILES, LANES), 1, 2).reshape(rows, SUBLANES, LANES)


def _mem_kv(mem, wk, wv, tm):
    m, k = mem.shape
    nl, _, n = wk.shape
    nbytes = 2 * (tm * k * 4 + 2 * k * n * 4 + 4 * tm * n * 4) + 2 * tm * n * 4 + 2 * k * n * 2
    wspec = pl.BlockSpec((None, k, n), lambda l, i: (l, 0, 0))
    ospec = pl.BlockSpec((None, tm, n), lambda l, i: (l, i, 0))
    tspec = pl.BlockSpec((None, tm, SUBLANES, LANES), lambda l, i: (l, i, 0, 0))
    shape = jax.ShapeDtypeStruct((nl, m, n), F32)
    tshape = jax.ShapeDtypeStruct((nl, m, SUBLANES, LANES), F32)
    return pl.pallas_call(
        _mem_kv_kernel,
        out_shape=(shape, shape, tshape, tshape),
        grid=(nl, m // tm),
        in_specs=[pl.BlockSpec((tm, k), lambda l, i: (i, 0)), wspec, wspec],
        out_specs=(ospec, ospec, tspec, tspec),
        scratch_shapes=[pltpu.VMEM((k, n), BF16), pltpu.VMEM((k, n), BF16)],
        compiler_params=_params(("arbitrary", "arbitrary"), nbytes),
        name="mem_kv",
    )(mem, wk, wv)


def _mm_res_ln_kernel(*refs, n_in, alpha):
    xs = refs[:n_in]
    ws = refs[n_in:2 * n_in]
    h_ref, g_ref, b_ref, o_ref = refs[2 * n_in:]
    y = _dot(xs[0][...].astype(BF16), ws[0][...].astype(BF16))
    for x_ref, w_ref in zip(xs[1:], ws[1:]):
        y = y + _dot(x_ref[...].astype(BF16), w_ref[...].astype(BF16))
    o_ref[...] = _ln(alpha * h_ref[...] + y, g_ref[...], b_ref[...])


def _mm_res_ln(xs, w, layer, h, g, b, tm, alpha, name, w_layer=None):
    m, n = h.shape
    w_layer = layer if w_layer is None else w_layer
    nbytes = 2 * sum(tm * x.shape[1] * 4 + x.shape[1] * n * 4 for x in xs) + 5 * tm * n * 4 + w.shape[1] * n * 2
    in_specs = [pl.BlockSpec((tm, x.shape[1]), lambda i: (i, 0)) for x in xs]
    start = 0
    for x in xs:
        width = x.shape[1]
        assert start % width == 0
        in_specs.append(pl.BlockSpec((None, width, n), lambda i, blk=start // width: (w_layer, blk, 0)))
        start += width
    assert start == w.shape[1]
    in_specs += [pl.BlockSpec((tm, n), lambda i: (i, 0)), _lspec(g, layer), _lspec(b, layer)]
    return pl.pallas_call(
        functools.partial(_mm_res_ln_kernel, n_in=len(xs), alpha=alpha),
        out_shape=jax.ShapeDtypeStruct((m, n), F32),
        grid=(m // tm,),
        in_specs=in_specs,
        out_specs=pl.BlockSpec((tm, n), lambda i: (i, 0)),
        compiler_params=_params(("parallel",), nbytes),
        name=name,
    )(*xs, *([w] * len(xs)), h, _operand(g), _operand(b))


def _ffn_kernel(x_ref, w1_ref, w3_ref, w2_ref, g_ref, b_ref, o_ref, xb_ref, acc_ref, *, alpha):
    j = pl.program_id(1)

    @pl.when(j == 0)
    def _():
        xb_ref[...] = x_ref[...].astype(BF16)
        acc_ref[...] = jnp.zeros_like(acc_ref)

    xb = xb_ref[...]
    h1 = _dot(xb, w1_ref[...])
    h3 = _dot(xb, w3_ref[...])
    a = (h1 * _sigmoid(h1) * h3).astype(BF16)
    acc_ref[...] += _dot(a, w2_ref[...])

    @pl.when(j == pl.num_programs(1) - 1)
    def _():
        o_ref[...] = _ln(alpha * x_ref[...] + acc_ref[...], g_ref[...], b_ref[...])


def _ffn(x, w1, w3, w2, g, b, layer, tm, tf, alpha, name):
    m, d = x.shape
    nbytes = 4 * tm * d * 4 + tm * d * 2 + tm * d * 4 + 2 * 3 * d * tf * 2 + 3 * tm * tf * 4
    return pl.pallas_call(
        functools.partial(_ffn_kernel, alpha=alpha),
        out_shape=jax.ShapeDtypeStruct((m, d), F32),
        grid=(m // tm, D_FF // tf),
        in_specs=[pl.BlockSpec((tm, d), lambda i, j: (i, 0)),
                  pl.BlockSpec((None, d, tf), lambda i, j: (0, 0, j)),
                  pl.BlockSpec((None, d, tf), lambda i, j: (0, 0, j)),
                  pl.BlockSpec((None, tf, d), lambda i, j: (0, j, 0)),
                  _lspec(g, layer), _lspec(b, layer)],
        out_specs=pl.BlockSpec((tm, d), lambda i, j: (i, 0)),
        scratch_shapes=[pltpu.VMEM((tm, d), BF16), pltpu.VMEM((tm, d), F32)],
        compiler_params=_params(("parallel", "arbitrary"), nbytes),
        name=name,
    )(x, w1, w3, w2, _operand(g), _operand(b))


def _softmax_rows(s):
    m = jnp.max(s, axis=-1, keepdims=True)
    e = jnp.exp(s - m)
    return e / jnp.sum(e, axis=-1, keepdims=True)


ATTN_PIECES = 2


def _attn_prompt_kernel(mix_ref, yd_ref, h_ref, mk_ref, mv_ref, wma_ref, wmd_ref, g1_ref, b1_ref,
                        wq_ref, wo_ref, g_ref, b_ref, o_ref, ob_ref, *, alpha):
    rows = h_ref.shape[1] // ATTN_PIECES
    scale = X_HEAD_DIM ** -0.5
    sls = [slice(hd * X_HEAD_DIM, (hd + 1) * X_HEAD_DIM) for hd in range(X_HEADS)]
    kb = [mk_ref[0, :, sl].astype(BF16) for sl in sls]
    vb = [mv_ref[0, :, sl].astype(BF16) for sl in sls]
    y, h, q, sc = {}, {}, {}, {}

    def project(p, rs):
        y[p] = (_dot(mix_ref[0, rs, :].astype(BF16), wma_ref[...].astype(BF16))
                + _dot(yd_ref[0, rs, :].astype(BF16), wmd_ref[...].astype(BF16)))

    def query(p, rs):
        h[p] = _ln(alpha * h_ref[0, rs, :] + y[p], g1_ref[...], b1_ref[...])
        q[p] = _dot(h[p].astype(BF16), wq_ref[...].astype(BF16)).astype(BF16)

    def scores(p, rs):
        sc[p] = [_dot(q[p][:, sl], k, NT) * scale for sl, k in zip(sls, kb)]

    def values(p, rs):
        for sl, s, v in zip(sls, sc[p], vb):
            ob_ref[rs, sl] = _dot(_softmax_rows(s).astype(BF16), v).astype(BF16)

    def output(p, rs):
        o_ref[0, rs, :] = _ln(alpha * h[p] + _dot(ob_ref[rs, :], wo_ref[...].astype(BF16)), g_ref[...], b_ref[...])

    stages = (project, query, scores, values, output)
    for step in range(len(stages) + ATTN_PIECES - 1):
        for p in range(ATTN_PIECES):
            if 0 <= step - p < len(stages):
                stages[step - p](p, slice(p * rows, (p + 1) * rows))


def _attn_prompt(mix, yd, h, mk, mv, w_mix, g1, b1, wq, wo, g, b, layer, tq, alpha, w_layer):
    bn, t, d = h.shape
    nw = mix.shape[2]
    assert nw % W_D == 0 and w_mix.shape[1] == nw + W_D
    nbytes = (6 * tq * d * 4 + 4 * MEM_LEN * d * 4 + 6 * d * d * 4 + 3 * d * d * 2 + tq * d * 2
              + 3 * tq * d * 4 + 3 * tq * MEM_LEN * 4)
    tile = lambda n: pl.BlockSpec((1, tq, n), lambda i, j: (i, j, 0))
    return pl.pallas_call(
        functools.partial(_attn_prompt_kernel, alpha=alpha),
        out_shape=jax.ShapeDtypeStruct((bn, t, d), F32),
        grid=(bn, t // tq),
        in_specs=[tile(nw), tile(W_D), tile(d),
                  pl.BlockSpec((None, 1, MEM_LEN, d), lambda i, j: (layer, i, 0, 0)),
                  pl.BlockSpec((None, 1, MEM_LEN, d), lambda i, j: (layer, i, 0, 0)),
                  pl.BlockSpec((None, nw, d), lambda i, j: (w_layer, 0, 0)),
                  pl.BlockSpec((None, W_D, d), lambda i, j: (w_layer, nw // W_D, 0)),
                  _lspec(g1, layer), _lspec(b1, layer),
                  _lspec(wq, w_layer), _lspec(wo, w_layer), _lspec(g, layer), _lspec(b, layer)],
        out_specs=tile(d),
        scratch_shapes=[pltpu.VMEM((tq, d), BF16)],
        compiler_params=_params(("arbitrary", "arbitrary"), nbytes),
        name="attn_prompt",
    )(mix, yd, h, mk, mv, w_mix, w_mix, _operand(g1), _operand(b1), wq, wo, _operand(g), _operand(b))


LANES = 128
SUBLANES = 8
LANE_TILES = X_HEAD_DIM // LANES
MEM_ROWS = MEM_LEN * LANE_TILES * X_HEADS


def _cache_rows_view(cache):
    nl, n = cache.shape[:2]
    x = cache.reshape(nl, n, MEM_LEN, X_HEADS, LANE_TILES, LANES)
    return x.transpose(0, 1, 2, 4, 3, 5).reshape(nl, n, MEM_ROWS, LANES)


def _cache_from_rows(rows, n):
    nl = rows.shape[0]
    x = rows.reshape(nl, n, MEM_LEN, LANE_TILES, X_HEADS, LANES)
    return x.transpose(0, 1, 2, 4, 3, 5).reshape(nl, n, MEM_LEN, X_HEADS, X_HEAD_DIM)


def _attn_sample_steps(q_ref, k_ref, v_ref, o_ref, bb):
    scale = X_HEAD_DIM ** -0.5
    shape = (SUBLANES, MEM_ROWS)
    rowi = lax.broadcasted_iota(jnp.int32, shape, 0)
    coli = lax.broadcasted_iota(jnp.int32, shape, 1)
    valid = (coli % SUBLANES) == rowi
    raw = [_dot(q_ref[i].astype(BF16), k_ref[i].astype(BF16), NT) for i in range(bb)]
    yield
    probs = []
    for r in raw:
        r = jnp.where(valid, r, 0.0)
        other = pltpu.roll(r, X_HEADS, 0)
        other = jnp.where(rowi < X_HEADS, pltpu.roll(other, MEM_ROWS - X_HEADS, 1), pltpu.roll(other, X_HEADS, 1))
        sc = jnp.where(valid, (r + other) * scale, -jnp.inf)
        m = jnp.max(sc, axis=-1, keepdims=True)
        e = jnp.exp(sc - m)
        probs.append((e / jnp.sum(e, axis=-1, keepdims=True)).astype(BF16))
        yield
    for i in range(bb):
        o_ref[i] = _dot(probs[i], v_ref[i].astype(BF16))
        yield


def _to_pair_rows(q):
    n = q.shape[0]
    return q.reshape(n, X_HEADS, LANE_TILES, LANES).transpose(0, 2, 1, 3).reshape(n, SUBLANES, LANES)


def _from_pair_rows(o8):
    n = o8.shape[0]
    return o8.reshape(n, LANE_TILES, X_HEADS, LANES).transpose(0, 2, 1, 3).reshape(n, D_MODEL)


def _pool_window_sums(ext, tt):
    s2 = ext + pltpu.roll(ext, 1, 0)
    s4 = s2 + pltpu.roll(s2, 2, 0)
    s8 = s4 + pltpu.roll(s4, 4, 0)
    s16 = s8 + pltpu.roll(s8, 8, 0)
    grp = lax.broadcasted_iota(jnp.int32, (tt, W_B), 1) // POOL_GROUP
    lo = ext.shape[0] - tt
    return jnp.where(grp == 0, s2[lo:], jnp.where(grp == 1, s4[lo:], jnp.where(grp == 2, s8[lo:], s16[lo:])))


def _proj_abc_kernel(x_ref, w_ref, lng_ref, lnb_ref, ws_ref, bias_ref, wpool_ref, pscale_ref, cw_ref,
                     *refs, tt, n_round):
    f_refs, refs = refs[:n_round], refs[n_round:]
    pd_ref, mix_ref, vlast_ref, pool_ref, conv_ref = refs[:5]
    fb_refs = refs[5:5 + n_round]
    pcar_ref, ccar_ref, wb_ref = refs[5 + n_round:]
    t = pl.program_id(1)
    nt = pl.num_programs(1)

    @pl.when((pl.program_id(0) == 0) & (t == 0))
    def _():
        wb_ref[...] = w_ref[...].astype(BF16)

    @pl.when(t == 0)
    def _():
        pcar_ref[...] = jnp.zeros_like(pcar_ref)
        ccar_ref[...] = jnp.zeros_like(ccar_ref)

    rows = lax.broadcasted_iota(jnp.int32, (A_HEADS * CHUNK, CHUNK), 0) % CHUNK
    cols = lax.broadcasted_iota(jnp.int32, (A_HEADS * CHUNK, CHUNK), 1)
    wsm = jnp.where(rows >= cols, ws_ref[...], 0.0).astype(BF16)
    hid = lax.broadcasted_iota(jnp.int32, (CHUNK, W_A), 1) // A_HEAD_DIM
    win = jnp.left_shift(2, lax.broadcasted_iota(jnp.int32, (CHUNK, W_B), 1) // POOL_GROUP)
    rowi = lax.broadcasted_iota(jnp.int32, (CHUNK, W_B), 0)
    w = wb_ref[...]
    pcar = pcar_ref[...]
    ccar = ccar_ref[...]
    v = None
    for c in range(tt // CHUNK):
        rs = slice(c * CHUNK, (c + 1) * CHUNK)
        y = _dot(x_ref[0, rs, :].astype(BF16), w)
        pd_ref[0, rs, :] = y[:, N_ABC:]

        ga = _gelu(y[:, :2 * W_A])
        u = ga[:, :W_A]
        v = _ln(ga[:, W_A:], lng_ref[...], lnb_ref[...])
        zz = _dot(wsm, v.astype(BF16))
        z = zz[(A_HEADS - 1) * CHUNK:]
        for hd in range(A_HEADS - 2, -1, -1):
            z = jnp.where(hid == hd, zz[hd * CHUNK:(hd + 1) * CHUNK], z)
        mix_ref[0, rs, 0:W_A] = (u * (z + bias_ref[...])).astype(mix_ref.dtype)

        xb = y[:, 2 * W_A:2 * W_A + W_B]
        ext = jnp.concatenate([pcar, xb], axis=0)
        sums = _pool_window_sums(ext, CHUNK)
        pos = t * tt + c * CHUNK + rowi
        cnt = jnp.minimum(win, pos + 1).astype(F32)
        d = sums / cnt - xb
        mix_ref[0, rs, W_A:W_A + W_B] = (_dot(d.astype(BF16), wpool_ref[...]) * pscale_ref[...]).astype(mix_ref.dtype)
        pcar = ext[CHUNK:]

        o = 2 * W_A + W_B
        bg = y[:, o:o + W_C]
        zc = y[:, o + W_C:o + 2 * W_C] * y[:, o + 2 * W_C:o + 3 * W_C]
        extz = jnp.concatenate([ccar, zc], axis=0)
        conv = (cw_ref[0:1, :] * pltpu.roll(extz, 2, 0) + cw_ref[1:2, :] * pltpu.roll(extz, 1, 0)
                + cw_ref[2:3, :] * extz)
        mix_ref[0, rs, W_A + W_B:] = (bg * conv[ROW_CARRY:]).astype(mix_ref.dtype)
        ccar = extz[CHUNK:]
    pcar_ref[...] = pcar
    ccar_ref[...] = ccar
    for src_ref, dst_ref in zip(f_refs, fb_refs):
        dst_ref[...] = src_ref[...].astype(BF16)

    @pl.when(t == nt - 1)
    def _():
        vlast_ref[0] = v
        pool_ref[0] = pcar_ref[POOL_CARRY - POOL_BUF:, :]
        conv_ref[0] = ccar_ref[ROW_CARRY - (CONV_W - 1):, :]


def _proj_abc_prompt(h, w, lng, lnb, ws, bias, wpool, pscale, cw, ffn_ws, layer, tt):
    bn, t, d = h.shape
    nw = W_A + W_B + W_C
    nt = t // tt
    steps = bn * nt
    assert all(z.shape[1] % (steps * 2 * SUBLANES) == 0 for z in ffn_ws)
    f_in = [pl.BlockSpec((None, z.shape[1] // steps, z.shape[2]), lambda i, j: (layer, i * nt + j, 0)) for z in ffn_ws]
    f_out = [pl.BlockSpec((None, z.shape[1] // steps, z.shape[2]), lambda i, j: (0, i * nt + j, 0)) for z in ffn_ws]
    f_shapes = [jax.ShapeDtypeStruct((1,) + z.shape[1:], BF16) for z in ffn_ws]
    nbytes = (2 * tt * d * 4 + 2 * d * PROJ * 4 + d * PROJ * 2 + 2 * tt * (D_PROJ + nw) * 4 + 3 * CHUNK * PROJ * 4
              + 4 * A_HEADS * CHUNK * CHUNK * 4 + 12 * sum(z.shape[1] * z.shape[2] for z in ffn_ws) // steps)
    tile = lambda n: pl.BlockSpec((1, tt, n), lambda i, j: (i, j, 0))
    last = lambda r, n: pl.BlockSpec((1, r, n), lambda i, j: (i, 0, 0))
    return pl.pallas_call(
        functools.partial(_proj_abc_kernel, tt=tt, n_round=len(ffn_ws)),
        out_shape=(jax.ShapeDtypeStruct((bn, t, D_PROJ), F32),
                   jax.ShapeDtypeStruct((bn, t, nw), BF16),
                   jax.ShapeDtypeStruct((bn, CHUNK, W_A), F32),
                   jax.ShapeDtypeStruct((bn, POOL_BUF, W_B), F32),
                   jax.ShapeDtypeStruct((bn, CONV_W - 1, W_C), F32), *f_shapes),
        grid=(bn, nt),
        in_specs=[tile(d), *[_lspec(z, layer) for z in (w, lng, lnb, ws, bias, wpool, pscale, cw)], *f_in],
        out_specs=(tile(D_PROJ), tile(nw), last(CHUNK, W_A), last(POOL_BUF, W_B), last(CONV_W - 1, W_C), *f_out),
        scratch_shapes=[pltpu.VMEM((POOL_CARRY, W_B), F32), pltpu.VMEM((ROW_CARRY, W_C), F32),
                        pltpu.VMEM((d, PROJ), BF16)],
        compiler_params=_params(("arbitrary", "arbitrary"), nbytes),
        name="proj_abc_prompt",
    )(*map(_operand, (h, w, lng, lnb, ws, bias, wpool, pscale, cw, *ffn_ws)))


def _abc_sample_kernel(x_ref, pool_ref, conv_ref, lng_ref, lnb_ref, ws0_ref, bias0_ref, wpool_ref,
                       pscale_ref, cw_ref, mix_ref, v_ref, poolo_ref, convo_ref, *, pos0):
    x = x_ref[...]
    n = x.shape[0]
    ga = _gelu(x[:, :2 * W_A])
    u = ga[:, :W_A]
    v = _ln(ga[:, W_A:], lng_ref[...], lnb_ref[...])
    v_ref[...] = v
    mix_ref[:, 0:W_A] = u * (v * ws0_ref[...] + bias0_ref[0:1, :])

    xb = x[:, 2 * W_A:2 * W_A + W_B]
    grp = lax.broadcasted_iota(jnp.int32, (n, W_B), 1) // POOL_GROUP
    run = xb
    sums = jnp.zeros_like(xb)
    for back in range(1, max(POOL_WINDOWS)):
        run = run + pool_ref[POOL_BUF - back]
        for gi, win in enumerate(POOL_WINDOWS):
            if back == win - 1:
                sums = jnp.where(grp == gi, run, sums)
    win = jnp.left_shift(2, grp)
    cnt = jnp.minimum(win, pos0 + 1).astype(F32)
    d = sums / cnt - xb
    mix_ref[:, W_A:W_A + W_B] = _dot(d.astype(BF16), wpool_ref[...]) * pscale_ref[...]
    for r in range(POOL_BUF - 1):
        poolo_ref[r] = pool_ref[r + 1]
    poolo_ref[POOL_BUF - 1] = xb

    o = 2 * W_A + W_B
    bg = x[:, o:o + W_C]
    zc = x[:, o + W_C:o + 2 * W_C] * x[:, o + 2 * W_C:o + 3 * W_C]
    z0 = conv_ref[:, :W_C]
    z1 = conv_ref[:, W_C:]
    y = cw_ref[0:1, :] * z0 + cw_ref[1:2, :] * z1 + cw_ref[2:3, :] * zc
    mix_ref[:, W_A + W_B:] = bg * y
    convo_ref[:, :W_C] = z1
    convo_ref[:, W_C:] = zc


def _abc_sample(pabc, pool, conv, lng, lnb, ws0, bias, wpool, pscale, cw, layer, pos0):
    n = pabc.shape[0]
    nw = W_A + W_B + W_C
    whole = lambda shape: pl.BlockSpec(shape, lambda i: (0,) * len(shape))
    return pl.pallas_call(
        functools.partial(_abc_sample_kernel, pos0=pos0),
        out_shape=(jax.ShapeDtypeStruct((n, nw), F32),
                   jax.ShapeDtypeStruct((n, W_A), F32),
                   jax.ShapeDtypeStruct(pool.shape[1:], F32),
                   jax.ShapeDtypeStruct(conv.shape[1:], F32)),
        grid=(1,),
        in_specs=[whole(pabc.shape)] + [_lspec(z, layer) for z in (pool, conv, lng, lnb, ws0, bias, wpool, pscale, cw)],
        out_specs=(whole((n, nw)), whole((n, W_A)), whole(pool.shape[1:]), whole(conv.shape[1:])),
        name="mixer_abc_sample",
    )(*map(_operand, (pabc, pool, conv, lng, lnb, ws0, bias, wpool, pscale, cw)))


def _rwkv_inputs(xs, w0, w2, a0, a2, g2, kk_w, ka_w, seg):
    r = xs[:, 0:W_D]
    k = xs[:, W_D:2 * W_D]
    v = xs[:, 2 * W_D:3 * W_D]
    o = 3 * W_D
    dw = xs[:, o:o + R_DECAY]
    da = xs[:, o + R_DECAY:o + R_DECAY + R_AAA]
    dg = xs[:, o + R_DECAY + R_AAA:]
    w_log = -_softplus(-(w0 + _dot_hi(jnp.tanh(dw), w2))) - 0.5
    logdecay = -jnp.exp(w_log)
    a = _sigmoid(a0 + _dot_hi(da, a2))
    g = _dot_hi(_sigmoid(dg), g2)
    kk = k * kk_w
    kk = kk * lax.rsqrt(jnp.maximum(_dot_ones(kk * kk, seg), 1e-12))
    k = k * (1.0 + (a - 1.0) * ka_w)
    return r, k, v, kk, a, logdecay, g


def _rwkv_finish(o, r, k, v, g, rk_w, lnx_g, lnx_b, seg):
    inv = 1.0 / D_HEAD_DIM
    mu = _dot_ones(o, seg) * inv
    oc = o - mu
    var = _dot_ones(oc * oc, seg) * inv
    on = oc * lax.rsqrt(var + GN_EPS) * lnx_g + lnx_b
    bonus = _dot_ones(r * k * rk_w, seg) * v
    return (on + bonus) * g


def _bdot(a, b, dims=NN):
    return _dot(a.astype(BF16), b.astype(BF16), dims)


def _head_cols(x, hd):
    return x[:, hd * D_HEAD_DIM:(hd + 1) * D_HEAD_DIM]


HEAD_PAIRS = W_D // LANES


def _pair_diag(y, low):
    zero = jnp.zeros_like(y)
    return jnp.concatenate([jnp.where(low, y, zero), jnp.where(low, zero, y)], axis=0)


def _wkv_precompute(qt, rt, bt, kt, bbar, kbar, vm, nc):
    c = WKV_CHUNK
    n = D_HEAD_DIM
    probs = [(ch, pr) for ch in range(nc) for pr in range(HEAD_PAIRS)]
    cut = lambda x, p: x[p[0] * c:(p[0] + 1) * c, p[1] * LANES:(p[1] + 1) * LANES]
    qt, rt, bt, kt, bbar, kbar, vm = (x.astype(BF16) for x in (qt, rt, bt, kt, bbar, kbar, vm))
    ri = lax.broadcasted_iota(jnp.int32, (c, LANES), 0)
    li = lax.broadcasted_iota(jnp.int32, (c, LANES), 1)
    low = li < n
    strict = ri > li % n
    incl = ri >= li % n
    eye = jnp.where(ri == li % n, 1.0, 0.0)
    diag = lambda y: _pair_diag(y, low)
    halves = lambda x: jnp.where(low, x[:n], x[n:])

    qs = {p: cut(qt, p) for p in probs}
    vd = {p: diag(cut(vm, p)) for p in probs}
    aa = {p: _dot(jnp.concatenate([qs[p], cut(rt, p)], axis=0),
                  jnp.concatenate([diag(cut(bt, p)), diag(cut(kt, p))], axis=0), NT) for p in probs}
    lk = {p: jnp.where(strict, aa[p][:c, LANES:], 0.0).astype(BF16) for p in probs}
    ab = {p: jnp.where(incl, aa[p][c:, :LANES], 0.0).astype(BF16) for p in probs}
    ak = {p: jnp.where(incl, aa[p][c:, LANES:], 0.0).astype(BF16) for p in probs}
    npow = {p: jnp.where(strict, -aa[p][:c, :LANES], 0.0).astype(BF16) for p in probs}
    tinv = {p: eye + npow[p].astype(F32) for p in probs}
    lv = {p: _dot(jnp.concatenate([lk[p], ak[p]], axis=0), vd[p]) for p in probs}
    zp = {p: lv[p][:c].astype(BF16) for p in probs}
    npow = {p: _dot(npow[p], diag(npow[p])).astype(BF16) for p in probs}
    for _ in range(int(math.log2(c)) - 2):
        both = {p: _dot(jnp.concatenate([tinv[p].astype(BF16), npow[p]], axis=0), diag(npow[p])) for p in probs}
        tinv = {p: tinv[p] + both[p][:c] for p in probs}
        npow = {p: both[p][c:].astype(BF16) for p in probs}
    tinv = {p: (tinv[p] + _dot(tinv[p].astype(BF16), diag(npow[p]))).astype(BF16) for p in probs}
    tq = {p: _dot(tinv[p], jnp.concatenate([diag(qs[p]), diag(zp[p])], axis=1)) for p in probs}
    qh = {p: tq[p][:, :LANES].astype(BF16) for p in probs}
    zn = {p: (-tq[p][:, LANES:]).astype(BF16) for p in probs}
    abq = {p: _dot(ab[p], jnp.concatenate([diag(qh[p]), diag(zn[p])], axis=1)) for p in probs}
    rh = {p: (cut(rt, p).astype(F32) - abq[p][:, :LANES]).astype(BF16) for p in probs}
    pv = {p: abq[p][:, LANES:] + lv[p][c:] for p in probs}
    gt = {p: halves(-_dot(qh[p], cut(bbar, p), TN)).astype(BF16) for p in probs}
    cst = {p: halves(_dot(jnp.concatenate([cut(vm, p), zn[p]], axis=0),
                          jnp.concatenate([cut(kbar, p), cut(bbar, p)], axis=0), TN)) for p in probs}
    return rh, pv, gt, cst


def _wkv_scan_steps(s_ref, o_ref, pre, gam, nc):
    c = WKV_CHUNK
    rh, pv, gt, cst = pre
    low = lax.broadcasted_iota(jnp.int32, (D_HEAD_DIM, LANES), 1) < D_HEAD_DIM
    for ch in range(nc):
        for pr in range(HEAD_PAIRS):
            p = (ch, pr)
            s0 = s_ref[pr]
            s0b = s0.astype(BF16)
            o_ref[ch * c:(ch + 1) * c, pr * LANES:(pr + 1) * LANES] = _dot(rh[p], _pair_diag(s0b, low), NT) + pv[p]
            s_ref[pr] = (s0 * gam[ch][:, pr * LANES:(pr + 1) * LANES] + _dot(s0b, _pair_diag(gt[p], low))
                         + cst[p])
        yield


def _rwkv_prompt_kernel(pd_ref, mu_ref, w0_ref, w2_ref, a0_ref, a2_ref, g2_ref, kk_ref, ka_ref, rk_ref,
                        lg_ref, lb_ref, q8_ref, ck_ref, cv_ref, yd_ref, shift_ref, wkv_ref, o8_ref,
                        car_ref, s_ref, o_ref, *, tt, bb):
    t = pl.program_id(1)
    nt = pl.num_programs(1)

    @pl.when(t == 0)
    def _():
        car_ref[...] = jnp.zeros_like(car_ref)
        s_ref[...] = jnp.zeros_like(s_ref)

    pd = pd_ref[0]
    ext = jnp.concatenate([car_ref[...], pd], axis=0)
    prev = pltpu.roll(ext, 1, 0)[ROW_CARRY:]
    car_ref[...] = ext[tt:]
    xs = pd + (prev - pd) * mu_ref[...]
    seg = _head_ones(W_D, D_HEAD_DIM)
    r, k, v, kk, a, ld, g = _rwkv_inputs(xs, w0_ref[...], w2_ref[...], a0_ref[...], a2_ref[...],
                                         g2_ref[...], kk_ref[...], ka_ref[...], seg)
    c = WKV_CHUNK
    nc = tt // c
    tri = jnp.where(lax.broadcasted_iota(jnp.int32, (c, c), 0) >= lax.broadcasted_iota(jnp.int32, (c, c), 1),
                    1.0, 0.0).astype(BF16)
    cs_chunks = [_ones_dot(tri, ld[ch * c:(ch + 1) * c]) for ch in range(nc)]
    cs = jnp.concatenate(cs_chunks, axis=0)
    cs_end = [x[c - 1:c] for x in cs_chunks]
    cs_last = jnp.concatenate([jnp.broadcast_to(x, (c, W_D)) for x in cs_end], axis=0)
    e_neg = jnp.exp(-cs)
    e_tail = jnp.exp(cs_last - cs)
    b = kk * a
    pre = _wkv_precompute(kk * jnp.exp(cs - ld), r * jnp.exp(cs), b * e_neg, k * e_neg, b * e_tail, k * e_tail, v, nc)
    _interleave(_wkv_scan_steps(s_ref, o_ref, pre, [jnp.exp(x) for x in cs_end], nc),
                _attn_sample_steps(q8_ref, ck_ref, cv_ref, o8_ref, bb))
    yd_ref[0] = _rwkv_finish(o_ref[...], r, k, v, g, rk_ref[...], lg_ref[...], lb_ref[...], seg).astype(yd_ref.dtype)

    @pl.when(t == nt - 1)
    def _():
        shift_ref[0] = pd[tt - 1:tt]
        for hd in range(D_HEADS):
            wkv_ref[0, hd] = _head_cols(s_ref[hd // 2], hd % 2)


def _rwkv_prompt(pd, params, q, cache_k, cache_v, layer, tt):
    bn, t, _ = pd.shape
    nt = t // tt
    n = q.shape[0]
    bb = n // (bn * nt)
    assert bb * bn * nt == n
    nbytes = 2 * tt * D_PROJ * 4 + 40 * tt * W_D * 4 + 4 * bb * MEM_ROWS * LANES * 4 + 2 * bb * MEM_ROWS * LANES * 2
    req = pl.BlockSpec((bb, SUBLANES, LANES), lambda i, j: (i * nt + j, 0, 0))
    cache_spec = pl.BlockSpec((None, bb, MEM_ROWS, LANES), lambda i, j: (layer, i * nt + j, 0, 0))
    yd, shift, wkv, o8 = pl.pallas_call(
        functools.partial(_rwkv_prompt_kernel, tt=tt, bb=bb),
        out_shape=(jax.ShapeDtypeStruct((bn, t, W_D), BF16),
                   jax.ShapeDtypeStruct((bn, 1, D_PROJ), F32),
                   jax.ShapeDtypeStruct((bn, D_HEADS, D_HEAD_DIM, D_HEAD_DIM), F32),
                   jax.ShapeDtypeStruct((n, SUBLANES, LANES), F32)),
        grid=(bn, nt),
        in_specs=[pl.BlockSpec((1, tt, D_PROJ), lambda i, j: (i, j, 0))] + [_lspec(z, layer) for z in params]
                 + [req, cache_spec, cache_spec],
        out_specs=(pl.BlockSpec((1, tt, W_D), lambda i, j: (i, j, 0)),
                   pl.BlockSpec((1, 1, D_PROJ), lambda i, j: (i, 0, 0)),
                   pl.BlockSpec((1, D_HEADS, D_HEAD_DIM, D_HEAD_DIM), lambda i, j: (i, 0, 0, 0)),
                   req),
        scratch_shapes=[pltpu.VMEM((ROW_CARRY, D_PROJ), F32),
                        pltpu.VMEM((HEAD_PAIRS, D_HEAD_DIM, LANES), F32),
                        pltpu.VMEM((tt, W_D), F32)],
        compiler_params=_params(("arbitrary", "arbitrary"), nbytes),
        name="rwkv_prompt",
    )(pd, *map(_operand, params), _to_pair_rows(q), _cache_rows_view(cache_k), _cache_rows_view(cache_v))
    return yd, shift, wkv, _from_pair_rows(o8)


def _rwkv_sample_kernel(pd_ref, sh_ref, st_ref, mu_ref, w0_ref, w2_ref, a0_ref, a2_ref, g2_ref, kk_ref,
                        ka_ref, rk_ref, lg_ref, lb_ref, yd_ref, so_ref, rows_ref, cols_ref, ot_ref):
    h = pl.program_id(0)
    n = D_HEAD_DIM

    @pl.when(h == 0)
    def _():
        pd = pd_ref[...]
        xs = pd + (sh_ref[...] - pd) * mu_ref[...]
        seg = _head_ones(W_D, D_HEAD_DIM)
        r, k, v, kk, a, ld, g = _rwkv_inputs(xs, w0_ref[...], w2_ref[...], a0_ref[...], a2_ref[...],
                                             g2_ref[...], kk_ref[...], ka_ref[...], seg)
        for j, x in enumerate((r, k, v, g)):
            rows_ref[j] = x
        for j, x in enumerate((kk, jnp.exp(ld), kk * a, k, r, v)):
            cols_ref[j] = x.T

    base = pl.multiple_of(h * n, n)
    kap, dec, bvec, kvec, rvec = (cols_ref[j, pl.ds(base, n), :] for j in range(5))
    for vi in range(n):
        s = st_ref[0, vi]
        u = -jnp.sum(s * kap, axis=0, keepdims=True)
        s = s * dec + u * bvec + cols_ref[5, pl.ds(base + vi, 1), :] * kvec
        so_ref[0, vi] = s
        ot_ref[pl.ds(base + vi, 1), :] = jnp.sum(s * rvec, axis=0, keepdims=True)

    @pl.when(h == pl.num_programs(0) - 1)
    def _():
        seg = _head_ones(W_D, D_HEAD_DIM)
        yd_ref[...] = _rwkv_finish(ot_ref[...].T, rows_ref[0], rows_ref[1], rows_ref[2], rows_ref[3],
                                   rk_ref[...], lg_ref[...], lb_ref[...], seg)


def _rwkv_sample(pd, shift, state, params, layer):
    n = pd.shape[0]
    sblock = (1, D_HEAD_DIM, D_HEAD_DIM, n)
    return pl.pallas_call(
        _rwkv_sample_kernel,
        out_shape=(jax.ShapeDtypeStruct((n, W_D), F32), jax.ShapeDtypeStruct(state.shape[1:], F32)),
        grid=(D_HEADS,),
        in_specs=[pl.BlockSpec((n, D_PROJ), lambda i: (0, 0)), _lspec(shift, layer),
                  pl.BlockSpec((None,) + sblock, lambda i: (layer, i, 0, 0, 0))]
                 + [_lspec(z, layer) for z in params],
        out_specs=(pl.BlockSpec((n, W_D), lambda i: (0, 0)), pl.BlockSpec(sblock, lambda i: (i, 0, 0, 0))),
        scratch_shapes=[pltpu.VMEM((4, n, W_D), F32), pltpu.VMEM((6, W_D, n), F32), pltpu.VMEM((W_D, n), F32)],
        compiler_params=pltpu.CompilerParams(dimension_semantics=("arbitrary",)),
        name="rwkv_sample",
    )(pd, shift, state, *map(_operand, params))


def _block_diag(w):
    gn, n, _ = w.shape
    eye = jnp.eye(gn, dtype=w.dtype)
    return (eye[:, None, :, None] * w[:, :, None, :]).reshape(gn * n, gn * n)


def kernel(x_prompt, x_sample, mem_prompt, cache_mem_k, cache_mem_v, state_pool, state_conv, state_shift, state_wkv,
           w_in, mu_d, ln_v_g, ln_v_b, ws_chunk, b_chunk, w_pool, pool_scale, conv_w,
           rwkv_w0, rwkv_w2, rwkv_a0, rwkv_a2, rwkv_g2, rwkv_k_k, rwkv_k_a, rwkv_r_k, rwkv_lnx_g, rwkv_lnx_b,
           w_out, ln1_g, ln1_b, w_xq, w_xk, w_xv, w_xo, ln2_g, ln2_b, ffn_w1, ffn_w3, ffn_w2, ln3_g, ln3_b):
    bp, t_p, d = x_prompt.shape
    ns, t_s, _ = x_sample.shape
    depth = w_in.shape[0]
    assert d == D_MODEL and t_s == 1 and t_p % CHUNK == 0 and w_in.shape[2] == PROJ
    alpha = (2 * depth) ** 0.25
    mp = bp * t_p
    nw = W_A + W_B + W_C
    row = _pack_rows(dict(
        ln_v_g=ln_v_g, ln_v_b=ln_v_b, pool_scale=pool_scale, mu_d=mu_d, rwkv_w0=rwkv_w0, rwkv_a0=rwkv_a0,
        rwkv_k_k=rwkv_k_k, rwkv_k_a=rwkv_k_a, rwkv_r_k=rwkv_r_k.reshape(depth, W_D), rwkv_lnx_g=rwkv_lnx_g,
        rwkv_lnx_b=rwkv_lnx_b, ln1_g=ln1_g, ln1_b=ln1_b, ln2_g=ln2_g, ln2_b=ln2_b, ln3_g=ln3_g, ln3_b=ln3_b))

    w_in_b, w_out_b = w_in, w_out
    w_xq_b, w_xk_b, w_xv_b, w_xo_b = w_xq, w_xk, w_xv, w_xo
    w1_b, w3_b, w2_b = ffn_w1, ffn_w3, ffn_w2
    ws_flat = ws_chunk.reshape(depth, A_HEADS * CHUNK, CHUNK)
    bias_full = jnp.repeat(jnp.swapaxes(b_chunk, 1, 2), A_HEAD_DIM, axis=2)
    ws0 = jnp.repeat(ws_chunk[:, :, 0, 0], A_HEAD_DIM, axis=1).reshape(depth, 1, W_A)
    wpool_bd = jnp.stack([_block_diag(w_pool[l]) for l in range(depth)]).astype(BF16)
    abc_w = (row['ln_v_g'], row['ln_v_b'])
    abc_w2 = (wpool_bd, row['pool_scale'], conv_w)
    rw = (row['mu_d'], row['rwkv_w0'], rwkv_w2, row['rwkv_a0'], rwkv_a2, rwkv_g2,
          row['rwkv_k_k'], row['rwkv_k_a'], row['rwkv_r_k'], row['rwkv_lnx_g'], row['rwkv_lnx_b'])
    ln1, ln2, ln3 = (row['ln1_g'], row['ln1_b']), (row['ln2_g'], row['ln2_b']), (row['ln3_g'], row['ln3_b'])
    pool_view = jnp.swapaxes(state_pool, 1, 2)
    wkv_view = state_wkv.transpose(0, 2, 3, 4, 1)
    conv_view = state_conv.reshape(depth, ns, (CONV_W - 1) * W_C)
    shift_view = state_shift.reshape(depth, ns, D_PROJ)

    hp = x_prompt
    hs = x_sample.reshape(ns, d)
    mem = mem_prompt.reshape(bp * MEM_LEN, d)
    mk_all, mv_all, mk_rows, mv_rows = _mem_kv(mem, w_xk_b, w_xv_b, 512)
    mk_all = mk_all.reshape(depth, bp, MEM_LEN, d)
    mv_all = mv_all.reshape(depth, bp, MEM_LEN, d)
    outs = [[] for _ in range(10)]
    for l in range(depth):
        pd, mix, v_last, pool_p, conv_p, *rounded = _proj_abc_prompt(hp, w_in_b, *abc_w, ws_flat, bias_full, *abc_w2,
                                                                     (w1_b, w3_b, w2_b, w_out_b, w_xq_b, w_xo_b), l,
                                                                     tt=1024)
        ffn_b, (w_out_s, w_xq_s, w_xo_s) = rounded[:3], rounded[3:]

        pabc_s, pd_s = _proj(hs, w_in_b, l, ns)
        mix_s, v_s, pool_s, conv_s = _abc_sample(pabc_s, pool_view, conv_view, *abc_w, ws0, bias_full, *abc_w2,
                                                 l, pos0=PAST_LEN)
        yd_s, wkv_s = _rwkv_sample(pd_s, shift_view, wkv_view, rw, l)
        hs = _mm_res_ln([mix_s, yd_s], w_out_s, l, hs, *ln1, tm=ns, alpha=alpha, name="out_proj_s", w_layer=0)
        q_s = _mm(hs, w_xq_s, 0, ns, "q_s")

        yd, shift_p, wkv_p, o_s = _rwkv_prompt(pd, rw, q_s, cache_mem_k, cache_mem_v, l, tt=512)
        hp = _attn_prompt(mix, yd, hp, mk_all, mv_all, w_out_s, *ln1, w_xq_s, w_xo_s, *ln2, l, tq=512, alpha=alpha,
                          w_layer=0)
        hp = _ffn(hp.reshape(mp, d), *ffn_b, *ln3, l, tm=1024, tf=256, alpha=alpha, name="ffn").reshape(bp, t_p, d)
        for lst, val in zip(outs[:5], (v_last, pool_p, conv_p, shift_p, wkv_p)):
            lst.append(val)

        hs = _mm_res_ln([o_s], w_xo_s, l, hs, *ln2, tm=ns, alpha=alpha, name="xo_s", w_layer=0)
        hs = _ffn(hs, *ffn_b, *ln3, l, tm=ns, tf=256, alpha=alpha, name="ffn_s")
        for lst, val in zip(outs[5:], (v_s.reshape(ns, 1, W_A), pool_s,
                                       conv_s.reshape(ns, CONV_W - 1, W_C), pd_s.reshape(ns, 1, D_PROJ), wkv_s)):
            lst.append(val)

    stacked = [jnp.stack(o) for o in outs]
    stacked[6] = jnp.swapaxes(stacked[6], 1, 2)
    stacked[9] = stacked[9].transpose(0, 4, 1, 2, 3)
    return ((hp, hs.reshape(ns, 1, d)) + tuple(stacked[:5]) + (_cache_from_rows(mk_rows, bp), _cache_from_rows(mv_rows, bp))
            + tuple(stacked[5:]))
```
